```python
import jax
import jax.numpy as jnp
from jax import lax
import numpy as np

D_MODEL = 1024
BATCH = 16
SEQ = 2048
DEPTH = 1
DEC_BATCH = 128
DEC_SEQ = 4
PAST_LEN = 16384
PAGE_SIZE = 128

N_Q_HEADS = 8
N_KV_HEADS = 2
HEAD_DIM = 64
GROUP = N_Q_HEADS // N_KV_HEADS
WINDOW = 128
ROT_DIM = HEAD_DIM // 4
ROPE_THETA = 500000.0
ATTN_DIM = N_Q_HEADS * HEAD_DIM
KV_DIM = N_KV_HEADS * HEAD_DIM
DN_HEADS = 4
DN_DK = 128
DN_DV = 128
DN_QK_DIM = DN_HEADS * DN_DK
DN_V_DIM = DN_HEADS * DN_DV
DN_CONV_DIM = 2 * DN_QK_DIM + DN_V_DIM
CONV_W = 4
DN_CHUNK = 64
IN_DIM = ATTN_DIM + 2 * KV_DIM + DN_CONV_DIM + DN_V_DIM + 2 * DN_HEADS + 2 * D_MODEL
N_EXPERTS = 32
TOP_K = 4
D_EXPERT = D_MODEL
SWIGLU_LIMIT = 7.0
SWIGLU_ALPHA = 1.702
EXPERT_BLOCK = 128
DEEPNORM_ALPHA = (2 * DEPTH) ** 0.25
DEEPNORM_BETA = (8 * DEPTH) ** -0.25
LN_EPS = 1e-5
RMS_EPS = 1e-6
L2_EPS = 1e-6

kernel_name = 'hybrid_swa_sink_gdn_moe_step'


def layer_norm(x, w, b):
    xf = x.astype(jnp.float32)
    mu = jnp.mean(xf, -1, keepdims=True)
    var = jnp.mean(jnp.square(xf - mu), -1, keepdims=True)
    return ((xf - mu) * lax.rsqrt(var + LN_EPS) * w.astype(jnp.float32) + b.astype(jnp.float32)).astype(x.dtype)


def l2_normalize(x):
    xf = x.astype(jnp.float32)
    return xf * lax.rsqrt(jnp.sum(xf * xf, -1, keepdims=True) + L2_EPS)


def apply_rotary(x, pos):
    half = ROT_DIM // 2
    inv_freq = jnp.power(jnp.float32(ROPE_THETA), -jnp.arange(half, dtype=jnp.float32) * (2.0 / ROT_DIM))
    ang = pos.astype(jnp.float32)[:, None] * inv_freq[None, :]
    cos = jnp.cos(ang)[:, None, :]
    sin = jnp.sin(ang)[:, None, :]
    xr = x[..., :ROT_DIM].astype(jnp.float32)
    x1, x2 = xr[..., :half], xr[..., half:]
    rot = jnp.concatenate([x1 * cos - x2 * sin, x2 * cos + x1 * sin], -1)
    return jnp.concatenate([rot.astype(x.dtype), x[..., ROT_DIM:]], -1)


def banded_sink_attention(qb, kb, vb, q_pos, k_pos, sinks):
    N, NB, LQ = qb.shape[:3]
    qg = qb.reshape(N, NB, LQ, N_KV_HEADS, GROUP, HEAD_DIM)
    s = jnp.einsum('bnqhgd,bnkhd->bnhgqk', qg, kb, preferred_element_type=jnp.float32) * (HEAD_DIM ** -0.5)
    rel = q_pos[:, :, None] - k_pos[:, None, :]
    valid = (rel >= 0) & (rel < WINDOW) & (k_pos[:, None, :] >= 0)
    s = jnp.where(valid[None, :, None, None], s, -jnp.inf)
    sink = sinks.astype(jnp.float32).reshape(N_KV_HEADS, GROUP)[None, None, :, :, None, None]
    m = jnp.maximum(jnp.max(s, -1, keepdims=True), sink)
    p = jnp.exp(s - m)
    denom = jnp.sum(p, -1, keepdims=True) + jnp.exp(sink - m)
    o = jnp.einsum('bnhgqk,bnkhd->bnqhgd', (p / denom).astype(vb.dtype), vb)
    return o.reshape(N, NB, LQ, ATTN_DIM)


def gated_delta_rule(q, k, v, g, beta, state):
    N, L, H, DK = q.shape
    DV = v.shape[-1]
    C = min(DN_CHUNK, L)
    pad = (-L) % C
    NC = (L + pad) // C

    def chunks(t):
        t = t.astype(jnp.float32)
        t = jnp.pad(t, [(0, 0), (0, pad)] + [(0, 0)] * (t.ndim - 2))
        t = t.reshape((N, NC, C) + t.shape[2:])
        if t.ndim == 5:
            return t.transpose(1, 0, 3, 2, 4)
        return t.transpose(1, 0, 3, 2)

    qc, kc, vc = chunks(q), chunks(k), chunks(v)
    gc = jnp.cumsum(chunks(g), axis=-1)
    bc = chunks(beta)
    causal = jnp.tril(jnp.ones((C, C), bool))
    strict = jnp.tril(jnp.ones((C, C), bool), -1)
    decay = jnp.exp(jnp.where(causal, gc[..., :, None] - gc[..., None, :], -jnp.inf))
    kb = kc * bc[..., None]
    a_mat = jnp.where(strict, jnp.einsum('...id,...jd->...ij', kb, kc) * decay, 0.0)
    eye = jnp.eye(C, dtype=jnp.float32)
    rhs = jnp.concatenate([vc * bc[..., None], kb * jnp.exp(gc)[..., None]], -1)
    sol = lax.linalg.triangular_solve(eye + a_mat, rhs, left_side=True, lower=True, unit_diagonal=True)
    u, w = sol[..., :DV], sol[..., DV:]
    qk = jnp.einsum('...id,...jd->...ij', qc, kc) * decay

    def step(S, xs):
        q_i, k_i, u_i, w_i, qk_i, g_i = xs
        v_new = u_i - jnp.einsum('nhcd,nhde->nhce', w_i, S)
        o = jnp.einsum('nhcd,nhde->nhce', q_i * jnp.exp(g_i)[..., None], S) + jnp.einsum('nhij,nhje->nhie', qk_i, v_new)
        g_last = g_i[..., -1:]
        S = S * jnp.exp(g_last)[..., None] + jnp.einsum('nhcd,nhce->nhde', k_i * jnp.exp(g_last - g_i)[..., None], v_new)
        return S, o

    S, o = lax.scan(step, state.astype(jnp.float32), (qc, kc, u, w, qk, gc))
    o = o.transpose(1, 0, 3, 2, 4).reshape(N, NC * C, H, DV)[:, :L]
    return o, S


def gated_deltanet(qkv, z, b, a, conv_state, ssm_state, w_conv, a_log, dt_bias, norm_w):
    N, L, _ = qkv.shape
    xin = jnp.concatenate([conv_state.astype(qkv.dtype), qkv], axis=1)
    y = lax.conv_general_dilated(xin, w_conv.astype(qkv.dtype)[:, None, :], (1,), 'VALID',
                                 dimension_numbers=('NWC', 'WIO', 'NWC'), feature_group_count=DN_CONV_DIM)
    y = jax.nn.silu(y)
    q = l2_normalize(y[..., :DN_QK_DIM].reshape(N, L, DN_HEADS, DN_DK)) * (DN_DK ** -0.5)
    k = l2_normalize(y[..., DN_QK_DIM:2 * DN_QK_DIM].reshape(N, L, DN_HEADS, DN_DK))
    v = y[..., 2 * DN_QK_DIM:].reshape(N, L, DN_HEADS, DN_DV)
    beta = jax.nn.sigmoid(b.astype(jnp.float32))
    g = -jnp.exp(a_log.astype(jnp.float32)) * jax.nn.softplus(a.astype(jnp.float32) + dt_bias.astype(jnp.float32))
    o, new_ssm = gated_delta_rule(q, k, v, g, beta, ssm_state)
    o = o * lax.rsqrt(jnp.mean(o * o, -1, keepdims=True) + RMS_EPS) * norm_w.astype(jnp.float32)
    o = o * jax.nn.silu(z.astype(jnp.float32).reshape(N, L, DN_HEADS, DN_DV))
    return o.astype(qkv.dtype).reshape(N, L, DN_V_DIM), xin[:, -(CONV_W - 1):], new_ssm


def token_mixer(h, start, is_prompt, win_k, win_v, conv_state, ssm_state, lw):
    N, L, _ = h.shape
    sizes = [ATTN_DIM, KV_DIM, KV_DIM, DN_CONV_DIM, DN_V_DIM, DN_HEADS, DN_HEADS, D_MODEL, D_MODEL]
    cuts = [int(s) for s in np.cumsum(sizes)[:-1]]
    q, k, v, qkv_dn, z_dn, b_dn, a_dn, gate_attn, gate_dn = jnp.split(h @ lw['w_in'], cuts, axis=-1)
    pos = start + jnp.arange(L, dtype=jnp.int32)
    q = apply_rotary(q.reshape(N, L, N_Q_HEADS, HEAD_DIM), pos)
    k = apply_rotary(k.reshape(N, L, N_KV_HEADS, HEAD_DIM), pos)
    v = v.reshape(N, L, N_KV_HEADS, HEAD_DIM)
    if is_prompt:
        nb = L // WINDOW

        def blocks_with_prev(t):
            tb = t.reshape(N, nb, WINDOW, N_KV_HEADS, HEAD_DIM)
            prev = jnp.concatenate([jnp.zeros_like(tb[:, :1]), tb[:, :-1]], axis=1)
            return jnp.concatenate([prev, tb], axis=2)

        q_pos = pos.reshape(nb, WINDOW)
        k_pos = jnp.concatenate([q_pos - WINDOW, q_pos], axis=1)
        attn = banded_sink_attention(q.reshape(N, nb, WINDOW, N_Q_HEADS, HEAD_DIM), blocks_with_prev(k),
                                     blocks_with_prev(v), q_pos, k_pos, lw['attn_sinks'])
        k_all, v_all = k, v
        conv_state = jnp.zeros((N, CONV_W - 1, DN_CONV_DIM), h.dtype)
        ssm_state = jnp.zeros((N, DN_HEADS, DN_DK, DN_DV), jnp.float32)
    else:
        k_all = jnp.concatenate([win_k.astype(k.dtype), k], axis=1)
        v_all = jnp.concatenate([win_v.astype(v.dtype), v], axis=1)
        q_pos = pos[None]
        k_pos = (start - WINDOW + jnp.arange(WINDOW + L, dtype=jnp.int32))[None]
        attn = banded_sink_attention(q[:, None], k_all[:, None], v_all[:, None], q_pos, k_pos, lw['attn_sinks'])
    attn = attn.reshape(N, L, ATTN_DIM)
    dn, new_conv, new_ssm = gated_deltanet(qkv_dn, z_dn, b_dn, a_dn, conv_state, ssm_state, lw['w_conv'],
                                           lw['dn_a_log'], lw['dn_dt_bias'], lw['dn_norm_w'])
    merged = (jax.nn.sigmoid(gate_attn) * (attn @ lw['w_proj_attn'])
              + jax.nn.sigmoid(gate_dn) * (dn @ lw['w_proj_dn']))
    return merged @ lw['w_out'], (k_all[:, -WINDOW:], v_all[:, -WINDOW:], new_conv, new_ssm)


def moe_ffn(h, w_router, b_router, w_gu, b_gu, w_down, b_down):
    shp = h.shape
    t = h.reshape(-1, shp[-1])
    T = t.shape[0]
    logits = (t @ w_router).astype(jnp.float32) + b_router.astype(jnp.float32)
    top_val, top_idx = lax.top_k(logits, TOP_K)
    gate = jax.nn.softmax(top_val, axis=-1)
    M = T * TOP_K
    flat_e = top_idx.reshape(M).astype(jnp.int32)
    flat_tok = jnp.repeat(jnp.arange(T, dtype=jnp.int32), TOP_K)
    flat_gate = gate.reshape(M)
    order = jnp.argsort(flat_e)
    sorted_e = flat_e[order]
    counts = jnp.bincount(flat_e, length=N_EXPERTS).astype(jnp.int32)
    padded = (counts + EXPERT_BLOCK - 1) // EXPERT_BLOCK * EXPERT_BLOCK
    grp_start = jnp.cumsum(counts) - counts
    pad_end = jnp.cumsum(padded)
    pad_start = pad_end - padded
    dest = pad_start[sorted_e] + jnp.arange(M, dtype=jnp.int32) - grp_start[sorted_e]
    n_blocks = -(-(M + N_EXPERTS * (EXPERT_BLOCK - 1)) // EXPERT_BLOCK)
    P = n_blocks * EXPERT_BLOCK
    slot_tok = jnp.zeros((P,), jnp.int32).at[dest].set(flat_tok[order])
    slot_gate = jnp.zeros((P,), jnp.float32).at[dest].set(flat_gate[order])
    block_e = jnp.minimum(jnp.searchsorted(pad_end, jnp.arange(n_blocks, dtype=jnp.int32) * EXPERT_BLOCK, side='right'),
                          N_EXPERTS - 1)
    xb = t[slot_tok].reshape(n_blocks, EXPERT_BLOCK, shp[-1])

    def expert_block(args):
        xblk, e = args
        gu = xblk @ w_gu[e] + b_gu[e]
        glu = jnp.minimum(gu[:, :D_EXPERT], SWIGLU_LIMIT)
        lin = jnp.clip(gu[:, D_EXPERT:], -SWIGLU_LIMIT, SWIGLU_LIMIT)
        act = glu * jax.nn.sigmoid(SWIGLU_ALPHA * glu) * (lin + 1.0)
        return act @ w_down[e] + b_down[e]

    yb = lax.map(expert_block, (xb, block_e)).reshape(P, shp[-1])
    out = jnp.zeros((T, shp[-1]), jnp.float32).at[slot_tok].add(yb.astype(jnp.float32) * slot_gate[:, None])
    return out.astype(h.dtype).reshape(shp)


def decoder_layer(x, c, start, is_prompt, win_k, win_v, conv_state, ssm_state, lw):
    mod = (jax.nn.silu(c) @ lw['w_ada'] + lw['b_ada'])[:, None, :]
    sh1, sc1, gt1, sh2, sc2, gt2 = jnp.split(mod, 6, axis=-1)
    h = x * (1.0 + sc1) + sh1
    mix, new_state = token_mixer(h, start, is_prompt, win_k, win_v, conv_state, ssm_state, lw)
    x = layer_norm(DEEPNORM_ALPHA * x + gt1 * mix, lw['ln1_w'], lw['ln1_b'])
    h = x * (1.0 + sc2) + sh2
    ff = moe_ffn(h, lw['w_router'], lw['b_router'], lw['w_gu'], lw['b_gu'], lw['w_down'], lw['b_down'])
    x = layer_norm(DEEPNORM_ALPHA * x + gt2 * ff, lw['ln2_w'], lw['ln2_b'])
    return x, new_state


def setup_inputs(seed: int = 0) -> dict:
    key = jax.random.key(seed)
    ks = iter(jax.random.split(key, 40))

    def nrm(shape, scale):
        return jax.random.normal(next(ks), shape, jnp.float32) * scale

    x_prompt = nrm((BATCH, SEQ, D_MODEL), 1.0)
    x_sample = nrm((DEC_BATCH, DEC_SEQ, D_MODEL), 1.0)
    state_win_k = nrm((DEPTH, DEC_BATCH, WINDOW, N_KV_HEADS, HEAD_DIM), 1.0)
    state_win_v = nrm((DEPTH, DEC_BATCH, WINDOW, N_KV_HEADS, HEAD_DIM), 1.0)
    state_conv = nrm((DEPTH, DEC_BATCH, CONV_W - 1, DN_CONV_DIM), 1.0)
    state_ssm = nrm((DEPTH, DEC_BATCH, DN_HEADS, DN_DK, DN_DV), 0.1)
    c_prompt = nrm((BATCH, D_MODEL), 1.0)
    c_sample = nrm((DEC_BATCH, D_MODEL), 1.0)
    w_ada = nrm((DEPTH, D_MODEL, 6 * D_MODEL), 0.5 * D_MODEL ** -0.5)
    b_ada = nrm((DEPTH, 6 * D_MODEL), 0.02)
    w_in = nrm((DEPTH, D_MODEL, IN_DIM), D_MODEL ** -0.5)
    attn_sinks = nrm((DEPTH, N_Q_HEADS), 1.0)
    w_conv = nrm((DEPTH, CONV_W, DN_CONV_DIM), CONV_W ** -0.5)
    dn_a_log = jnp.log(jax.random.uniform(next(ks), (DEPTH, DN_HEADS), jnp.float32, 1.0, 16.0))
    dt = jnp.exp(jax.random.uniform(next(ks), (DEPTH, DN_HEADS), jnp.float32, jnp.log(1e-3), jnp.log(1e-1)))
    dn_dt_bias = dt + jnp.log(-jnp.expm1(-dt))
    dn_norm_w = 1.0 + nrm((DEPTH, DN_DV), 0.02)
    w_proj_attn = nrm((DEPTH, ATTN_DIM, D_MODEL), ATTN_DIM ** -0.5)
    w_proj_dn = nrm((DEPTH, DN_V_DIM, D_MODEL), DN_V_DIM ** -0.5)
    w_out = nrm((DEPTH, D_MODEL, D_MODEL), DEEPNORM_BETA * D_MODEL ** -0.5)
    ln1_w = 1.0 + nrm((DEPTH, D_MODEL), 0.02)
    ln1_b = nrm((DEPTH, D_MODEL), 0.02)
    w_router = nrm((DEPTH, D_MODEL, N_EXPERTS), D_MODEL ** -0.5)
    b_router = nrm((DEPTH, N_EXPERTS), 0.01)
    w_gu = nrm((DEPTH, N_EXPERTS, D_MODEL, 2 * D_EXPERT), D_MODEL ** -0.5)
    b_gu = nrm((DEPTH, N_EXPERTS, 2 * D_EXPERT), 0.02)
    w_down = nrm((DEPTH, N_EXPERTS, D_EXPERT, D_MODEL), DEEPNORM_BETA * D_EXPERT ** -0.5)
    b_down = nrm((DEPTH, N_EXPERTS, D_MODEL), 0.02)
    ln2_w = 1.0 + nrm((DEPTH, D_MODEL), 0.02)
    ln2_b = nrm((DEPTH, D_MODEL), 0.02)
    return {'x_prompt': x_prompt, 'x_sample': x_sample,
            'state_win_k': state_win_k, 'state_win_v': state_win_v, 'state_conv': state_conv, 'state_ssm': state_ssm,
            'c_prompt': c_prompt, 'c_sample': c_sample,
            'w_ada': w_ada, 'b_ada': b_ada, 'w_in': w_in, 'attn_sinks': attn_sinks, 'w_conv': w_conv,
            'dn_a_log': dn_a_log, 'dn_dt_bias': dn_dt_bias, 'dn_norm_w': dn_norm_w,
            'w_proj_attn': w_proj_attn, 'w_proj_dn': w_proj_dn, 'w_out': w_out, 'ln1_w': ln1_w, 'ln1_b': ln1_b,
            'w_router': w_router, 'b_router': b_router, 'w_gu': w_gu, 'b_gu': b_gu, 'w_down': w_down,
            'b_down': b_down, 'ln2_w': ln2_w, 'ln2_b': ln2_b}


def reference(x_prompt, x_sample, state_win_k, state_win_v, state_conv, state_ssm, c_prompt, c_sample,
              w_ada, b_ada, w_in, attn_sinks, w_conv, dn_a_log, dn_dt_bias, dn_norm_w,
              w_proj_attn, w_proj_dn, w_out, ln1_w, ln1_b, w_router, b_router, w_gu, b_gu, w_down, b_down,
              ln2_w, ln2_b):
    y_prompt = x_prompt
    y_sample = x_sample
    p_wk, p_wv, p_conv, p_ssm = [], [], [], []
    s_wk, s_wv, s_conv, s_ssm = [], [], [], []
    for l in range(DEPTH):
        lw = {'w_ada': w_ada[l], 'b_ada': b_ada[l], 'w_in': w_in[l], 'attn_sinks': attn_sinks[l],
              'w_conv': w_conv[l], 'dn_a_log': dn_a_log[l], 'dn_dt_bias': dn_dt_bias[l], 'dn_norm_w': dn_norm_w[l],
              'w_proj_attn': w_proj_attn[l], 'w_proj_dn': w_proj_dn[l], 'w_out': w_out[l],
              'ln1_w': ln1_w[l], 'ln1_b': ln1_b[l], 'w_router': w_router[l], 'b_router': b_router[l],
              'w_gu': w_gu[l], 'b_gu': b_gu[l], 'w_down': w_down[l], 'b_down': b_down[l],
              'ln2_w': ln2_w[l], 'ln2_b': ln2_b[l]}
        y_prompt, (wk, wv, cv, ss) = decoder_layer(y_prompt, c_prompt, 0, True, None, None, None, None, lw)
        p_wk.append(wk)
        p_wv.append(wv)
        p_conv.append(cv)
        p_ssm.append(ss)
        y_sample, (wk, wv, cv, ss) = decoder_layer(y_sample, c_sample, PAST_LEN, False, state_win_k[l],
                                                   state_win_v[l], state_conv[l], state_ssm[l], lw)
        s_wk.append(wk)
        s_wv.append(wv)
        s_conv.append(cv)
        s_ssm.append(ss)
    return (y_prompt, y_sample, jnp.stack(p_wk), jnp.stack(p_wv), jnp.stack(p_conv), jnp.stack(p_ssm),
            jnp.stack(s_wk), jnp.stack(s_wv), jnp.stack(s_conv), jnp.stack(s_ssm))
```

```python
import functools

import numpy as np
import jax
import jax.numpy as jnp
from jax import lax
from jax.experimental import pallas as pl
from jax.experimental.pallas import tpu as pltpu

F32 = jnp.float32
BF16 = jnp.bfloat16
I32 = jnp.int32
HIGHEST = lax.Precision.HIGHEST

PAST_LEN = 16384
ROPE_THETA = 500000.0
TOP_K = 4
SWIGLU_LIMIT = 7.0
SWIGLU_ALPHA = 1.702
DN_CHUNK = 64
LN_EPS = 1e-5
RMS_EPS = 1e-6
L2_EPS = 1e-6

LANES = 128
SUBLANES = 8
VMEM_LIMIT_BYTES = 56 * 1024 * 1024

TOKEN_TILE = 256
EXPERT_ROWS = 512
NEG_BIG = -1e30


def _cparams(*sem):
    return pltpu.CompilerParams(dimension_semantics=sem, vmem_limit_bytes=VMEM_LIMIT_BYTES)


def _silu(x):
    return x * jax.nn.sigmoid(x)


def _bdot(a, b):
    return jnp.dot(a.astype(BF16), b.astype(BF16), preferred_element_type=F32)


def _ada_body(c_ref, w_ref, b_ref, o_ref):
    o_ref[...] = _bdot(_silu(c_ref[...]), w_ref[...]) + b_ref[...]


def _ada(c_all, w_ada, b_ada):
    n, d = c_all.shape
    dout = w_ada.shape[1]
    tn = d
    return pl.pallas_call(
        _ada_body,
        grid=(dout // tn,),
        in_specs=[pl.BlockSpec((n, d), lambda j: (0, 0)),
                  pl.BlockSpec((d, tn), lambda j: (0, j)),
                  pl.BlockSpec((1, tn), lambda j: (0, j))],
        out_specs=pl.BlockSpec((n, tn), lambda j: (0, j)),
        out_shape=jax.ShapeDtypeStruct((n, dout), F32),
        compiler_params=_cparams("arbitrary"),
        name="ada",
    )(c_all, w_ada, b_ada.reshape(1, dout))


def _inproj_body(x_ref, sh_ref, sc_ref, ct_ref, s1_ref, s2_ref, w_ref,
                 a_ref, dn_ref, z_ref, g_ref, ba_ref, *win_refs, cuts, n_rot_chunks, window):
    h = (x_ref[...] * (1.0 + sc_ref[...]) + sh_ref[...]).astype(BF16)

    def mm(lo, hi):
        return jnp.dot(h, w_ref[:, lo:hi], preferred_element_type=F32)

    c_a, c_dn, c_z, c_g, c_ba = cuts
    qkv = mm(0, c_a)
    ct, s1, s2 = ct_ref[...], s1_ref[...], s2_ref[...]
    cols = []
    for c in range(n_rot_chunks):
        xc = qkv[:, c * LANES:(c + 1) * LANES]
        cols.append(xc * ct + pltpu.roll(xc, LANES - SUBLANES, 1) * s1 + pltpu.roll(xc, SUBLANES, 1) * s2)
    cols.append(qkv[:, n_rot_chunks * LANES:])
    rot = jnp.concatenate(cols, axis=1)
    a_ref[...] = rot.astype(a_ref.dtype)
    dn = mm(c_a, c_dn)
    dn_ref[...] = dn.astype(dn_ref.dtype)
    z_ref[...] = mm(c_dn, c_z).astype(z_ref.dtype)
    g_ref[...] = mm(c_z, c_g).astype(g_ref.dtype)
    ba_ref[...] = mm(c_g, c_ba)
    if win_refs:
        tail_ref, kvw_ref = win_refs
        tm = dn.shape[0]
        tail_ref[...] = dn[tm - SUBLANES:, :]
        kvw_ref[...] = rot[tm - window:, n_rot_chunks * LANES - LANES:]


def _inproj(x, sh, sc, tabs, w_perm, cuts, *, per_token_mod, tiles_per_seq, act_dtype, window, kv_cols):
    t, d = x.shape
    tm = TOKEN_TILE
    nt = t // tm
    c_a, c_dn, c_z, c_g, c_ba = cuts
    n_rot_chunks = (c_a - kv_cols // 2) // LANES
    if per_token_mod:
        mod_spec = pl.BlockSpec((tm, d), lambda i: (i, 0))
        tab_spec = pl.BlockSpec((tm, LANES), lambda i: (0, 0))
    else:
        mod_spec = pl.BlockSpec((None, 1, d), lambda i: (i // tiles_per_seq, 0, 0))
        tab_spec = pl.BlockSpec((tm, LANES), lambda i: (i % tiles_per_seq, 0))
    out_shape = [jax.ShapeDtypeStruct((t, c_a), act_dtype),
                 jax.ShapeDtypeStruct((t, c_dn - c_a), act_dtype),
                 jax.ShapeDtypeStruct((t, c_z - c_dn), act_dtype),
                 jax.ShapeDtypeStruct((t, c_g - c_z), act_dtype),
                 jax.ShapeDtypeStruct((t, c_ba - c_g), F32)]
    out_specs = [pl.BlockSpec((tm, s.shape[1]), lambda i: (i, 0)) for s in out_shape]
    with_win = not per_token_mod
    if with_win:
        n_seq = nt // tiles_per_seq
        out_shape += [jax.ShapeDtypeStruct((nt, SUBLANES, c_dn - c_a), F32),
                      jax.ShapeDtypeStruct((n_seq, window, kv_cols), F32)]
        out_specs += [pl.BlockSpec((None, SUBLANES, c_dn - c_a), lambda i: (i, 0, 0)),
                      pl.BlockSpec((None, window, kv_cols), lambda i: (i // tiles_per_seq, 0, 0))]
    body = functools.partial(_inproj_body, cuts=cuts, n_rot_chunks=n_rot_chunks, window=window)
    return pl.pallas_call(
        body,
        grid=(nt,),
        in_specs=[pl.BlockSpec((tm, d), lambda i: (i, 0)), mod_spec, mod_spec,
                  tab_spec, tab_spec, tab_spec,
                  pl.BlockSpec((d, c_ba), lambda i: (0, 0))],
        out_specs=out_specs,
        out_shape=out_shape,
        compiler_params=_cparams("arbitrary"),
        name="inproj",
    )(x, sh, sc, *tabs, w_perm)


def _softmax_sink_pv(s, valid, sink, v):
    s = jnp.where(valid, s, -jnp.inf)
    m = jnp.maximum(jnp.max(s, axis=-1, keepdims=True), sink)
    p = jnp.exp(s - m)
    denom = jnp.sum(p, axis=-1, keepdims=True) + jnp.exp(sink - m)
    return jnp.dot((p / denom).astype(BF16), v, preferred_element_type=F32)


def _attn_prompt_body(sink_ref, q_ref, kvp_ref, kvc_ref, o_ref, *, n_q, n_kv, hd, window):
    j = pl.program_id(1)
    group = n_q // n_kv
    q = q_ref[...]
    kv = jnp.concatenate([kvp_ref[...], kvc_ref[...]], axis=0)
    r = lax.broadcasted_iota(I32, (window, 2 * window), 0)
    c = lax.broadcasted_iota(I32, (window, 2 * window), 1)
    rel = window + r - c
    valid = (rel >= 0) & (rel < window) & ((c >= window) | (j > 0))
    outs = []
    for h in range(n_q):
        kvh = h // group
        qh = q[:, h * hd:(h + 1) * hd]
        kh = kv[:, kvh * hd:(kvh + 1) * hd]
        vh = kv[:, (n_kv + kvh) * hd:(n_kv + kvh + 1) * hd]
        s = lax.dot_general(qh, kh, (((1,), (1,)), ((), ())), preferred_element_type=F32) * (hd ** -0.5)
        outs.append(_softmax_sink_pv(s, valid, sink_ref[h], vh))
    o_ref[...] = jnp.concatenate(outs, axis=1).astype(o_ref.dtype)


def _attn_prompt(qkva, sinks, n_seq, seq, n_q, n_kv, hd, window):
    qd, kvd = n_q * hd, 2 * n_kv * hd
    x3 = qkva.reshape(n_seq, seq, qd + kvd)
    nb = seq // window
    kv_blk = qd // kvd
    body = functools.partial(_attn_prompt_body, n_q=n_q, n_kv=n_kv, hd=hd, window=window)
    out = pl.pallas_call(
        body,
        grid=(n_seq, nb),
        in_specs=[pl.BlockSpec(memory_space=pltpu.SMEM),
                  pl.BlockSpec((None, window, qd), lambda n, j: (n, j, 0)),
                  pl.BlockSpec((None, window, kvd), lambda n, j: (n, jnp.maximum(j - 1, 0), kv_blk)),
                  pl.BlockSpec((None, window, kvd), lambda n, j: (n, j, kv_blk))],
        out_specs=pl.BlockSpec((None, window, qd), lambda n, j: (n, j, 0)),
        out_shape=jax.ShapeDtypeStruct((n_seq, seq, qd), BF16),
        compiler_params=_cparams("arbitrary", "arbitrary"),
        name="attn_prompt",
    )(sinks, x3, x3, x3)
    return out.reshape(n_seq * seq, qd)


def _attn_sample_body(sink_ref, q_ref, wk_ref, wv_ref, o_ref, wko_ref, wvo_ref,
                      *, bs, n_q, n_kv, hd, window, l_new, l_pad):
    group = n_q // n_kv
    qd = n_q * hd
    kd = n_kv * hd
    rows = group * l_pad
    r = lax.broadcasted_iota(I32, (rows, window + l_pad), 0) % l_pad
    c = lax.broadcasted_iota(I32, (rows, window + l_pad), 1)
    rel = window + r - c
    valid = (rel >= 0) & (rel < window) & (c < window + l_new)
    for b in range(bs):
        x = q_ref[b]
        k_new = x[:, qd:qd + kd]
        v_new = x[:, qd + kd:]
        wko_ref[b, 0:window - l_new, :] = wk_ref[b, l_new:window, :]
        wko_ref[b, window - l_new:window, :] = k_new[0:l_new, :]
        wvo_ref[b, 0:window - l_new, :] = wv_ref[b, l_new:window, :]
        wvo_ref[b, window - l_new:window, :] = v_new[0:l_new, :]
        k_all = jnp.concatenate([wk_ref[b], k_new], axis=0).astype(BF16)
        v_all = jnp.concatenate([wv_ref[b], v_new], axis=0).astype(BF16)
        outs = [None] * n_q
        for kvh in range(n_kv):
            qg = jnp.concatenate([x[:, (kvh * group + g) * hd:(kvh * group + g + 1) * hd] for g in range(group)],
                                 axis=0).astype(BF16)
            kh = k_all[:, kvh * hd:(kvh + 1) * hd]
            vh = v_all[:, kvh * hd:(kvh + 1) * hd]
            s = lax.dot_general(qg, kh, (((1,), (1,)), ((), ())), preferred_element_type=F32) * (hd ** -0.5)
            sink = jnp.concatenate([jnp.full((l_pad, 1), sink_ref[kvh * group + g], F32) for g in range(group)], axis=0)
            o = _softmax_sink_pv(s, valid, sink, vh)
            for g in range(group):
                outs[kvh * group + g] = o[g * l_pad:(g + 1) * l_pad, :]
        o_ref[b] = jnp.concatenate(outs, axis=1).astype(o_ref.dtype)


def _attn_sample(qkva_pad, win_k, win_v, sinks, n_q, n_kv, hd, l_new):
    n, l_pad, width = qkva_pad.shape
    window, kd = win_k.shape[1], win_k.shape[2]
    qd = n_q * hd
    bs = 8
    body = functools.partial(_attn_sample_body, bs=bs, n_q=n_q, n_kv=n_kv, hd=hd, window=window,
                             l_new=l_new, l_pad=l_pad)
    return pl.pallas_call(
        body,
        grid=(n // bs,),
        in_specs=[pl.BlockSpec(memory_space=pltpu.SMEM),
                  pl.BlockSpec((bs, l_pad, width), lambda i: (i, 0, 0)),
                  pl.BlockSpec((bs, window, kd), lambda i: (i, 0, 0)),
                  pl.BlockSpec((bs, window, kd), lambda i: (i, 0, 0))],
        out_specs=[pl.BlockSpec((bs, l_pad, qd), lambda i: (i, 0, 0)),
                   pl.BlockSpec((bs, window, kd), lambda i: (i, 0, 0)),
                   pl.BlockSpec((bs, window, kd), lambda i: (i, 0, 0))],
        out_shape=[jax.ShapeDtypeStruct((n, l_pad, qd), BF16),
                   jax.ShapeDtypeStruct((n, window, kd), F32),
                   jax.ShapeDtypeStruct((n, window, kd), F32)],
        compiler_params=_cparams("arbitrary"),
        name="attn_sample",
    )(sinks, qkva_pad, win_k, win_v)


def _hdot(a, b):
    return jnp.dot(a, b, preferred_element_type=F32, precision=HIGHEST)


def _dn_body(qkv_ref, z_ref, ba_ref, bat_ref, cs0_ref, s0_ref, wc_ref, hp_ref, nw_ref,
             o_ref, s_ref, xbuf, *, nb, chunk, heads, dk, dv, l_real, conv_w):
    c_idx = pl.program_id(1)
    hc = SUBLANES

    @pl.when(c_idx == 0)
    def _():
        xbuf[:, 0:hc, :] = cs0_ref[...]
        s_ref[...] = s0_ref[...]

    qk_dim = heads * dk
    row = lax.broadcasted_iota(I32, (chunk, chunk), 0)
    col = lax.broadcasted_iota(I32, (chunk, chunk), 1)
    incl = row >= col
    strict = row > col
    eye = (row == col).astype(F32)
    valid_c = row[:, 0:1] < l_real
    valid_r = col[0:1, :] < l_real
    n_levels = max(1, int(np.ceil(np.log2(chunk))))
    wc = wc_ref[...]
    hp = hp_ref[...]
    neg_exp_alog = -jnp.exp(hp[0:1, :])
    dt_bias = hp[1:2, :]
    nw = nw_ref[...]

    for b in range(nb):
        xbuf[b, hc:hc + chunk, :] = qkv_ref[b].astype(F32)
        y = xbuf[b, hc:hc + chunk, :] * wc[conv_w - 1:conv_w, :]
        for j in range(conv_w - 1):
            off = hc - (conv_w - 1) + j
            y = y + xbuf[b, off:off + chunk, :] * wc[j:j + 1, :]
        y = _silu(y)
        xbuf[b, 0:hc, :] = xbuf[b, chunk:chunk + hc, :]
        zt = z_ref[b].astype(F32)
        ba = ba_ref[b]
        bat = bat_ref[b]
        outs = []
        for h in range(heads):
            qh = y[:, h * dk:(h + 1) * dk]
            kh = y[:, qk_dim + h * dk:qk_dim + (h + 1) * dk]
            vh = y[:, 2 * qk_dim + h * dv:2 * qk_dim + (h + 1) * dv]
            qn = qh * lax.rsqrt(jnp.sum(qh * qh, -1, keepdims=True) + L2_EPS) * (dk ** -0.5)
            kn = kh * lax.rsqrt(jnp.sum(kh * kh, -1, keepdims=True) + L2_EPS)
            ne = neg_exp_alog[:, h:h + 1]
            db = dt_bias[:, h:h + 1]
            beta = jnp.where(valid_c, jax.nn.sigmoid(ba[:, h:h + 1]), 0.0)
            g_col = jnp.where(valid_c, ne * jax.nn.softplus(ba[:, heads + h:heads + h + 1] + db), 0.0)
            g_row = jnp.where(valid_r, ne * jax.nn.softplus(bat[heads + h:heads + h + 1, :] + db), 0.0)
            gc_col = jnp.sum(jnp.where(incl, g_row, 0.0), axis=1, keepdims=True)
            gc_row = jnp.sum(jnp.where(row <= col, g_col, 0.0), axis=0, keepdims=True)
            decay = jnp.where(incl, jnp.exp(gc_col - gc_row), 0.0)
            kb = kn * beta
            sc = lax.dot_general(jnp.concatenate([qn, kb], axis=0).astype(BF16), kn.astype(BF16),
                                 (((1,), (1,)), ((), ())), preferred_element_type=F32)
            qk = sc[:chunk] * decay
            a_mat = jnp.where(strict, sc[chunk:] * decay, 0.0)
            p = -a_mat
            t_inv = eye + p
            if n_levels > 1:
                p = _hdot(p, p)
            for lvl in range(1, n_levels):
                if lvl < n_levels - 1:
                    yp = _hdot(jnp.concatenate([t_inv, p], axis=0), p)
                    t_inv = t_inv + yp[:chunk]
                    p = yp[chunk:]
                else:
                    t_inv = t_inv + _hdot(t_inv, p)
            e_gc = jnp.exp(gc_col)
            sol = _hdot(t_inv, jnp.concatenate([vh * beta, kb * e_gc], axis=1))
            u = sol[:, :dv]
            w = sol[:, dv:]
            s_old = s_ref[b, h]
            wq = _bdot(jnp.concatenate([w, qn * e_gc], axis=0), s_old)
            v_new = u - wq[:chunk]
            o = wq[chunk:] + _bdot(qk, v_new)
            g_last = gc_col[chunk - 1:chunk, :]
            kt = kn * jnp.exp(g_last - gc_col)
            s_ref[b, h] = s_old * jnp.exp(g_last) + lax.dot_general(
                kt.astype(BF16), v_new.astype(BF16), (((0,), (0,)), ((), ())), preferred_element_type=F32)
            on = o * lax.rsqrt(jnp.mean(o * o, -1, keepdims=True) + RMS_EPS) * nw
            outs.append(on * _silu(zt[:, h * dv:(h + 1) * dv]))
        o_ref[b] = jnp.concatenate(outs, axis=1).astype(o_ref.dtype)


def _deltanet(qkv, z, ba, bat, cs0, s0, w_conv, hp, norm_w, *, chunk, l_real, nb):
    n, l, conv_dim = qkv.shape
    heads, dk, dv = s0.shape[1:]
    nc = l // chunk
    conv_w = w_conv.shape[0]
    body = functools.partial(_dn_body, nb=nb, chunk=chunk, heads=heads, dk=dk, dv=dv, l_real=l_real, conv_w=conv_w)
    return pl.pallas_call(
        body,
        grid=(n // nb, nc),
        in_specs=[pl.BlockSpec((nb, chunk, conv_dim), lambda i, c: (i, c, 0)),
                  pl.BlockSpec((nb, chunk, heads * dv), lambda i, c: (i, c, 0)),
                  pl.BlockSpec((nb, chunk, LANES), lambda i, c: (i, c, 0)),
                  pl.BlockSpec((nb, None, SUBLANES, chunk), lambda i, c: (i, c, 0, 0)),
                  pl.BlockSpec((nb, SUBLANES, conv_dim), lambda i, c: (i, 0, 0)),
                  pl.BlockSpec((nb, heads, dk, dv), lambda i, c: (i, 0, 0, 0)),
                  pl.BlockSpec((conv_w, conv_dim), lambda i, c: (0, 0)),
                  pl.BlockSpec((SUBLANES, LANES), lambda i, c: (0, 0)),
                  pl.BlockSpec((1, dv), lambda i, c: (0, 0))],
        out_specs=[pl.BlockSpec((nb, chunk, heads * dv), lambda i, c: (i, c, 0)),
                   pl.BlockSpec((nb, heads, dk, dv), lambda i, c: (i, 0, 0, 0))],
        out_shape=[jax.ShapeDtypeStruct((n, l, heads * dv), BF16),
                   jax.ShapeDtypeStruct((n, heads, dk, dv), F32)],
        scratch_shapes=[pltpu.VMEM((nb, SUBLANES + chunk, conv_dim), F32)],
        compiler_params=_cparams("arbitrary", "arbitrary"),
        name="deltanet",
    )(qkv, z, ba, bat, cs0, s0, w_conv, hp, norm_w)


def _layer_norm(r, w, b):
    mu = jnp.mean(r, -1, keepdims=True)
    var = jnp.mean(jnp.square(r - mu), -1, keepdims=True)
    return (r - mu) * lax.rsqrt(var + LN_EPS) * w + b


def _outproj_body(attn_ref, dn_ref, g_ref, x_ref, gt_ref, sh2_ref, sc2_ref, wpa_ref, wpd_ref, wo_ref,
                  lnw_ref, lnb_ref, wr_ref, br_ref, cnt0_ref, *refs, alpha, top_k, aliased):
    if aliased:
        refs = refs[4:]
    x1_ref, eidx_ref, gate_ref, rank_ref, cnt_ref, run_ref = refs
    i = pl.program_id(0)

    @pl.when(i == 0)
    def _():
        run_ref[...] = cnt0_ref[0:1, :]

    d = x_ref.shape[1]
    tm = x_ref.shape[0]
    g = g_ref[...].astype(F32)
    pa = jnp.dot(attn_ref[...], wpa_ref[...], preferred_element_type=F32)
    pd = jnp.dot(dn_ref[...], wpd_ref[...], preferred_element_type=F32)
    merged = jax.nn.sigmoid(g[:, :d]) * pa + jax.nn.sigmoid(g[:, d:]) * pd
    mix = jnp.dot(merged.astype(BF16), wo_ref[...], preferred_element_type=F32)
    x1 = _layer_norm(alpha * x_ref[...] + gt_ref[...] * mix, lnw_ref[...], lnb_ref[...])
    x1_ref[...] = x1
    h2 = x1 * (1.0 + sc2_ref[...]) + sh2_ref[...]
    logits = jnp.dot(h2, wr_ref[...], preferred_element_type=F32, precision=HIGHEST) + br_ref[...]
    lane = lax.broadcasted_iota(I32, (tm, LANES), 1)
    lane_f = lane.astype(F32)
    vals, idxs, sels = [], [], []
    l = logits
    for _ in range(top_k):
        m = jnp.max(l, axis=1, keepdims=True)
        idx = jnp.min(jnp.where(l == m, lane_f, float(LANES)), axis=1, keepdims=True)
        sel = lane_f == idx
        vals.append(m)
        idxs.append(idx)
        sels.append(sel)
        l = jnp.where(sel, -jnp.inf, l)
    ex = [jnp.exp(v - vals[0]) for v in vals]
    den = ex[0]
    for e in ex[1:]:
        den = den + e
    multi_hot = jnp.zeros((tm, LANES), F32)
    for sel in sels:
        multi_hot = multi_hot + jnp.where(sel, 1.0, 0.0)
    r_i = lax.broadcasted_iota(I32, (tm, tm), 0)
    c_i = lax.broadcasted_iota(I32, (tm, tm), 1)
    lower = jnp.where(r_i > c_i, 1.0, 0.0).astype(BF16)
    prefix = jnp.dot(lower, multi_hot.astype(BF16), preferred_element_type=F32) + run_ref[...]
    e_out = jnp.zeros((tm, LANES), F32)
    g_out = jnp.zeros((tm, LANES), F32)
    r_out = jnp.zeros((tm, LANES), F32)
    for k in range(top_k):
        rank_k = jnp.sum(jnp.where(sels[k], prefix, 0.0), axis=1, keepdims=True)
        e_out = jnp.where(lane == k, idxs[k], e_out)
        g_out = jnp.where(lane == k, ex[k] / den, g_out)
        r_out = jnp.where(lane == k, rank_k, r_out)
    eidx_ref[...] = e_out.astype(I32)
    gate_ref[...] = g_out
    rank_ref[...] = r_out.astype(I32)
    run_ref[...] = run_ref[...] + jnp.sum(multi_hot, axis=0, keepdims=True)
    cnt_ref[...] = jnp.broadcast_to(run_ref[...], cnt_ref.shape)


def _outproj(attn, dn, gates, x, gt, sh2, sc2, wts, cnt0, bufs, *, per_token_mod, tiles_per_seq, t_total, tile_off,
             alpha):
    t, d = x.shape
    tm = TOKEN_TILE
    nt = t // tm
    wpa, wpd, wo, lnw, lnb, wr, br = wts
    if per_token_mod:
        mod_spec = pl.BlockSpec((tm, d), lambda i: (i, 0))
    else:
        mod_spec = pl.BlockSpec((None, 1, d), lambda i: (i // tiles_per_seq, 0, 0))

    def row(width):
        return pl.BlockSpec((tm, width), lambda i: (i, 0))

    def full(a):
        return pl.BlockSpec(a.shape, lambda i: (0,) * a.ndim)

    aliased = bufs is not None
    in_specs = [row(attn.shape[1]), row(dn.shape[1]), row(gates.shape[1]), row(d), mod_spec, mod_spec, mod_spec,
                full(wpa), full(wpd), full(wo), full(lnw), full(lnb), full(wr), full(br), full(cnt0)]
    args = [attn, dn, gates, x, gt, sh2, sc2, wpa, wpd, wo, lnw, lnb, wr, br, cnt0]
    io_alias = {}
    if aliased:
        for k, bfr in enumerate(bufs):
            in_specs.append(pl.BlockSpec(memory_space=pl.ANY))
            io_alias[len(args)] = k
            args.append(bfr)
    out_row = lambda width: pl.BlockSpec((tm, width), lambda i: (i + tile_off, 0))
    out_shape = [jax.ShapeDtypeStruct((t_total, d), F32), jax.ShapeDtypeStruct((t_total, LANES), I32),
                 jax.ShapeDtypeStruct((t_total, LANES), F32), jax.ShapeDtypeStruct((t_total, LANES), I32),
                 jax.ShapeDtypeStruct((SUBLANES, LANES), F32)]
    out_specs = [out_row(d), out_row(LANES), out_row(LANES), out_row(LANES),
                 pl.BlockSpec((SUBLANES, LANES), lambda i: (0, 0))]
    body = functools.partial(_outproj_body, alpha=alpha, top_k=TOP_K, aliased=aliased)
    return pl.pallas_call(
        body,
        grid=(nt,),
        in_specs=in_specs,
        out_specs=out_specs,
        out_shape=out_shape,
        scratch_shapes=[pltpu.VMEM((1, LANES), F32)],
        input_output_aliases=io_alias,
        compiler_params=_cparams("arbitrary"),
        name="outproj",
    )(*args)


def _select_mod(i, n_ptiles, seq_ref, tok_ref):
    return jnp.where(i < n_ptiles, seq_ref[...], tok_ref[...])


def _dispatch_body(cnt_ref, pstart_ref, pend_ref, dest_ref, x1_ref, shs_ref, scs_ref, sht_ref, sct_ref,
                   xs_ref, hbuf, zbuf, sem, *, tm, top_k, n_tiles, n_ptiles, n_experts):
    i = pl.program_id(0)
    sc = _select_mod(i, n_ptiles, scs_ref, sct_ref)
    sh = _select_mod(i, n_ptiles, shs_ref, sht_ref)
    hbuf[...] = x1_ref[...] * (1.0 + sc) + sh

    def row_copy(t, k):
        d = dest_ref[t * top_k + k]
        return pltpu.make_async_copy(hbuf.at[pl.ds(t, 1), :], xs_ref.at[pl.ds(d, 1), :], sem)

    def issue(t, carry):
        for k in range(top_k):
            row_copy(t, k).start()
        return carry

    def drain(t, carry):
        for k in range(top_k):
            row_copy(t, k).wait()
        return carry

    lax.fori_loop(0, tm, issue, 0)
    lax.fori_loop(0, tm, drain, 0)

    @pl.when(i == n_tiles - 1)
    def _():
        zbuf[...] = jnp.zeros(zbuf.shape, zbuf.dtype)

        def per_expert(e, carry):
            def pad_row(r, c2):
                cp = pltpu.make_async_copy(zbuf.at[pl.ds(0, 1), :], xs_ref.at[pl.ds(r, 1), :], sem)
                cp.start()
                cp.wait()
                return c2
            return lax.fori_loop(pstart_ref[e] + cnt_ref[e], pend_ref[e], pad_row, carry)

        lax.fori_loop(0, n_experts, per_expert, 0)


def _dispatch(counts, pad_start, pad_end, dest_flat, x1, sh_seq, sc_seq, sh_tok, sc_tok, *, n_rows, n_ptiles,
              tiles_per_seq):
    t, d = x1.shape
    tm = TOKEN_TILE
    nt = t // tm
    n_seq = sh_seq.shape[0]
    n_experts = counts.shape[0]
    body = functools.partial(_dispatch_body, tm=tm, top_k=TOP_K, n_tiles=nt, n_ptiles=n_ptiles, n_experts=n_experts)
    seq_spec = pl.BlockSpec((None, 1, d), lambda i, *_: (jnp.minimum(i // tiles_per_seq, n_seq - 1), 0, 0))
    tok_spec = pl.BlockSpec((tm, d), lambda i, *_: (jnp.maximum(i - n_ptiles, 0), 0))
    return pl.pallas_call(
        body,
        grid_spec=pltpu.PrefetchScalarGridSpec(
            num_scalar_prefetch=3,
            grid=(nt,),
            in_specs=[pl.BlockSpec((tm * TOP_K,), lambda i, *_: (i,), memory_space=pltpu.SMEM),
                      pl.BlockSpec((tm, d), lambda i, *_: (i, 0)),
                      seq_spec, seq_spec, tok_spec, tok_spec],
            out_specs=pl.BlockSpec(memory_space=pl.ANY),
            scratch_shapes=[pltpu.VMEM((tm, d), F32), pltpu.VMEM((SUBLANES, d), F32), pltpu.SemaphoreType.DMA(())],
        ),
        out_shape=jax.ShapeDtypeStruct((n_rows, d), F32),
        compiler_params=_cparams("arbitrary"),
        name="dispatch",
    )(counts, pad_start, pad_end, dest_flat, x1, sh_seq, sc_seq, sh_tok, sc_tok)


def _expert_body(be_ref, nu_ref, xs_ref, wgu_ref, bgu_ref, wd_ref, bd_ref, y_ref, *, de):
    @pl.when(pl.program_id(0) < nu_ref[0])
    def _():
        gu = jnp.dot(xs_ref[...].astype(BF16), wgu_ref[...], preferred_element_type=F32) + bgu_ref[...]
        glu = jnp.minimum(gu[:, :de], SWIGLU_LIMIT)
        lin = jnp.clip(gu[:, de:], -SWIGLU_LIMIT, SWIGLU_LIMIT)
        act = glu * jax.nn.sigmoid(SWIGLU_ALPHA * glu) * (lin + 1.0)
        y_ref[...] = jnp.dot(act.astype(BF16), wd_ref[...], preferred_element_type=F32) + bd_ref[...]


def _experts(block_e, n_used, xs, w_gu, b_gu, w_down, b_down):
    p, d = xs.shape
    bm = EXPERT_ROWS
    n_e, _, de2 = w_gu.shape
    de = de2 // 2
    nblk = p // bm

    def blk(i, be, nu):
        return jnp.minimum(i, nu[0] - 1)

    body = functools.partial(_expert_body, de=de)
    return pl.pallas_call(
        body,
        grid_spec=pltpu.PrefetchScalarGridSpec(
            num_scalar_prefetch=2,
            grid=(nblk,),
            in_specs=[pl.BlockSpec((bm, d), lambda i, be, nu: (blk(i, be, nu), 0)),
                      pl.BlockSpec((None, d, de2), lambda i, be, nu: (be[blk(i, be, nu)], 0, 0)),
                      pl.BlockSpec((None, 1, de2), lambda i, be, nu: (be[blk(i, be, nu)], 0, 0)),
                      pl.BlockSpec((None, de, d), lambda i, be, nu: (be[blk(i, be, nu)], 0, 0)),
                      pl.BlockSpec((None, 1, d), lambda i, be, nu: (be[blk(i, be, nu)], 0, 0))],
            out_specs=pl.BlockSpec((bm, d), lambda i, be, nu: (blk(i, be, nu), 0)),
        ),
        out_shape=jax.ShapeDtypeStruct((p, d), F32),
        compiler_params=_cparams("arbitrary"),
        name="experts",
    )(block_e, n_used, xs, w_gu, b_gu.reshape(n_e, 1, de2), w_down, b_down.reshape(n_e, 1, d))


def _combine_body(dest_ref, x1_ref, gts_ref, gtt_ref, gate_ref, lnw_ref, lnb_ref, yb_ref, yp_ref, ys_ref,
                  ybuf, sem, *, tm, top_k, n_ptiles, alpha):
    i = pl.program_id(0)

    def row_copy(t, k):
        d = dest_ref[t * top_k + k]
        return pltpu.make_async_copy(yb_ref.at[pl.ds(d, 1), :], ybuf.at[k, pl.ds(t, 1), :], sem)

    def issue(t, carry):
        for k in range(top_k):
            row_copy(t, k).start()
        return carry

    def drain(t, carry):
        for k in range(top_k):
            row_copy(t, k).wait()
        return carry

    lax.fori_loop(0, tm, issue, 0)
    lax.fori_loop(0, tm, drain, 0)
    gate = gate_ref[...]
    ff = gate[:, 0:1] * ybuf[0]
    for k in range(1, top_k):
        ff = ff + gate[:, k:k + 1] * ybuf[k]
    gt = _select_mod(i, n_ptiles, gts_ref, gtt_ref)
    y = _layer_norm(alpha * x1_ref[...] + gt * ff, lnw_ref[...], lnb_ref[...])

    @pl.when(i < n_ptiles)
    def _():
        yp_ref[...] = y

    @pl.when(i >= n_ptiles)
    def _():
        ys_ref[...] = y


def _combine(dest_flat, x1, gt_seq, gt_tok, gate, lnw, lnb, yb, *, n_ptiles, tiles_per_seq, alpha):
    t, d = x1.shape
    tm = TOKEN_TILE
    nt = t // tm
    n_seq = gt_seq.shape[0]
    body = functools.partial(_combine_body, tm=tm, top_k=TOP_K, n_ptiles=n_ptiles, alpha=alpha)
    return pl.pallas_call(
        body,
        grid=(nt,),
        in_specs=[pl.BlockSpec((tm * TOP_K,), lambda i: (i,), memory_space=pltpu.SMEM),
                  pl.BlockSpec((tm, d), lambda i: (i, 0)),
                  pl.BlockSpec((None, 1, d), lambda i: (jnp.minimum(i // tiles_per_seq, n_seq - 1), 0, 0)),
                  pl.BlockSpec((tm, d), lambda i: (jnp.maximum(i - n_ptiles, 0), 0)),
                  pl.BlockSpec((tm, LANES), lambda i: (i, 0)),
                  pl.BlockSpec((1, d), lambda i: (0, 0)),
                  pl.BlockSpec((1, d), lambda i: (0, 0)),
                  pl.BlockSpec(memory_space=pl.ANY)],
        out_specs=[pl.BlockSpec((tm, d), lambda i: (jnp.minimum(i, n_ptiles - 1), 0)),
                   pl.BlockSpec((tm, d), lambda i: (jnp.maximum(i - n_ptiles, 0), 0))],
        out_shape=[jax.ShapeDtypeStruct((n_ptiles * tm, d), F32),
                   jax.ShapeDtypeStruct(((nt - n_ptiles) * tm, d), F32)],
        scratch_shapes=[pltpu.VMEM((TOP_K, tm, d), F32), pltpu.SemaphoreType.DMA(())],
        compiler_params=_cparams("arbitrary"),
        name="combine",
    )(dest_flat, x1, gt_seq, gt_tok, gate, lnw, lnb, yb)


def _rotary_tables(pos, hd, rot_dim):
    half = rot_dim // 2
    inv_freq = jnp.power(jnp.float32(ROPE_THETA), -jnp.arange(half, dtype=F32) * (2.0 / rot_dim))
    ang = pos.astype(F32)[:, None] * inv_freq[None, :]
    cos, sin = jnp.cos(ang), jnp.sin(ang)
    n = pos.shape[0]
    ones = jnp.ones((n, hd - rot_dim), F32)
    zeros = jnp.zeros((n, hd - rot_dim), F32)
    zh = jnp.zeros((n, half), F32)
    ct = jnp.concatenate([cos, cos, ones], axis=1)
    s1 = jnp.concatenate([-sin, zh, zeros], axis=1)
    s2 = jnp.concatenate([zh, sin, zeros], axis=1)
    reps = LANES // hd
    return tuple(jnp.tile(a, (1, reps)) for a in (ct, s1, s2))


def kernel(x_prompt, x_sample, state_win_k, state_win_v, state_conv, state_ssm, c_prompt, c_sample, w_ada, b_ada, w_in, attn_sinks, w_conv, dn_a_log, dn_dt_bias, dn_norm_w, w_proj_attn, w_proj_dn, w_out, ln1_w, ln1_b, w_router, b_router, w_gu, b_gu, w_down, b_down, ln2_w, ln2_b):
    n_p, seq, d = x_prompt.shape
    n_s, l_s, _ = x_sample.shape
    depth = w_ada.shape[0]
    window, n_kv, hd = state_win_k.shape[2:]
    n_q = attn_sinks.shape[1]
    heads, dk, dv = state_ssm.shape[2:]
    conv_w, conv_dim = w_conv.shape[1:]
    n_e = w_router.shape[2]
    qd, kd = n_q * hd, n_kv * hd
    vdim = heads * dv
    rot_dim = hd // 4
    alpha = float((2 * depth) ** 0.25)
    tm = TOKEN_TILE
    t_p, t_s = n_p * seq, n_s * l_s
    t_all = t_p + t_s
    tps = seq // tm
    n_ptiles = t_p // tm
    l_pad = SUBLANES
    assert seq % tm == 0 and t_s % tm == 0 and tm % l_s == 0 and l_s <= l_pad and l_s >= conv_w - 1
    assert 2 * heads <= SUBLANES and n_e <= LANES and hd * 2 == LANES and rot_dim == 2 * SUBLANES

    sizes = [qd, kd, kd, conv_dim, vdim, heads, heads, d, d]
    offs = np.concatenate([[0], np.cumsum(sizes)])
    seg = lambda k: np.arange(offs[k], offs[k + 1])
    perm = np.concatenate([seg(0), seg(1), seg(2), seg(3), seg(4), seg(7), seg(8), seg(5), seg(6)])
    cuts = (qd + 2 * kd, qd + 2 * kd + conv_dim, qd + 2 * kd + conv_dim + vdim, qd + 2 * kd + conv_dim + vdim + 2 * d)
    cuts = cuts + (cuts[-1] + LANES,)

    tabs_p = _rotary_tables(jnp.arange(seq, dtype=I32), hd, rot_dim)
    tabs_s = tuple(jnp.tile(a, (tm // l_s, 1))
                   for a in _rotary_tables(PAST_LEN + jnp.arange(l_s, dtype=I32), hd, rot_dim))

    x_p = x_prompt.reshape(t_p, d)
    x_s = x_sample.reshape(t_s, d)
    c_all = jnp.concatenate([c_prompt, c_sample], axis=0)
    outs = {k: [] for k in ("pwk", "pwv", "pcv", "pss", "swk", "swv", "scv", "sss")}

    for l in range(depth):
        w_in_l = jnp.pad(w_in[l][:, perm], ((0, 0), (0, LANES - 2 * heads))).astype(BF16)
        mod = _ada(c_all, w_ada[l], b_ada[l])
        mod_p = mod[:n_p].reshape(n_p, 6, 1, d)
        mod_s = jnp.repeat(mod[n_p:].reshape(n_s, 6, d), l_s, axis=0)
        sh1p, sc1p, gt1p, sh2p, sc2p, gt2p = [mod_p[:, k] for k in range(6)]
        sh1s, sc1s, gt1s, sh2s, sc2s, gt2s = [mod_s[:, k] for k in range(6)]

        qkva_p, dn_p, z_p, g_p, ba_p, tail_p, kvw_p = _inproj(
            x_p, sh1p, sc1p, tabs_p, w_in_l, cuts, per_token_mod=False, tiles_per_seq=tps, act_dtype=BF16,
            window=window, kv_cols=2 * kd)
        attn_p = _attn_prompt(qkva_p, attn_sinks[l], n_p, seq, n_q, n_kv, hd, window)
        chunk = min(DN_CHUNK, seq)
        nc = seq // chunk
        bat_p = ba_p[:, :SUBLANES].reshape(n_p, nc, chunk, SUBLANES).transpose(0, 1, 3, 2)
        hp = jnp.zeros((SUBLANES, LANES), F32).at[0, :heads].set(dn_a_log[l]).at[1, :heads].set(dn_dt_bias[l])
        nw = dn_norm_w[l].reshape(1, dv)
        o_p, ssm_p = _deltanet(dn_p.reshape(n_p, seq, conv_dim), z_p.reshape(n_p, seq, vdim),
                               ba_p.reshape(n_p, seq, LANES), bat_p,
                               jnp.zeros((n_p, SUBLANES, conv_dim), F32), jnp.zeros((n_p, heads, dk, dv), F32),
                               w_conv[l], hp, nw, chunk=chunk, l_real=chunk, nb=2)
        outs["pwk"].append(kvw_p[:, :, :kd].reshape(n_p, window, n_kv, hd))
        outs["pwv"].append(kvw_p[:, :, kd:].reshape(n_p, window, n_kv, hd))
        outs["pcv"].append(tail_p.reshape(n_p, tps, SUBLANES, conv_dim)[:, -1, SUBLANES - (conv_w - 1):])
        outs["pss"].append(ssm_p)

        qkva_s, dn_s, z_s, g_s, ba_s = _inproj(
            x_s, sh1s, sc1s, tabs_s, w_in_l, cuts, per_token_mod=True, tiles_per_seq=1, act_dtype=F32,
            window=window, kv_cols=2 * kd)
        pad_l = lambda a: jnp.pad(a.reshape(n_s, l_s, a.shape[-1]), ((0, 0), (0, l_pad - l_s), (0, 0)))
        attn_s, wk_s, wv_s = _attn_sample(pad_l(qkva_s), state_win_k[l].reshape(n_s, window, kd),
                                          state_win_v[l].reshape(n_s, window, kd), attn_sinks[l], n_q, n_kv, hd, l_s)
        attn_s = attn_s[:, :l_s].reshape(t_s, qd)
        ba_s3 = pad_l(ba_s)
        bat_s = ba_s3[:, :, :SUBLANES].transpose(0, 2, 1).reshape(n_s, 1, SUBLANES, l_pad)
        cs0 = jnp.pad(state_conv[l], ((0, 0), (SUBLANES - (conv_w - 1), 0), (0, 0)))
        o_s, ssm_s = _deltanet(pad_l(dn_s), pad_l(z_s), ba_s3, bat_s, cs0, state_ssm[l], w_conv[l], hp, nw,
                               chunk=l_pad, l_real=l_s, nb=8)
        o_s = o_s[:, :l_s].reshape(t_s, vdim)
        outs["swk"].append(wk_s.reshape(n_s, window, n_kv, hd))
        outs["swv"].append(wv_s.reshape(n_s, window, n_kv, hd))
        outs["scv"].append(jnp.concatenate([state_conv[l], dn_s.reshape(n_s, l_s, conv_dim)], axis=1)[:, -(conv_w - 1):])
        outs["sss"].append(ssm_s)

        wr = jnp.pad(w_router[l], ((0, 0), (0, LANES - n_e)))
        br = jnp.pad(b_router[l], (0, LANES - n_e), constant_values=NEG_BIG).reshape(1, LANES)
        wts = (w_proj_attn[l].astype(BF16), w_proj_dn[l].astype(BF16), w_out[l].astype(BF16),
               ln1_w[l].reshape(1, d), ln1_b[l].reshape(1, d), wr, br)
        res_p = _outproj(attn_p, o_p.reshape(t_p, vdim), g_p, x_p, gt1p, sh2p, sc2p, wts,
                         jnp.zeros((SUBLANES, LANES), F32), None, per_token_mod=False, tiles_per_seq=tps,
                         t_total=t_all, tile_off=0, alpha=alpha)
        x1, e_idx, gate, rank, cnt = _outproj(
            attn_s, o_s.astype(BF16), g_s.astype(BF16), x_s, gt1s, sh2s, sc2s, wts, res_p[4], res_p[:4],
            per_token_mod=True, tiles_per_seq=1, t_total=t_all, tile_off=n_ptiles, alpha=alpha)

        bm = EXPERT_ROWS
        counts = cnt[0, :n_e].astype(I32)
        padded = (counts + bm - 1) // bm * bm
        pad_end = jnp.cumsum(padded)
        pad_start = pad_end - padded
        n_rows = -(-(t_all * TOP_K + n_e * (bm - 1)) // bm) * bm
        nblk = n_rows // bm
        n_used = jnp.maximum(pad_end[-1:] // bm, 1).astype(I32)
        block_e = jnp.minimum(jnp.searchsorted(pad_end, jnp.arange(nblk, dtype=I32) * bm, side='right'),
                              n_e - 1).astype(I32)
        dest = (pad_start[e_idx[:, :TOP_K]] + rank[:, :TOP_K]).reshape(t_all * TOP_K)

        xs = _dispatch(counts, pad_start.astype(I32), pad_end.astype(I32), dest, x1, sh2p, sc2p, sh2s, sc2s,
                       n_rows=n_rows, n_ptiles=n_ptiles, tiles_per_seq=tps)
        yb = _experts(block_e, n_used, xs, w_gu[l].astype(BF16), b_gu[l], w_down[l].astype(BF16), b_down[l])
        x_p, x_s = _combine(dest, x1, gt2p, gt2s, gate, ln2_w[l].reshape(1, d), ln2_b[l].reshape(1, d), yb,
                            n_ptiles=n_ptiles, tiles_per_seq=tps, alpha=alpha)

    st = lambda k: jnp.stack(outs[k])
    return (x_p.reshape(n_p, seq, d), x_s.reshape(n_s, l_s, d), st("pwk"), st("pwv"), st("pcv"), st("pss"),
            st("swk"), st("swv"), st("scv"), st("sss"))
```

```python
import functools

import numpy as np
import jax
import jax.numpy as jnp
from jax import lax
from jax.experimental import pallas as pl
from jax.experimental.pallas import tpu as pltpu

F32 = jnp.float32
BF16 = jnp.bfloat16
I32 = jnp.int32
HIGHEST = lax.Precision.HIGHEST

PAST_LEN = 16384
ROPE_THETA = 500000.0
TOP_K = 4
SWIGLU_LIMIT = 7.0
SWIGLU_ALPHA = 1.702
DN_CHUNK = 64
LN_EPS = 1e-5
RMS_EPS = 1e-6
L2_EPS = 1e-6

LANES = 128
SUBLANES = 8
VMEM_LIMIT_BYTES = 56 * 1024 * 1024

TOKEN_TILE = 256
EXPERT_ROWS = 512
NEG_BIG = -1e30


def _cparams(*sem):
    return pltpu.CompilerParams(dimension_semantics=sem, vmem_limit_bytes=VMEM_LIMIT_BYTES)


def _silu(x):
    return x * jax.nn.sigmoid(x)


def _bdot(a, b):
    return jnp.dot(a.astype(BF16), b.astype(BF16), preferred_element_type=F32)


def _ada_body(c_ref, w_ref, b_ref, o_ref):
    o_ref[...] = _bdot(_silu(c_ref[...]), w_ref[...]) + b_ref[...]


def _ada(c_all, w_ada, b_ada):
    n, d = c_all.shape
    dout = w_ada.shape[1]
    tn = d
    return pl.pallas_call(
        _ada_body,
        grid=(dout // tn,),
        in_specs=[pl.BlockSpec((n, d), lambda j: (0, 0)),
                  pl.BlockSpec((d, tn), lambda j: (0, j)),
                  pl.BlockSpec((1, tn), lambda j: (0, j))],
        out_specs=pl.BlockSpec((n, tn), lambda j: (0, j)),
        out_shape=jax.ShapeDtypeStruct((n, dout), F32),
        compiler_params=_cparams("arbitrary"),
        name="ada",
    )(c_all, w_ada, b_ada.reshape(1, dout))


def _inproj_body(x_ref, sh_ref, sc_ref, ct_ref, s1_ref, s2_ref, w_ref,
                 a_ref, dn_ref, z_ref, g_ref, ba_ref, *win_refs, cuts, n_rot_chunks, window):
    h = (x_ref[...] * (1.0 + sc_ref[...]) + sh_ref[...]).astype(BF16)

    def mm(lo, hi):
        return jnp.dot(h, w_ref[:, lo:hi], preferred_element_type=F32)

    c_a, c_dn, c_z, c_g, c_ba = cuts
    qkv = mm(0, c_a)
    ct, s1, s2 = ct_ref[...], s1_ref[...], s2_ref[...]
    cols = []
    for c in range(n_rot_chunks):
        xc = qkv[:, c * LANES:(c + 1) * LANES]
        cols.append(xc * ct + pltpu.roll(xc, LANES - SUBLANES, 1) * s1 + pltpu.roll(xc, SUBLANES, 1) * s2)
    cols.append(qkv[:, n_rot_chunks * LANES:])
    rot = jnp.concatenate(cols, axis=1)
    a_ref[...] = rot.astype(a_ref.dtype)
    dn = mm(c_a, c_dn)
    dn_ref[...] = dn.astype(dn_ref.dtype)
    z_ref[...] = mm(c_dn, c_z).astype(z_ref.dtype)
    g_ref[...] = mm(c_z, c_g).astype(g_ref.dtype)
    ba_ref[...] = mm(c_g, c_ba)
    if win_refs:
        tail_ref, kvw_ref = win_refs
        tm = dn.shape[0]
        tail_ref[...] = dn[tm - SUBLANES:, :]
        kvw_ref[...] = rot[tm - window:, n_rot_chunks * LANES - LANES:]


def _inproj(x, sh, sc, tabs, w_perm, cuts, *, per_token_mod, tiles_per_seq, act_dtype, window, kv_cols):
    t, d = x.shape
    tm = TOKEN_TILE
    nt = t // tm
    c_a, c_dn, c_z, c_g, c_ba = cuts
    n_rot_chunks = (c_a - kv_cols // 2) // LANES
    if per_token_mod:
        mod_spec = pl.BlockSpec((tm, d), lambda i: (i, 0))
        tab_spec = pl.BlockSpec((tm, LANES), lambda i: (0, 0))
    else:
        mod_spec = pl.BlockSpec((None, 1, d), lambda i: (i // tiles_per_seq, 0, 0))
        tab_spec = pl.BlockSpec((tm, LANES), lambda i: (i % tiles_per_seq, 0))
    out_shape = [jax.ShapeDtypeStruct((t, c_a), act_dtype),
                 jax.ShapeDtypeStruct((t, c_dn - c_a), act_dtype),
                 jax.ShapeDtypeStruct((t, c_z - c_dn), act_dtype),
                 jax.ShapeDtypeStruct((t, c_g - c_z), act_dtype),
                 jax.ShapeDtypeStruct((t, c_ba - c_g), F32)]
    out_specs = [pl.BlockSpec((tm, s.shape[1]), lambda i: (i, 0)) for s in out_shape]
    with_win = not per_token_mod
    if with_win:
        n_seq = nt // tiles_per_seq
        out_shape += [jax.ShapeDtypeStruct((nt, SUBLANES, c_dn - c_a), F32),
                      jax.ShapeDtypeStruct((n_seq, window, kv_cols), F32)]
        out_specs += [pl.BlockSpec((None, SUBLANES, c_dn - c_a), lambda i: (i, 0, 0)),
                      pl.BlockSpec((None, window, kv_cols), lambda i: (i // tiles_per_seq, 0, 0))]
    body = functools.partial(_inproj_body, cuts=cuts, n_rot_chunks=n_rot_chunks, window=window)
    return pl.pallas_call(
        body,
        grid=(nt,),
        in_specs=[pl.BlockSpec((tm, d), lambda i: (i, 0)), mod_spec, mod_spec,
                  tab_spec, tab_spec, tab_spec,
                  pl.BlockSpec((d, c_ba), lambda i: (0, 0))],
        out_specs=out_specs,
        out_shape=out_shape,
        compiler_params=_cparams("arbitrary"),
        name="inproj",
    )(x, sh, sc, *tabs, w_perm)


def _softmax_sink_pv(s, valid, sink, v):
    s = jnp.where(valid, s, -jnp.inf)
    m = jnp.maximum(jnp.max(s, axis=-1, keepdims=True), sink)
    p = jnp.exp(s - m)
    denom = jnp.sum(p, axis=-1, keepdims=True) + jnp.exp(sink - m)
    return jnp.dot((p / denom).astype(BF16), v, preferred_element_type=F32)


def _attn_prompt_body(sink_ref, q_ref, kvp_ref, kvc_ref, o_ref, *, n_q, n_kv, hd, window):
    j = pl.program_id(1)
    group = n_q // n_kv
    q = q_ref[...]
    kv = jnp.concatenate([kvp_ref[...], kvc_ref[...]], axis=0)
    r = lax.broadcasted_iota(I32, (window, 2 * window), 0)
    c = lax.broadcasted_iota(I32, (window, 2 * window), 1)
    rel = window + r - c
    valid = (rel >= 0) & (rel < window) & ((c >= window) | (j > 0))
    outs = []
    for h in range(n_q):
        kvh = h // group
        qh = q[:, h * hd:(h + 1) * hd]
        kh = kv[:, kvh * hd:(kvh + 1) * hd]
        vh = kv[:, (n_kv + kvh) * hd:(n_kv + kvh + 1) * hd]
        s = lax.dot_general(qh, kh, (((1,), (1,)), ((), ())), preferred_element_type=F32) * (hd ** -0.5)
        outs.append(_softmax_sink_pv(s, valid, sink_ref[h], vh))
    o_ref[...] = jnp.concatenate(outs, axis=1).astype(o_ref.dtype)


def _attn_prompt(qkva, sinks, n_seq, seq, n_q, n_kv, hd, window):
    qd, kvd = n_q * hd, 2 * n_kv * hd
    x3 = qkva.reshape(n_seq, seq, qd + kvd)
    nb = seq // window
    kv_blk = qd // kvd
    body = functools.partial(_attn_prompt_body, n_q=n_q, n_kv=n_kv, hd=hd, window=window)
    out = pl.pallas_call(
        body,
        grid=(n_seq, nb),
        in_specs=[pl.BlockSpec(memory_space=pltpu.SMEM),
                  pl.BlockSpec((None, window, qd), lambda n, j: (n, j, 0)),
                  pl.BlockSpec((None, window, kvd), lambda n, j: (n, jnp.maximum(j - 1, 0), kv_blk)),
                  pl.BlockSpec((None, window, kvd), lambda n, j: (n, j, kv_blk))],
        out_specs=pl.BlockSpec((None, window, qd), lambda n, j: (n, j, 0)),
        out_shape=jax.ShapeDtypeStruct((n_seq, seq, qd), BF16),
        compiler_params=_cparams("arbitrary", "arbitrary"),
        name="attn_prompt",
    )(sinks, x3, x3, x3)
    return out.reshape(n_seq * seq, qd)


def _attn_sample_body(sink_ref, q_ref, wk_ref, wv_ref, o_ref, wko_ref, wvo_ref,
                      *, bs, n_q, n_kv, hd, window, l_new, l_pad):
    group = n_q // n_kv
    qd = n_q * hd
    kd = n_kv * hd
    rows = group * l_pad
    r = lax.broadcasted_iota(I32, (rows, window + l_pad), 0) % l_pad
    c = lax.broadcasted_iota(I32, (rows, window + l_pad), 1)
    rel = window + r - c
    valid = (rel >= 0) & (rel < window) & (c < window + l_new)
    for b in range(bs):
        x = q_ref[b]
        k_new = x[:, qd:qd + kd]
        v_new = x[:, qd + kd:]
        wko_ref[b, 0:window - l_new, :] = wk_ref[b, l_new:window, :]
        wko_ref[b, window - l_new:window, :] = k_new[0:l_new, :]
        wvo_ref[b, 0:window - l_new, :] = wv_ref[b, l_new:window, :]
        wvo_ref[b, window - l_new:window, :] = v_new[0:l_new, :]
        k_all = jnp.concatenate([wk_ref[b], k_new], axis=0).astype(BF16)
        v_all = jnp.concatenate([wv_ref[b], v_new], axis=0).astype(BF16)
        outs = [None] * n_q
        for kvh in range(n_kv):
            qg = jnp.concatenate([x[:, (kvh * group + g) * hd:(kvh * group + g + 1) * hd] for g in range(group)],
                                 axis=0).astype(BF16)
            kh = k_all[:, kvh * hd:(kvh + 1) * hd]
            vh = v_all[:, kvh * hd:(kvh + 1) * hd]
            s = lax.dot_general(qg, kh, (((1,), (1,)), ((), ())), preferred_element_type=F32) * (hd ** -0.5)
            sink = jnp.concatenate([jnp.full((l_pad, 1), sink_ref[kvh * group + g], F32) for g in range(group)], axis=0)
            o = _softmax_sink_pv(s, valid, sink, vh)
            for g in range(group):
                outs[kvh * group + g] = o[g * l_pad:(g + 1) * l_pad, :]
        o_ref[b] = jnp.concatenate(outs, axis=1).astype(o_ref.dtype)


def _attn_sample(qkva_pad, win_k, win_v, sinks, n_q, n_kv, hd, l_new):
    n, l_pad, width = qkva_pad.shape
    window, kd = win_k.shape[1], win_k.shape[2]
    qd = n_q * hd
    bs = 8
    body = functools.partial(_attn_sample_body, bs=bs, n_q=n_q, n_kv=n_kv, hd=hd, window=window,
                             l_new=l_new, l_pad=l_pad)
    return pl.pallas_call(
        body,
        grid=(n // bs,),
        in_specs=[pl.BlockSpec(memory_space=pltpu.SMEM),
                  pl.BlockSpec((bs, l_pad, width), lambda i: (i, 0, 0)),
                  pl.BlockSpec((bs, window, kd), lambda i: (i, 0, 0)),
                  pl.BlockSpec((bs, window, kd), lambda i: (i, 0, 0))],
        out_specs=[pl.BlockSpec((bs, l_pad, qd), lambda i: (i, 0, 0)),
                   pl.BlockSpec((bs, window, kd), lambda i: (i, 0, 0)),
                   pl.BlockSpec((bs, window, kd), lambda i: (i, 0, 0))],
        out_shape=[jax.ShapeDtypeStruct((n, l_pad, qd), BF16),
                   jax.ShapeDtypeStruct((n, window, kd), F32),
                   jax.ShapeDtypeStruct((n, window, kd), F32)],
        compiler_params=_cparams("arbitrary"),
        name="attn_sample",
    )(sinks, qkva_pad, win_k, win_v)


def _split_bf16(x):
    hi = x.astype(BF16)
    return hi, (x - hi.astype(F32)).astype(BF16)


def _tdot(a, b):
    ah, al = _split_bf16(a)
    bh, bl = _split_bf16(b)
    m = a.shape[0]
    t = jnp.dot(jnp.concatenate([ah, al], axis=0), bh, preferred_element_type=F32)
    return t[:m] + t[m:] + jnp.dot(ah, bl, preferred_element_type=F32)


def _dn_body(qkv_ref, z_ref, ba_ref, bat_ref, cs0_ref, s0_ref, wc_ref, hp_ref, nw_ref,
             o_ref, s_ref, xbuf, *, nb, chunk, heads, dk, dv, l_real, conv_w):
    c_idx = pl.program_id(1)
    hc = SUBLANES

    @pl.when(c_idx == 0)
    def _():
        xbuf[:, 0:hc, :] = cs0_ref[...]
        s_ref[...] = s0_ref[...]

    qk_dim = heads * dk
    row = lax.broadcasted_iota(I32, (chunk, chunk), 0)
    col = lax.broadcasted_iota(I32, (chunk, chunk), 1)
    incl = row >= col
    strict = row > col
    eye = (row == col).astype(F32)
    valid_c = row[:, 0:1] < l_real
    valid_r = col[0:1, :] < l_real
    n_levels = max(1, int(np.ceil(np.log2(chunk))))
    wc = wc_ref[...]
    hp = hp_ref[...]
    neg_exp_alog = -jnp.exp(hp[0:1, :])
    dt_bias = hp[1:2, :]
    nw = nw_ref[...]
    chains = [(b, h) for b in range(nb) for h in range(heads)]
    n = len(chains)

    ys = []
    for b in range(nb):
        xbuf[b, hc:hc + chunk, :] = qkv_ref[b].astype(F32)
        y = xbuf[b, hc:hc + chunk, :] * wc[conv_w - 1:conv_w, :]
        for j in range(conv_w - 1):
            off = hc - (conv_w - 1) + j
            y = y + xbuf[b, off:off + chunk, :] * wc[j:j + 1, :]
        ys.append(_silu(y))
        xbuf[b, 0:hc, :] = xbuf[b, chunk:chunk + hc, :]

    qn, kn, kb, vb, decay, e_gc, e_rest, e_last = [], [], [], [], [], [], [], []
    for b, h in chains:
        y = ys[b]
        qh = y[:, h * dk:(h + 1) * dk]
        kh = y[:, qk_dim + h * dk:qk_dim + (h + 1) * dk]
        vh = y[:, 2 * qk_dim + h * dv:2 * qk_dim + (h + 1) * dv]
        ba = ba_ref[b]
        bat = bat_ref[b]
        ne = neg_exp_alog[:, h:h + 1]
        db = dt_bias[:, h:h + 1]
        beta = jnp.where(valid_c, jax.nn.sigmoid(ba[:, h:h + 1]), 0.0)
        g_col = jnp.where(valid_c, ne * jax.nn.softplus(ba[:, heads + h:heads + h + 1] + db), 0.0)
        g_row = jnp.where(valid_r, ne * jax.nn.softplus(bat[heads + h:heads + h + 1, :] + db), 0.0)
        gc_col = jnp.sum(jnp.where(incl, g_row, 0.0), axis=1, keepdims=True)
        gc_row = jnp.sum(jnp.where(row <= col, g_col, 0.0), axis=0, keepdims=True)
        g_last = gc_col[chunk - 1:chunk, :]
        q_ = qh * lax.rsqrt(jnp.sum(qh * qh, -1, keepdims=True) + L2_EPS) * (dk ** -0.5)
        k_ = kh * lax.rsqrt(jnp.sum(kh * kh, -1, keepdims=True) + L2_EPS)
        qn.append(q_)
        kn.append(k_)
        kb.append(k_ * beta)
        vb.append(vh * beta)
        decay.append(jnp.where(incl, jnp.exp(gc_col - gc_row), 0.0))
        e_gc.append(jnp.exp(gc_col))
        e_rest.append(jnp.exp(g_last - gc_col))
        e_last.append(jnp.exp(g_last))

    sc = [lax.dot_general(jnp.concatenate([qn[i], kb[i]], axis=0).astype(BF16), kn[i].astype(BF16),
                          (((1,), (1,)), ((), ())), preferred_element_type=F32) for i in range(n)]
    qk = [sc[i][:chunk] * decay[i] for i in range(n)]
    p = [jnp.where(strict, -(sc[i][chunk:] * decay[i]), 0.0) for i in range(n)]
    t_inv = [eye + p[i] for i in range(n)]
    if n_levels > 1:
        p = [_tdot(p[i], p[i]) for i in range(n)]
    for lvl in range(1, n_levels):
        if lvl < n_levels - 1:
            yp = [_tdot(jnp.concatenate([t_inv[i], p[i]], axis=0), p[i]) for i in range(n)]
            t_inv = [t_inv[i] + yp[i][:chunk] for i in range(n)]
            p = [yp[i][chunk:] for i in range(n)]
        else:
            t_inv = [t_inv[i] + _tdot(t_inv[i], p[i]) for i in range(n)]
    sol = [_tdot(t_inv[i], jnp.concatenate([vb[i], kb[i] * e_gc[i]], axis=1)) for i in range(n)]
    s_old = [s_ref[b, h] for b, h in chains]
    wq = [_bdot(jnp.concatenate([sol[i][:, dv:], qn[i] * e_gc[i]], axis=0), s_old[i]) for i in range(n)]
    v_new = [sol[i][:, :dv] - wq[i][:chunk] for i in range(n)]
    o = [wq[i][chunk:] + _bdot(qk[i], v_new[i]) for i in range(n)]
    for i, (b, h) in enumerate(chains):
        s_ref[b, h] = s_old[i] * e_last[i] + lax.dot_general(
            (kn[i] * e_rest[i]).astype(BF16), v_new[i].astype(BF16), (((0,), (0,)), ((), ())),
            preferred_element_type=F32)
    for b in range(nb):
        zt = z_ref[b].astype(F32)
        outs = []
        for h in range(heads):
            oi = o[b * heads + h]
            on = oi * lax.rsqrt(jnp.mean(oi * oi, -1, keepdims=True) + RMS_EPS) * nw
            outs.append(on * _silu(zt[:, h * dv:(h + 1) * dv]))
        o_ref[b] = jnp.concatenate(outs, axis=1).astype(o_ref.dtype)


def _deltanet(qkv, z, ba, bat, cs0, s0, w_conv, hp, norm_w, *, chunk, l_real, nb):
    n, l, conv_dim = qkv.shape
    heads, dk, dv = s0.shape[1:]
    nc = l // chunk
    conv_w = w_conv.shape[0]
    body = functools.partial(_dn_body, nb=nb, chunk=chunk, heads=heads, dk=dk, dv=dv, l_real=l_real, conv_w=conv_w)
    return pl.pallas_call(
        body,
        grid=(n // nb, nc),
        in_specs=[pl.BlockSpec((nb, chunk, conv_dim), lambda i, c: (i, c, 0)),
                  pl.BlockSpec((nb, chunk, heads * dv), lambda i, c: (i, c, 0)),
                  pl.BlockSpec((nb, chunk, LANES), lambda i, c: (i, c, 0)),
                  pl.BlockSpec((nb, None, SUBLANES, chunk), lambda i, c: (i, c, 0, 0)),
                  pl.BlockSpec((nb, SUBLANES, conv_dim), lambda i, c: (i, 0, 0)),
                  pl.BlockSpec((nb, heads, dk, dv), lambda i, c: (i, 0, 0, 0)),
                  pl.BlockSpec((conv_w, conv_dim), lambda i, c: (0, 0)),
                  pl.BlockSpec((SUBLANES, LANES), lambda i, c: (0, 0)),
                  pl.BlockSpec((1, dv), lambda i, c: (0, 0))],
        out_specs=[pl.BlockSpec((nb, chunk, heads * dv), lambda i, c: (i, c, 0)),
                   pl.BlockSpec((nb, heads, dk, dv), lambda i, c: (i, 0, 0, 0))],
        out_shape=[jax.ShapeDtypeStruct((n, l, heads * dv), BF16),
                   jax.ShapeDtypeStruct((n, heads, dk, dv), F32)],
        scratch_shapes=[pltpu.VMEM((nb, SUBLANES + chunk, conv_dim), F32)],
        compiler_params=_cparams("arbitrary", "arbitrary"),
        name="deltanet",
    )(qkv, z, ba, bat, cs0, s0, w_conv, hp, norm_w)


def _layer_norm(r, w, b):
    mu = jnp.mean(r, -1, keepdims=True)
    var = jnp.mean(jnp.square(r - mu), -1, keepdims=True)
    return (r - mu) * lax.rsqrt(var + LN_EPS) * w + b


def _outproj_body(attn_ref, dn_ref, g_ref, x_ref, gt_ref, sh2_ref, sc2_ref, wpa_ref, wpd_ref, wo_ref,
                  lnw_ref, lnb_ref, wr_ref, br_ref, cnt0_ref, *refs, alpha, top_k, aliased):
    if aliased:
        refs = refs[4:]
    x1_ref, eidx_ref, gate_ref, rank_ref, cnt_ref, run_ref = refs
    i = pl.program_id(0)

    @pl.when(i == 0)
    def _():
        run_ref[...] = cnt0_ref[0:1, :]

    d = x_ref.shape[1]
    tm = x_ref.shape[0]
    g = g_ref[...].astype(F32)
    pa = jnp.dot(attn_ref[...], wpa_ref[...], preferred_element_type=F32)
    pd = jnp.dot(dn_ref[...], wpd_ref[...], preferred_element_type=F32)
    merged = jax.nn.sigmoid(g[:, :d]) * pa + jax.nn.sigmoid(g[:, d:]) * pd
    mix = jnp.dot(merged.astype(BF16), wo_ref[...], preferred_element_type=F32)
    x1 = _layer_norm(alpha * x_ref[...] + gt_ref[...] * mix, lnw_ref[...], lnb_ref[...])
    x1_ref[...] = x1
    h2 = x1 * (1.0 + sc2_ref[...]) + sh2_ref[...]
    logits = jnp.dot(h2, wr_ref[...], preferred_element_type=F32, precision=HIGHEST) + br_ref[...]
    lane = lax.broadcasted_iota(I32, (tm, LANES), 1)
    lane_f = lane.astype(F32)
    vals, idxs, sels = [], [], []
    l = logits
    for _ in range(top_k):
        m = jnp.max(l, axis=1, keepdims=True)
        idx = jnp.min(jnp.where(l == m, lane_f, float(LANES)), axis=1, keepdims=True)
        sel = lane_f == idx
        vals.append(m)
        idxs.append(idx)
        sels.append(sel)
        l = jnp.where(sel, -jnp.inf, l)
    ex = [jnp.exp(v - vals[0]) for v in vals]
    den = ex[0]
    for e in ex[1:]:
        den = den + e
    multi_hot = jnp.zeros((tm, LANES), F32)
    for sel in sels:
        multi_hot = multi_hot + jnp.where(sel, 1.0, 0.0)
    r_i = lax.broadcasted_iota(I32, (tm, tm), 0)
    c_i = lax.broadcasted_iota(I32, (tm, tm), 1)
    lower = jnp.where(r_i > c_i, 1.0, 0.0).astype(BF16)
    prefix = jnp.dot(lower, multi_hot.astype(BF16), preferred_element_type=F32) + run_ref[...]
    e_out = jnp.zeros((tm, LANES), F32)
    g_out = jnp.zeros((tm, LANES), F32)
    r_out = jnp.zeros((tm, LANES), F32)
    for k in range(top_k):
        rank_k = jnp.sum(jnp.where(sels[k], prefix, 0.0), axis=1, keepdims=True)
        e_out = jnp.where(lane == k, idxs[k], e_out)
        g_out = jnp.where(lane == k, ex[k] / den, g_out)
        r_out = jnp.where(lane == k, rank_k, r_out)
    eidx_ref[...] = e_out.astype(I32)
    gate_ref[...] = g_out
    rank_ref[...] = r_out.astype(I32)
    run_ref[...] = run_ref[...] + jnp.sum(multi_hot, axis=0, keepdims=True)
    cnt_ref[...] = jnp.broadcast_to(run_ref[...], cnt_ref.shape)


def _outproj(attn, dn, gates, x, gt, sh2, sc2, wts, cnt0, bufs, *, per_token_mod, tiles_per_seq, t_total, tile_off,
             alpha):
    t, d = x.shape
    tm = TOKEN_TILE
    nt = t // tm
    wpa, wpd, wo, lnw, lnb, wr, br = wts
    if per_token_mod:
        mod_spec = pl.BlockSpec((tm, d), lambda i: (i, 0))
    else:
        mod_spec = pl.BlockSpec((None, 1, d), lambda i: (i // tiles_per_seq, 0, 0))

    def row(width):
        return pl.BlockSpec((tm, width), lambda i: (i, 0))

    def full(a):
        return pl.BlockSpec(a.shape, lambda i: (0,) * a.ndim)

    aliased = bufs is not None
    in_specs = [row(attn.shape[1]), row(dn.shape[1]), row(gates.shape[1]), row(d), mod_spec, mod_spec, mod_spec,
                full(wpa), full(wpd), full(wo), full(lnw), full(lnb), full(wr), full(br), full(cnt0)]
    args = [attn, dn, gates, x, gt, sh2, sc2, wpa, wpd, wo, lnw, lnb, wr, br, cnt0]
    io_alias = {}
    if aliased:
        for k, bfr in enumerate(bufs):
            in_specs.append(pl.BlockSpec(memory_space=pl.ANY))
            io_alias[len(args)] = k
            args.append(bfr)
    out_row = lambda width: pl.BlockSpec((tm, width), lambda i: (i + tile_off, 0))
    out_shape = [jax.ShapeDtypeStruct((t_total, d), F32), jax.ShapeDtypeStruct((t_total, LANES), I32),
                 jax.ShapeDtypeStruct((t_total, LANES), F32), jax.ShapeDtypeStruct((t_total, LANES), I32),
                 jax.ShapeDtypeStruct((SUBLANES, LANES), F32)]
    out_specs = [out_row(d), out_row(LANES), out_row(LANES), out_row(LANES),
                 pl.BlockSpec((SUBLANES, LANES), lambda i: (0, 0))]
    body = functools.partial(_outproj_body, alpha=alpha, top_k=TOP_K, aliased=aliased)
    return pl.pallas_call(
        body,
        grid=(nt,),
        in_specs=in_specs,
        out_specs=out_specs,
        out_shape=out_shape,
        scratch_shapes=[pltpu.VMEM((1, LANES), F32)],
        input_output_aliases=io_alias,
        compiler_params=_cparams("arbitrary"),
        name="outproj",
    )(*args)


def _select_mod(i, n_ptiles, seq_ref, tok_ref):
    return jnp.where(i < n_ptiles, seq_ref[...], tok_ref[...])


def _dispatch_body(cnt_ref, pstart_ref, pend_ref, dest_ref, x1_ref, shs_ref, scs_ref, sht_ref, sct_ref,
                   xs_ref, hbuf, zbuf, sem, *, tm, top_k, n_tiles, n_ptiles, n_experts):
    i = pl.program_id(0)
    sc = _select_mod(i, n_ptiles, scs_ref, sct_ref)
    sh = _select_mod(i, n_ptiles, shs_ref, sht_ref)
    hbuf[...] = x1_ref[...] * (1.0 + sc) + sh

    def row_copy(t, k):
        d = dest_ref[t * top_k + k]
        return pltpu.make_async_copy(hbuf.at[pl.ds(t, 1), :], xs_ref.at[pl.ds(d, 1), :], sem)

    def issue(t, carry):
        for k in range(top_k):
            row_copy(t, k).start()
        return carry

    def drain(t, carry):
        for k in range(top_k):
            row_copy(t, k).wait()
        return carry

    lax.fori_loop(0, tm, issue, 0)
    lax.fori_loop(0, tm, drain, 0)

    @pl.when(i == n_tiles - 1)
    def _():
        zbuf[...] = jnp.zeros(zbuf.shape, zbuf.dtype)

        def per_expert(e, carry):
            def pad_row(r, c2):
                cp = pltpu.make_async_copy(zbuf.at[pl.ds(0, 1), :], xs_ref.at[pl.ds(r, 1), :], sem)
                cp.start()
                cp.wait()
                return c2
            return lax.fori_loop(pstart_ref[e] + cnt_ref[e], pend_ref[e], pad_row, carry)

        lax.fori_loop(0, n_experts, per_expert, 0)


def _dispatch(counts, pad_start, pad_end, dest_flat, x1, sh_seq, sc_seq, sh_tok, sc_tok, *, n_rows, n_ptiles,
              tiles_per_seq):
    t, d = x1.shape
    tm = TOKEN_TILE
    nt = t // tm
    n_seq = sh_seq.shape[0]
    n_experts = counts.shape[0]
    body = functools.partial(_dispatch_body, tm=tm, top_k=TOP_K, n_tiles=nt, n_ptiles=n_ptiles, n_experts=n_experts)
    seq_spec = pl.BlockSpec((None, 1, d), lambda i, *_: (jnp.minimum(i // tiles_per_seq, n_seq - 1), 0, 0))
    tok_spec = pl.BlockSpec((tm, d), lambda i, *_: (jnp.maximum(i - n_ptiles, 0), 0))
    return pl.pallas_call(
        body,
        grid_spec=pltpu.PrefetchScalarGridSpec(
            num_scalar_prefetch=3,
            grid=(nt,),
            in_specs=[pl.BlockSpec((tm * TOP_K,), lambda i, *_: (i,), memory_space=pltpu.SMEM),
                      pl.BlockSpec((tm, d), lambda i, *_: (i, 0)),
                      seq_spec, seq_spec, tok_spec, tok_spec],
            out_specs=pl.BlockSpec(memory_space=pl.ANY),
            scratch_shapes=[pltpu.VMEM((tm, d), F32), pltpu.VMEM((SUBLANES, d), F32), pltpu.SemaphoreType.DMA(())],
        ),
        out_shape=jax.ShapeDtypeStruct((n_rows, d), F32),
        compiler_params=_cparams("arbitrary"),
        name="dispatch",
    )(counts, pad_start, pad_end, dest_flat, x1, sh_seq, sc_seq, sh_tok, sc_tok)


def _expert_body(be_ref, nu_ref, xs_ref, wgu_ref, bgu_ref, wd_ref, bd_ref, y_ref, *, de):
    @pl.when(pl.program_id(0) < nu_ref[0])
    def _():
        gu = jnp.dot(xs_ref[...].astype(BF16), wgu_ref[...], preferred_element_type=F32) + bgu_ref[...]
        glu = jnp.minimum(gu[:, :de], SWIGLU_LIMIT)
        lin = jnp.clip(gu[:, de:], -SWIGLU_LIMIT, SWIGLU_LIMIT)
        act = glu * jax.nn.sigmoid(SWIGLU_ALPHA * glu) * (lin + 1.0)
        y_ref[...] = jnp.dot(act.astype(BF16), wd_ref[...], preferred_element_type=F32) + bd_ref[...]


def _experts(block_e, n_used, xs, w_gu, b_gu, w_down, b_down):
    p, d = xs.shape
    bm = EXPERT_ROWS
    n_e, _, de2 = w_gu.shape
    de = de2 // 2
    nblk = p // bm

    def blk(i, be, nu):
        return jnp.minimum(i, nu[0] - 1)

    body = functools.partial(_expert_body, de=de)
    return pl.pallas_call(
        body,
        grid_spec=pltpu.PrefetchScalarGridSpec(
            num_scalar_prefetch=2,
            grid=(nblk,),
            in_specs=[pl.BlockSpec((bm, d), lambda i, be, nu: (blk(i, be, nu), 0)),
                      pl.BlockSpec((None, d, de2), lambda i, be, nu: (be[blk(i, be, nu)], 0, 0)),
                      pl.BlockSpec((None, 1, de2), lambda i, be, nu: (be[blk(i, be, nu)], 0, 0)),
                      pl.BlockSpec((None, de, d), lambda i, be, nu: (be[blk(i, be, nu)], 0, 0)),
                      pl.BlockSpec((None, 1, d), lambda i, be, nu: (be[blk(i, be, nu)], 0, 0))],
            out_specs=pl.BlockSpec((bm, d), lambda i, be, nu: (blk(i, be, nu), 0)),
        ),
        out_shape=jax.ShapeDtypeStruct((p, d), F32),
        compiler_params=_cparams("arbitrary"),
        name="experts",
    )(block_e, n_used, xs, w_gu, b_gu.reshape(n_e, 1, de2), w_down, b_down.reshape(n_e, 1, d))


def _combine_body(dest_ref, x1_ref, gts_ref, gtt_ref, gate_ref, lnw_ref, lnb_ref, yb_ref, yp_ref, ys_ref,
                  ybuf, sem, *, tm, top_k, n_ptiles, alpha):
    i = pl.program_id(0)

    def row_copy(t, k):
        d = dest_ref[t * top_k + k]
        return pltpu.make_async_copy(yb_ref.at[pl.ds(d, 1), :], ybuf.at[k, pl.ds(t, 1), :], sem)

    def issue(t, carry):
        for k in range(top_k):
            row_copy(t, k).start()
        return carry

    def drain(t, carry):
        for k in range(top_k):
            row_copy(t, k).wait()
        return carry

    lax.fori_loop(0, tm, issue, 0)
    lax.fori_loop(0, tm, drain, 0)
    gate = gate_ref[...]
    ff = gate[:, 0:1] * ybuf[0]
    for k in range(1, top_k):
        ff = ff + gate[:, k:k + 1] * ybuf[k]
    gt = _select_mod(i, n_ptiles, gts_ref, gtt_ref)
    y = _layer_norm(alpha * x1_ref[...] + gt * ff, lnw_ref[...], lnb_ref[...])

    @pl.when(i < n_ptiles)
    def _():
        yp_ref[...] = y

    @pl.when(i >= n_ptiles)
    def _():
        ys_ref[...] = y


def _combine(dest_flat, x1, gt_seq, gt_tok, gate, lnw, lnb, yb, *, n_ptiles, tiles_per_seq, alpha):
    t, d = x1.shape
    tm = TOKEN_TILE
    nt = t // tm
    n_seq = gt_seq.shape[0]
    body = functools.partial(_combine_body, tm=tm, top_k=TOP_K, n_ptiles=n_ptiles, alpha=alpha)
    return pl.pallas_call(
        body,
        grid=(nt,),
        in_specs=[pl.BlockSpec((tm * TOP_K,), lambda i: (i,), memory_space=pltpu.SMEM),
                  pl.BlockSpec((tm, d), lambda i: (i, 0)),
                  pl.BlockSpec((None, 1, d), lambda i: (jnp.minimum(i // tiles_per_seq, n_seq - 1), 0, 0)),
                  pl.BlockSpec((tm, d), lambda i: (jnp.maximum(i - n_ptiles, 0), 0)),
                  pl.BlockSpec((tm, LANES), lambda i: (i, 0)),
                  pl.BlockSpec((1, d), lambda i: (0, 0)),
                  pl.BlockSpec((1, d), lambda i: (0, 0)),
                  pl.BlockSpec(memory_space=pl.ANY)],
        out_specs=[pl.BlockSpec((tm, d), lambda i: (jnp.minimum(i, n_ptiles - 1), 0)),
                   pl.BlockSpec((tm, d), lambda i: (jnp.maximum(i - n_ptiles, 0), 0))],
        out_shape=[jax.ShapeDtypeStruct((n_ptiles * tm, d), F32),
                   jax.ShapeDtypeStruct(((nt - n_ptiles) * tm, d), F32)],
        scratch_shapes=[pltpu.VMEM((TOP_K, tm, d), F32), pltpu.SemaphoreType.DMA(())],
        compiler_params=_cparams("arbitrary"),
        name="combine",
    )(dest_flat, x1, gt_seq, gt_tok, gate, lnw, lnb, yb)


def _rotary_tables(pos, hd, rot_dim):
    half = rot_dim // 2
    inv_freq = jnp.power(jnp.float32(ROPE_THETA), -jnp.arange(half, dtype=F32) * (2.0 / rot_dim))
    ang = pos.astype(F32)[:, None] * inv_freq[None, :]
    cos, sin = jnp.cos(ang), jnp.sin(ang)
    n = pos.shape[0]
    ones = jnp.ones((n, hd - rot_dim), F32)
    zeros = jnp.zeros((n, hd - rot_dim), F32)
    zh = jnp.zeros((n, half), F32)
    ct = jnp.concatenate([cos, cos, ones], axis=1)
    s1 = jnp.concatenate([-sin, zh, zeros], axis=1)
    s2 = jnp.concatenate([zh, sin, zeros], axis=1)
    reps = LANES // hd
    return tuple(jnp.tile(a, (1, reps)) for a in (ct, s1, s2))


def kernel(x_prompt, x_sample, state_win_k, state_win_v, state_conv, state_ssm, c_prompt, c_sample, w_ada, b_ada, w_in, attn_sinks, w_conv, dn_a_log, dn_dt_bias, dn_norm_w, w_proj_attn, w_proj_dn, w_out, ln1_w, ln1_b, w_router, b_router, w_gu, b_gu, w_down, b_down, ln2_w, ln2_b):
    n_p, seq, d = x_prompt.shape
    n_s, l_s, _ = x_sample.shape
    depth = w_ada.shape[0]
    window, n_kv, hd = state_win_k.shape[2:]
    n_q = attn_sinks.shape[1]
    heads, dk, dv = state_ssm.shape[2:]
    conv_w, conv_dim = w_conv.shape[1:]
    n_e = w_router.shape[2]
    qd, kd = n_q * hd, n_kv * hd
    vdim = heads * dv
    rot_dim = hd // 4
    alpha = float((2 * depth) ** 0.25)
    tm = TOKEN_TILE
    t_p, t_s = n_p * seq, n_s * l_s
    t_all = t_p + t_s
    tps = seq // tm
    n_ptiles = t_p // tm
    l_pad = SUBLANES
    assert seq % tm == 0 and t_s % tm == 0 and tm % l_s == 0 and l_s <= l_pad and l_s >= conv_w - 1
    assert 2 * heads <= SUBLANES and n_e <= LANES and hd * 2 == LANES and rot_dim == 2 * SUBLANES

    sizes = [qd, kd, kd, conv_dim, vdim, heads, heads, d, d]
    offs = np.concatenate([[0], np.cumsum(sizes)])
    seg = lambda k: np.arange(offs[k], offs[k + 1])
    perm = np.concatenate([seg(0), seg(1), seg(2), seg(3), seg(4), seg(7), seg(8), seg(5), seg(6)])
    cuts = (qd + 2 * kd, qd + 2 * kd + conv_dim, qd + 2 * kd + conv_dim + vdim, qd + 2 * kd + conv_dim + vdim + 2 * d)
    cuts = cuts + (cuts[-1] + LANES,)

    tabs_p = _rotary_tables(jnp.arange(seq, dtype=I32), hd, rot_dim)
    tabs_s = tuple(jnp.tile(a, (tm // l_s, 1))
                   for a in _rotary_tables(PAST_LEN + jnp.arange(l_s, dtype=I32), hd, rot_dim))

    x_p = x_prompt.reshape(t_p, d)
    x_s = x_sample.reshape(t_s, d)
    c_all = jnp.concatenate([c_prompt, c_sample], axis=0)
    outs = {k: [] for k in ("pwk", "pwv", "pcv", "pss", "swk", "swv", "scv", "sss")}

    for l in range(depth):
        w_in_l = jnp.pad(w_in[l][:, perm], ((0, 0), (0, LANES - 2 * heads))).astype(BF16)
        mod = _ada(c_all, w_ada[l], b_ada[l])
        mod_p = mod[:n_p].reshape(n_p, 6, 1, d)
        mod_s = jnp.repeat(mod[n_p:].reshape(n_s, 6, d), l_s, axis=0)
        sh1p, sc1p, gt1p, sh2p, sc2p, gt2p = [mod_p[:, k] for k in range(6)]
        sh1s, sc1s, gt1s, sh2s, sc2s, gt2s = [mod_s[:, k] for k in range(6)]

        qkva_p, dn_p, z_p, g_p, ba_p, tail_p, kvw_p = _inproj(
            x_p, sh1p, sc1p, tabs_p, w_in_l, cuts, per_token_mod=False, tiles_per_seq=tps, act_dtype=BF16,
            window=window, kv_cols=2 * kd)
        attn_p = _attn_prompt(qkva_p, attn_sinks[l], n_p, seq, n_q, n_kv, hd, window)
        chunk = min(DN_CHUNK, seq)
        nc = seq // chunk
        bat_p = ba_p[:, :SUBLANES].reshape(n_p, nc, chunk, SUBLANES).transpose(0, 1, 3, 2)
        hp = jnp.zeros((SUBLANES, LANES), F32).at[0, :heads].set(dn_a_log[l]).at[1, :heads].set(dn_dt_bias[l])
        nw = dn_norm_w[l].reshape(1, dv)
        o_p, ssm_p = _deltanet(dn_p.reshape(n_p, seq, conv_dim), z_p.reshape(n_p, seq, vdim),
                               ba_p.reshape(n_p, seq, LANES), bat_p,
                               jnp.zeros((n_p, SUBLANES, conv_dim), F32), jnp.zeros((n_p, heads, dk, dv), F32),
                               w_conv[l], hp, nw, chunk=chunk, l_real=chunk, nb=4)
        outs["pwk"].append(kvw_p[:, :, :kd].reshape(n_p, window, n_kv, hd))
        outs["pwv"].append(kvw_p[:, :, kd:].reshape(n_p, window, n_kv, hd))
        outs["pcv"].append(tail_p.reshape(n_p, tps, SUBLANES, conv_dim)[:, -1, SUBLANES - (conv_w - 1):])
        outs["pss"].append(ssm_p)

        qkva_s, dn_s, z_s, g_s, ba_s = _inproj(
            x_s, sh1s, sc1s, tabs_s, w_in_l, cuts, per_token_mod=True, tiles_per_seq=1, act_dtype=F32,
            window=window, kv_cols=2 * kd)
        pad_l = lambda a: jnp.pad(a.reshape(n_s, l_s, a.shape[-1]), ((0, 0), (0, l_pad - l_s), (0, 0)))
        attn_s, wk_s, wv_s = _attn_sample(pad_l(qkva_s), state_win_k[l].reshape(n_s, window, kd),
                                          state_win_v[l].reshape(n_s, window, kd), attn_sinks[l], n_q, n_kv, hd, l_s)
        attn_s = attn_s[:, :l_s].reshape(t_s, qd)
        ba_s3 = pad_l(ba_s)
        bat_s = ba_s3[:, :, :SUBLANES].transpose(0, 2, 1).reshape(n_s, 1, SUBLANES, l_pad)
        cs0 = jnp.pad(state_conv[l], ((0, 0), (SUBLANES - (conv_w - 1), 0), (0, 0)))
        o_s, ssm_s = _deltanet(pad_l(dn_s), pad_l(z_s), ba_s3, bat_s, cs0, state_ssm[l], w_conv[l], hp, nw,
                               chunk=l_pad, l_real=l_s, nb=8)
        o_s = o_s[:, :l_s].reshape(t_s, vdim)
        outs["swk"].append(wk_s.reshape(n_s, window, n_kv, hd))
        outs["swv"].append(wv_s.reshape(n_s, window, n_kv, hd))
        outs["scv"].append(jnp.concatenate([state_conv[l], dn_s.reshape(n_s, l_s, conv_dim)], axis=1)[:, -(conv_w - 1):])
        outs["sss"].append(ssm_s)

        wr = jnp.pad(w_router[l], ((0, 0), (0, LANES - n_e)))
        br = jnp.pad(b_router[l], (0, LANES - n_e), constant_values=NEG_BIG).reshape(1, LANES)
        wts = (w_proj_attn[l].astype(BF16), w_proj_dn[l].astype(BF16), w_out[l].astype(BF16),
               ln1_w[l].reshape(1, d), ln1_b[l].reshape(1, d), wr, br)
        res_p = _outproj(attn_p, o_p.reshape(t_p, vdim), g_p, x_p, gt1p, sh2p, sc2p, wts,
                         jnp.zeros((SUBLANES, LANES), F32), None, per_token_mod=False, tiles_per_seq=tps,
                         t_total=t_all, tile_off=0, alpha=alpha)
        x1, e_idx, gate, rank, cnt = _outproj(
            attn_s, o_s.astype(BF16), g_s.astype(BF16), x_s, gt1s, sh2s, sc2s, wts, res_p[4], res_p[:4],
            per_token_mod=True, tiles_per_seq=1, t_total=t_all, tile_off=n_ptiles, alpha=alpha)

        bm = EXPERT_ROWS
        counts = cnt[0, :n_e].astype(I32)
        padded = (counts + bm - 1) // bm * bm
        pad_end = jnp.cumsum(padded)
        pad_start = pad_end - padded
        n_rows = -(-(t_all * TOP_K + n_e * (bm - 1)) // bm) * bm
        nblk = n_rows // bm
        n_used = jnp.maximum(pad_end[-1:] // bm, 1).astype(I32)
        block_e = jnp.minimum(jnp.searchsorted(pad_end, jnp.arange(nblk, dtype=I32) * bm, side='right'),
                              n_e - 1).astype(I32)
        dest = (pad_start[e_idx[:, :TOP_K]] + rank[:, :TOP_K]).reshape(t_all * TOP_K)

        xs = _dispatch(counts, pad_start.astype(I32), pad_end.astype(I32), dest, x1, sh2p, sc2p, sh2s, sc2s,
                       n_rows=n_rows, n_ptiles=n_ptiles, tiles_per_seq=tps)
        yb = _experts(block_e, n_used, xs, w_gu[l].astype(BF16), b_gu[l], w_down[l].astype(BF16), b_down[l])
        x_p, x_s = _combine(dest, x1, gt2p, gt2s, gate, ln2_w[l].reshape(1, d), ln2_b[l].reshape(1, d), yb,
                            n_ptiles=n_ptiles, tiles_per_seq=tps, alpha=alpha)

    st = lambda k: jnp.stack(outs[k])
    return (x_p.reshape(n_p, seq, d), x_s.reshape(n_s, l_s, d), st("pwk"), st("pwv"), st("pcv"), st("pss"),
            st("swk"), st("swv"), st("scv"), st("sss"))
```

```python
import functools

import numpy as np
import jax
import jax.numpy as jnp
from jax import lax
from jax.experimental import pallas as pl
from jax.experimental.pallas import tpu as pltpu

F32 = jnp.float32
BF16 = jnp.bfloat16
I32 = jnp.int32
HIGHEST = lax.Precision.HIGHEST

PAST_LEN = 16384
ROPE_THETA = 500000.0
TOP_K = 4
SWIGLU_LIMIT = 7.0
SWIGLU_ALPHA = 1.702
DN_CHUNK = 64
LN_EPS = 1e-5
RMS_EPS = 1e-6
L2_EPS = 1e-6

LANES = 128
SUBLANES = 8
VMEM_LIMIT_BYTES = 56 * 1024 * 1024

TOKEN_TILE = 256
EXPERT_ROWS = 512
RUN_ALIGN = SUBLANES
RUN_BITS = (TOKEN_TILE // RUN_ALIGN).bit_length()
TAIL_BITS = (EXPERT_ROWS // RUN_ALIGN - 1).bit_length()
N_ROUTE_BUFS = 5
NEG_BIG = -1e30


def _cparams(*sem):
    return pltpu.CompilerParams(dimension_semantics=sem, vmem_limit_bytes=VMEM_LIMIT_BYTES)


def _silu(x):
    return x * jax.nn.sigmoid(x)


def _bdot(a, b):
    return jnp.dot(a.astype(BF16), b.astype(BF16), preferred_element_type=F32)


def _ada_body(c_ref, w_ref, b_ref, o_ref):
    o_ref[...] = _bdot(_silu(c_ref[...]), w_ref[...]) + b_ref[...]


def _ada(c_all, w_ada, b_ada):
    n, d = c_all.shape
    dout = w_ada.shape[1]
    tn = d
    return pl.pallas_call(
        _ada_body,
        grid=(dout // tn,),
        in_specs=[pl.BlockSpec((n, d), lambda j: (0, 0)),
                  pl.BlockSpec((d, tn), lambda j: (0, j)),
                  pl.BlockSpec((1, tn), lambda j: (0, j))],
        out_specs=pl.BlockSpec((n, tn), lambda j: (0, j)),
        out_shape=jax.ShapeDtypeStruct((n, dout), F32),
        compiler_params=_cparams("arbitrary"),
        name="ada",
    )(c_all, w_ada, b_ada.reshape(1, dout))


def _inproj_body(x_ref, sh_ref, sc_ref, ct_ref, s1_ref, s2_ref, w_ref,
                 a_ref, dn_ref, z_ref, g_ref, ba_ref, *win_refs, cuts, n_rot_chunks, window):
    h = (x_ref[...] * (1.0 + sc_ref[...]) + sh_ref[...]).astype(BF16)

    def mm(lo, hi):
        return jnp.dot(h, w_ref[:, lo:hi], preferred_element_type=F32)

    c_a, c_dn, c_z, c_g, c_ba = cuts
    qkv = mm(0, c_a)
    ct, s1, s2 = ct_ref[...], s1_ref[...], s2_ref[...]
    cols = []
    for c in range(n_rot_chunks):
        xc = qkv[:, c * LANES:(c + 1) * LANES]
        cols.append(xc * ct + pltpu.roll(xc, LANES - SUBLANES, 1) * s1 + pltpu.roll(xc, SUBLANES, 1) * s2)
    cols.append(qkv[:, n_rot_chunks * LANES:])
    rot = jnp.concatenate(cols, axis=1)
    a_ref[...] = rot.astype(a_ref.dtype)
    dn = mm(c_a, c_dn)
    dn_ref[...] = dn.astype(dn_ref.dtype)
    z_ref[...] = mm(c_dn, c_z).astype(z_ref.dtype)
    g_ref[...] = mm(c_z, c_g).astype(g_ref.dtype)
    ba_ref[...] = mm(c_g, c_ba)
    if win_refs:
        tail_ref, kvw_ref = win_refs
        tm = dn.shape[0]
        tail_ref[...] = dn[tm - SUBLANES:, :]
        kvw_ref[...] = rot[tm - window:, n_rot_chunks * LANES - LANES:]


def _inproj(x, sh, sc, tabs, w_perm, cuts, *, per_token_mod, tiles_per_seq, act_dtype, window, kv_cols):
    t, d = x.shape
    tm = TOKEN_TILE
    nt = t // tm
    c_a, c_dn, c_z, c_g, c_ba = cuts
    n_rot_chunks = (c_a - kv_cols // 2) // LANES
    if per_token_mod:
        mod_spec = pl.BlockSpec((tm, d), lambda i: (i, 0))
        tab_spec = pl.BlockSpec((tm, LANES), lambda i: (0, 0))
    else:
        mod_spec = pl.BlockSpec((None, 1, d), lambda i: (i // tiles_per_seq, 0, 0))
        tab_spec = pl.BlockSpec((tm, LANES), lambda i: (i % tiles_per_seq, 0))
    out_shape = [jax.ShapeDtypeStruct((t, c_a), act_dtype),
                 jax.ShapeDtypeStruct((t, c_dn - c_a), act_dtype),
                 jax.ShapeDtypeStruct((t, c_z - c_dn), act_dtype),
                 jax.ShapeDtypeStruct((t, c_g - c_z), act_dtype),
                 jax.ShapeDtypeStruct((t, c_ba - c_g), F32)]
    out_specs = [pl.BlockSpec((tm, s.shape[1]), lambda i: (i, 0)) for s in out_shape]
    with_win = not per_token_mod
    if with_win:
        n_seq = nt // tiles_per_seq
        out_shape += [jax.ShapeDtypeStruct((nt, SUBLANES, c_dn - c_a), F32),
                      jax.ShapeDtypeStruct((n_seq, window, kv_cols), F32)]
        out_specs += [pl.BlockSpec((None, SUBLANES, c_dn - c_a), lambda i: (i, 0, 0)),
                      pl.BlockSpec((None, window, kv_cols), lambda i: (i // tiles_per_seq, 0, 0))]
    body = functools.partial(_inproj_body, cuts=cuts, n_rot_chunks=n_rot_chunks, window=window)
    return pl.pallas_call(
        body,
        grid=(nt,),
        in_specs=[pl.BlockSpec((tm, d), lambda i: (i, 0)), mod_spec, mod_spec,
                  tab_spec, tab_spec, tab_spec,
                  pl.BlockSpec((d, c_ba), lambda i: (0, 0))],
        out_specs=out_specs,
        out_shape=out_shape,
        compiler_params=_cparams("arbitrary"),
        name="inproj",
    )(x, sh, sc, *tabs, w_perm)


def _softmax_sink_pv(s, valid, sink, v):
    s = jnp.where(valid, s, -jnp.inf)
    m = jnp.maximum(jnp.max(s, axis=-1, keepdims=True), sink)
    p = jnp.exp(s - m)
    denom = jnp.sum(p, axis=-1, keepdims=True) + jnp.exp(sink - m)
    return jnp.dot((p / denom).astype(BF16), v, preferred_element_type=F32)


def _attn_prompt_body(sink_ref, q_ref, kvp_ref, kvc_ref, o_ref, *, n_q, n_kv, hd, window):
    j = pl.program_id(1)
    group = n_q // n_kv
    q = q_ref[...]
    kv = jnp.concatenate([kvp_ref[...], kvc_ref[...]], axis=0)
    r = lax.broadcasted_iota(I32, (window, 2 * window), 0)
    c = lax.broadcasted_iota(I32, (window, 2 * window), 1)
    rel = window + r - c
    valid = (rel >= 0) & (rel < window) & ((c >= window) | (j > 0))
    outs = []
    for h in range(n_q):
        kvh = h // group
        qh = q[:, h * hd:(h + 1) * hd]
        kh = kv[:, kvh * hd:(kvh + 1) * hd]
        vh = kv[:, (n_kv + kvh) * hd:(n_kv + kvh + 1) * hd]
        s = lax.dot_general(qh, kh, (((1,), (1,)), ((), ())), preferred_element_type=F32) * (hd ** -0.5)
        outs.append(_softmax_sink_pv(s, valid, sink_ref[h], vh))
    o_ref[...] = jnp.concatenate(outs, axis=1).astype(o_ref.dtype)


def _attn_prompt(qkva, sinks, n_seq, seq, n_q, n_kv, hd, window):
    qd, kvd = n_q * hd, 2 * n_kv * hd
    x3 = qkva.reshape(n_seq, seq, qd + kvd)
    nb = seq // window
    kv_blk = qd // kvd
    body = functools.partial(_attn_prompt_body, n_q=n_q, n_kv=n_kv, hd=hd, window=window)
    out = pl.pallas_call(
        body,
        grid=(n_seq, nb),
        in_specs=[pl.BlockSpec(memory_space=pltpu.SMEM),
                  pl.BlockSpec((None, window, qd), lambda n, j: (n, j, 0)),
                  pl.BlockSpec((None, window, kvd), lambda n, j: (n, jnp.maximum(j - 1, 0), kv_blk)),
                  pl.BlockSpec((None, window, kvd), lambda n, j: (n, j, kv_blk))],
        out_specs=pl.BlockSpec((None, window, qd), lambda n, j: (n, j, 0)),
        out_shape=jax.ShapeDtypeStruct((n_seq, seq, qd), BF16),
        compiler_params=_cparams("arbitrary", "arbitrary"),
        name="attn_prompt",
    )(sinks, x3, x3, x3)
    return out.reshape(n_seq * seq, qd)


def _attn_sample_body(sink_ref, q_ref, wk_ref, wv_ref, o_ref, wko_ref, wvo_ref,
                      *, bs, n_q, n_kv, hd, window, l_new, l_pad):
    group = n_q // n_kv
    qd = n_q * hd
    kd = n_kv * hd
    rows = group * l_pad
    r = lax.broadcasted_iota(I32, (rows, window + l_pad), 0) % l_pad
    c = lax.broadcasted_iota(I32, (rows, window + l_pad), 1)
    rel = window + r - c
    valid = (rel >= 0) & (rel < window) & (c < window + l_new)
    for b in range(bs):
        x = q_ref[b]
        k_new = x[:, qd:qd + kd]
        v_new = x[:, qd + kd:]
        wko_ref[b, 0:window - l_new, :] = wk_ref[b, l_new:window, :]
        wko_ref[b, window - l_new:window, :] = k_new[0:l_new, :]
        wvo_ref[b, 0:window - l_new, :] = wv_ref[b, l_new:window, :]
        wvo_ref[b, window - l_new:window, :] = v_new[0:l_new, :]
        k_all = jnp.concatenate([wk_ref[b], k_new], axis=0).astype(BF16)
        v_all = jnp.concatenate([wv_ref[b], v_new], axis=0).astype(BF16)
        outs = [None] * n_q
        for kvh in range(n_kv):
            qg = jnp.concatenate([x[:, (kvh * group + g) * hd:(kvh * group + g + 1) * hd] for g in range(group)],
                                 axis=0).astype(BF16)
            kh = k_all[:, kvh * hd:(kvh + 1) * hd]
            vh = v_all[:, kvh * hd:(kvh + 1) * hd]
            s = lax.dot_general(qg, kh, (((1,), (1,)), ((), ())), preferred_element_type=F32) * (hd ** -0.5)
            sink = jnp.concatenate([jnp.full((l_pad, 1), sink_ref[kvh * group + g], F32) for g in range(group)], axis=0)
            o = _softmax_sink_pv(s, valid, sink, vh)
            for g in range(group):
                outs[kvh * group + g] = o[g * l_pad:(g + 1) * l_pad, :]
        o_ref[b] = jnp.concatenate(outs, axis=1).astype(o_ref.dtype)


def _attn_sample(qkva_pad, win_k, win_v, sinks, n_q, n_kv, hd, l_new):
    n, l_pad, width = qkva_pad.shape
    window, kd = win_k.shape[1], win_k.shape[2]
    qd = n_q * hd
    bs = 8
    body = functools.partial(_attn_sample_body, bs=bs, n_q=n_q, n_kv=n_kv, hd=hd, window=window,
                             l_new=l_new, l_pad=l_pad)
    return pl.pallas_call(
        body,
        grid=(n // bs,),
        in_specs=[pl.BlockSpec(memory_space=pltpu.SMEM),
                  pl.BlockSpec((bs, l_pad, width), lambda i: (i, 0, 0)),
                  pl.BlockSpec((bs, window, kd), lambda i: (i, 0, 0)),
                  pl.BlockSpec((bs, window, kd), lambda i: (i, 0, 0))],
        out_specs=[pl.BlockSpec((bs, l_pad, qd), lambda i: (i, 0, 0)),
                   pl.BlockSpec((bs, window, kd), lambda i: (i, 0, 0)),
                   pl.BlockSpec((bs, window, kd), lambda i: (i, 0, 0))],
        out_shape=[jax.ShapeDtypeStruct((n, l_pad, qd), BF16),
                   jax.ShapeDtypeStruct((n, window, kd), F32),
                   jax.ShapeDtypeStruct((n, window, kd), F32)],
        compiler_params=_cparams("arbitrary"),
        name="attn_sample",
    )(sinks, qkva_pad, win_k, win_v)


def _split_bf16(x):
    hi = x.astype(BF16)
    return hi, (x - hi.astype(F32)).astype(BF16)


def _tdot(a, b):
    ah, al = _split_bf16(a)
    bh, bl = _split_bf16(b)
    m = a.shape[0]
    t = jnp.dot(jnp.concatenate([ah, al], axis=0), bh, preferred_element_type=F32)
    return t[:m] + t[m:] + jnp.dot(ah, bl, preferred_element_type=F32)


def _dn_body(qkv_ref, z_ref, ba_ref, bat_ref, cs0_ref, s0_ref, wc_ref, hp_ref, nw_ref,
             o_ref, s_ref, xbuf, *, nb, chunk, heads, dk, dv, l_real, conv_w):
    c_idx = pl.program_id(1)
    hc = SUBLANES

    @pl.when(c_idx == 0)
    def _():
        xbuf[:, 0:hc, :] = cs0_ref[...]
        s_ref[...] = s0_ref[...]

    qk_dim = heads * dk
    row = lax.broadcasted_iota(I32, (chunk, chunk), 0)
    col = lax.broadcasted_iota(I32, (chunk, chunk), 1)
    incl = row >= col
    strict = row > col
    eye = (row == col).astype(F32)
    valid_c = row[:, 0:1] < l_real
    valid_r = col[0:1, :] < l_real
    n_levels = max(1, int(np.ceil(np.log2(chunk))))
    wc = wc_ref[...]
    hp = hp_ref[...]
    neg_exp_alog = -jnp.exp(hp[0:1, :])
    dt_bias = hp[1:2, :]
    nw = nw_ref[...]
    chains = [(b, h) for b in range(nb) for h in range(heads)]
    n = len(chains)

    ys = []
    for b in range(nb):
        xbuf[b, hc:hc + chunk, :] = qkv_ref[b].astype(F32)
        y = xbuf[b, hc:hc + chunk, :] * wc[conv_w - 1:conv_w, :]
        for j in range(conv_w - 1):
            off = hc - (conv_w - 1) + j
            y = y + xbuf[b, off:off + chunk, :] * wc[j:j + 1, :]
        ys.append(_silu(y))
        xbuf[b, 0:hc, :] = xbuf[b, chunk:chunk + hc, :]

    qn, kn, kb, vb, decay, e_gc, e_rest, e_last = [], [], [], [], [], [], [], []
    for b, h in chains:
        y = ys[b]
        qh = y[:, h * dk:(h + 1) * dk]
        kh = y[:, qk_dim + h * dk:qk_dim + (h + 1) * dk]
        vh = y[:, 2 * qk_dim + h * dv:2 * qk_dim + (h + 1) * dv]
        ba = ba_ref[b]
        bat = bat_ref[b]
        ne = neg_exp_alog[:, h:h + 1]
        db = dt_bias[:, h:h + 1]
        beta = jnp.where(valid_c, jax.nn.sigmoid(ba[:, h:h + 1]), 0.0)
        g_col = jnp.where(valid_c, ne * jax.nn.softplus(ba[:, heads + h:heads + h + 1] + db), 0.0)
        g_row = jnp.where(valid_r, ne * jax.nn.softplus(bat[heads + h:heads + h + 1, :] + db), 0.0)
        gc_col = jnp.sum(jnp.where(incl, g_row, 0.0), axis=1, keepdims=True)
        gc_row = jnp.sum(jnp.where(row <= col, g_col, 0.0), axis=0, keepdims=True)
        g_last = gc_col[chunk - 1:chunk, :]
        q_ = qh * lax.rsqrt(jnp.sum(qh * qh, -1, keepdims=True) + L2_EPS) * (dk ** -0.5)
        k_ = kh * lax.rsqrt(jnp.sum(kh * kh, -1, keepdims=True) + L2_EPS)
        qn.append(q_)
        kn.append(k_)
        kb.append(k_ * beta)
        vb.append(vh * beta)
        decay.append(jnp.where(incl, jnp.exp(gc_col - gc_row), 0.0))
        e_gc.append(jnp.exp(gc_col))
        e_rest.append(jnp.exp(g_last - gc_col))
        e_last.append(jnp.exp(g_last))

    sc = [lax.dot_general(jnp.concatenate([qn[i], kb[i]], axis=0).astype(BF16), kn[i].astype(BF16),
                          (((1,), (1,)), ((), ())), preferred_element_type=F32) for i in range(n)]
    qk = [sc[i][:chunk] * decay[i] for i in range(n)]
    p = [jnp.where(strict, -(sc[i][chunk:] * decay[i]), 0.0) for i in range(n)]
    t_inv = [eye + p[i] for i in range(n)]
    if n_levels > 1:
        p = [_tdot(p[i], p[i]) for i in range(n)]
    for lvl in range(1, n_levels):
        if lvl < n_levels - 1:
            yp = [_tdot(jnp.concatenate([t_inv[i], p[i]], axis=0), p[i]) for i in range(n)]
            t_inv = [t_inv[i] + yp[i][:chunk] for i in range(n)]
            p = [yp[i][chunk:] for i in range(n)]
        else:
            t_inv = [t_inv[i] + _tdot(t_inv[i], p[i]) for i in range(n)]
    sol = [_tdot(t_inv[i], jnp.concatenate([vb[i], kb[i] * e_gc[i]], axis=1)) for i in range(n)]
    s_old = [s_ref[b, h] for b, h in chains]
    wq = [_bdot(jnp.concatenate([sol[i][:, dv:], qn[i] * e_gc[i]], axis=0), s_old[i]) for i in range(n)]
    v_new = [sol[i][:, :dv] - wq[i][:chunk] for i in range(n)]
    o = [wq[i][chunk:] + _bdot(qk[i], v_new[i]) for i in range(n)]
    for i, (b, h) in enumerate(chains):
        s_ref[b, h] = s_old[i] * e_last[i] + lax.dot_general(
            (kn[i] * e_rest[i]).astype(BF16), v_new[i].astype(BF16), (((0,), (0,)), ((), ())),
            preferred_element_type=F32)
    for b in range(nb):
        zt = z_ref[b].astype(F32)
        outs = []
        for h in range(heads):
            oi = o[b * heads + h]
            on = oi * lax.rsqrt(jnp.mean(oi * oi, -1, keepdims=True) + RMS_EPS) * nw
            outs.append(on * _silu(zt[:, h * dv:(h + 1) * dv]))
        o_ref[b] = jnp.concatenate(outs, axis=1).astype(o_ref.dtype)


def _deltanet(qkv, z, ba, bat, cs0, s0, w_conv, hp, norm_w, *, chunk, l_real, nb):
    n, l, conv_dim = qkv.shape
    heads, dk, dv = s0.shape[1:]
    nc = l // chunk
    assert n % nb == 0 and l % chunk == 0
    conv_w = w_conv.shape[0]
    body = functools.partial(_dn_body, nb=nb, chunk=chunk, heads=heads, dk=dk, dv=dv, l_real=l_real, conv_w=conv_w)
    return pl.pallas_call(
        body,
        grid=(n // nb, nc),
        in_specs=[pl.BlockSpec((nb, chunk, conv_dim), lambda i, c: (i, c, 0)),
                  pl.BlockSpec((nb, chunk, heads * dv), lambda i, c: (i, c, 0)),
                  pl.BlockSpec((nb, chunk, LANES), lambda i, c: (i, c, 0)),
                  pl.BlockSpec((nb, None, SUBLANES, chunk), lambda i, c: (i, c, 0, 0)),
                  pl.BlockSpec((nb, SUBLANES, conv_dim), lambda i, c: (i, 0, 0)),
                  pl.BlockSpec((nb, heads, dk, dv), lambda i, c: (i, 0, 0, 0)),
                  pl.BlockSpec((conv_w, conv_dim), lambda i, c: (0, 0)),
                  pl.BlockSpec((SUBLANES, LANES), lambda i, c: (0, 0)),
                  pl.BlockSpec((1, dv), lambda i, c: (0, 0))],
        out_specs=[pl.BlockSpec((nb, chunk, heads * dv), lambda i, c: (i, c, 0)),
                   pl.BlockSpec((nb, heads, dk, dv), lambda i, c: (i, 0, 0, 0))],
        out_shape=[jax.ShapeDtypeStruct((n, l, heads * dv), BF16),
                   jax.ShapeDtypeStruct((n, heads, dk, dv), F32)],
        scratch_shapes=[pltpu.VMEM((nb, SUBLANES + chunk, conv_dim), F32)],
        compiler_params=_cparams("arbitrary", "arbitrary"),
        name="deltanet",
    )(qkv, z, ba, bat, cs0, s0, w_conv, hp, norm_w)


def _layer_norm(r, w, b):
    mu = jnp.mean(r, -1, keepdims=True)
    var = jnp.mean(jnp.square(r - mu), -1, keepdims=True)
    return (r - mu) * lax.rsqrt(var + LN_EPS) * w + b


def _outproj_body(attn_ref, dn_ref, g_ref, x_ref, gt_ref, sh2_ref, sc2_ref, wpa_ref, wpd_ref, wo_ref,
                  lnw_ref, lnb_ref, wr_ref, br_ref, *refs, alpha, top_k, aliased):
    if aliased:
        refs = refs[N_ROUTE_BUFS:]
    x1_ref, eidx_ref, gate_ref, rank_ref, cnt_ref = refs
    d = x_ref.shape[1]
    tm = x_ref.shape[0]
    g = g_ref[...].astype(F32)
    pa = jnp.dot(attn_ref[...], wpa_ref[...], preferred_element_type=F32)
    pd = jnp.dot(dn_ref[...], wpd_ref[...], preferred_element_type=F32)
    merged = jax.nn.sigmoid(g[:, :d]) * pa + jax.nn.sigmoid(g[:, d:]) * pd
    mix = jnp.dot(merged.astype(BF16), wo_ref[...], preferred_element_type=F32)
    x1 = _layer_norm(alpha * x_ref[...] + gt_ref[...] * mix, lnw_ref[...], lnb_ref[...])
    x1_ref[...] = x1
    h2 = x1 * (1.0 + sc2_ref[...]) + sh2_ref[...]
    logits = jnp.dot(h2, wr_ref[...], preferred_element_type=F32, precision=HIGHEST) + br_ref[...]
    lane = lax.broadcasted_iota(I32, (tm, LANES), 1)
    lane_f = lane.astype(F32)
    vals, idxs, sels = [], [], []
    l = logits
    for _ in range(top_k):
        m = jnp.max(l, axis=1, keepdims=True)
        idx = jnp.min(jnp.where(l == m, lane_f, float(LANES)), axis=1, keepdims=True)
        sel = lane_f == idx
        vals.append(m)
        idxs.append(idx)
        sels.append(sel)
        l = jnp.where(sel, -jnp.inf, l)
    ex = [jnp.exp(v - vals[0]) for v in vals]
    den = ex[0]
    for e in ex[1:]:
        den = den + e
    multi_hot = jnp.zeros((tm, LANES), F32)
    for sel in sels:
        multi_hot = multi_hot + jnp.where(sel, 1.0, 0.0)
    r_i = lax.broadcasted_iota(I32, (tm, tm), 0)
    c_i = lax.broadcasted_iota(I32, (tm, tm), 1)
    lower = jnp.where(r_i > c_i, 1.0, 0.0).astype(BF16)
    prefix = jnp.dot(lower, multi_hot.astype(BF16), preferred_element_type=F32)
    e_out = jnp.zeros((tm, LANES), F32)
    g_out = jnp.zeros((tm, LANES), F32)
    r_out = jnp.zeros((tm, LANES), F32)
    for k in range(top_k):
        rank_k = jnp.sum(jnp.where(sels[k], prefix, 0.0), axis=1, keepdims=True)
        e_out = jnp.where(lane == k, idxs[k], e_out)
        g_out = jnp.where(lane == k, ex[k] / den, g_out)
        r_out = jnp.where(lane == k, rank_k, r_out)
    eidx_ref[...] = e_out.astype(I32)
    gate_ref[...] = g_out
    rank_ref[...] = r_out.astype(I32)
    cnt_ref[...] = jnp.broadcast_to(jnp.sum(multi_hot, axis=0, keepdims=True), cnt_ref.shape)


def _outproj(attn, dn, gates, x, gt, sh2, sc2, wts, bufs, *, per_token_mod, tiles_per_seq, t_total, tile_off, alpha):
    t, d = x.shape
    tm = TOKEN_TILE
    nt = t // tm
    wpa, wpd, wo, lnw, lnb, wr, br = wts
    if per_token_mod:
        mod_spec = pl.BlockSpec((tm, d), lambda i: (i, 0))
    else:
        mod_spec = pl.BlockSpec((None, 1, d), lambda i: (i // tiles_per_seq, 0, 0))

    def row(width):
        return pl.BlockSpec((tm, width), lambda i: (i, 0))

    def full(a):
        return pl.BlockSpec(a.shape, lambda i: (0,) * a.ndim)

    aliased = bufs is not None
    in_specs = [row(attn.shape[1]), row(dn.shape[1]), row(gates.shape[1]), row(d), mod_spec, mod_spec, mod_spec,
                full(wpa), full(wpd), full(wo), full(lnw), full(lnb), full(wr), full(br)]
    args = [attn, dn, gates, x, gt, sh2, sc2, wpa, wpd, wo, lnw, lnb, wr, br]
    io_alias = {}
    if aliased:
        for k, bfr in enumerate(bufs):
            in_specs.append(pl.BlockSpec(memory_space=pl.ANY))
            io_alias[len(args)] = k
            args.append(bfr)
    out_row = lambda width: pl.BlockSpec((tm, width), lambda i: (i + tile_off, 0))
    out_shape = [jax.ShapeDtypeStruct((t_total, d), F32), jax.ShapeDtypeStruct((t_total, LANES), I32),
                 jax.ShapeDtypeStruct((t_total, LANES), F32), jax.ShapeDtypeStruct((t_total, LANES), I32),
                 jax.ShapeDtypeStruct((t_total // tm, SUBLANES, LANES), F32)]
    assert len(out_shape) == N_ROUTE_BUFS
    out_specs = [out_row(d), out_row(LANES), out_row(LANES), out_row(LANES),
                 pl.BlockSpec((None, SUBLANES, LANES), lambda i: (i + tile_off, 0, 0))]
    body = functools.partial(_outproj_body, alpha=alpha, top_k=TOP_K, aliased=aliased)
    return pl.pallas_call(
        body,
        grid=(nt,),
        in_specs=in_specs,
        out_specs=out_specs,
        out_shape=out_shape,
        input_output_aliases=io_alias,
        compiler_params=_cparams("arbitrary"),
        name="outproj",
    )(*args)


def _select_mod(i, n_ptiles, seq_ref, tok_ref):
    return jnp.where(i < n_ptiles, seq_ref[...], tok_ref[...])


def _for_each_piece(units, n_bits, local_off, global_off, fn):
    done = jnp.int32(0)
    for bit in reversed(range(n_bits)):
        rows = RUN_ALIGN << bit
        take = (units >> bit) & 1

        @pl.when(take == 1)
        def _():
            fn(pl.multiple_of(local_off + done, RUN_ALIGN), pl.multiple_of(global_off + done, RUN_ALIGN), rows)

        done = done + take * rows


def _for_each_run(i, n_experts, toff_ref, cnt8_ref, gbase_ref, fn):
    for e in range(n_experts):
        idx = i * n_experts + e
        _for_each_piece(cnt8_ref[idx] // RUN_ALIGN, RUN_BITS, toff_ref[idx], gbase_ref[idx], fn)


def _local_positions(eidx_ref, lrank_ref, toffv_ref, top_k):
    eidx = eidx_ref[...]
    lrank = lrank_ref[...]
    toffv = toffv_ref[...]
    lane = lax.broadcasted_iota(I32, eidx.shape, 1)
    pos = []
    for k in range(top_k):
        base = jnp.sum(jnp.where(lane == eidx[:, k:k + 1], toffv, 0.0), axis=1, keepdims=True)
        pos.append(base.astype(I32) + lrank[:, k:k + 1])
    return pos


def _dispatch_body(toff_ref, cnt8_ref, gbase_ref, tstart_ref, tunits_ref,
                   x1_ref, shs_ref, scs_ref, sht_ref, sct_ref, eidx_ref, lrank_ref, toffv_ref,
                   xs_ref, lbuf, zbuf, sem, *, top_k, n_tiles, n_ptiles, n_experts):
    i = pl.program_id(0)
    sc = _select_mod(i, n_ptiles, scs_ref, sct_ref)
    sh = _select_mod(i, n_ptiles, shs_ref, sht_ref)
    h2 = (x1_ref[...] * (1.0 + sc) + sh).astype(BF16)
    tm = h2.shape[0]
    n_local = lbuf.shape[0]
    col = lax.broadcasted_iota(I32, (tm, n_local), 1)
    onehot = jnp.zeros((tm, n_local), F32)
    for pos in _local_positions(eidx_ref, lrank_ref, toffv_ref, top_k):
        onehot = onehot + jnp.where(col == pos, 1.0, 0.0)
    lbuf[...] = lax.dot_general(onehot.astype(BF16), h2, (((0,), (0,)), ((), ())), preferred_element_type=F32)

    def piece(lo, go, rows):
        return pltpu.make_async_copy(lbuf.at[pl.ds(lo, rows), :], xs_ref.at[pl.ds(go, rows), :], sem)

    _for_each_run(i, n_experts, toff_ref, cnt8_ref, gbase_ref, lambda lo, go, rows: piece(lo, go, rows).start())
    _for_each_run(i, n_experts, toff_ref, cnt8_ref, gbase_ref, lambda lo, go, rows: piece(lo, go, rows).wait())

    @pl.when(i == n_tiles - 1)
    def _():
        zbuf[...] = jnp.zeros(zbuf.shape, zbuf.dtype)

        def zero_piece(lo, go, rows):
            return pltpu.make_async_copy(zbuf.at[pl.ds(0, rows), :], xs_ref.at[pl.ds(go, rows), :], sem)

        for e in range(n_experts):
            _for_each_piece(tunits_ref[e], TAIL_BITS, 0, tstart_ref[e], lambda lo, go, rows: zero_piece(lo, go, rows).start())
        for e in range(n_experts):
            _for_each_piece(tunits_ref[e], TAIL_BITS, 0, tstart_ref[e], lambda lo, go, rows: zero_piece(lo, go, rows).wait())


def _dispatch(tables, x1, sh_seq, sc_seq, sh_tok, sc_tok, e_idx, lrank, toff_vec, *, n_rows, n_local, n_ptiles,
              tiles_per_seq, n_experts):
    t, d = x1.shape
    tm = TOKEN_TILE
    nt = t // tm
    n_seq = sh_seq.shape[0]
    body = functools.partial(_dispatch_body, top_k=TOP_K, n_tiles=nt, n_ptiles=n_ptiles, n_experts=n_experts)
    seq_spec = pl.BlockSpec((None, 1, d), lambda i, *_: (jnp.minimum(i // tiles_per_seq, n_seq - 1), 0, 0))
    tok_spec = pl.BlockSpec((tm, d), lambda i, *_: (jnp.maximum(i - n_ptiles, 0), 0))
    lane_spec = pl.BlockSpec((tm, LANES), lambda i, *_: (i, 0))
    return pl.pallas_call(
        body,
        grid_spec=pltpu.PrefetchScalarGridSpec(
            num_scalar_prefetch=len(tables),
            grid=(nt,),
            in_specs=[pl.BlockSpec((tm, d), lambda i, *_: (i, 0)),
                      seq_spec, seq_spec, tok_spec, tok_spec, lane_spec, lane_spec,
                      pl.BlockSpec((None, 1, LANES), lambda i, *_: (i, 0, 0))],
            out_specs=pl.BlockSpec(memory_space=pl.ANY),
            scratch_shapes=[pltpu.VMEM((n_local, d), F32), pltpu.VMEM((RUN_ALIGN << (TAIL_BITS - 1), d), F32),
                            pltpu.SemaphoreType.DMA(())],
        ),
        out_shape=jax.ShapeDtypeStruct((n_rows, d), F32),
        compiler_params=_cparams("arbitrary"),
        name="dispatch",
    )(*tables, x1, sh_seq, sc_seq, sh_tok, sc_tok, e_idx, lrank, toff_vec)


def _expert_body(be_ref, nu_ref, xs_ref, wgu_ref, bgu_ref, wd_ref, bd_ref, y_ref, *, de):
    @pl.when(pl.program_id(0) < nu_ref[0])
    def _():
        gu = jnp.dot(xs_ref[...].astype(BF16), wgu_ref[...], preferred_element_type=F32) + bgu_ref[...]
        glu = jnp.minimum(gu[:, :de], SWIGLU_LIMIT)
        lin = jnp.clip(gu[:, de:], -SWIGLU_LIMIT, SWIGLU_LIMIT)
        act = glu * jax.nn.sigmoid(SWIGLU_ALPHA * glu) * (lin + 1.0)
        y_ref[...] = jnp.dot(act.astype(BF16), wd_ref[...], preferred_element_type=F32) + bd_ref[...]


def _experts(block_e, n_used, xs, w_gu, b_gu, w_down, b_down):
    p, d = xs.shape
    bm = EXPERT_ROWS
    n_e, _, de2 = w_gu.shape
    de = de2 // 2
    nblk = p // bm

    def blk(i, be, nu):
        return jnp.minimum(i, nu[0] - 1)

    body = functools.partial(_expert_body, de=de)
    return pl.pallas_call(
        body,
        grid_spec=pltpu.PrefetchScalarGridSpec(
            num_scalar_prefetch=2,
            grid=(nblk,),
            in_specs=[pl.BlockSpec((bm, d), lambda i, be, nu: (blk(i, be, nu), 0)),
                      pl.BlockSpec((None, d, de2), lambda i, be, nu: (be[blk(i, be, nu)], 0, 0)),
                      pl.BlockSpec((None, 1, de2), lambda i, be, nu: (be[blk(i, be, nu)], 0, 0)),
                      pl.BlockSpec((None, de, d), lambda i, be, nu: (be[blk(i, be, nu)], 0, 0)),
                      pl.BlockSpec((None, 1, d), lambda i, be, nu: (be[blk(i, be, nu)], 0, 0))],
            out_specs=pl.BlockSpec((bm, d), lambda i, be, nu: (blk(i, be, nu), 0)),
        ),
        out_shape=jax.ShapeDtypeStruct((p, d), F32),
        compiler_params=_cparams("arbitrary"),
        name="experts",
    )(block_e, n_used, xs, w_gu, b_gu.reshape(n_e, 1, de2), w_down, b_down.reshape(n_e, 1, d))


def _combine_body(toff_ref, cnt8_ref, gbase_ref, x1_ref, gts_ref, gtt_ref, gate_ref, eidx_ref, lrank_ref, toffv_ref,
                  lnw_ref, lnb_ref, yb_ref, yp_ref, ys_ref, ybuf, sem, *, top_k, n_ptiles, n_experts, alpha):
    i = pl.program_id(0)
    tm = x1_ref.shape[0]
    n_local = ybuf.shape[0]
    ybuf[tm * top_k:, :] = jnp.zeros((n_local - tm * top_k, ybuf.shape[1]), ybuf.dtype)

    def piece(lo, go, rows):
        return pltpu.make_async_copy(yb_ref.at[pl.ds(go, rows), :], ybuf.at[pl.ds(lo, rows), :], sem)

    _for_each_run(i, n_experts, toff_ref, cnt8_ref, gbase_ref, lambda lo, go, rows: piece(lo, go, rows).start())
    gate = gate_ref[...]
    col = lax.broadcasted_iota(I32, (tm, n_local), 1)
    weights = jnp.zeros((tm, n_local), F32)
    for k, pos in enumerate(_local_positions(eidx_ref, lrank_ref, toffv_ref, top_k)):
        weights = weights + jnp.where(col == pos, gate[:, k:k + 1], 0.0)
    _for_each_run(i, n_experts, toff_ref, cnt8_ref, gbase_ref, lambda lo, go, rows: piece(lo, go, rows).wait())
    ff = _bdot(weights, ybuf[...])
    gt = _select_mod(i, n_ptiles, gts_ref, gtt_ref)
    y = _layer_norm(alpha * x1_ref[...] + gt * ff, lnw_ref[...], lnb_ref[...])

    @pl.when(i < n_ptiles)
    def _():
        yp_ref[...] = y

    @pl.when(i >= n_ptiles)
    def _():
        ys_ref[...] = y


def _combine(tables, x1, gt_seq, gt_tok, gate, e_idx, lrank, toff_vec, lnw, lnb, yb, *, n_local, n_ptiles,
             tiles_per_seq, n_experts, alpha):
    t, d = x1.shape
    tm = TOKEN_TILE
    nt = t // tm
    n_seq = gt_seq.shape[0]
    body = functools.partial(_combine_body, top_k=TOP_K, n_ptiles=n_ptiles, n_experts=n_experts, alpha=alpha)
    lane_spec = pl.BlockSpec((tm, LANES), lambda i, *_: (i, 0))
    return pl.pallas_call(
        body,
        grid_spec=pltpu.PrefetchScalarGridSpec(
            num_scalar_prefetch=len(tables),
            grid=(nt,),
            in_specs=[pl.BlockSpec((tm, d), lambda i, *_: (i, 0)),
                      pl.BlockSpec((None, 1, d), lambda i, *_: (jnp.minimum(i // tiles_per_seq, n_seq - 1), 0, 0)),
                      pl.BlockSpec((tm, d), lambda i, *_: (jnp.maximum(i - n_ptiles, 0), 0)),
                      lane_spec, lane_spec, lane_spec,
                      pl.BlockSpec((None, 1, LANES), lambda i, *_: (i, 0, 0)),
                      pl.BlockSpec((1, d), lambda i, *_: (0, 0)),
                      pl.BlockSpec((1, d), lambda i, *_: (0, 0)),
                      pl.BlockSpec(memory_space=pl.ANY)],
            out_specs=[pl.BlockSpec((tm, d), lambda i, *_: (jnp.minimum(i, n_ptiles - 1), 0)),
                       pl.BlockSpec((tm, d), lambda i, *_: (jnp.maximum(i - n_ptiles, 0), 0))],
            scratch_shapes=[pltpu.VMEM((n_local, d), F32), pltpu.SemaphoreType.DMA(())],
        ),
        out_shape=[jax.ShapeDtypeStruct((n_ptiles * tm, d), F32),
                   jax.ShapeDtypeStruct(((nt - n_ptiles) * tm, d), F32)],
        compiler_params=_cparams("arbitrary"),
        name="combine",
    )(*tables, x1, gt_seq, gt_tok, gate, e_idx, lrank, toff_vec, lnw, lnb, yb)


def _rotary_tables(pos, hd, rot_dim):
    half = rot_dim // 2
    inv_freq = jnp.power(jnp.float32(ROPE_THETA), -jnp.arange(half, dtype=F32) * (2.0 / rot_dim))
    ang = pos.astype(F32)[:, None] * inv_freq[None, :]
    cos, sin = jnp.cos(ang), jnp.sin(ang)
    n = pos.shape[0]
    ones = jnp.ones((n, hd - rot_dim), F32)
    zeros = jnp.zeros((n, hd - rot_dim), F32)
    zh = jnp.zeros((n, half), F32)
    ct = jnp.concatenate([cos, cos, ones], axis=1)
    s1 = jnp.concatenate([-sin, zh, zeros], axis=1)
    s2 = jnp.concatenate([zh, sin, zeros], axis=1)
    reps = LANES // hd
    return tuple(jnp.tile(a, (1, reps)) for a in (ct, s1, s2))


def kernel(x_prompt, x_sample, state_win_k, state_win_v, state_conv, state_ssm, c_prompt, c_sample, w_ada, b_ada, w_in, attn_sinks, w_conv, dn_a_log, dn_dt_bias, dn_norm_w, w_proj_attn, w_proj_dn, w_out, ln1_w, ln1_b, w_router, b_router, w_gu, b_gu, w_down, b_down, ln2_w, ln2_b):
    n_p, seq, d = x_prompt.shape
    n_s, l_s, _ = x_sample.shape
    depth = w_ada.shape[0]
    window, n_kv, hd = state_win_k.shape[2:]
    n_q = attn_sinks.shape[1]
    heads, dk, dv = state_ssm.shape[2:]
    conv_w, conv_dim = w_conv.shape[1:]
    n_e = w_router.shape[2]
    qd, kd = n_q * hd, n_kv * hd
    vdim = heads * dv
    rot_dim = hd // 4
    alpha = float((2 * depth) ** 0.25)
    tm = TOKEN_TILE
    t_p, t_s = n_p * seq, n_s * l_s
    t_all = t_p + t_s
    tps = seq // tm
    n_ptiles = t_p // tm
    l_pad = SUBLANES
    assert seq % tm == 0 and t_s % tm == 0 and tm % l_s == 0 and l_s <= l_pad and l_s >= conv_w - 1
    assert 2 * heads <= SUBLANES and n_e <= LANES and hd * 2 == LANES and rot_dim == 2 * SUBLANES

    sizes = [qd, kd, kd, conv_dim, vdim, heads, heads, d, d]
    offs = np.concatenate([[0], np.cumsum(sizes)])
    seg = lambda k: np.arange(offs[k], offs[k + 1])
    perm = np.concatenate([seg(0), seg(1), seg(2), seg(3), seg(4), seg(7), seg(8), seg(5), seg(6)])
    cuts = (qd + 2 * kd, qd + 2 * kd + conv_dim, qd + 2 * kd + conv_dim + vdim, qd + 2 * kd + conv_dim + vdim + 2 * d)
    cuts = cuts + (cuts[-1] + LANES,)

    tabs_p = _rotary_tables(jnp.arange(seq, dtype=I32), hd, rot_dim)
    tabs_s = tuple(jnp.tile(a, (tm // l_s, 1))
                   for a in _rotary_tables(PAST_LEN + jnp.arange(l_s, dtype=I32), hd, rot_dim))

    x_p = x_prompt.reshape(t_p, d)
    x_s = x_sample.reshape(t_s, d)
    c_all = jnp.concatenate([c_prompt, c_sample], axis=0)
    outs = {k: [] for k in ("pwk", "pwv", "pcv", "pss", "swk", "swv", "scv", "sss")}

    for l in range(depth):
        w_in_l = jnp.pad(w_in[l][:, perm], ((0, 0), (0, LANES - 2 * heads))).astype(BF16)
        mod = _ada(c_all, w_ada[l], b_ada[l])
        mod_p = mod[:n_p].reshape(n_p, 6, 1, d)
        mod_s = jnp.repeat(mod[n_p:].reshape(n_s, 6, d), l_s, axis=0)
        sh1p, sc1p, gt1p, sh2p, sc2p, gt2p = [mod_p[:, k] for k in range(6)]
        sh1s, sc1s, gt1s, sh2s, sc2s, gt2s = [mod_s[:, k] for k in range(6)]

        qkva_p, dn_p, z_p, g_p, ba_p, tail_p, kvw_p = _inproj(
            x_p, sh1p, sc1p, tabs_p, w_in_l, cuts, per_token_mod=False, tiles_per_seq=tps, act_dtype=BF16,
            window=window, kv_cols=2 * kd)
        attn_p = _attn_prompt(qkva_p, attn_sinks[l], n_p, seq, n_q, n_kv, hd, window)
        chunk = min(DN_CHUNK, seq)
        nc = seq // chunk
        bat_p = ba_p[:, :SUBLANES].reshape(n_p, nc, chunk, SUBLANES).transpose(0, 1, 3, 2)
        hp = jnp.zeros((SUBLANES, LANES), F32).at[0, :heads].set(dn_a_log[l]).at[1, :heads].set(dn_dt_bias[l])
        nw = dn_norm_w[l].reshape(1, dv)
        o_p, ssm_p = _deltanet(dn_p.reshape(n_p, seq, conv_dim), z_p.reshape(n_p, seq, vdim),
                               ba_p.reshape(n_p, seq, LANES), bat_p,
                               jnp.zeros((n_p, SUBLANES, conv_dim), F32), jnp.zeros((n_p, heads, dk, dv), F32),
                               w_conv[l], hp, nw, chunk=chunk, l_real=chunk, nb=4)
        outs["pwk"].append(kvw_p[:, :, :kd].reshape(n_p, window, n_kv, hd))
        outs["pwv"].append(kvw_p[:, :, kd:].reshape(n_p, window, n_kv, hd))
        outs["pcv"].append(tail_p.reshape(n_p, tps, SUBLANES, conv_dim)[:, -1, SUBLANES - (conv_w - 1):])
        outs["pss"].append(ssm_p)

        qkva_s, dn_s, z_s, g_s, ba_s = _inproj(
            x_s, sh1s, sc1s, tabs_s, w_in_l, cuts, per_token_mod=True, tiles_per_seq=1, act_dtype=F32,
            window=window, kv_cols=2 * kd)
        pad_l = lambda a: jnp.pad(a.reshape(n_s, l_s, a.shape[-1]), ((0, 0), (0, l_pad - l_s), (0, 0)))
        attn_s, wk_s, wv_s = _attn_sample(pad_l(qkva_s), state_win_k[l].reshape(n_s, window, kd),
                                          state_win_v[l].reshape(n_s, window, kd), attn_sinks[l], n_q, n_kv, hd, l_s)
        attn_s = attn_s[:, :l_s].reshape(t_s, qd)
        ba_s3 = pad_l(ba_s)
        bat_s = ba_s3[:, :, :SUBLANES].transpose(0, 2, 1).reshape(n_s, 1, SUBLANES, l_pad)
        cs0 = jnp.pad(state_conv[l], ((0, 0), (SUBLANES - (conv_w - 1), 0), (0, 0)))
        o_s, ssm_s = _deltanet(pad_l(dn_s), pad_l(z_s), ba_s3, bat_s, cs0, state_ssm[l], w_conv[l], hp, nw,
                               chunk=l_pad, l_real=l_s, nb=8)
        o_s = o_s[:, :l_s].reshape(t_s, vdim)
        outs["swk"].append(wk_s.reshape(n_s, window, n_kv, hd))
        outs["swv"].append(wv_s.reshape(n_s, window, n_kv, hd))
        outs["scv"].append(jnp.concatenate([state_conv[l], dn_s.reshape(n_s, l_s, conv_dim)], axis=1)[:, -(conv_w - 1):])
        outs["sss"].append(ssm_s)

        wr = jnp.pad(w_router[l], ((0, 0), (0, LANES - n_e)))
        br = jnp.pad(b_router[l], (0, LANES - n_e), constant_values=NEG_BIG).reshape(1, LANES)
        wts = (w_proj_attn[l].astype(BF16), w_proj_dn[l].astype(BF16), w_out[l].astype(BF16),
               ln1_w[l].reshape(1, d), ln1_b[l].reshape(1, d), wr, br)
        res_p = _outproj(attn_p, o_p.reshape(t_p, vdim), g_p, x_p, gt1p, sh2p, sc2p, wts, None,
                         per_token_mod=False, tiles_per_seq=tps, t_total=t_all, tile_off=0, alpha=alpha)
        x1, e_idx, gate, rank, cnt = _outproj(attn_s, o_s, g_s, x_s, gt1s, sh2s, sc2s, wts, res_p,
                                              per_token_mod=True, tiles_per_seq=1, t_total=t_all, tile_off=n_ptiles,
                                              alpha=alpha)

        bm = EXPERT_ROWS
        nt_all = t_all // tm
        cnt8 = (cnt[:, 0, :n_e].astype(I32) + RUN_ALIGN - 1) // RUN_ALIGN * RUN_ALIGN
        tot = jnp.sum(cnt8, axis=0)
        padded = (tot + bm - 1) // bm * bm
        pad_end = jnp.cumsum(padded)
        pad_start = pad_end - padded
        gbase = pad_start[None, :] + jnp.cumsum(cnt8, axis=0) - cnt8
        toff = jnp.cumsum(cnt8, axis=1) - cnt8
        n_local = -(-(tm * TOP_K + n_e * (RUN_ALIGN - 1)) // LANES) * LANES
        n_rows = -(-(t_all * TOP_K + nt_all * n_e * (RUN_ALIGN - 1) + n_e * (bm - 1)) // bm) * bm
        nblk = n_rows // bm
        n_used = jnp.maximum(pad_end[-1:] // bm, 1).astype(I32)
        block_e = jnp.minimum(jnp.sum(pad_end[None, :] <= (jnp.arange(nblk, dtype=I32) * bm)[:, None], axis=1),
                              n_e - 1).astype(I32)
        flat = lambda a: a.astype(I32).reshape(nt_all * n_e)
        run_tables = (flat(toff), flat(cnt8), flat(gbase))
        tail_tables = ((pad_start + tot).astype(I32), ((padded - tot) // RUN_ALIGN).astype(I32))
        toff_vec = jnp.pad(toff.astype(F32), ((0, 0), (0, LANES - n_e))).reshape(nt_all, 1, LANES)

        xs = _dispatch(run_tables + tail_tables, x1, sh2p, sc2p, sh2s, sc2s, e_idx, rank, toff_vec, n_rows=n_rows,
                       n_local=n_local, n_ptiles=n_ptiles, tiles_per_seq=tps, n_experts=n_e)
        yb = _experts(block_e, n_used, xs, w_gu[l].astype(BF16), b_gu[l], w_down[l].astype(BF16), b_down[l])
        x_p, x_s = _combine(run_tables, x1, gt2p, gt2s, gate, e_idx, rank, toff_vec, ln2_w[l].reshape(1, d),
                            ln2_b[l].reshape(1, d), yb, n_local=n_local, n_ptiles=n_ptiles, tiles_per_seq=tps,
                            n_experts=n_e, alpha=alpha)

    st = lambda k: jnp.stack(outs[k])
    return (x_p.reshape(n_p, seq, d), x_s.reshape(n_s, l_s, d), st("pwk"), st("pwv"), st("pcv"), st("pss"),
            st("swk"), st("swv"), st("scv"), st("sss"))
```

```python
import functools

import numpy as np
import jax
import jax.numpy as jnp
from jax import lax
from jax.experimental import pallas as pl
from jax.experimental.pallas import tpu as pltpu

F32 = jnp.float32
BF16 = jnp.bfloat16
I32 = jnp.int32
HIGHEST = lax.Precision.HIGHEST

PAST_LEN = 16384
ROPE_THETA = 500000.0
TOP_K = 4
SWIGLU_LIMIT = 7.0
SWIGLU_ALPHA = 1.702
DN_CHUNK = 64
LN_EPS = 1e-5
RMS_EPS = 1e-6
L2_EPS = 1e-6

LANES = 128
SUBLANES = 8
VMEM_LIMIT_BYTES = 56 * 1024 * 1024

TOKEN_TILE = 256
EXPERT_ROWS = 512
RUN_ALIGN = SUBLANES
N_ROUTE_BUFS = 5
NEG_BIG = -1e30


def _cparams(*sem):
    return pltpu.CompilerParams(dimension_semantics=sem, vmem_limit_bytes=VMEM_LIMIT_BYTES)


def _silu(x):
    return x * jax.nn.sigmoid(x)


def _bdot(a, b):
    return jnp.dot(a.astype(BF16), b.astype(BF16), preferred_element_type=F32)


def _ada_body(c_ref, w_ref, b_ref, o_ref):
    o_ref[...] = _bdot(_silu(c_ref[...]), w_ref[...]) + b_ref[...]


def _ada(c_all, w_ada, b_ada):
    n, d = c_all.shape
    dout = w_ada.shape[1]
    tn = d
    return pl.pallas_call(
        _ada_body,
        grid=(dout // tn,),
        in_specs=[pl.BlockSpec((n, d), lambda j: (0, 0)),
                  pl.BlockSpec((d, tn), lambda j: (0, j)),
                  pl.BlockSpec((1, tn), lambda j: (0, j))],
        out_specs=pl.BlockSpec((n, tn), lambda j: (0, j)),
        out_shape=jax.ShapeDtypeStruct((n, dout), F32),
        compiler_params=_cparams("arbitrary"),
        name="ada",
    )(c_all, w_ada, b_ada.reshape(1, dout))


def _inproj_body(x_ref, sh_ref, sc_ref, ct_ref, s1_ref, s2_ref, w_ref,
                 a_ref, dn_ref, z_ref, g_ref, ba_ref, *win_refs, cuts, n_rot_chunks, window):
    h = (x_ref[...] * (1.0 + sc_ref[...]) + sh_ref[...]).astype(BF16)

    def mm(lo, hi):
        return jnp.dot(h, w_ref[:, lo:hi], preferred_element_type=F32)

    c_a, c_dn, c_z, c_g, c_ba = cuts
    qkv = mm(0, c_a)
    ct, s1, s2 = ct_ref[...], s1_ref[...], s2_ref[...]
    cols = []
    for c in range(n_rot_chunks):
        xc = qkv[:, c * LANES:(c + 1) * LANES]
        cols.append(xc * ct + pltpu.roll(xc, LANES - SUBLANES, 1) * s1 + pltpu.roll(xc, SUBLANES, 1) * s2)
    cols.append(qkv[:, n_rot_chunks * LANES:])
    rot = jnp.concatenate(cols, axis=1)
    a_ref[...] = rot.astype(a_ref.dtype)
    dn = mm(c_a, c_dn)
    dn_ref[...] = dn.astype(dn_ref.dtype)
    z_ref[...] = mm(c_dn, c_z).astype(z_ref.dtype)
    g_ref[...] = mm(c_z, c_g).astype(g_ref.dtype)
    ba_ref[...] = mm(c_g, c_ba)
    if win_refs:
        tail_ref, kvw_ref = win_refs
        tm = dn.shape[0]
        tail_ref[...] = dn[tm - SUBLANES:, :]
        kvw_ref[...] = rot[tm - window:, n_rot_chunks * LANES - LANES:]


def _inproj(x, sh, sc, tabs, w_perm, cuts, *, per_token_mod, tiles_per_seq, act_dtype, window, kv_cols):
    t, d = x.shape
    tm = TOKEN_TILE
    nt = t // tm
    c_a, c_dn, c_z, c_g, c_ba = cuts
    n_rot_chunks = (c_a - kv_cols // 2) // LANES
    if per_token_mod:
        mod_spec = pl.BlockSpec((tm, d), lambda i: (i, 0))
        tab_spec = pl.BlockSpec((tm, LANES), lambda i: (0, 0))
    else:
        mod_spec = pl.BlockSpec((None, 1, d), lambda i: (i // tiles_per_seq, 0, 0))
        tab_spec = pl.BlockSpec((tm, LANES), lambda i: (i % tiles_per_seq, 0))
    out_shape = [jax.ShapeDtypeStruct((t, c_a), act_dtype),
                 jax.ShapeDtypeStruct((t, c_dn - c_a), act_dtype),
                 jax.ShapeDtypeStruct((t, c_z - c_dn), act_dtype),
                 jax.ShapeDtypeStruct((t, c_g - c_z), act_dtype),
                 jax.ShapeDtypeStruct((t, c_ba - c_g), F32)]
    out_specs = [pl.BlockSpec((tm, s.shape[1]), lambda i: (i, 0)) for s in out_shape]
    with_win = not per_token_mod
    if with_win:
        n_seq = nt // tiles_per_seq
        out_shape += [jax.ShapeDtypeStruct((nt, SUBLANES, c_dn - c_a), F32),
                      jax.ShapeDtypeStruct((n_seq, window, kv_cols), F32)]
        out_specs += [pl.BlockSpec((None, SUBLANES, c_dn - c_a), lambda i: (i, 0, 0)),
                      pl.BlockSpec((None, window, kv_cols), lambda i: (i // tiles_per_seq, 0, 0))]
    body = functools.partial(_inproj_body, cuts=cuts, n_rot_chunks=n_rot_chunks, window=window)
    return pl.pallas_call(
        body,
        grid=(nt,),
        in_specs=[pl.BlockSpec((tm, d), lambda i: (i, 0)), mod_spec, mod_spec,
                  tab_spec, tab_spec, tab_spec,
                  pl.BlockSpec((d, c_ba), lambda i: (0, 0))],
        out_specs=out_specs,
        out_shape=out_shape,
        compiler_params=_cparams("arbitrary"),
        name="inproj",
    )(x, sh, sc, *tabs, w_perm)


def _softmax_sink_pv(s, valid, sink, v):
    s = jnp.where(valid, s, -jnp.inf)
    m = jnp.maximum(jnp.max(s, axis=-1, keepdims=True), sink)
    p = jnp.exp(s - m)
    denom = jnp.sum(p, axis=-1, keepdims=True) + jnp.exp(sink - m)
    return jnp.dot((p / denom).astype(BF16), v, preferred_element_type=F32)


def _attn_prompt_body(sink_ref, q_ref, kvp_ref, kvc_ref, o_ref, *, n_q, n_kv, hd, window):
    j = pl.program_id(1)
    group = n_q // n_kv
    q = q_ref[...]
    kv = jnp.concatenate([kvp_ref[...], kvc_ref[...]], axis=0)
    r = lax.broadcasted_iota(I32, (window, 2 * window), 0)
    c = lax.broadcasted_iota(I32, (window, 2 * window), 1)
    rel = window + r - c
    valid = (rel >= 0) & (rel < window) & ((c >= window) | (j > 0))
    outs = []
    for h in range(n_q):
        kvh = h // group
        qh = q[:, h * hd:(h + 1) * hd]
        kh = kv[:, kvh * hd:(kvh + 1) * hd]
        vh = kv[:, (n_kv + kvh) * hd:(n_kv + kvh + 1) * hd]
        s = lax.dot_general(qh, kh, (((1,), (1,)), ((), ())), preferred_element_type=F32) * (hd ** -0.5)
        outs.append(_softmax_sink_pv(s, valid, sink_ref[h], vh))
    o_ref[...] = jnp.concatenate(outs, axis=1).astype(o_ref.dtype)


def _attn_prompt(qkva, sinks, n_seq, seq, n_q, n_kv, hd, window):
    qd, kvd = n_q * hd, 2 * n_kv * hd
    x3 = qkva.reshape(n_seq, seq, qd + kvd)
    nb = seq // window
    kv_blk = qd // kvd
    body = functools.partial(_attn_prompt_body, n_q=n_q, n_kv=n_kv, hd=hd, window=window)
    out = pl.pallas_call(
        body,
        grid=(n_seq, nb),
        in_specs=[pl.BlockSpec(memory_space=pltpu.SMEM),
                  pl.BlockSpec((None, window, qd), lambda n, j: (n, j, 0)),
                  pl.BlockSpec((None, window, kvd), lambda n, j: (n, jnp.maximum(j - 1, 0), kv_blk)),
                  pl.BlockSpec((None, window, kvd), lambda n, j: (n, j, kv_blk))],
        out_specs=pl.BlockSpec((None, window, qd), lambda n, j: (n, j, 0)),
        out_shape=jax.ShapeDtypeStruct((n_seq, seq, qd), BF16),
        compiler_params=_cparams("arbitrary", "arbitrary"),
        name="attn_prompt",
    )(sinks, x3, x3, x3)
    return out.reshape(n_seq * seq, qd)


def _attn_sample_body(sink_ref, q_ref, wk_ref, wv_ref, o_ref, wko_ref, wvo_ref,
                      *, bs, n_q, n_kv, hd, window, l_new, l_pad):
    group = n_q // n_kv
    qd = n_q * hd
    kd = n_kv * hd
    rows = group * l_pad
    r = lax.broadcasted_iota(I32, (rows, window + l_pad), 0) % l_pad
    c = lax.broadcasted_iota(I32, (rows, window + l_pad), 1)
    rel = window + r - c
    valid = (rel >= 0) & (rel < window) & (c < window + l_new)
    for b in range(bs):
        x = q_ref[b]
        k_new = x[:, qd:qd + kd]
        v_new = x[:, qd + kd:]
        wko_ref[b, 0:window - l_new, :] = wk_ref[b, l_new:window, :]
        wko_ref[b, window - l_new:window, :] = k_new[0:l_new, :]
        wvo_ref[b, 0:window - l_new, :] = wv_ref[b, l_new:window, :]
        wvo_ref[b, window - l_new:window, :] = v_new[0:l_new, :]
        k_all = jnp.concatenate([wk_ref[b], k_new], axis=0).astype(BF16)
        v_all = jnp.concatenate([wv_ref[b], v_new], axis=0).astype(BF16)
        outs = [None] * n_q
        for kvh in range(n_kv):
            qg = jnp.concatenate([x[:, (kvh * group + g) * hd:(kvh * group + g + 1) * hd] for g in range(group)],
                                 axis=0).astype(BF16)
            kh = k_all[:, kvh * hd:(kvh + 1) * hd]
            vh = v_all[:, kvh * hd:(kvh + 1) * hd]
            s = lax.dot_general(qg, kh, (((1,), (1,)), ((), ())), preferred_element_type=F32) * (hd ** -0.5)
            sink = jnp.concatenate([jnp.full((l_pad, 1), sink_ref[kvh * group + g], F32) for g in range(group)], axis=0)
            o = _softmax_sink_pv(s, valid, sink, vh)
            for g in range(group):
                outs[kvh * group + g] = o[g * l_pad:(g + 1) * l_pad, :]
        o_ref[b] = jnp.concatenate(outs, axis=1).astype(o_ref.dtype)


def _attn_sample(qkva_pad, win_k, win_v, sinks, n_q, n_kv, hd, l_new):
    n, l_pad, width = qkva_pad.shape
    window, kd = win_k.shape[1], win_k.shape[2]
    qd = n_q * hd
    bs = 8
    body = functools.partial(_attn_sample_body, bs=bs, n_q=n_q, n_kv=n_kv, hd=hd, window=window,
                             l_new=l_new, l_pad=l_pad)
    return pl.pallas_call(
        body,
        grid=(n // bs,),
        in_specs=[pl.BlockSpec(memory_space=pltpu.SMEM),
                  pl.BlockSpec((bs, l_pad, width), lambda i: (i, 0, 0)),
                  pl.BlockSpec((bs, window, kd), lambda i: (i, 0, 0)),
                  pl.BlockSpec((bs, window, kd), lambda i: (i, 0, 0))],
        out_specs=[pl.BlockSpec((bs, l_pad, qd), lambda i: (i, 0, 0)),
                   pl.BlockSpec((bs, window, kd), lambda i: (i, 0, 0)),
                   pl.BlockSpec((bs, window, kd), lambda i: (i, 0, 0))],
        out_shape=[jax.ShapeDtypeStruct((n, l_pad, qd), BF16),
                   jax.ShapeDtypeStruct((n, window, kd), F32),
                   jax.ShapeDtypeStruct((n, window, kd), F32)],
        compiler_params=_cparams("arbitrary"),
        name="attn_sample",
    )(sinks, qkva_pad, win_k, win_v)


def _split_bf16(x):
    hi = x.astype(BF16)
    return hi, (x - hi.astype(F32)).astype(BF16)


def _tdot(a, b):
    ah, al = _split_bf16(a)
    bh, bl = _split_bf16(b)
    m = a.shape[0]
    t = jnp.dot(jnp.concatenate([ah, al], axis=0), bh, preferred_element_type=F32)
    return t[:m] + t[m:] + jnp.dot(ah, bl, preferred_element_type=F32)


def _dn_body(qkv_ref, z_ref, ba_ref, bat_ref, cs0_ref, s0_ref, wc_ref, hp_ref, nw_ref,
             o_ref, s_ref, xbuf, *, nb, chunk, heads, dk, dv, l_real, conv_w):
    c_idx = pl.program_id(1)
    hc = SUBLANES

    @pl.when(c_idx == 0)
    def _():
        xbuf[:, 0:hc, :] = cs0_ref[...]
        s_ref[...] = s0_ref[...]

    qk_dim = heads * dk
    row = lax.broadcasted_iota(I32, (chunk, chunk), 0)
    col = lax.broadcasted_iota(I32, (chunk, chunk), 1)
    incl = row >= col
    strict = row > col
    eye = (row == col).astype(F32)
    valid_c = row[:, 0:1] < l_real
    valid_r = col[0:1, :] < l_real
    n_levels = max(1, int(np.ceil(np.log2(chunk))))
    wc = wc_ref[...]
    hp = hp_ref[...]
    neg_exp_alog = -jnp.exp(hp[0:1, :])
    dt_bias = hp[1:2, :]
    nw = nw_ref[...]
    chains = [(b, h) for b in range(nb) for h in range(heads)]
    n = len(chains)

    ys = []
    for b in range(nb):
        xbuf[b, hc:hc + chunk, :] = qkv_ref[b].astype(F32)
        y = xbuf[b, hc:hc + chunk, :] * wc[conv_w - 1:conv_w, :]
        for j in range(conv_w - 1):
            off = hc - (conv_w - 1) + j
            y = y + xbuf[b, off:off + chunk, :] * wc[j:j + 1, :]
        ys.append(_silu(y))
        xbuf[b, 0:hc, :] = xbuf[b, chunk:chunk + hc, :]

    qn, kn, kb, vb, decay, e_gc, e_rest, e_last = [], [], [], [], [], [], [], []
    for b, h in chains:
        y = ys[b]
        qh = y[:, h * dk:(h + 1) * dk]
        kh = y[:, qk_dim + h * dk:qk_dim + (h + 1) * dk]
        vh = y[:, 2 * qk_dim + h * dv:2 * qk_dim + (h + 1) * dv]
        ba = ba_ref[b]
        bat = bat_ref[b]
        ne = neg_exp_alog[:, h:h + 1]
        db = dt_bias[:, h:h + 1]
        beta = jnp.where(valid_c, jax.nn.sigmoid(ba[:, h:h + 1]), 0.0)
        g_col = jnp.where(valid_c, ne * jax.nn.softplus(ba[:, heads + h:heads + h + 1] + db), 0.0)
        g_row = jnp.where(valid_r, ne * jax.nn.softplus(bat[heads + h:heads + h + 1, :] + db), 0.0)
        gc_col = jnp.sum(jnp.where(incl, g_row, 0.0), axis=1, keepdims=True)
        gc_row = jnp.sum(jnp.where(row <= col, g_col, 0.0), axis=0, keepdims=True)
        g_last = gc_col[chunk - 1:chunk, :]
        q_ = qh * lax.rsqrt(jnp.sum(qh * qh, -1, keepdims=True) + L2_EPS) * (dk ** -0.5)
        k_ = kh * lax.rsqrt(jnp.sum(kh * kh, -1, keepdims=True) + L2_EPS)
        qn.append(q_)
        kn.append(k_)
        kb.append(k_ * beta)
        vb.append(vh * beta)
        decay.append(jnp.where(incl, jnp.exp(gc_col - gc_row), 0.0))
        e_gc.append(jnp.exp(gc_col))
        e_rest.append(jnp.exp(g_last - gc_col))
        e_last.append(jnp.exp(g_last))

    sc = [lax.dot_general(jnp.concatenate([qn[i], kb[i]], axis=0).astype(BF16), kn[i].astype(BF16),
                          (((1,), (1,)), ((), ())), preferred_element_type=F32) for i in range(n)]
    qk = [sc[i][:chunk] * decay[i] for i in range(n)]
    p = [jnp.where(strict, -(sc[i][chunk:] * decay[i]), 0.0) for i in range(n)]
    t_inv = [eye + p[i] for i in range(n)]
    if n_levels > 1:
        p = [_tdot(p[i], p[i]) for i in range(n)]
    for lvl in range(1, n_levels):
        if lvl < n_levels - 1:
            yp = [_tdot(jnp.concatenate([t_inv[i], p[i]], axis=0), p[i]) for i in range(n)]
            t_inv = [t_inv[i] + yp[i][:chunk] for i in range(n)]
            p = [yp[i][chunk:] for i in range(n)]
        else:
            t_inv = [t_inv[i] + _tdot(t_inv[i], p[i]) for i in range(n)]
    sol = [_tdot(t_inv[i], jnp.concatenate([vb[i], kb[i] * e_gc[i]], axis=1)) for i in range(n)]
    s_old = [s_ref[b, h] for b, h in chains]
    wq = [_bdot(jnp.concatenate([sol[i][:, dv:], qn[i] * e_gc[i]], axis=0), s_old[i]) for i in range(n)]
    v_new = [sol[i][:, :dv] - wq[i][:chunk] for i in range(n)]
    o = [wq[i][chunk:] + _bdot(qk[i], v_new[i]) for i in range(n)]
    for i, (b, h) in enumerate(chains):
        s_ref[b, h] = s_old[i] * e_last[i] + lax.dot_general(
            (kn[i] * e_rest[i]).astype(BF16), v_new[i].astype(BF16), (((0,), (0,)), ((), ())),
            preferred_element_type=F32)
    for b in range(nb):
        zt = z_ref[b].astype(F32)
        outs = []
        for h in range(heads):
            oi = o[b * heads + h]
            on = oi * lax.rsqrt(jnp.mean(oi * oi, -1, keepdims=True) + RMS_EPS) * nw
            outs.append(on * _silu(zt[:, h * dv:(h + 1) * dv]))
        o_ref[b] = jnp.concatenate(outs, axis=1).astype(o_ref.dtype)


def _deltanet(qkv, z, ba, bat, cs0, s0, w_conv, hp, norm_w, *, chunk, l_real, nb):
    n, l, conv_dim = qkv.shape
    heads, dk, dv = s0.shape[1:]
    nc = l // chunk
    assert n % nb == 0 and l % chunk == 0
    conv_w = w_conv.shape[0]
    body = functools.partial(_dn_body, nb=nb, chunk=chunk, heads=heads, dk=dk, dv=dv, l_real=l_real, conv_w=conv_w)
    return pl.pallas_call(
        body,
        grid=(n // nb, nc),
        in_specs=[pl.BlockSpec((nb, chunk, conv_dim), lambda i, c: (i, c, 0)),
                  pl.BlockSpec((nb, chunk, heads * dv), lambda i, c: (i, c, 0)),
                  pl.BlockSpec((nb, chunk, LANES), lambda i, c: (i, c, 0)),
                  pl.BlockSpec((nb, None, SUBLANES, chunk), lambda i, c: (i, c, 0, 0)),
                  pl.BlockSpec((nb, SUBLANES, conv_dim), lambda i, c: (i, 0, 0)),
                  pl.BlockSpec((nb, heads, dk, dv), lambda i, c: (i, 0, 0, 0)),
                  pl.BlockSpec((conv_w, conv_dim), lambda i, c: (0, 0)),
                  pl.BlockSpec((SUBLANES, LANES), lambda i, c: (0, 0)),
                  pl.BlockSpec((1, dv), lambda i, c: (0, 0))],
        out_specs=[pl.BlockSpec((nb, chunk, heads * dv), lambda i, c: (i, c, 0)),
                   pl.BlockSpec((nb, heads, dk, dv), lambda i, c: (i, 0, 0, 0))],
        out_shape=[jax.ShapeDtypeStruct((n, l, heads * dv), BF16),
                   jax.ShapeDtypeStruct((n, heads, dk, dv), F32)],
        scratch_shapes=[pltpu.VMEM((nb, SUBLANES + chunk, conv_dim), F32)],
        compiler_params=_cparams("arbitrary", "arbitrary"),
        name="deltanet",
    )(qkv, z, ba, bat, cs0, s0, w_conv, hp, norm_w)


def _layer_norm(r, w, b):
    mu = jnp.mean(r, -1, keepdims=True)
    var = jnp.mean(jnp.square(r - mu), -1, keepdims=True)
    return (r - mu) * lax.rsqrt(var + LN_EPS) * w + b


def _outproj_body(attn_ref, dn_ref, g_ref, x_ref, gt_ref, sh2_ref, sc2_ref, wpa_ref, wpd_ref, wo_ref,
                  lnw_ref, lnb_ref, wr_ref, br_ref, *refs, alpha, top_k, aliased):
    if aliased:
        refs = refs[N_ROUTE_BUFS:]
    x1_ref, eidx_ref, gate_ref, rank_ref, cnt_ref = refs
    d = x_ref.shape[1]
    tm = x_ref.shape[0]
    g = g_ref[...].astype(F32)
    pa = jnp.dot(attn_ref[...], wpa_ref[...], preferred_element_type=F32)
    pd = jnp.dot(dn_ref[...], wpd_ref[...], preferred_element_type=F32)
    merged = jax.nn.sigmoid(g[:, :d]) * pa + jax.nn.sigmoid(g[:, d:]) * pd
    mix = jnp.dot(merged.astype(BF16), wo_ref[...], preferred_element_type=F32)
    x1 = _layer_norm(alpha * x_ref[...] + gt_ref[...] * mix, lnw_ref[...], lnb_ref[...])
    x1_ref[...] = x1
    h2 = x1 * (1.0 + sc2_ref[...]) + sh2_ref[...]
    logits = jnp.dot(h2, wr_ref[...], preferred_element_type=F32, precision=HIGHEST) + br_ref[...]
    lane = lax.broadcasted_iota(I32, (tm, LANES), 1)
    lane_f = lane.astype(F32)
    vals, idxs, sels = [], [], []
    l = logits
    for _ in range(top_k):
        m = jnp.max(l, axis=1, keepdims=True)
        idx = jnp.min(jnp.where(l == m, lane_f, float(LANES)), axis=1, keepdims=True)
        sel = lane_f == idx
        vals.append(m)
        idxs.append(idx)
        sels.append(sel)
        l = jnp.where(sel, -jnp.inf, l)
    ex = [jnp.exp(v - vals[0]) for v in vals]
    den = ex[0]
    for e in ex[1:]:
        den = den + e
    multi_hot = jnp.zeros((tm, LANES), F32)
    for sel in sels:
        multi_hot = multi_hot + jnp.where(sel, 1.0, 0.0)
    r_i = lax.broadcasted_iota(I32, (tm, tm), 0)
    c_i = lax.broadcasted_iota(I32, (tm, tm), 1)
    lower = jnp.where(r_i > c_i, 1.0, 0.0).astype(BF16)
    prefix = jnp.dot(lower, multi_hot.astype(BF16), preferred_element_type=F32)
    e_out = jnp.zeros((tm, LANES), F32)
    g_out = jnp.zeros((tm, LANES), F32)
    r_out = jnp.zeros((tm, LANES), F32)
    for k in range(top_k):
        rank_k = jnp.sum(jnp.where(sels[k], prefix, 0.0), axis=1, keepdims=True)
        e_out = jnp.where(lane == k, idxs[k], e_out)
        g_out = jnp.where(lane == k, ex[k] / den, g_out)
        r_out = jnp.where(lane == k, rank_k, r_out)
    eidx_ref[...] = e_out.astype(I32)
    gate_ref[...] = g_out
    rank_ref[...] = r_out.astype(I32)
    cnt_ref[...] = jnp.broadcast_to(jnp.sum(multi_hot, axis=0, keepdims=True), cnt_ref.shape)


def _outproj(attn, dn, gates, x, gt, sh2, sc2, wts, bufs, *, per_token_mod, tiles_per_seq, t_total, tile_off, alpha):
    t, d = x.shape
    tm = TOKEN_TILE
    nt = t // tm
    wpa, wpd, wo, lnw, lnb, wr, br = wts
    if per_token_mod:
        mod_spec = pl.BlockSpec((tm, d), lambda i: (i, 0))
    else:
        mod_spec = pl.BlockSpec((None, 1, d), lambda i: (i // tiles_per_seq, 0, 0))

    def row(width):
        return pl.BlockSpec((tm, width), lambda i: (i, 0))

    def full(a):
        return pl.BlockSpec(a.shape, lambda i: (0,) * a.ndim)

    aliased = bufs is not None
    in_specs = [row(attn.shape[1]), row(dn.shape[1]), row(gates.shape[1]), row(d), mod_spec, mod_spec, mod_spec,
                full(wpa), full(wpd), full(wo), full(lnw), full(lnb), full(wr), full(br)]
    args = [attn, dn, gates, x, gt, sh2, sc2, wpa, wpd, wo, lnw, lnb, wr, br]
    io_alias = {}
    if aliased:
        for k, bfr in enumerate(bufs):
            in_specs.append(pl.BlockSpec(memory_space=pl.ANY))
            io_alias[len(args)] = k
            args.append(bfr)
    out_row = lambda width: pl.BlockSpec((tm, width), lambda i: (i + tile_off, 0))
    out_shape = [jax.ShapeDtypeStruct((t_total, d), F32), jax.ShapeDtypeStruct((t_total, LANES), I32),
                 jax.ShapeDtypeStruct((t_total, LANES), F32), jax.ShapeDtypeStruct((t_total, LANES), I32),
                 jax.ShapeDtypeStruct((t_total // tm, SUBLANES, LANES), F32)]
    assert len(out_shape) == N_ROUTE_BUFS
    out_specs = [out_row(d), out_row(LANES), out_row(LANES), out_row(LANES),
                 pl.BlockSpec((None, SUBLANES, LANES), lambda i: (i + tile_off, 0, 0))]
    body = functools.partial(_outproj_body, alpha=alpha, top_k=TOP_K, aliased=aliased)
    return pl.pallas_call(
        body,
        grid=(nt,),
        in_specs=in_specs,
        out_specs=out_specs,
        out_shape=out_shape,
        input_output_aliases=io_alias,
        compiler_params=_cparams("arbitrary"),
        name="outproj",
    )(*args)


def _select_mod(i, n_ptiles, seq_ref, tok_ref):
    return jnp.where(i < n_ptiles, seq_ref[...], tok_ref[...])


def _for_run(rows, local_off, global_off, fn):
    @pl.when(rows > 0)
    def _():
        fn(pl.multiple_of(local_off, RUN_ALIGN), pl.multiple_of(global_off, RUN_ALIGN),
           pl.multiple_of(rows, RUN_ALIGN))


def _for_each_run(i, n_experts, toff_ref, cnt8_ref, gbase_ref, fn):
    for e in range(n_experts):
        idx = i * n_experts + e
        _for_run(cnt8_ref[idx], toff_ref[idx], gbase_ref[idx], fn)


def _local_positions(eidx_ref, lrank_ref, toffv_ref, top_k):
    eidx = eidx_ref[...]
    lrank = lrank_ref[...]
    toffv = toffv_ref[...]
    lane = lax.broadcasted_iota(I32, eidx.shape, 1)
    pos = []
    for k in range(top_k):
        base = jnp.sum(jnp.where(lane == eidx[:, k:k + 1], toffv, 0.0), axis=1, keepdims=True)
        pos.append(base.astype(I32) + lrank[:, k:k + 1])
    return pos


def _dispatch_body(toff_ref, cnt8_ref, gbase_ref, tstart_ref, trows_ref,
                   x1_ref, shs_ref, scs_ref, sht_ref, sct_ref, eidx_ref, lrank_ref, toffv_ref,
                   xs_ref, lbuf, zbuf, sem, *, top_k, n_tiles, n_ptiles, n_experts):
    i = pl.program_id(0)
    sc = _select_mod(i, n_ptiles, scs_ref, sct_ref)
    sh = _select_mod(i, n_ptiles, shs_ref, sht_ref)
    h2 = (x1_ref[...] * (1.0 + sc) + sh).astype(BF16)
    tm = h2.shape[0]
    n_local = lbuf.shape[0]
    col = lax.broadcasted_iota(I32, (tm, n_local), 1)
    onehot = jnp.zeros((tm, n_local), F32)
    for pos in _local_positions(eidx_ref, lrank_ref, toffv_ref, top_k):
        onehot = onehot + jnp.where(col == pos, 1.0, 0.0)
    lbuf[...] = lax.dot_general(onehot.astype(BF16), h2, (((0,), (0,)), ((), ())), preferred_element_type=F32)

    def piece(lo, go, rows):
        return pltpu.make_async_copy(lbuf.at[pl.ds(lo, rows), :], xs_ref.at[pl.ds(go, rows), :], sem)

    _for_each_run(i, n_experts, toff_ref, cnt8_ref, gbase_ref, lambda lo, go, rows: piece(lo, go, rows).start())
    _for_each_run(i, n_experts, toff_ref, cnt8_ref, gbase_ref, lambda lo, go, rows: piece(lo, go, rows).wait())

    @pl.when(i == n_tiles - 1)
    def _():
        zbuf[...] = jnp.zeros(zbuf.shape, zbuf.dtype)

        def zero_piece(lo, go, rows):
            return pltpu.make_async_copy(zbuf.at[pl.ds(0, rows), :], xs_ref.at[pl.ds(go, rows), :], sem)

        for e in range(n_experts):
            _for_run(trows_ref[e], 0, tstart_ref[e], lambda lo, go, rows: zero_piece(lo, go, rows).start())
        for e in range(n_experts):
            _for_run(trows_ref[e], 0, tstart_ref[e], lambda lo, go, rows: zero_piece(lo, go, rows).wait())


def _dispatch(tables, x1, sh_seq, sc_seq, sh_tok, sc_tok, e_idx, lrank, toff_vec, *, n_rows, n_local, n_ptiles,
              tiles_per_seq, n_experts):
    t, d = x1.shape
    tm = TOKEN_TILE
    nt = t // tm
    n_seq = sh_seq.shape[0]
    body = functools.partial(_dispatch_body, top_k=TOP_K, n_tiles=nt, n_ptiles=n_ptiles, n_experts=n_experts)
    seq_spec = pl.BlockSpec((None, 1, d), lambda i, *_: (jnp.minimum(i // tiles_per_seq, n_seq - 1), 0, 0))
    tok_spec = pl.BlockSpec((tm, d), lambda i, *_: (jnp.maximum(i - n_ptiles, 0), 0))
    lane_spec = pl.BlockSpec((tm, LANES), lambda i, *_: (i, 0))
    return pl.pallas_call(
        body,
        grid_spec=pltpu.PrefetchScalarGridSpec(
            num_scalar_prefetch=len(tables),
            grid=(nt,),
            in_specs=[pl.BlockSpec((tm, d), lambda i, *_: (i, 0)),
                      seq_spec, seq_spec, tok_spec, tok_spec, lane_spec, lane_spec,
                      pl.BlockSpec((None, 1, LANES), lambda i, *_: (i, 0, 0))],
            out_specs=pl.BlockSpec(memory_space=pl.ANY),
            scratch_shapes=[pltpu.VMEM((n_local, d), F32), pltpu.VMEM((EXPERT_ROWS, d), F32),
                            pltpu.SemaphoreType.DMA(())],
        ),
        out_shape=jax.ShapeDtypeStruct((n_rows, d), F32),
        compiler_params=_cparams("arbitrary"),
        name="dispatch",
    )(*tables, x1, sh_seq, sc_seq, sh_tok, sc_tok, e_idx, lrank, toff_vec)


def _expert_body(be_ref, nu_ref, xs_ref, wgu_ref, bgu_ref, wd_ref, bd_ref, y_ref, *, de):
    @pl.when(pl.program_id(0) < nu_ref[0])
    def _():
        gu = jnp.dot(xs_ref[...].astype(BF16), wgu_ref[...], preferred_element_type=F32) + bgu_ref[...]
        glu = jnp.minimum(gu[:, :de], SWIGLU_LIMIT)
        lin = jnp.clip(gu[:, de:], -SWIGLU_LIMIT, SWIGLU_LIMIT)
        act = glu * jax.nn.sigmoid(SWIGLU_ALPHA * glu) * (lin + 1.0)
        y_ref[...] = jnp.dot(act.astype(BF16), wd_ref[...], preferred_element_type=F32) + bd_ref[...]


def _experts(block_e, n_used, xs, w_gu, b_gu, w_down, b_down):
    p, d = xs.shape
    bm = EXPERT_ROWS
    n_e, _, de2 = w_gu.shape
    de = de2 // 2
    nblk = p // bm

    def blk(i, be, nu):
        return jnp.minimum(i, nu[0] - 1)

    body = functools.partial(_expert_body, de=de)
    return pl.pallas_call(
        body,
        grid_spec=pltpu.PrefetchScalarGridSpec(
            num_scalar_prefetch=2,
            grid=(nblk,),
            in_specs=[pl.BlockSpec((bm, d), lambda i, be, nu: (blk(i, be, nu), 0)),
                      pl.BlockSpec((None, d, de2), lambda i, be, nu: (be[blk(i, be, nu)], 0, 0)),
                      pl.BlockSpec((None, 1, de2), lambda i, be, nu: (be[blk(i, be, nu)], 0, 0)),
                      pl.BlockSpec((None, de, d), lambda i, be, nu: (be[blk(i, be, nu)], 0, 0)),
                      pl.BlockSpec((None, 1, d), lambda i, be, nu: (be[blk(i, be, nu)], 0, 0))],
            out_specs=pl.BlockSpec((bm, d), lambda i, be, nu: (blk(i, be, nu), 0)),
        ),
        out_shape=jax.ShapeDtypeStruct((p, d), F32),
        compiler_params=_cparams("arbitrary"),
        name="experts",
    )(block_e, n_used, xs, w_gu, b_gu.reshape(n_e, 1, de2), w_down, b_down.reshape(n_e, 1, d))


def _combine_body(toff_ref, cnt8_ref, gbase_ref, x1_ref, gts_ref, gtt_ref, gate_ref, eidx_ref, lrank_ref, toffv_ref,
                  lnw_ref, lnb_ref, yb_ref, yp_ref, ys_ref, ybuf, sem, *, top_k, n_ptiles, n_experts, alpha):
    i = pl.program_id(0)
    tm = x1_ref.shape[0]
    n_local = ybuf.shape[0]
    ybuf[tm * top_k:, :] = jnp.zeros((n_local - tm * top_k, ybuf.shape[1]), ybuf.dtype)

    def piece(lo, go, rows):
        return pltpu.make_async_copy(yb_ref.at[pl.ds(go, rows), :], ybuf.at[pl.ds(lo, rows), :], sem)

    _for_each_run(i, n_experts, toff_ref, cnt8_ref, gbase_ref, lambda lo, go, rows: piece(lo, go, rows).start())
    gate = gate_ref[...]
    col = lax.broadcasted_iota(I32, (tm, n_local), 1)
    weights = jnp.zeros((tm, n_local), F32)
    for k, pos in enumerate(_local_positions(eidx_ref, lrank_ref, toffv_ref, top_k)):
        weights = weights + jnp.where(col == pos, gate[:, k:k + 1], 0.0)
    _for_each_run(i, n_experts, toff_ref, cnt8_ref, gbase_ref, lambda lo, go, rows: piece(lo, go, rows).wait())
    ff = _bdot(weights, ybuf[...])
    gt = _select_mod(i, n_ptiles, gts_ref, gtt_ref)
    y = _layer_norm(alpha * x1_ref[...] + gt * ff, lnw_ref[...], lnb_ref[...])

    @pl.when(i < n_ptiles)
    def _():
        yp_ref[...] = y

    @pl.when(i >= n_ptiles)
    def _():
        ys_ref[...] = y


def _combine(tables, x1, gt_seq, gt_tok, gate, e_idx, lrank, toff_vec, lnw, lnb, yb, *, n_local, n_ptiles,
             tiles_per_seq, n_experts, alpha):
    t, d = x1.shape
    tm = TOKEN_TILE
    nt = t // tm
    n_seq = gt_seq.shape[0]
    body = functools.partial(_combine_body, top_k=TOP_K, n_ptiles=n_ptiles, n_experts=n_experts, alpha=alpha)
    lane_spec = pl.BlockSpec((tm, LANES), lambda i, *_: (i, 0))
    return pl.pallas_call(
        body,
        grid_spec=pltpu.PrefetchScalarGridSpec(
            num_scalar_prefetch=len(tables),
            grid=(nt,),
            in_specs=[pl.BlockSpec((tm, d), lambda i, *_: (i, 0)),
                      pl.BlockSpec((None, 1, d), lambda i, *_: (jnp.minimum(i // tiles_per_seq, n_seq - 1), 0, 0)),
                      pl.BlockSpec((tm, d), lambda i, *_: (jnp.maximum(i - n_ptiles, 0), 0)),
                      lane_spec, lane_spec, lane_spec,
                      pl.BlockSpec((None, 1, LANES), lambda i, *_: (i, 0, 0)),
                      pl.BlockSpec((1, d), lambda i, *_: (0, 0)),
                      pl.BlockSpec((1, d), lambda i, *_: (0, 0)),
                      pl.BlockSpec(memory_space=pl.ANY)],
            out_specs=[pl.BlockSpec((tm, d), lambda i, *_: (jnp.minimum(i, n_ptiles - 1), 0)),
                       pl.BlockSpec((tm, d), lambda i, *_: (jnp.maximum(i - n_ptiles, 0), 0))],
            scratch_shapes=[pltpu.VMEM((n_local, d), F32), pltpu.SemaphoreType.DMA(())],
        ),
        out_shape=[jax.ShapeDtypeStruct((n_ptiles * tm, d), F32),
                   jax.ShapeDtypeStruct(((nt - n_ptiles) * tm, d), F32)],
        compiler_params=_cparams("arbitrary"),
        name="combine",
    )(*tables, x1, gt_seq, gt_tok, gate, e_idx, lrank, toff_vec, lnw, lnb, yb)


def _rotary_tables(pos, hd, rot_dim):
    half = rot_dim // 2
    inv_freq = jnp.power(jnp.float32(ROPE_THETA), -jnp.arange(half, dtype=F32) * (2.0 / rot_dim))
    ang = pos.astype(F32)[:, None] * inv_freq[None, :]
    cos, sin = jnp.cos(ang), jnp.sin(ang)
    n = pos.shape[0]
    ones = jnp.ones((n, hd - rot_dim), F32)
    zeros = jnp.zeros((n, hd - rot_dim), F32)
    zh = jnp.zeros((n, half), F32)
    ct = jnp.concatenate([cos, cos, ones], axis=1)
    s1 = jnp.concatenate([-sin, zh, zeros], axis=1)
    s2 = jnp.concatenate([zh, sin, zeros], axis=1)
    reps = LANES // hd
    return tuple(jnp.tile(a, (1, reps)) for a in (ct, s1, s2))


def kernel(x_prompt, x_sample, state_win_k, state_win_v, state_conv, state_ssm, c_prompt, c_sample, w_ada, b_ada, w_in, attn_sinks, w_conv, dn_a_log, dn_dt_bias, dn_norm_w, w_proj_attn, w_proj_dn, w_out, ln1_w, ln1_b, w_router, b_router, w_gu, b_gu, w_down, b_down, ln2_w, ln2_b):
    n_p, seq, d = x_prompt.shape
    n_s, l_s, _ = x_sample.shape
    depth = w_ada.shape[0]
    window, n_kv, hd = state_win_k.shape[2:]
    n_q = attn_sinks.shape[1]
    heads, dk, dv = state_ssm.shape[2:]
    conv_w, conv_dim = w_conv.shape[1:]
    n_e = w_router.shape[2]
    qd, kd = n_q * hd, n_kv * hd
    vdim = heads * dv
    rot_dim = hd // 4
    alpha = float((2 * depth) ** 0.25)
    tm = TOKEN_TILE
    t_p, t_s = n_p * seq, n_s * l_s
    t_all = t_p + t_s
    tps = seq // tm
    n_ptiles = t_p // tm
    l_pad = SUBLANES
    assert seq % tm == 0 and t_s % tm == 0 and tm % l_s == 0 and l_s <= l_pad and l_s >= conv_w - 1
    assert 2 * heads <= SUBLANES and n_e <= LANES and hd * 2 == LANES and rot_dim == 2 * SUBLANES

    sizes = [qd, kd, kd, conv_dim, vdim, heads, heads, d, d]
    offs = np.concatenate([[0], np.cumsum(sizes)])
    seg = lambda k: np.arange(offs[k], offs[k + 1])
    perm = np.concatenate([seg(0), seg(1), seg(2), seg(3), seg(4), seg(7), seg(8), seg(5), seg(6)])
    cuts = (qd + 2 * kd, qd + 2 * kd + conv_dim, qd + 2 * kd + conv_dim + vdim, qd + 2 * kd + conv_dim + vdim + 2 * d)
    cuts = cuts + (cuts[-1] + LANES,)

    tabs_p = _rotary_tables(jnp.arange(seq, dtype=I32), hd, rot_dim)
    tabs_s = tuple(jnp.tile(a, (tm // l_s, 1))
                   for a in _rotary_tables(PAST_LEN + jnp.arange(l_s, dtype=I32), hd, rot_dim))

    x_p = x_prompt.reshape(t_p, d)
    x_s = x_sample.reshape(t_s, d)
    c_all = jnp.concatenate([c_prompt, c_sample], axis=0)
    outs = {k: [] for k in ("pwk", "pwv", "pcv", "pss", "swk", "swv", "scv", "sss")}

    for l in range(depth):
        w_in_l = jnp.pad(w_in[l][:, perm], ((0, 0), (0, LANES - 2 * heads))).astype(BF16)
        mod = _ada(c_all, w_ada[l], b_ada[l])
        mod_p = mod[:n_p].reshape(n_p, 6, 1, d)
        mod_s = jnp.repeat(mod[n_p:].reshape(n_s, 6, d), l_s, axis=0)
        sh1p, sc1p, gt1p, sh2p, sc2p, gt2p = [mod_p[:, k] for k in range(6)]
        sh1s, sc1s, gt1s, sh2s, sc2s, gt2s = [mod_s[:, k] for k in range(6)]

        qkva_p, dn_p, z_p, g_p, ba_p, tail_p, kvw_p = _inproj(
            x_p, sh1p, sc1p, tabs_p, w_in_l, cuts, per_token_mod=False, tiles_per_seq=tps, act_dtype=BF16,
            window=window, kv_cols=2 * kd)
        attn_p = _attn_prompt(qkva_p, attn_sinks[l], n_p, seq, n_q, n_kv, hd, window)
        chunk = min(DN_CHUNK, seq)
        nc = seq // chunk
        bat_p = ba_p[:, :SUBLANES].reshape(n_p, nc, chunk, SUBLANES).transpose(0, 1, 3, 2)
        hp = jnp.zeros((SUBLANES, LANES), F32).at[0, :heads].set(dn_a_log[l]).at[1, :heads].set(dn_dt_bias[l])
        nw = dn_norm_w[l].reshape(1, dv)
        o_p, ssm_p = _deltanet(dn_p.reshape(n_p, seq, conv_dim), z_p.reshape(n_p, seq, vdim),
                               ba_p.reshape(n_p, seq, LANES), bat_p,
                               jnp.zeros((n_p, SUBLANES, conv_dim), F32), jnp.zeros((n_p, heads, dk, dv), F32),
                               w_conv[l], hp, nw, chunk=chunk, l_real=chunk, nb=4)
        outs["pwk"].append(kvw_p[:, :, :kd].reshape(n_p, window, n_kv, hd))
        outs["pwv"].append(kvw_p[:, :, kd:].reshape(n_p, window, n_kv, hd))
        outs["pcv"].append(tail_p.reshape(n_p, tps, SUBLANES, conv_dim)[:, -1, SUBLANES - (conv_w - 1):])
        outs["pss"].append(ssm_p)

        qkva_s, dn_s, z_s, g_s, ba_s = _inproj(
            x_s, sh1s, sc1s, tabs_s, w_in_l, cuts, per_token_mod=True, tiles_per_seq=1, act_dtype=F32,
            window=window, kv_cols=2 * kd)
        pad_l = lambda a: jnp.pad(a.reshape(n_s, l_s, a.shape[-1]), ((0, 0), (0, l_pad - l_s), (0, 0)))
        attn_s, wk_s, wv_s = _attn_sample(pad_l(qkva_s), state_win_k[l].reshape(n_s, window, kd),
                                          state_win_v[l].reshape(n_s, window, kd), attn_sinks[l], n_q, n_kv, hd, l_s)
        attn_s = attn_s[:, :l_s].reshape(t_s, qd)
        ba_s3 = pad_l(ba_s)
        bat_s = ba_s3[:, :, :SUBLANES].transpose(0, 2, 1).reshape(n_s, 1, SUBLANES, l_pad)
        cs0 = jnp.pad(state_conv[l], ((0, 0), (SUBLANES - (conv_w - 1), 0), (0, 0)))
        o_s, ssm_s = _deltanet(pad_l(dn_s), pad_l(z_s), ba_s3, bat_s, cs0, state_ssm[l], w_conv[l], hp, nw,
                               chunk=l_pad, l_real=l_s, nb=8)
        o_s = o_s[:, :l_s].reshape(t_s, vdim)
        outs["swk"].append(wk_s.reshape(n_s, window, n_kv, hd))
        outs["swv"].append(wv_s.reshape(n_s, window, n_kv, hd))
        outs["scv"].append(jnp.concatenate([state_conv[l], dn_s.reshape(n_s, l_s, conv_dim)], axis=1)[:, -(conv_w - 1):])
        outs["sss"].append(ssm_s)

        wr = jnp.pad(w_router[l], ((0, 0), (0, LANES - n_e)))
        br = jnp.pad(b_router[l], (0, LANES - n_e), constant_values=NEG_BIG).reshape(1, LANES)
        wts = (w_proj_attn[l].astype(BF16), w_proj_dn[l].astype(BF16), w_out[l].astype(BF16),
               ln1_w[l].reshape(1, d), ln1_b[l].reshape(1, d), wr, br)
        res_p = _outproj(attn_p, o_p.reshape(t_p, vdim), g_p, x_p, gt1p, sh2p, sc2p, wts, None,
                         per_token_mod=False, tiles_per_seq=tps, t_total=t_all, tile_off=0, alpha=alpha)
        x1, e_idx, gate, rank, cnt = _outproj(attn_s, o_s, g_s, x_s, gt1s, sh2s, sc2s, wts, res_p,
                                              per_token_mod=True, tiles_per_seq=1, t_total=t_all, tile_off=n_ptiles,
                                              alpha=alpha)

        bm = EXPERT_ROWS
        nt_all = t_all // tm
        cnt8 = (cnt[:, 0, :n_e].astype(I32) + RUN_ALIGN - 1) // RUN_ALIGN * RUN_ALIGN
        tot = jnp.sum(cnt8, axis=0)
        padded = (tot + bm - 1) // bm * bm
        pad_end = jnp.cumsum(padded)
        pad_start = pad_end - padded
        gbase = pad_start[None, :] + jnp.cumsum(cnt8, axis=0) - cnt8
        toff = jnp.cumsum(cnt8, axis=1) - cnt8
        n_local = -(-(tm * TOP_K + n_e * (RUN_ALIGN - 1)) // LANES) * LANES
        n_rows = -(-(t_all * TOP_K + nt_all * n_e * (RUN_ALIGN - 1) + n_e * (bm - 1)) // bm) * bm
        nblk = n_rows // bm
        n_used = jnp.maximum(pad_end[-1:] // bm, 1).astype(I32)
        block_e = jnp.minimum(jnp.sum(pad_end[None, :] <= (jnp.arange(nblk, dtype=I32) * bm)[:, None], axis=1),
                              n_e - 1).astype(I32)
        flat = lambda a: a.astype(I32).reshape(nt_all * n_e)
        run_tables = (flat(toff), flat(cnt8), flat(gbase))
        tail_tables = ((pad_start + tot).astype(I32), (padded - tot).astype(I32))
        toff_vec = jnp.pad(toff.astype(F32), ((0, 0), (0, LANES - n_e))).reshape(nt_all, 1, LANES)

        xs = _dispatch(run_tables + tail_tables, x1, sh2p, sc2p, sh2s, sc2s, e_idx, rank, toff_vec, n_rows=n_rows,
                       n_local=n_local, n_ptiles=n_ptiles, tiles_per_seq=tps, n_experts=n_e)
        yb = _experts(block_e, n_used, xs, w_gu[l].astype(BF16), b_gu[l], w_down[l].astype(BF16), b_down[l])
        x_p, x_s = _combine(run_tables, x1, gt2p, gt2s, gate, e_idx, rank, toff_vec, ln2_w[l].reshape(1, d),
                            ln2_b[l].reshape(1, d), yb, n_local=n_local, n_ptiles=n_ptiles, tiles_per_seq=tps,
                            n_experts=n_e, alpha=alpha)

    st = lambda k: jnp.stack(outs[k])
    return (x_p.reshape(n_p, seq, d), x_s.reshape(n_s, l_s, d), st("pwk"), st("pwv"), st("pcv"), st("pss"),
            st("swk"), st("swv"), st("scv"), st("sss"))
```

```python
import functools

import numpy as np
import jax
import jax.numpy as jnp
from jax import lax
from jax.experimental import pallas as pl
from jax.experimental.pallas import tpu as pltpu

F32 = jnp.float32
BF16 = jnp.bfloat16
I32 = jnp.int32
U32 = jnp.uint32
HIGHEST = lax.Precision.HIGHEST

PAST_LEN = 16384
ROPE_THETA = 500000.0
TOP_K = 4
SWIGLU_LIMIT = 7.0
SWIGLU_ALPHA = 1.702
DN_CHUNK = 64
LN_EPS = 1e-5
RMS_EPS = 1e-6
L2_EPS = 1e-6

LANES = 128
SUBLANES = 8
VMEM_LIMIT_BYTES = 56 * 1024 * 1024

TOKEN_TILE = 256
EXPERT_ROWS = 512
RUN_ALIGN = SUBLANES
N_ROUTE_BUFS = 5
NEG_BIG = -1e30


def _cparams(*sem):
    return pltpu.CompilerParams(dimension_semantics=sem, vmem_limit_bytes=VMEM_LIMIT_BYTES)


def _silu(x):
    return x * jax.nn.sigmoid(x)


def _bdot(a, b):
    return jnp.dot(a.astype(BF16), b.astype(BF16), preferred_element_type=F32)


def _pack_halves(x):
    n = x.shape[1] // 2
    lo = lax.bitcast_convert_type(x[:, :n], U32)
    hi = lax.bitcast_convert_type(x[:, n:], U32)
    return (hi & jnp.uint32(0xFFFF0000)) | (lo >> 16)


def _unpack_halves(w):
    lo = lax.bitcast_convert_type(w << 16, F32).astype(BF16)
    hi = lax.bitcast_convert_type(w & jnp.uint32(0xFFFF0000), F32).astype(BF16)
    return lo, hi


def _ada_body(c_ref, w_ref, b_ref, o_ref):
    o_ref[...] = _bdot(_silu(c_ref[...]), w_ref[...]) + b_ref[...]


def _ada(c_all, w_ada, b_ada):
    n, d = c_all.shape
    dout = w_ada.shape[1]
    tn = d
    return pl.pallas_call(
        _ada_body,
        grid=(dout // tn,),
        in_specs=[pl.BlockSpec((n, d), lambda j: (0, 0)),
                  pl.BlockSpec((d, tn), lambda j: (0, j)),
                  pl.BlockSpec((1, tn), lambda j: (0, j))],
        out_specs=pl.BlockSpec((n, tn), lambda j: (0, j)),
        out_shape=jax.ShapeDtypeStruct((n, dout), F32),
        compiler_params=_cparams("arbitrary"),
        name="ada",
    )(c_all, w_ada, b_ada.reshape(1, dout))


def _inproj_body(x_ref, sh_ref, sc_ref, ct_ref, s1_ref, s2_ref, w_ref,
                 a_ref, dn_ref, z_ref, g_ref, ba_ref, *win_refs, cuts, n_rot_chunks, window):
    h = (x_ref[...] * (1.0 + sc_ref[...]) + sh_ref[...]).astype(BF16)

    def mm(lo, hi):
        return jnp.dot(h, w_ref[:, lo:hi], preferred_element_type=F32)

    c_a, c_dn, c_z, c_g, c_ba = cuts
    qkv = mm(0, c_a)
    ct, s1, s2 = ct_ref[...], s1_ref[...], s2_ref[...]
    cols = []
    for c in range(n_rot_chunks):
        xc = qkv[:, c * LANES:(c + 1) * LANES]
        cols.append(xc * ct + pltpu.roll(xc, LANES - SUBLANES, 1) * s1 + pltpu.roll(xc, SUBLANES, 1) * s2)
    cols.append(qkv[:, n_rot_chunks * LANES:])
    rot = jnp.concatenate(cols, axis=1)
    a_ref[...] = rot.astype(a_ref.dtype)
    dn = mm(c_a, c_dn)
    dn_ref[...] = dn.astype(dn_ref.dtype)
    z_ref[...] = mm(c_dn, c_z).astype(z_ref.dtype)
    g_ref[...] = mm(c_z, c_g).astype(g_ref.dtype)
    ba_ref[...] = mm(c_g, c_ba)
    if win_refs:
        tail_ref, kvw_ref = win_refs
        tm = dn.shape[0]
        tail_ref[...] = dn[tm - SUBLANES:, :]
        kvw_ref[...] = rot[tm - window:, n_rot_chunks * LANES - LANES:]


def _inproj(x, sh, sc, tabs, w_perm, cuts, *, per_token_mod, tiles_per_seq, act_dtype, window, kv_cols):
    t, d = x.shape
    tm = TOKEN_TILE
    nt = t // tm
    c_a, c_dn, c_z, c_g, c_ba = cuts
    n_rot_chunks = (c_a - kv_cols // 2) // LANES
    if per_token_mod:
        mod_spec = pl.BlockSpec((tm, d), lambda i: (i, 0))
        tab_spec = pl.BlockSpec((tm, LANES), lambda i: (0, 0))
    else:
        mod_spec = pl.BlockSpec((None, 1, d), lambda i: (i // tiles_per_seq, 0, 0))
        tab_spec = pl.BlockSpec((tm, LANES), lambda i: (i % tiles_per_seq, 0))
    out_shape = [jax.ShapeDtypeStruct((t, c_a), act_dtype),
                 jax.ShapeDtypeStruct((t, c_dn - c_a), act_dtype),
                 jax.ShapeDtypeStruct((t, c_z - c_dn), act_dtype),
                 jax.ShapeDtypeStruct((t, c_g - c_z), act_dtype),
                 jax.ShapeDtypeStruct((t, c_ba - c_g), F32)]
    out_specs = [pl.BlockSpec((tm, s.shape[1]), lambda i: (i, 0)) for s in out_shape]
    with_win = not per_token_mod
    if with_win:
        n_seq = nt // tiles_per_seq
        out_shape += [jax.ShapeDtypeStruct((nt, SUBLANES, c_dn - c_a), F32),
                      jax.ShapeDtypeStruct((n_seq, window, kv_cols), F32)]
        out_specs += [pl.BlockSpec((None, SUBLANES, c_dn - c_a), lambda i: (i, 0, 0)),
                      pl.BlockSpec((None, window, kv_cols), lambda i: (i // tiles_per_seq, 0, 0))]
    body = functools.partial(_inproj_body, cuts=cuts, n_rot_chunks=n_rot_chunks, window=window)
    return pl.pallas_call(
        body,
        grid=(nt,),
        in_specs=[pl.BlockSpec((tm, d), lambda i: (i, 0)), mod_spec, mod_spec,
                  tab_spec, tab_spec, tab_spec,
                  pl.BlockSpec((d, c_ba), lambda i: (0, 0))],
        out_specs=out_specs,
        out_shape=out_shape,
        compiler_params=_cparams("arbitrary"),
        name="inproj",
    )(x, sh, sc, *tabs, w_perm)


def _softmax_sink_pv(s, valid, sink, v):
    s = jnp.where(valid, s, -jnp.inf)
    m = jnp.maximum(jnp.max(s, axis=-1, keepdims=True), sink)
    p = jnp.exp(s - m)
    denom = jnp.sum(p, axis=-1, keepdims=True) + jnp.exp(sink - m)
    return jnp.dot((p / denom).astype(BF16), v, preferred_element_type=F32)


def _attn_prompt_body(sink_ref, q_ref, kvp_ref, kvc_ref, o_ref, *, n_q, n_kv, hd, window):
    j = pl.program_id(1)
    group = n_q // n_kv
    q = q_ref[...]
    kv = jnp.concatenate([kvp_ref[...], kvc_ref[...]], axis=0)
    r = lax.broadcasted_iota(I32, (window, 2 * window), 0)
    c = lax.broadcasted_iota(I32, (window, 2 * window), 1)
    rel = window + r - c
    valid = (rel >= 0) & (rel < window) & ((c >= window) | (j > 0))
    outs = []
    for h in range(n_q):
        kvh = h // group
        qh = q[:, h * hd:(h + 1) * hd]
        kh = kv[:, kvh * hd:(kvh + 1) * hd]
        vh = kv[:, (n_kv + kvh) * hd:(n_kv + kvh + 1) * hd]
        s = lax.dot_general(qh, kh, (((1,), (1,)), ((), ())), preferred_element_type=F32) * (hd ** -0.5)
        outs.append(_softmax_sink_pv(s, valid, sink_ref[h], vh))
    o_ref[...] = jnp.concatenate(outs, axis=1).astype(o_ref.dtype)


def _attn_prompt(qkva, sinks, n_seq, seq, n_q, n_kv, hd, window):
    qd, kvd = n_q * hd, 2 * n_kv * hd
    x3 = qkva.reshape(n_seq, seq, qd + kvd)
    nb = seq // window
    kv_blk = qd // kvd
    body = functools.partial(_attn_prompt_body, n_q=n_q, n_kv=n_kv, hd=hd, window=window)
    out = pl.pallas_call(
        body,
        grid=(n_seq, nb),
        in_specs=[pl.BlockSpec(memory_space=pltpu.SMEM),
                  pl.BlockSpec((None, window, qd), lambda n, j: (n, j, 0)),
                  pl.BlockSpec((None, window, kvd), lambda n, j: (n, jnp.maximum(j - 1, 0), kv_blk)),
                  pl.BlockSpec((None, window, kvd), lambda n, j: (n, j, kv_blk))],
        out_specs=pl.BlockSpec((None, window, qd), lambda n, j: (n, j, 0)),
        out_shape=jax.ShapeDtypeStruct((n_seq, seq, qd), BF16),
        compiler_params=_cparams("arbitrary", "arbitrary"),
        name="attn_prompt",
    )(sinks, x3, x3, x3)
    return out.reshape(n_seq * seq, qd)


def _attn_sample_body(sink_ref, q_ref, wk_ref, wv_ref, o_ref, wko_ref, wvo_ref,
                      *, bs, n_q, n_kv, hd, window, l_new, l_pad):
    group = n_q // n_kv
    qd = n_q * hd
    kd = n_kv * hd
    rows = group * l_pad
    r = lax.broadcasted_iota(I32, (rows, window + l_pad), 0) % l_pad
    c = lax.broadcasted_iota(I32, (rows, window + l_pad), 1)
    rel = window + r - c
    valid = (rel >= 0) & (rel < window) & (c < window + l_new)
    for b in range(bs):
        x = q_ref[b]
        k_new = x[:, qd:qd + kd]
        v_new = x[:, qd + kd:]
        wko_ref[b, 0:window - l_new, :] = wk_ref[b, l_new:window, :]
        wko_ref[b, window - l_new:window, :] = k_new[0:l_new, :]
        wvo_ref[b, 0:window - l_new, :] = wv_ref[b, l_new:window, :]
        wvo_ref[b, window - l_new:window, :] = v_new[0:l_new, :]
        k_all = jnp.concatenate([wk_ref[b], k_new], axis=0).astype(BF16)
        v_all = jnp.concatenate([wv_ref[b], v_new], axis=0).astype(BF16)
        outs = [None] * n_q
        for kvh in range(n_kv):
            qg = jnp.concatenate([x[:, (kvh * group + g) * hd:(kvh * group + g + 1) * hd] for g in range(group)],
                                 axis=0).astype(BF16)
            kh = k_all[:, kvh * hd:(kvh + 1) * hd]
            vh = v_all[:, kvh * hd:(kvh + 1) * hd]
            s = lax.dot_general(qg, kh, (((1,), (1,)), ((), ())), preferred_element_type=F32) * (hd ** -0.5)
            sink = jnp.concatenate([jnp.full((l_pad, 1), sink_ref[kvh * group + g], F32) for g in range(group)], axis=0)
            o = _softmax_sink_pv(s, valid, sink, vh)
            for g in range(group):
                outs[kvh * group + g] = o[g * l_pad:(g + 1) * l_pad, :]
        o_ref[b] = jnp.concatenate(outs, axis=1).astype(o_ref.dtype)


def _attn_sample(qkva_pad, win_k, win_v, sinks, n_q, n_kv, hd, l_new):
    n, l_pad, width = qkva_pad.shape
    window, kd = win_k.shape[1], win_k.shape[2]
    qd = n_q * hd
    bs = 8
    body = functools.partial(_attn_sample_body, bs=bs, n_q=n_q, n_kv=n_kv, hd=hd, window=window,
                             l_new=l_new, l_pad=l_pad)
    return pl.pallas_call(
        body,
        grid=(n // bs,),
        in_specs=[pl.BlockSpec(memory_space=pltpu.SMEM),
                  pl.BlockSpec((bs, l_pad, width), lambda i: (i, 0, 0)),
                  pl.BlockSpec((bs, window, kd), lambda i: (i, 0, 0)),
                  pl.BlockSpec((bs, window, kd), lambda i: (i, 0, 0))],
        out_specs=[pl.BlockSpec((bs, l_pad, qd), lambda i: (i, 0, 0)),
                   pl.BlockSpec((bs, window, kd), lambda i: (i, 0, 0)),
                   pl.BlockSpec((bs, window, kd), lambda i: (i, 0, 0))],
        out_shape=[jax.ShapeDtypeStruct((n, l_pad, qd), BF16),
                   jax.ShapeDtypeStruct((n, window, kd), F32),
                   jax.ShapeDtypeStruct((n, window, kd), F32)],
        compiler_params=_cparams("arbitrary"),
        name="attn_sample",
    )(sinks, qkva_pad, win_k, win_v)


def _split_bf16(x):
    hi = x.astype(BF16)
    return hi, (x - hi.astype(F32)).astype(BF16)


def _tdot(a, b):
    ah, al = _split_bf16(a)
    bh, bl = _split_bf16(b)
    m = a.shape[0]
    t = jnp.dot(jnp.concatenate([ah, al], axis=0), bh, preferred_element_type=F32)
    return t[:m] + t[m:] + jnp.dot(ah, bl, preferred_element_type=F32)


def _dn_body(qkv_ref, z_ref, ba_ref, bat_ref, cs0_ref, s0_ref, wc_ref, hp_ref, nw_ref,
             o_ref, s_ref, xbuf, *, nb, chunk, heads, dk, dv, l_real, conv_w):
    c_idx = pl.program_id(1)
    hc = SUBLANES

    @pl.when(c_idx == 0)
    def _():
        xbuf[:, 0:hc, :] = cs0_ref[...]
        s_ref[...] = s0_ref[...]

    qk_dim = heads * dk
    row = lax.broadcasted_iota(I32, (chunk, chunk), 0)
    col = lax.broadcasted_iota(I32, (chunk, chunk), 1)
    incl = row >= col
    strict = row > col
    eye = (row == col).astype(F32)
    valid_c = row[:, 0:1] < l_real
    valid_r = col[0:1, :] < l_real
    n_levels = max(1, int(np.ceil(np.log2(chunk))))
    wc = wc_ref[...]
    hp = hp_ref[...]
    neg_exp_alog = -jnp.exp(hp[0:1, :])
    dt_bias = hp[1:2, :]
    nw = nw_ref[...]
    chains = [(b, h) for b in range(nb) for h in range(heads)]
    n = len(chains)

    ys = []
    for b in range(nb):
        xbuf[b, hc:hc + chunk, :] = qkv_ref[b].astype(F32)
        y = xbuf[b, hc:hc + chunk, :] * wc[conv_w - 1:conv_w, :]
        for j in range(conv_w - 1):
            off = hc - (conv_w - 1) + j
            y = y + xbuf[b, off:off + chunk, :] * wc[j:j + 1, :]
        ys.append(_silu(y))
        xbuf[b, 0:hc, :] = xbuf[b, chunk:chunk + hc, :]

    qn, kn, kb, vb, decay, e_gc, e_rest, e_last = [], [], [], [], [], [], [], []
    for b, h in chains:
        y = ys[b]
        qh = y[:, h * dk:(h + 1) * dk]
        kh = y[:, qk_dim + h * dk:qk_dim + (h + 1) * dk]
        vh = y[:, 2 * qk_dim + h * dv:2 * qk_dim + (h + 1) * dv]
        ba = ba_ref[b]
        bat = bat_ref[b]
        ne = neg_exp_alog[:, h:h + 1]
        db = dt_bias[:, h:h + 1]
        beta = jnp.where(valid_c, jax.nn.sigmoid(ba[:, h:h + 1]), 0.0)
        g_col = jnp.where(valid_c, ne * jax.nn.softplus(ba[:, heads + h:heads + h + 1] + db), 0.0)
        g_row = jnp.where(valid_r, ne * jax.nn.softplus(bat[heads + h:heads + h + 1, :] + db), 0.0)
        gc_col = jnp.sum(jnp.where(incl, g_row, 0.0), axis=1, keepdims=True)
        gc_row = jnp.sum(jnp.where(row <= col, g_col, 0.0), axis=0, keepdims=True)
        g_last = gc_col[chunk - 1:chunk, :]
        q_ = qh * lax.rsqrt(jnp.sum(qh * qh, -1, keepdims=True) + L2_EPS) * (dk ** -0.5)
        k_ = kh * lax.rsqrt(jnp.sum(kh * kh, -1, keepdims=True) + L2_EPS)
        qn.append(q_)
        kn.append(k_)
        kb.append(k_ * beta)
        vb.append(vh * beta)
        decay.append(jnp.where(incl, jnp.exp(gc_col - gc_row), 0.0))
        e_gc.append(jnp.exp(gc_col))
        e_rest.append(jnp.exp(g_last - gc_col))
        e_last.append(jnp.exp(g_last))

    sc = [lax.dot_general(jnp.concatenate([qn[i], kb[i]], axis=0).astype(BF16), kn[i].astype(BF16),
                          (((1,), (1,)), ((), ())), preferred_element_type=F32) for i in range(n)]
    qk = [sc[i][:chunk] * decay[i] for i in range(n)]
    p = [jnp.where(strict, -(sc[i][chunk:] * decay[i]), 0.0) for i in range(n)]
    t_inv = [eye + p[i] for i in range(n)]
    if n_levels > 1:
        p = [_tdot(p[i], p[i]) for i in range(n)]
    for lvl in range(1, n_levels):
        if lvl < n_levels - 1:
            yp = [_tdot(jnp.concatenate([t_inv[i], p[i]], axis=0), p[i]) for i in range(n)]
            t_inv = [t_inv[i] + yp[i][:chunk] for i in range(n)]
            p = [yp[i][chunk:] for i in range(n)]
        else:
            t_inv = [t_inv[i] + _tdot(t_inv[i], p[i]) for i in range(n)]
    sol = [_tdot(t_inv[i], jnp.concatenate([vb[i], kb[i] * e_gc[i]], axis=1)) for i in range(n)]
    s_old = [s_ref[b, h] for b, h in chains]
    wq = [_bdot(jnp.concatenate([sol[i][:, dv:], qn[i] * e_gc[i]], axis=0), s_old[i]) for i in range(n)]
    v_new = [sol[i][:, :dv] - wq[i][:chunk] for i in range(n)]
    o = [wq[i][chunk:] + _bdot(qk[i], v_new[i]) for i in range(n)]
    for i, (b, h) in enumerate(chains):
        s_ref[b, h] = s_old[i] * e_last[i] + lax.dot_general(
            (kn[i] * e_rest[i]).astype(BF16), v_new[i].astype(BF16), (((0,), (0,)), ((), ())),
            preferred_element_type=F32)
    for b in range(nb):
        zt = z_ref[b].astype(F32)
        outs = []
        for h in range(heads):
            oi = o[b * heads + h]
            on = oi * lax.rsqrt(jnp.mean(oi * oi, -1, keepdims=True) + RMS_EPS) * nw
            outs.append(on * _silu(zt[:, h * dv:(h + 1) * dv]))
        o_ref[b] = jnp.concatenate(outs, axis=1).astype(o_ref.dtype)


def _deltanet(qkv, z, ba, bat, cs0, s0, w_conv, hp, norm_w, *, chunk, l_real, nb):
    n, l, conv_dim = qkv.shape
    heads, dk, dv = s0.shape[1:]
    nc = l // chunk
    assert n % nb == 0 and l % chunk == 0
    conv_w = w_conv.shape[0]
    body = functools.partial(_dn_body, nb=nb, chunk=chunk, heads=heads, dk=dk, dv=dv, l_real=l_real, conv_w=conv_w)
    return pl.pallas_call(
        body,
        grid=(n // nb, nc),
        in_specs=[pl.BlockSpec((nb, chunk, conv_dim), lambda i, c: (i, c, 0)),
                  pl.BlockSpec((nb, chunk, heads * dv), lambda i, c: (i, c, 0)),
                  pl.BlockSpec((nb, chunk, LANES), lambda i, c: (i, c, 0)),
                  pl.BlockSpec((nb, None, SUBLANES, chunk), lambda i, c: (i, c, 0, 0)),
                  pl.BlockSpec((nb, SUBLANES, conv_dim), lambda i, c: (i, 0, 0)),
                  pl.BlockSpec((nb, heads, dk, dv), lambda i, c: (i, 0, 0, 0)),
                  pl.BlockSpec((conv_w, conv_dim), lambda i, c: (0, 0)),
                  pl.BlockSpec((SUBLANES, LANES), lambda i, c: (0, 0)),
                  pl.BlockSpec((1, dv), lambda i, c: (0, 0))],
        out_specs=[pl.BlockSpec((nb, chunk, heads * dv), lambda i, c: (i, c, 0)),
                   pl.BlockSpec((nb, heads, dk, dv), lambda i, c: (i, 0, 0, 0))],
        out_shape=[jax.ShapeDtypeStruct((n, l, heads * dv), BF16),
                   jax.ShapeDtypeStruct((n, heads, dk, dv), F32)],
        scratch_shapes=[pltpu.VMEM((nb, SUBLANES + chunk, conv_dim), F32)],
        compiler_params=_cparams("arbitrary", "arbitrary"),
        name="deltanet",
    )(qkv, z, ba, bat, cs0, s0, w_conv, hp, norm_w)


def _layer_norm(r, w, b):
    mu = jnp.mean(r, -1, keepdims=True)
    var = jnp.mean(jnp.square(r - mu), -1, keepdims=True)
    return (r - mu) * lax.rsqrt(var + LN_EPS) * w + b


def _outproj_body(attn_ref, dn_ref, g_ref, x_ref, gt_ref, sh2_ref, sc2_ref, wpa_ref, wpd_ref, wo_ref,
                  lnw_ref, lnb_ref, wr_ref, br_ref, *refs, alpha, top_k, aliased):
    if aliased:
        refs = refs[N_ROUTE_BUFS:]
    x1_ref, eidx_ref, gate_ref, rank_ref, cnt_ref = refs
    d = x_ref.shape[1]
    tm = x_ref.shape[0]
    g = g_ref[...].astype(F32)
    pa = jnp.dot(attn_ref[...], wpa_ref[...], preferred_element_type=F32)
    pd = jnp.dot(dn_ref[...], wpd_ref[...], preferred_element_type=F32)
    merged = jax.nn.sigmoid(g[:, :d]) * pa + jax.nn.sigmoid(g[:, d:]) * pd
    mix = jnp.dot(merged.astype(BF16), wo_ref[...], preferred_element_type=F32)
    x1 = _layer_norm(alpha * x_ref[...] + gt_ref[...] * mix, lnw_ref[...], lnb_ref[...])
    x1_ref[...] = x1
    h2 = x1 * (1.0 + sc2_ref[...]) + sh2_ref[...]
    logits = jnp.dot(h2, wr_ref[...], preferred_element_type=F32, precision=HIGHEST) + br_ref[...]
    lane = lax.broadcasted_iota(I32, (tm, LANES), 1)
    lane_f = lane.astype(F32)
    vals, idxs, sels = [], [], []
    l = logits
    for _ in range(top_k):
        m = jnp.max(l, axis=1, keepdims=True)
        idx = jnp.min(jnp.where(l == m, lane_f, float(LANES)), axis=1, keepdims=True)
        sel = lane_f == idx
        vals.append(m)
        idxs.append(idx)
        sels.append(sel)
        l = jnp.where(sel, -jnp.inf, l)
    ex = [jnp.exp(v - vals[0]) for v in vals]
    den = ex[0]
    for e in ex[1:]:
        den = den + e
    multi_hot = jnp.zeros((tm, LANES), F32)
    for sel in sels:
        multi_hot = multi_hot + jnp.where(sel, 1.0, 0.0)
    r_i = lax.broadcasted_iota(I32, (tm, tm), 0)
    c_i = lax.broadcasted_iota(I32, (tm, tm), 1)
    lower = jnp.where(r_i > c_i, 1.0, 0.0).astype(BF16)
    prefix = jnp.dot(lower, multi_hot.astype(BF16), preferred_element_type=F32)
    e_out = jnp.zeros((tm, LANES), F32)
    g_out = jnp.zeros((tm, LANES), F32)
    r_out = jnp.zeros((tm, LANES), F32)
    for k in range(top_k):
        rank_k = jnp.sum(jnp.where(sels[k], prefix, 0.0), axis=1, keepdims=True)
        e_out = jnp.where(lane == k, idxs[k], e_out)
        g_out = jnp.where(lane == k, ex[k] / den, g_out)
        r_out = jnp.where(lane == k, rank_k, r_out)
    eidx_ref[...] = e_out.astype(I32)
    gate_ref[...] = g_out
    rank_ref[...] = r_out.astype(I32)
    cnt_ref[...] = jnp.broadcast_to(jnp.sum(multi_hot, axis=0, keepdims=True), cnt_ref.shape)


def _outproj(attn, dn, gates, x, gt, sh2, sc2, wts, bufs, *, per_token_mod, tiles_per_seq, t_total, tile_off, alpha):
    t, d = x.shape
    tm = TOKEN_TILE
    nt = t // tm
    wpa, wpd, wo, lnw, lnb, wr, br = wts
    if per_token_mod:
        mod_spec = pl.BlockSpec((tm, d), lambda i: (i, 0))
    else:
        mod_spec = pl.BlockSpec((None, 1, d), lambda i: (i // tiles_per_seq, 0, 0))

    def row(width):
        return pl.BlockSpec((tm, width), lambda i: (i, 0))

    def full(a):
        return pl.BlockSpec(a.shape, lambda i: (0,) * a.ndim)

    aliased = bufs is not None
    in_specs = [row(attn.shape[1]), row(dn.shape[1]), row(gates.shape[1]), row(d), mod_spec, mod_spec, mod_spec,
                full(wpa), full(wpd), full(wo), full(lnw), full(lnb), full(wr), full(br)]
    args = [attn, dn, gates, x, gt, sh2, sc2, wpa, wpd, wo, lnw, lnb, wr, br]
    io_alias = {}
    if aliased:
        for k, bfr in enumerate(bufs):
            in_specs.append(pl.BlockSpec(memory_space=pl.ANY))
            io_alias[len(args)] = k
            args.append(bfr)
    out_row = lambda width: pl.BlockSpec((tm, width), lambda i: (i + tile_off, 0))
    out_shape = [jax.ShapeDtypeStruct((t_total, d), F32), jax.ShapeDtypeStruct((t_total, LANES), I32),
                 jax.ShapeDtypeStruct((t_total, LANES), F32), jax.ShapeDtypeStruct((t_total, LANES), I32),
                 jax.ShapeDtypeStruct((t_total // tm, SUBLANES, LANES), F32)]
    assert len(out_shape) == N_ROUTE_BUFS
    out_specs = [out_row(d), out_row(LANES), out_row(LANES), out_row(LANES),
                 pl.BlockSpec((None, SUBLANES, LANES), lambda i: (i + tile_off, 0, 0))]
    body = functools.partial(_outproj_body, alpha=alpha, top_k=TOP_K, aliased=aliased)
    return pl.pallas_call(
        body,
        grid=(nt,),
        in_specs=in_specs,
        out_specs=out_specs,
        out_shape=out_shape,
        input_output_aliases=io_alias,
        compiler_params=_cparams("arbitrary"),
        name="outproj",
    )(*args)


def _select_mod(i, n_ptiles, seq_ref, tok_ref):
    return jnp.where(i < n_ptiles, seq_ref[...], tok_ref[...])


def _for_run(rows, local_off, global_off, fn):
    @pl.when(rows > 0)
    def _():
        fn(pl.multiple_of(local_off, RUN_ALIGN), pl.multiple_of(global_off, RUN_ALIGN),
           pl.multiple_of(rows, RUN_ALIGN))


def _for_each_run(i, n_experts, toff_ref, cnt8_ref, gbase_ref, fn):
    for e in range(n_experts):
        idx = i * n_experts + e
        _for_run(cnt8_ref[idx], toff_ref[idx], gbase_ref[idx], fn)


def _local_positions(eidx_ref, lrank_ref, toffv_ref, top_k):
    eidx = eidx_ref[...]
    lrank = lrank_ref[...]
    toffv = toffv_ref[...]
    lane = lax.broadcasted_iota(I32, eidx.shape, 1)
    pos = []
    for k in range(top_k):
        base = jnp.sum(jnp.where(lane == eidx[:, k:k + 1], toffv, 0.0), axis=1, keepdims=True)
        pos.append(base.astype(I32) + lrank[:, k:k + 1])
    return pos


def _dispatch_body(toff_ref, cnt8_ref, gbase_ref, tstart_ref, trows_ref,
                   x1_ref, shs_ref, scs_ref, sht_ref, sct_ref, eidx_ref, lrank_ref, toffv_ref,
                   xs_ref, lbuf, zbuf, sem, *, top_k, n_tiles, n_ptiles, n_experts):
    i = pl.program_id(0)
    sc = _select_mod(i, n_ptiles, scs_ref, sct_ref)
    sh = _select_mod(i, n_ptiles, shs_ref, sht_ref)
    h2 = (x1_ref[...] * (1.0 + sc) + sh).astype(BF16)
    tm = h2.shape[0]
    n_local = lbuf.shape[1]
    slot = i % 2

    def runs(tile, sl, act):
        def fn(lo, go, rows):
            act(pltpu.make_async_copy(lbuf.at[sl, pl.ds(lo, rows), :], xs_ref.at[pl.ds(go, rows), :], sem.at[sl]))
        _for_each_run(tile, n_experts, toff_ref, cnt8_ref, gbase_ref, fn)

    start = lambda cp: cp.start()
    wait = lambda cp: cp.wait()

    @pl.when(i >= 2)
    def _():
        runs(i - 2, slot, wait)

    col = lax.broadcasted_iota(I32, (tm, n_local), 1)
    onehot = jnp.zeros((tm, n_local), F32)
    for pos in _local_positions(eidx_ref, lrank_ref, toffv_ref, top_k):
        onehot = onehot + jnp.where(col == pos, 1.0, 0.0)
    srt = lax.dot_general(onehot.astype(BF16), h2, (((0,), (0,)), ((), ())), preferred_element_type=F32)
    lbuf[slot] = _pack_halves(srt)
    runs(i, slot, start)

    @pl.when(i == n_tiles - 1)
    def _():
        runs(i - 1, 1 - slot, wait)
        runs(i, slot, wait)
        zbuf[...] = jnp.zeros(zbuf.shape, zbuf.dtype)

        def zero_fill(e, act):
            _for_run(trows_ref[e], 0, tstart_ref[e], lambda lo, go, rows: act(pltpu.make_async_copy(
                zbuf.at[pl.ds(0, rows), :], xs_ref.at[pl.ds(go, rows), :], sem.at[0])))

        for e in range(n_experts):
            zero_fill(e, start)
        for e in range(n_experts):
            zero_fill(e, wait)


def _dispatch(tables, x1, sh_seq, sc_seq, sh_tok, sc_tok, e_idx, lrank, toff_vec, *, n_rows, n_local, n_ptiles,
              tiles_per_seq, n_experts):
    t, d = x1.shape
    tm = TOKEN_TILE
    nt = t // tm
    n_seq = sh_seq.shape[0]
    assert nt >= 2 and d % 2 == 0
    body = functools.partial(_dispatch_body, top_k=TOP_K, n_tiles=nt, n_ptiles=n_ptiles, n_experts=n_experts)
    seq_spec = pl.BlockSpec((None, 1, d), lambda i, *_: (jnp.minimum(i // tiles_per_seq, n_seq - 1), 0, 0))
    tok_spec = pl.BlockSpec((tm, d), lambda i, *_: (jnp.maximum(i - n_ptiles, 0), 0))
    lane_spec = pl.BlockSpec((tm, LANES), lambda i, *_: (i, 0))
    return pl.pallas_call(
        body,
        grid_spec=pltpu.PrefetchScalarGridSpec(
            num_scalar_prefetch=len(tables),
            grid=(nt,),
            in_specs=[pl.BlockSpec((tm, d), lambda i, *_: (i, 0)),
                      seq_spec, seq_spec, tok_spec, tok_spec, lane_spec, lane_spec,
                      pl.BlockSpec((None, 1, LANES), lambda i, *_: (i, 0, 0))],
            out_specs=pl.BlockSpec(memory_space=pl.ANY),
            scratch_shapes=[pltpu.VMEM((2, n_local, d // 2), U32), pltpu.VMEM((EXPERT_ROWS, d // 2), U32),
                            pltpu.SemaphoreType.DMA((2,))],
        ),
        out_shape=jax.ShapeDtypeStruct((n_rows, d // 2), U32),
        compiler_params=_cparams("arbitrary"),
        name="dispatch",
    )(*tables, x1, sh_seq, sc_seq, sh_tok, sc_tok, e_idx, lrank, toff_vec)


def _expert_body(be_ref, nu_ref, xs_ref, wgu_ref, bgu_ref, wd_ref, bd_ref, y_ref, *, de):
    @pl.when(pl.program_id(0) < nu_ref[0])
    def _():
        x_lo, x_hi = _unpack_halves(xs_ref[...])
        half = x_lo.shape[1]
        gu = (jnp.dot(x_lo, wgu_ref[:half, :], preferred_element_type=F32)
              + jnp.dot(x_hi, wgu_ref[half:, :], preferred_element_type=F32) + bgu_ref[...])
        glu = jnp.minimum(gu[:, :de], SWIGLU_LIMIT)
        lin = jnp.clip(gu[:, de:], -SWIGLU_LIMIT, SWIGLU_LIMIT)
        act = glu * jax.nn.sigmoid(SWIGLU_ALPHA * glu) * (lin + 1.0)
        y = jnp.dot(act.astype(BF16), wd_ref[...], preferred_element_type=F32) + bd_ref[...]
        y_ref[...] = _pack_halves(y.astype(BF16).astype(F32))


def _experts(block_e, n_used, xs, w_gu, b_gu, w_down, b_down):
    p, dh = xs.shape
    bm = EXPERT_ROWS
    n_e, d, de2 = w_gu.shape
    de = de2 // 2
    nblk = p // bm

    def blk(i, be, nu):
        return jnp.minimum(i, nu[0] - 1)

    body = functools.partial(_expert_body, de=de)
    return pl.pallas_call(
        body,
        grid_spec=pltpu.PrefetchScalarGridSpec(
            num_scalar_prefetch=2,
            grid=(nblk,),
            in_specs=[pl.BlockSpec((bm, dh), lambda i, be, nu: (blk(i, be, nu), 0)),
                      pl.BlockSpec((None, d, de2), lambda i, be, nu: (be[blk(i, be, nu)], 0, 0)),
                      pl.BlockSpec((None, 1, de2), lambda i, be, nu: (be[blk(i, be, nu)], 0, 0)),
                      pl.BlockSpec((None, de, d), lambda i, be, nu: (be[blk(i, be, nu)], 0, 0)),
                      pl.BlockSpec((None, 1, d), lambda i, be, nu: (be[blk(i, be, nu)], 0, 0))],
            out_specs=pl.BlockSpec((bm, dh), lambda i, be, nu: (blk(i, be, nu), 0)),
        ),
        out_shape=jax.ShapeDtypeStruct((p, dh), U32),
        compiler_params=_cparams("arbitrary"),
        name="experts",
    )(block_e, n_used, xs, w_gu, b_gu.reshape(n_e, 1, de2), w_down, b_down.reshape(n_e, 1, d))


def _combine_body(toff_ref, cnt8_ref, gbase_ref, x1_ref, gts_ref, gtt_ref, gate_ref, eidx_ref, lrank_ref, toffv_ref,
                  lnw_ref, lnb_ref, yb_ref, yp_ref, ys_ref, ybuf, sem, *, top_k, n_ptiles, n_experts, alpha):
    i = pl.program_id(0)
    n_tiles = pl.num_programs(0)
    tm = x1_ref.shape[0]
    n_local = ybuf.shape[1]
    slot = i % 2

    def fetch(tile, sl, act):
        def fn(lo, go, rows):
            act(pltpu.make_async_copy(yb_ref.at[pl.ds(go, rows), :], ybuf.at[sl, pl.ds(lo, rows), :], sem.at[sl]))
        _for_each_run(tile, n_experts, toff_ref, cnt8_ref, gbase_ref, fn)

    def start_fetch(tile, sl):
        ybuf[sl, tm * top_k:, :] = jnp.zeros((n_local - tm * top_k, ybuf.shape[2]), ybuf.dtype)
        fetch(tile, sl, lambda cp: cp.start())

    @pl.when(i == 0)
    def _():
        start_fetch(i, slot)

    @pl.when(i + 1 < n_tiles)
    def _():
        start_fetch(i + 1, 1 - slot)

    gate = gate_ref[...]
    col = lax.broadcasted_iota(I32, (tm, n_local), 1)
    weights = jnp.zeros((tm, n_local), F32)
    for k, pos in enumerate(_local_positions(eidx_ref, lrank_ref, toffv_ref, top_k)):
        weights = weights + jnp.where(col == pos, gate[:, k:k + 1], 0.0)
    weights = weights.astype(BF16)
    fetch(i, slot, lambda cp: cp.wait())
    y_lo, y_hi = _unpack_halves(ybuf[slot])
    ff = jnp.concatenate([jnp.dot(weights, y_lo, preferred_element_type=F32),
                          jnp.dot(weights, y_hi, preferred_element_type=F32)], axis=1)
    gt = _select_mod(i, n_ptiles, gts_ref, gtt_ref)
    y = _layer_norm(alpha * x1_ref[...] + gt * ff, lnw_ref[...], lnb_ref[...])

    @pl.when(i < n_ptiles)
    def _():
        yp_ref[...] = y

    @pl.when(i >= n_ptiles)
    def _():
        ys_ref[...] = y


def _combine(tables, x1, gt_seq, gt_tok, gate, e_idx, lrank, toff_vec, lnw, lnb, yb, *, n_local, n_ptiles,
             tiles_per_seq, n_experts, alpha):
    t, d = x1.shape
    tm = TOKEN_TILE
    nt = t // tm
    n_seq = gt_seq.shape[0]
    body = functools.partial(_combine_body, top_k=TOP_K, n_ptiles=n_ptiles, n_experts=n_experts, alpha=alpha)
    lane_spec = pl.BlockSpec((tm, LANES), lambda i, *_: (i, 0))
    return pl.pallas_call(
        body,
        grid_spec=pltpu.PrefetchScalarGridSpec(
            num_scalar_prefetch=len(tables),
            grid=(nt,),
            in_specs=[pl.BlockSpec((tm, d), lambda i, *_: (i, 0)),
                      pl.BlockSpec((None, 1, d), lambda i, *_: (jnp.minimum(i // tiles_per_seq, n_seq - 1), 0, 0)),
                      pl.BlockSpec((tm, d), lambda i, *_: (jnp.maximum(i - n_ptiles, 0), 0)),
                      lane_spec, lane_spec, lane_spec,
                      pl.BlockSpec((None, 1, LANES), lambda i, *_: (i, 0, 0)),
                      pl.BlockSpec((1, d), lambda i, *_: (0, 0)),
                      pl.BlockSpec((1, d), lambda i, *_: (0, 0)),
                      pl.BlockSpec(memory_space=pl.ANY)],
            out_specs=[pl.BlockSpec((tm, d), lambda i, *_: (jnp.minimum(i, n_ptiles - 1), 0)),
                       pl.BlockSpec((tm, d), lambda i, *_: (jnp.maximum(i - n_ptiles, 0), 0))],
            scratch_shapes=[pltpu.VMEM((2, n_local, d // 2), U32), pltpu.SemaphoreType.DMA((2,))],
        ),
        out_shape=[jax.ShapeDtypeStruct((n_ptiles * tm, d), F32),
                   jax.ShapeDtypeStruct(((nt - n_ptiles) * tm, d), F32)],
        compiler_params=_cparams("arbitrary"),
        name="combine",
    )(*tables, x1, gt_seq, gt_tok, gate, e_idx, lrank, toff_vec, lnw, lnb, yb)


def _rotary_tables(pos, hd, rot_dim):
    half = rot_dim // 2
    inv_freq = jnp.power(jnp.float32(ROPE_THETA), -jnp.arange(half, dtype=F32) * (2.0 / rot_dim))
    ang = pos.astype(F32)[:, None] * inv_freq[None, :]
    cos, sin = jnp.cos(ang), jnp.sin(ang)
    n = pos.shape[0]
    ones = jnp.ones((n, hd - rot_dim), F32)
    zeros = jnp.zeros((n, hd - rot_dim), F32)
    zh = jnp.zeros((n, half), F32)
    ct = jnp.concatenate([cos, cos, ones], axis=1)
    s1 = jnp.concatenate([-sin, zh, zeros], axis=1)
    s2 = jnp.concatenate([zh, sin, zeros], axis=1)
    reps = LANES // hd
    return tuple(jnp.tile(a, (1, reps)) for a in (ct, s1, s2))


def kernel(x_prompt, x_sample, state_win_k, state_win_v, state_conv, state_ssm, c_prompt, c_sample, w_ada, b_ada, w_in, attn_sinks, w_conv, dn_a_log, dn_dt_bias, dn_norm_w, w_proj_attn, w_proj_dn, w_out, ln1_w, ln1_b, w_router, b_router, w_gu, b_gu, w_down, b_down, ln2_w, ln2_b):
    n_p, seq, d = x_prompt.shape
    n_s, l_s, _ = x_sample.shape
    depth = w_ada.shape[0]
    window, n_kv, hd = state_win_k.shape[2:]
    n_q = attn_sinks.shape[1]
    heads, dk, dv = state_ssm.shape[2:]
    conv_w, conv_dim = w_conv.shape[1:]
    n_e = w_router.shape[2]
    qd, kd = n_q * hd, n_kv * hd
    vdim = heads * dv
    rot_dim = hd // 4
    alpha = float((2 * depth) ** 0.25)
    tm = TOKEN_TILE
    t_p, t_s = n_p * seq, n_s * l_s
    t_all = t_p + t_s
    tps = seq // tm
    n_ptiles = t_p // tm
    l_pad = SUBLANES
    assert seq % tm == 0 and t_s % tm == 0 and tm % l_s == 0 and l_s <= l_pad and l_s >= conv_w - 1
    assert 2 * heads <= SUBLANES and n_e <= LANES and hd * 2 == LANES and rot_dim == 2 * SUBLANES

    sizes = [qd, kd, kd, conv_dim, vdim, heads, heads, d, d]
    offs = np.concatenate([[0], np.cumsum(sizes)])
    seg = lambda k: np.arange(offs[k], offs[k + 1])
    perm = np.concatenate([seg(0), seg(1), seg(2), seg(3), seg(4), seg(7), seg(8), seg(5), seg(6)])
    cuts = (qd + 2 * kd, qd + 2 * kd + conv_dim, qd + 2 * kd + conv_dim + vdim, qd + 2 * kd + conv_dim + vdim + 2 * d)
    cuts = cuts + (cuts[-1] + LANES,)

    tabs_p = _rotary_tables(jnp.arange(seq, dtype=I32), hd, rot_dim)
    tabs_s = tuple(jnp.tile(a, (tm // l_s, 1))
                   for a in _rotary_tables(PAST_LEN + jnp.arange(l_s, dtype=I32), hd, rot_dim))

    x_p = x_prompt.reshape(t_p, d)
    x_s = x_sample.reshape(t_s, d)
    c_all = jnp.concatenate([c_prompt, c_sample], axis=0)
    outs = {k: [] for k in ("pwk", "pwv", "pcv", "pss", "swk", "swv", "scv", "sss")}

    for l in range(depth):
        w_in_l = jnp.pad(w_in[l][:, perm], ((0, 0), (0, LANES - 2 * heads))).astype(BF16)
        mod = _ada(c_all, w_ada[l], b_ada[l])
        mod_p = mod[:n_p].reshape(n_p, 6, 1, d)
        mod_s = jnp.repeat(mod[n_p:].reshape(n_s, 6, d), l_s, axis=0)
        sh1p, sc1p, gt1p, sh2p, sc2p, gt2p = [mod_p[:, k] for k in range(6)]
        sh1s, sc1s, gt1s, sh2s, sc2s, gt2s = [mod_s[:, k] for k in range(6)]

        qkva_p, dn_p, z_p, g_p, ba_p, tail_p, kvw_p = _inproj(
            x_p, sh1p, sc1p, tabs_p, w_in_l, cuts, per_token_mod=False, tiles_per_seq=tps, act_dtype=BF16,
            window=window, kv_cols=2 * kd)
        attn_p = _attn_prompt(qkva_p, attn_sinks[l], n_p, seq, n_q, n_kv, hd, window)
        chunk = min(DN_CHUNK, seq)
        nc = seq // chunk
        bat_p = ba_p[:, :SUBLANES].reshape(n_p, nc, chunk, SUBLANES).transpose(0, 1, 3, 2)
        hp = jnp.zeros((SUBLANES, LANES), F32).at[0, :heads].set(dn_a_log[l]).at[1, :heads].set(dn_dt_bias[l])
        nw = dn_norm_w[l].reshape(1, dv)
        o_p, ssm_p = _deltanet(dn_p.reshape(n_p, seq, conv_dim), z_p.reshape(n_p, seq, vdim),
                               ba_p.reshape(n_p, seq, LANES), bat_p,
                               jnp.zeros((n_p, SUBLANES, conv_dim), F32), jnp.zeros((n_p, heads, dk, dv), F32),
                               w_conv[l], hp, nw, chunk=chunk, l_real=chunk, nb=4)
        outs["pwk"].append(kvw_p[:, :, :kd].reshape(n_p, window, n_kv, hd))
        outs["pwv"].append(kvw_p[:, :, kd:].reshape(n_p, window, n_kv, hd))
        outs["pcv"].append(tail_p.reshape(n_p, tps, SUBLANES, conv_dim)[:, -1, SUBLANES - (conv_w - 1):])
        outs["pss"].append(ssm_p)

        qkva_s, dn_s, z_s, g_s, ba_s = _inproj(
            x_s, sh1s, sc1s, tabs_s, w_in_l, cuts, per_token_mod=True, tiles_per_seq=1, act_dtype=F32,
            window=window, kv_cols=2 * kd)
        pad_l = lambda a: jnp.pad(a.reshape(n_s, l_s, a.shape[-1]), ((0, 0), (0, l_pad - l_s), (0, 0)))
        attn_s, wk_s, wv_s = _attn_sample(pad_l(qkva_s), state_win_k[l].reshape(n_s, window, kd),
                                          state_win_v[l].reshape(n_s, window, kd), attn_sinks[l], n_q, n_kv, hd, l_s)
        attn_s = attn_s[:, :l_s].reshape(t_s, qd)
        ba_s3 = pad_l(ba_s)
        bat_s = ba_s3[:, :, :SUBLANES].transpose(0, 2, 1).reshape(n_s, 1, SUBLANES, l_pad)
        cs0 = jnp.pad(state_conv[l], ((0, 0), (SUBLANES - (conv_w - 1), 0), (0, 0)))
        o_s, ssm_s = _deltanet(pad_l(dn_s), pad_l(z_s), ba_s3, bat_s, cs0, state_ssm[l], w_conv[l], hp, nw,
                               chunk=l_pad, l_real=l_s, nb=8)
        o_s = o_s[:, :l_s].reshape(t_s, vdim)
        outs["swk"].append(wk_s.reshape(n_s, window, n_kv, hd))
        outs["swv"].append(wv_s.reshape(n_s, window, n_kv, hd))
        outs["scv"].append(jnp.concatenate([state_conv[l], dn_s.reshape(n_s, l_s, conv_dim)], axis=1)[:, -(conv_w - 1):])
        outs["sss"].append(ssm_s)

        wr = jnp.pad(w_router[l], ((0, 0), (0, LANES - n_e)))
        br = jnp.pad(b_router[l], (0, LANES - n_e), constant_values=NEG_BIG).reshape(1, LANES)
        wts = (w_proj_attn[l].astype(BF16), w_proj_dn[l].astype(BF16), w_out[l].astype(BF16),
               ln1_w[l].reshape(1, d), ln1_b[l].reshape(1, d), wr, br)
        res_p = _outproj(attn_p, o_p.reshape(t_p, vdim), g_p, x_p, gt1p, sh2p, sc2p, wts, None,
                         per_token_mod=False, tiles_per_seq=tps, t_total=t_all, tile_off=0, alpha=alpha)
        x1, e_idx, gate, rank, cnt = _outproj(attn_s, o_s, g_s, x_s, gt1s, sh2s, sc2s, wts, res_p,
                                              per_token_mod=True, tiles_per_seq=1, t_total=t_all, tile_off=n_ptiles,
                                              alpha=alpha)

        bm = EXPERT_ROWS
        nt_all = t_all // tm
        cnt8 = (cnt[:, 0, :n_e].astype(I32) + RUN_ALIGN - 1) // RUN_ALIGN * RUN_ALIGN
        tot = jnp.sum(cnt8, axis=0)
        padded = (tot + bm - 1) // bm * bm
        pad_end = jnp.cumsum(padded)
        pad_start = pad_end - padded
        gbase = pad_start[None, :] + jnp.cumsum(cnt8, axis=0) - cnt8
        toff = jnp.cumsum(cnt8, axis=1) - cnt8
        n_local = -(-(tm * TOP_K + n_e * (RUN_ALIGN - 1)) // LANES) * LANES
        n_rows = -(-(t_all * TOP_K + nt_all * n_e * (RUN_ALIGN - 1) + n_e * (bm - 1)) // bm) * bm
        nblk = n_rows // bm
        n_used = jnp.maximum(pad_end[-1:] // bm, 1).astype(I32)
        block_e = jnp.minimum(jnp.sum(pad_end[None, :] <= (jnp.arange(nblk, dtype=I32) * bm)[:, None], axis=1),
                              n_e - 1).astype(I32)
        flat = lambda a: a.astype(I32).reshape(nt_all * n_e)
        run_tables = (flat(toff), flat(cnt8), flat(gbase))
        tail_tables = ((pad_start + tot).astype(I32), (padded - tot).astype(I32))
        toff_vec = jnp.pad(toff.astype(F32), ((0, 0), (0, LANES - n_e))).reshape(nt_all, 1, LANES)

        xs = _dispatch(run_tables + tail_tables, x1, sh2p, sc2p, sh2s, sc2s, e_idx, rank, toff_vec, n_rows=n_rows,
                       n_local=n_local, n_ptiles=n_ptiles, tiles_per_seq=tps, n_experts=n_e)
        yb = _experts(block_e, n_used, xs, w_gu[l].astype(BF16), b_gu[l], w_down[l].astype(BF16), b_down[l])
        x_p, x_s = _combine(run_tables, x1, gt2p, gt2s, gate, e_idx, rank, toff_vec, ln2_w[l].reshape(1, d),
                            ln2_b[l].reshape(1, d), yb, n_local=n_local, n_ptiles=n_ptiles, tiles_per_seq=tps,
                            n_experts=n_e, alpha=alpha)

    st = lambda k: jnp.stack(outs[k])
    return (x_p.reshape(n_p, seq, d), x_s.reshape(n_s, l_s, d), st("pwk"), st("pwv"), st("pcv"), st("pss"),
            st("swk"), st("swv"), st("scv"), st("sss"))
```

```python
import functools

import numpy as np
import jax
import jax.numpy as jnp
from jax import lax
from jax.experimental import pallas as pl
from jax.experimental.pallas import tpu as pltpu

F32 = jnp.float32
BF16 = jnp.bfloat16
I32 = jnp.int32
U32 = jnp.uint32

PAST_LEN = 16384
ROPE_THETA = 500000.0
TOP_K = 4
SWIGLU_LIMIT = 7.0
SWIGLU_ALPHA = 1.702
DN_CHUNK = 64
LN_EPS = 1e-5
RMS_EPS = 1e-6
L2_EPS = 1e-6

LANES = 128
SUBLANES = 8
VMEM_LIMIT_BYTES = 56 * 1024 * 1024

TOKEN_TILE = 256
EXPERT_ROWS = 512
RUN_ALIGN = SUBLANES
OUTPROJ_SUBTILES = 2
N_ROUTE_BUFS = 5
NEG_BIG = -1e30


def _cparams(*sem):
    return pltpu.CompilerParams(dimension_semantics=sem, vmem_limit_bytes=VMEM_LIMIT_BYTES)


def _silu(x):
    return x * jax.nn.sigmoid(x)


def _bdot(a, b):
    return jnp.dot(a.astype(BF16), b.astype(BF16), preferred_element_type=F32)


def _pack_halves(x):
    n = x.shape[1] // 2
    lo = lax.bitcast_convert_type(x[:, :n], U32)
    hi = lax.bitcast_convert_type(x[:, n:], U32)
    return (hi & jnp.uint32(0xFFFF0000)) | (lo >> 16)


def _unpack_halves(w):
    lo = lax.bitcast_convert_type(w << 16, F32).astype(BF16)
    hi = lax.bitcast_convert_type(w & jnp.uint32(0xFFFF0000), F32).astype(BF16)
    return lo, hi


def _ada_body(c_ref, w_ref, b_ref, o_ref):
    o_ref[...] = _bdot(_silu(c_ref[...]), w_ref[...]) + b_ref[...]


def _ada(c_all, w_ada, b_ada):
    n, d = c_all.shape
    dout = w_ada.shape[1]
    tn = d
    return pl.pallas_call(
        _ada_body,
        grid=(dout // tn,),
        in_specs=[pl.BlockSpec((n, d), lambda j: (0, 0)),
                  pl.BlockSpec((d, tn), lambda j: (0, j)),
                  pl.BlockSpec((1, tn), lambda j: (0, j))],
        out_specs=pl.BlockSpec((n, tn), lambda j: (0, j)),
        out_shape=jax.ShapeDtypeStruct((n, dout), F32),
        compiler_params=_cparams("arbitrary"),
        name="ada",
    )(c_all, w_ada, b_ada.reshape(1, dout))


def _inproj_body(x_ref, sh_ref, sc_ref, ct_ref, s1_ref, s2_ref, w_ref,
                 a_ref, dn_ref, z_ref, g_ref, ba_ref, *win_refs, cuts, n_rot_chunks, window):
    h = (x_ref[...] * (1.0 + sc_ref[...]) + sh_ref[...]).astype(BF16)

    def mm(lo, hi):
        return jnp.dot(h, w_ref[:, lo:hi], preferred_element_type=F32)

    c_a, c_dn, c_z, c_g, c_ba = cuts
    qkv = mm(0, c_a)
    ct, s1, s2 = ct_ref[...], s1_ref[...], s2_ref[...]
    cols = []
    for c in range(n_rot_chunks):
        xc = qkv[:, c * LANES:(c + 1) * LANES]
        cols.append(xc * ct + pltpu.roll(xc, LANES - SUBLANES, 1) * s1 + pltpu.roll(xc, SUBLANES, 1) * s2)
    cols.append(qkv[:, n_rot_chunks * LANES:])
    rot = jnp.concatenate(cols, axis=1)
    a_ref[...] = rot.astype(a_ref.dtype)
    dn = mm(c_a, c_dn)
    dn_ref[...] = dn.astype(dn_ref.dtype)
    z_ref[...] = mm(c_dn, c_z).astype(z_ref.dtype)
    g_ref[...] = mm(c_z, c_g).astype(g_ref.dtype)
    ba_ref[...] = mm(c_g, c_ba)
    if win_refs:
        tail_ref, kvw_ref = win_refs
        tm = dn.shape[0]
        tail_ref[...] = dn[tm - SUBLANES:, :]
        kvw_ref[...] = rot[tm - window:, n_rot_chunks * LANES - LANES:]


def _inproj(x, sh, sc, tabs, w_perm, cuts, *, per_token_mod, tiles_per_seq, act_dtype, window, kv_cols):
    t, d = x.shape
    tm = TOKEN_TILE
    nt = t // tm
    c_a, c_dn, c_z, c_g, c_ba = cuts
    n_rot_chunks = (c_a - kv_cols // 2) // LANES
    if per_token_mod:
        mod_spec = pl.BlockSpec((tm, d), lambda i: (i, 0))
        tab_spec = pl.BlockSpec((tm, LANES), lambda i: (0, 0))
    else:
        mod_spec = pl.BlockSpec((None, 1, d), lambda i: (i // tiles_per_seq, 0, 0))
        tab_spec = pl.BlockSpec((tm, LANES), lambda i: (i % tiles_per_seq, 0))
    out_shape = [jax.ShapeDtypeStruct((t, c_a), act_dtype),
                 jax.ShapeDtypeStruct((t, c_dn - c_a), act_dtype),
                 jax.ShapeDtypeStruct((t, c_z - c_dn), act_dtype),
                 jax.ShapeDtypeStruct((t, c_g - c_z), act_dtype),
                 jax.ShapeDtypeStruct((t, c_ba - c_g), F32)]
    out_specs = [pl.BlockSpec((tm, s.shape[1]), lambda i: (i, 0)) for s in out_shape]
    with_win = not per_token_mod
    if with_win:
        n_seq = nt // tiles_per_seq
        out_shape += [jax.ShapeDtypeStruct((nt, SUBLANES, c_dn - c_a), F32),
                      jax.ShapeDtypeStruct((n_seq, window, kv_cols), F32)]
        out_specs += [pl.BlockSpec((None, SUBLANES, c_dn - c_a), lambda i: (i, 0, 0)),
                      pl.BlockSpec((None, window, kv_cols), lambda i: (i // tiles_per_seq, 0, 0))]
    body = functools.partial(_inproj_body, cuts=cuts, n_rot_chunks=n_rot_chunks, window=window)
    return pl.pallas_call(
        body,
        grid=(nt,),
        in_specs=[pl.BlockSpec((tm, d), lambda i: (i, 0)), mod_spec, mod_spec,
                  tab_spec, tab_spec, tab_spec,
                  pl.BlockSpec((d, c_ba), lambda i: (0, 0))],
        out_specs=out_specs,
        out_shape=out_shape,
        compiler_params=_cparams("arbitrary"),
        name="inproj",
    )(x, sh, sc, *tabs, w_perm)


def _softmax_sink_pv(s, valid, sink, v):
    s = jnp.where(valid, s, -jnp.inf)
    m = jnp.maximum(jnp.max(s, axis=-1, keepdims=True), sink)
    p = jnp.exp(s - m)
    denom = jnp.sum(p, axis=-1, keepdims=True) + jnp.exp(sink - m)
    return jnp.dot((p / denom).astype(BF16), v, preferred_element_type=F32)


def _attn_prompt_body(sink_ref, q_ref, kvp_ref, kvc_ref, o_ref, *, n_q, n_kv, hd, window):
    j = pl.program_id(1)
    group = n_q // n_kv
    q = q_ref[...]
    kv = jnp.concatenate([kvp_ref[...], kvc_ref[...]], axis=0)
    r = lax.broadcasted_iota(I32, (window, 2 * window), 0)
    c = lax.broadcasted_iota(I32, (window, 2 * window), 1)
    rel = window + r - c
    valid = (rel >= 0) & (rel < window) & ((c >= window) | (j > 0))
    outs = []
    for h in range(n_q):
        kvh = h // group
        qh = q[:, h * hd:(h + 1) * hd]
        kh = kv[:, kvh * hd:(kvh + 1) * hd]
        vh = kv[:, (n_kv + kvh) * hd:(n_kv + kvh + 1) * hd]
        s = lax.dot_general(qh, kh, (((1,), (1,)), ((), ())), preferred_element_type=F32) * (hd ** -0.5)
        outs.append(_softmax_sink_pv(s, valid, sink_ref[h], vh))
    o_ref[...] = jnp.concatenate(outs, axis=1).astype(o_ref.dtype)


def _attn_prompt(qkva, sinks, n_seq, seq, n_q, n_kv, hd, window):
    qd, kvd = n_q * hd, 2 * n_kv * hd
    x3 = qkva.reshape(n_seq, seq, qd + kvd)
    nb = seq // window
    kv_blk = qd // kvd
    body = functools.partial(_attn_prompt_body, n_q=n_q, n_kv=n_kv, hd=hd, window=window)
    out = pl.pallas_call(
        body,
        grid=(n_seq, nb),
        in_specs=[pl.BlockSpec(memory_space=pltpu.SMEM),
                  pl.BlockSpec((None, window, qd), lambda n, j: (n, j, 0)),
                  pl.BlockSpec((None, window, kvd), lambda n, j: (n, jnp.maximum(j - 1, 0), kv_blk)),
                  pl.BlockSpec((None, window, kvd), lambda n, j: (n, j, kv_blk))],
        out_specs=pl.BlockSpec((None, window, qd), lambda n, j: (n, j, 0)),
        out_shape=jax.ShapeDtypeStruct((n_seq, seq, qd), BF16),
        compiler_params=_cparams("arbitrary", "arbitrary"),
        name="attn_prompt",
    )(sinks, x3, x3, x3)
    return out.reshape(n_seq * seq, qd)


def _attn_sample_body(sink_ref, q_ref, wk_ref, wv_ref, o_ref, wko_ref, wvo_ref,
                      *, bs, n_q, n_kv, hd, window, l_new, l_pad):
    group = n_q // n_kv
    qd = n_q * hd
    kd = n_kv * hd
    rows = group * l_pad
    r = lax.broadcasted_iota(I32, (rows, window + l_pad), 0) % l_pad
    c = lax.broadcasted_iota(I32, (rows, window + l_pad), 1)
    rel = window + r - c
    valid = (rel >= 0) & (rel < window) & (c < window + l_new)
    for b in range(bs):
        x = q_ref[b]
        k_new = x[:, qd:qd + kd]
        v_new = x[:, qd + kd:]
        wko_ref[b, 0:window - l_new, :] = wk_ref[b, l_new:window, :]
        wko_ref[b, window - l_new:window, :] = k_new[0:l_new, :]
        wvo_ref[b, 0:window - l_new, :] = wv_ref[b, l_new:window, :]
        wvo_ref[b, window - l_new:window, :] = v_new[0:l_new, :]
        k_all = jnp.concatenate([wk_ref[b], k_new], axis=0).astype(BF16)
        v_all = jnp.concatenate([wv_ref[b], v_new], axis=0).astype(BF16)
        outs = [None] * n_q
        for kvh in range(n_kv):
            qg = jnp.concatenate([x[:, (kvh * group + g) * hd:(kvh * group + g + 1) * hd] for g in range(group)],
                                 axis=0).astype(BF16)
            kh = k_all[:, kvh * hd:(kvh + 1) * hd]
            vh = v_all[:, kvh * hd:(kvh + 1) * hd]
            s = lax.dot_general(qg, kh, (((1,), (1,)), ((), ())), preferred_element_type=F32) * (hd ** -0.5)
            sink = jnp.concatenate([jnp.full((l_pad, 1), sink_ref[kvh * group + g], F32) for g in range(group)], axis=0)
            o = _softmax_sink_pv(s, valid, sink, vh)
            for g in range(group):
                outs[kvh * group + g] = o[g * l_pad:(g + 1) * l_pad, :]
        o_ref[b] = jnp.concatenate(outs, axis=1).astype(o_ref.dtype)


def _attn_sample(qkva_pad, win_k, win_v, sinks, n_q, n_kv, hd, l_new):
    n, l_pad, width = qkva_pad.shape
    window, kd = win_k.shape[1], win_k.shape[2]
    qd = n_q * hd
    bs = 8
    body = functools.partial(_attn_sample_body, bs=bs, n_q=n_q, n_kv=n_kv, hd=hd, window=window,
                             l_new=l_new, l_pad=l_pad)
    return pl.pallas_call(
        body,
        grid=(n // bs,),
        in_specs=[pl.BlockSpec(memory_space=pltpu.SMEM),
                  pl.BlockSpec((bs, l_pad, width), lambda i: (i, 0, 0)),
                  pl.BlockSpec((bs, window, kd), lambda i: (i, 0, 0)),
                  pl.BlockSpec((bs, window, kd), lambda i: (i, 0, 0))],
        out_specs=[pl.BlockSpec((bs, l_pad, qd), lambda i: (i, 0, 0)),
                   pl.BlockSpec((bs, window, kd), lambda i: (i, 0, 0)),
                   pl.BlockSpec((bs, window, kd), lambda i: (i, 0, 0))],
        out_shape=[jax.ShapeDtypeStruct((n, l_pad, qd), BF16),
                   jax.ShapeDtypeStruct((n, window, kd), F32),
                   jax.ShapeDtypeStruct((n, window, kd), F32)],
        compiler_params=_cparams("arbitrary"),
        name="attn_sample",
    )(sinks, qkva_pad, win_k, win_v)


def _split_bf16(x):
    hi = x.astype(BF16)
    return hi, (x - hi.astype(F32)).astype(BF16)


def _tdot(a, b):
    ah, al = _split_bf16(a)
    bh, bl = _split_bf16(b)
    m = a.shape[0]
    t = jnp.dot(jnp.concatenate([ah, al], axis=0), bh, preferred_element_type=F32)
    return t[:m] + t[m:] + jnp.dot(ah, bl, preferred_element_type=F32)


def _dn_body(qkv_ref, z_ref, ba_ref, bat_ref, cs0_ref, s0_ref, wc_ref, hp_ref, nw_ref,
             o_ref, s_ref, xbuf, *, nb, chunk, heads, dk, dv, l_real, conv_w):
    c_idx = pl.program_id(1)
    hc = SUBLANES

    @pl.when(c_idx == 0)
    def _():
        xbuf[:, 0:hc, :] = cs0_ref[...]
        s_ref[...] = s0_ref[...]

    qk_dim = heads * dk
    row = lax.broadcasted_iota(I32, (chunk, chunk), 0)
    col = lax.broadcasted_iota(I32, (chunk, chunk), 1)
    incl = row >= col
    strict = row > col
    eye = (row == col).astype(F32)
    valid_c = row[:, 0:1] < l_real
    valid_r = col[0:1, :] < l_real
    n_levels = max(1, int(np.ceil(np.log2(chunk))))
    wc = wc_ref[...]
    hp = hp_ref[...]
    neg_exp_alog = -jnp.exp(hp[0:1, :])
    dt_bias = hp[1:2, :]
    nw = nw_ref[...]
    chains = [(b, h) for b in range(nb) for h in range(heads)]
    n = len(chains)

    ys = []
    for b in range(nb):
        xbuf[b, hc:hc + chunk, :] = qkv_ref[b].astype(F32)
        y = xbuf[b, hc:hc + chunk, :] * wc[conv_w - 1:conv_w, :]
        for j in range(conv_w - 1):
            off = hc - (conv_w - 1) + j
            y = y + xbuf[b, off:off + chunk, :] * wc[j:j + 1, :]
        ys.append(_silu(y))
        xbuf[b, 0:hc, :] = xbuf[b, chunk:chunk + hc, :]

    qn, kn, kb, vb, decay, e_gc, e_rest, e_last = [], [], [], [], [], [], [], []
    for b, h in chains:
        y = ys[b]
        qh = y[:, h * dk:(h + 1) * dk]
        kh = y[:, qk_dim + h * dk:qk_dim + (h + 1) * dk]
        vh = y[:, 2 * qk_dim + h * dv:2 * qk_dim + (h + 1) * dv]
        ba = ba_ref[b]
        bat = bat_ref[b]
        ne = neg_exp_alog[:, h:h + 1]
        db = dt_bias[:, h:h + 1]
        beta = jnp.where(valid_c, jax.nn.sigmoid(ba[:, h:h + 1]), 0.0)
        g_col = jnp.where(valid_c, ne * jax.nn.softplus(ba[:, heads + h:heads + h + 1] + db), 0.0)
        g_row = jnp.where(valid_r, ne * jax.nn.softplus(bat[heads + h:heads + h + 1, :] + db), 0.0)
        gc_col = jnp.sum(jnp.where(incl, g_row, 0.0), axis=1, keepdims=True)
        gc_row = jnp.sum(jnp.where(row <= col, g_col, 0.0), axis=0, keepdims=True)
        g_last = gc_col[chunk - 1:chunk, :]
        q_ = qh * lax.rsqrt(jnp.sum(qh * qh, -1, keepdims=True) + L2_EPS) * (dk ** -0.5)
        k_ = kh * lax.rsqrt(jnp.sum(kh * kh, -1, keepdims=True) + L2_EPS)
        qn.append(q_)
        kn.append(k_)
        kb.append(k_ * beta)
        vb.append(vh * beta)
        decay.append(jnp.where(incl, jnp.exp(gc_col - gc_row), 0.0))
        e_gc.append(jnp.exp(gc_col))
        e_rest.append(jnp.exp(g_last - gc_col))
        e_last.append(jnp.exp(g_last))

    sc = [lax.dot_general(jnp.concatenate([qn[i], kb[i]], axis=0).astype(BF16), kn[i].astype(BF16),
                          (((1,), (1,)), ((), ())), preferred_element_type=F32) for i in range(n)]
    qk = [sc[i][:chunk] * decay[i] for i in range(n)]
    p = [jnp.where(strict, -(sc[i][chunk:] * decay[i]), 0.0) for i in range(n)]
    t_inv = [eye + p[i] for i in range(n)]
    if n_levels > 1:
        p = [_tdot(p[i], p[i]) for i in range(n)]
    for lvl in range(1, n_levels):
        if lvl < n_levels - 1:
            yp = [_tdot(jnp.concatenate([t_inv[i], p[i]], axis=0), p[i]) for i in range(n)]
            t_inv = [t_inv[i] + yp[i][:chunk] for i in range(n)]
            p = [yp[i][chunk:] for i in range(n)]
        else:
            t_inv = [t_inv[i] + _tdot(t_inv[i], p[i]) for i in range(n)]
    sol = [_tdot(t_inv[i], jnp.concatenate([vb[i], kb[i] * e_gc[i]], axis=1)) for i in range(n)]
    s_old = [s_ref[b, h] for b, h in chains]
    wq = [_bdot(jnp.concatenate([sol[i][:, dv:], qn[i] * e_gc[i]], axis=0), s_old[i]) for i in range(n)]
    v_new = [sol[i][:, :dv] - wq[i][:chunk] for i in range(n)]
    o = [wq[i][chunk:] + _bdot(qk[i], v_new[i]) for i in range(n)]
    for i, (b, h) in enumerate(chains):
        s_ref[b, h] = s_old[i] * e_last[i] + lax.dot_general(
            (kn[i] * e_rest[i]).astype(BF16), v_new[i].astype(BF16), (((0,), (0,)), ((), ())),
            preferred_element_type=F32)
    for b in range(nb):
        zt = z_ref[b].astype(F32)
        outs = []
        for h in range(heads):
            oi = o[b * heads + h]
            on = oi * lax.rsqrt(jnp.mean(oi * oi, -1, keepdims=True) + RMS_EPS) * nw
            outs.append(on * _silu(zt[:, h * dv:(h + 1) * dv]))
        o_ref[b] = jnp.concatenate(outs, axis=1).astype(o_ref.dtype)


def _deltanet(qkv, z, ba, bat, cs0, s0, w_conv, hp, norm_w, *, chunk, l_real, nb):
    n, l, conv_dim = qkv.shape
    heads, dk, dv = s0.shape[1:]
    nc = l // chunk
    assert n % nb == 0 and l % chunk == 0
    conv_w = w_conv.shape[0]
    body = functools.partial(_dn_body, nb=nb, chunk=chunk, heads=heads, dk=dk, dv=dv, l_real=l_real, conv_w=conv_w)
    return pl.pallas_call(
        body,
        grid=(n // nb, nc),
        in_specs=[pl.BlockSpec((nb, chunk, conv_dim), lambda i, c: (i, c, 0)),
                  pl.BlockSpec((nb, chunk, heads * dv), lambda i, c: (i, c, 0)),
                  pl.BlockSpec((nb, chunk, LANES), lambda i, c: (i, c, 0)),
                  pl.BlockSpec((nb, None, SUBLANES, chunk), lambda i, c: (i, c, 0, 0)),
                  pl.BlockSpec((nb, SUBLANES, conv_dim), lambda i, c: (i, 0, 0)),
                  pl.BlockSpec((nb, heads, dk, dv), lambda i, c: (i, 0, 0, 0)),
                  pl.BlockSpec((conv_w, conv_dim), lambda i, c: (0, 0)),
                  pl.BlockSpec((SUBLANES, LANES), lambda i, c: (0, 0)),
                  pl.BlockSpec((1, dv), lambda i, c: (0, 0))],
        out_specs=[pl.BlockSpec((nb, chunk, heads * dv), lambda i, c: (i, c, 0)),
                   pl.BlockSpec((nb, heads, dk, dv), lambda i, c: (i, 0, 0, 0))],
        out_shape=[jax.ShapeDtypeStruct((n, l, heads * dv), BF16),
                   jax.ShapeDtypeStruct((n, heads, dk, dv), F32)],
        scratch_shapes=[pltpu.VMEM((nb, SUBLANES + chunk, conv_dim), F32)],
        compiler_params=_cparams("arbitrary", "arbitrary"),
        name="deltanet",
    )(qkv, z, ba, bat, cs0, s0, w_conv, hp, norm_w)


def _layer_norm(r, w, b):
    mu = jnp.mean(r, -1, keepdims=True)
    var = jnp.mean(jnp.square(r - mu), -1, keepdims=True)
    return (r - mu) * lax.rsqrt(var + LN_EPS) * w + b


def _outproj_body(attn_ref, dn_ref, g_ref, x_ref, gt_ref, sh2_ref, sc2_ref, wpa_ref, wpd_ref, wo_ref,
                  lnw_ref, lnb_ref, wrh_ref, wrl_ref, br_ref, *refs, alpha, top_k, aliased):
    if aliased:
        refs = refs[N_ROUTE_BUFS:]
    x1_ref, eidx_ref, gate_ref, rank_ref, cnt_ref = refs
    for s in range(cnt_ref.shape[0]):
        _outproj_tile(s, attn_ref, dn_ref, g_ref, x_ref, gt_ref, sh2_ref, sc2_ref, wpa_ref, wpd_ref, wo_ref, lnw_ref,
                      lnb_ref, wrh_ref, wrl_ref, br_ref, x1_ref, eidx_ref, gate_ref, rank_ref, cnt_ref,
                      alpha=alpha, top_k=top_k)


def _outproj_tile(s, attn_ref, dn_ref, g_ref, x_ref, gt_ref, sh2_ref, sc2_ref, wpa_ref, wpd_ref, wo_ref, lnw_ref,
                  lnb_ref, wrh_ref, wrl_ref, br_ref, x1_ref, eidx_ref, gate_ref, rank_ref, cnt_ref, *, alpha, top_k):
    d = x_ref.shape[1]
    tm = TOKEN_TILE
    rows = pl.ds(s * tm, tm)

    def mod(ref):
        return ref[...] if ref.shape[0] == 1 else ref[rows, :]

    g = g_ref[rows, :].astype(F32)
    pa = jnp.dot(attn_ref[rows, :], wpa_ref[...], preferred_element_type=F32)
    pd = jnp.dot(dn_ref[rows, :], wpd_ref[...], preferred_element_type=F32)
    merged = jax.nn.sigmoid(g[:, :d]) * pa + jax.nn.sigmoid(g[:, d:]) * pd
    mix = jnp.dot(merged.astype(BF16), wo_ref[...], preferred_element_type=F32)
    x1 = _layer_norm(alpha * x_ref[rows, :] + mod(gt_ref) * mix, lnw_ref[...], lnb_ref[...])
    x1_ref[rows, :] = x1
    h2 = x1 * (1.0 + mod(sc2_ref)) + mod(sh2_ref)
    h_hi, h_lo = _split_bf16(h2)
    lg = jnp.dot(jnp.concatenate([h_hi, h_lo], axis=0), wrh_ref[...], preferred_element_type=F32)
    logits = lg[:tm] + lg[tm:] + jnp.dot(h_hi, wrl_ref[...], preferred_element_type=F32) + br_ref[...]
    lane = lax.broadcasted_iota(I32, (tm, LANES), 1)
    lane_f = lane.astype(F32)
    vals, idxs, sels = [], [], []
    l = logits
    for _ in range(top_k):
        m = jnp.max(l, axis=1, keepdims=True)
        idx = jnp.min(jnp.where(l == m, lane_f, float(LANES)), axis=1, keepdims=True)
        sel = lane_f == idx
        vals.append(m)
        idxs.append(idx)
        sels.append(sel)
        l = jnp.where(sel, -jnp.inf, l)
    ex = [jnp.exp(v - vals[0]) for v in vals]
    den = ex[0]
    for e in ex[1:]:
        den = den + e
    multi_hot = jnp.zeros((tm, LANES), F32)
    for sel in sels:
        multi_hot = multi_hot + jnp.where(sel, 1.0, 0.0)
    r_i = lax.broadcasted_iota(I32, (tm, tm), 0)
    c_i = lax.broadcasted_iota(I32, (tm, tm), 1)
    lower = jnp.where(r_i > c_i, 1.0, 0.0).astype(BF16)
    prefix = jnp.dot(lower, multi_hot.astype(BF16), preferred_element_type=F32)
    e_out = jnp.zeros((tm, LANES), F32)
    g_out = jnp.zeros((tm, LANES), F32)
    r_out = jnp.zeros((tm, LANES), F32)
    for k in range(top_k):
        rank_k = jnp.sum(jnp.where(sels[k], prefix, 0.0), axis=1, keepdims=True)
        e_out = jnp.where(lane == k, idxs[k], e_out)
        g_out = jnp.where(lane == k, ex[k] / den, g_out)
        r_out = jnp.where(lane == k, rank_k, r_out)
    eidx_ref[rows, :] = e_out.astype(I32)
    gate_ref[rows, :] = g_out
    rank_ref[rows, :] = r_out.astype(I32)
    cnt_ref[s] = jnp.broadcast_to(jnp.sum(multi_hot, axis=0, keepdims=True), cnt_ref.shape[1:])


def _outproj(attn, dn, gates, x, gt, sh2, sc2, wts, bufs, *, per_token_mod, tiles_per_seq, t_total, tile_off, alpha):
    t, d = x.shape
    sub = OUTPROJ_SUBTILES
    tm = TOKEN_TILE * sub
    nt = t // tm
    assert t % tm == 0 and tile_off % sub == 0 and (per_token_mod or tiles_per_seq % sub == 0)
    wpa, wpd, wo, lnw, lnb, wrh, wrl, br = wts
    if per_token_mod:
        mod_spec = pl.BlockSpec((tm, d), lambda i: (i, 0))
    else:
        mod_spec = pl.BlockSpec((None, 1, d), lambda i: (i // (tiles_per_seq // sub), 0, 0))

    def row(width):
        return pl.BlockSpec((tm, width), lambda i: (i, 0))

    def full(a):
        return pl.BlockSpec(a.shape, lambda i: (0,) * a.ndim)

    aliased = bufs is not None
    in_specs = [row(attn.shape[1]), row(dn.shape[1]), row(gates.shape[1]), row(d), mod_spec, mod_spec, mod_spec,
                full(wpa), full(wpd), full(wo), full(lnw), full(lnb), full(wrh), full(wrl), full(br)]
    args = [attn, dn, gates, x, gt, sh2, sc2, wpa, wpd, wo, lnw, lnb, wrh, wrl, br]
    io_alias = {}
    if aliased:
        for k, bfr in enumerate(bufs):
            in_specs.append(pl.BlockSpec(memory_space=pl.ANY))
            io_alias[len(args)] = k
            args.append(bfr)
    step_off = tile_off // sub
    out_row = lambda width: pl.BlockSpec((tm, width), lambda i: (i + step_off, 0))
    out_shape = [jax.ShapeDtypeStruct((t_total, d), F32), jax.ShapeDtypeStruct((t_total, LANES), I32),
                 jax.ShapeDtypeStruct((t_total, LANES), F32), jax.ShapeDtypeStruct((t_total, LANES), I32),
                 jax.ShapeDtypeStruct((t_total // TOKEN_TILE, SUBLANES, LANES), F32)]
    assert len(out_shape) == N_ROUTE_BUFS
    out_specs = [out_row(d), out_row(LANES), out_row(LANES), out_row(LANES),
                 pl.BlockSpec((sub, SUBLANES, LANES), lambda i: (i + step_off, 0, 0))]
    body = functools.partial(_outproj_body, alpha=alpha, top_k=TOP_K, aliased=aliased)
    return pl.pallas_call(
        body,
        grid=(nt,),
        in_specs=in_specs,
        out_specs=out_specs,
        out_shape=out_shape,
        input_output_aliases=io_alias,
        compiler_params=_cparams("arbitrary"),
        name="outproj",
    )(*args)


def _select_mod(i, n_ptiles, seq_ref, tok_ref):
    return jnp.where(i < n_ptiles, seq_ref[...], tok_ref[...])


def _for_run(rows, local_off, global_off, fn):
    @pl.when(rows > 0)
    def _():
        fn(pl.multiple_of(local_off, RUN_ALIGN), pl.multiple_of(global_off, RUN_ALIGN),
           pl.multiple_of(rows, RUN_ALIGN))


def _for_each_run(i, n_experts, toff_ref, cnt8_ref, gbase_ref, fn):
    for e in range(n_experts):
        idx = i * n_experts + e
        _for_run(cnt8_ref[idx], toff_ref[idx], gbase_ref[idx], fn)


def _local_positions(eidx_ref, lrank_ref, toffv_ref, top_k):
    eidx = eidx_ref[...]
    lrank = lrank_ref[...]
    toffv = toffv_ref[...]
    lane = lax.broadcasted_iota(I32, eidx.shape, 1)
    pos = []
    for k in range(top_k):
        base = jnp.sum(jnp.where(lane == eidx[:, k:k + 1], toffv, 0.0), axis=1, keepdims=True)
        pos.append(base.astype(I32) + lrank[:, k:k + 1])
    return pos


def _dispatch_body(toff_ref, cnt8_ref, gbase_ref, tstart_ref, trows_ref,
                   x1_ref, shs_ref, scs_ref, sht_ref, sct_ref, eidx_ref, lrank_ref, toffv_ref,
                   xs_ref, lbuf, zbuf, sem, *, top_k, n_tiles, n_ptiles, n_experts):
    i = pl.program_id(0)
    sc = _select_mod(i, n_ptiles, scs_ref, sct_ref)
    sh = _select_mod(i, n_ptiles, shs_ref, sht_ref)
    h2 = (x1_ref[...] * (1.0 + sc) + sh).astype(BF16)
    tm = h2.shape[0]
    n_local = lbuf.shape[1]
    slot = i % 2

    def runs(tile, sl, act):
        def fn(lo, go, rows):
            act(pltpu.make_async_copy(lbuf.at[sl, pl.ds(lo, rows), :], xs_ref.at[pl.ds(go, rows), :], sem.at[sl]))
        _for_each_run(tile, n_experts, toff_ref, cnt8_ref, gbase_ref, fn)

    start = lambda cp: cp.start()
    wait = lambda cp: cp.wait()

    @pl.when(i >= 2)
    def _():
        runs(i - 2, slot, wait)

    col = lax.broadcasted_iota(I32, (tm, n_local), 1)
    onehot = jnp.zeros((tm, n_local), F32)
    for pos in _local_positions(eidx_ref, lrank_ref, toffv_ref, top_k):
        onehot = onehot + jnp.where(col == pos, 1.0, 0.0)
    srt = lax.dot_general(onehot.astype(BF16), h2, (((0,), (0,)), ((), ())), preferred_element_type=F32)
    lbuf[slot] = _pack_halves(srt)
    runs(i, slot, start)

    @pl.when(i == n_tiles - 1)
    def _():
        runs(i - 1, 1 - slot, wait)
        runs(i, slot, wait)
        zbuf[...] = jnp.zeros(zbuf.shape, zbuf.dtype)

        def zero_fill(e, act):
            _for_run(trows_ref[e], 0, tstart_ref[e], lambda lo, go, rows: act(pltpu.make_async_copy(
                zbuf.at[pl.ds(0, rows), :], xs_ref.at[pl.ds(go, rows), :], sem.at[0])))

        for e in range(n_experts):
            zero_fill(e, start)
        for e in range(n_experts):
            zero_fill(e, wait)


def _dispatch(tables, x1, sh_seq, sc_seq, sh_tok, sc_tok, e_idx, lrank, toff_vec, *, n_rows, n_local, n_ptiles,
              tiles_per_seq, n_experts):
    t, d = x1.shape
    tm = TOKEN_TILE
    nt = t // tm
    n_seq = sh_seq.shape[0]
    assert nt >= 2 and d % 2 == 0
    body = functools.partial(_dispatch_body, top_k=TOP_K, n_tiles=nt, n_ptiles=n_ptiles, n_experts=n_experts)
    seq_spec = pl.BlockSpec((None, 1, d), lambda i, *_: (jnp.minimum(i // tiles_per_seq, n_seq - 1), 0, 0))
    tok_spec = pl.BlockSpec((tm, d), lambda i, *_: (jnp.maximum(i - n_ptiles, 0), 0))
    lane_spec = pl.BlockSpec((tm, LANES), lambda i, *_: (i, 0))
    return pl.pallas_call(
        body,
        grid_spec=pltpu.PrefetchScalarGridSpec(
            num_scalar_prefetch=len(tables),
            grid=(nt,),
            in_specs=[pl.BlockSpec((tm, d), lambda i, *_: (i, 0)),
                      seq_spec, seq_spec, tok_spec, tok_spec, lane_spec, lane_spec,
                      pl.BlockSpec((None, 1, LANES), lambda i, *_: (i, 0, 0))],
            out_specs=pl.BlockSpec(memory_space=pl.ANY),
            scratch_shapes=[pltpu.VMEM((2, n_local, d // 2), U32), pltpu.VMEM((EXPERT_ROWS, d // 2), U32),
                            pltpu.SemaphoreType.DMA((2,))],
        ),
        out_shape=jax.ShapeDtypeStruct((n_rows, d // 2), U32),
        compiler_params=_cparams("arbitrary"),
        name="dispatch",
    )(*tables, x1, sh_seq, sc_seq, sh_tok, sc_tok, e_idx, lrank, toff_vec)


def _expert_body(be_ref, nu_ref, xs_ref, wgu_ref, bgu_ref, wd_ref, bd_ref, y_ref, *, de):
    @pl.when(pl.program_id(0) < nu_ref[0])
    def _():
        x_lo, x_hi = _unpack_halves(xs_ref[...])
        half = x_lo.shape[1]
        gu = (jnp.dot(x_lo, wgu_ref[:half, :], preferred_element_type=F32)
              + jnp.dot(x_hi, wgu_ref[half:, :], preferred_element_type=F32) + bgu_ref[...])
        glu = jnp.minimum(gu[:, :de], SWIGLU_LIMIT)
        lin = jnp.clip(gu[:, de:], -SWIGLU_LIMIT, SWIGLU_LIMIT)
        act = glu * jax.nn.sigmoid(SWIGLU_ALPHA * glu) * (lin + 1.0)
        y = jnp.dot(act.astype(BF16), wd_ref[...], preferred_element_type=F32) + bd_ref[...]
        y_ref[...] = _pack_halves(y.astype(BF16).astype(F32))


def _experts(block_e, n_used, xs, w_gu, b_gu, w_down, b_down):
    p, dh = xs.shape
    bm = EXPERT_ROWS
    n_e, d, de2 = w_gu.shape
    de = de2 // 2
    nblk = p // bm

    def blk(i, be, nu):
        return jnp.minimum(i, nu[0] - 1)

    body = functools.partial(_expert_body, de=de)
    return pl.pallas_call(
        body,
        grid_spec=pltpu.PrefetchScalarGridSpec(
            num_scalar_prefetch=2,
            grid=(nblk,),
            in_specs=[pl.BlockSpec((bm, dh), lambda i, be, nu: (blk(i, be, nu), 0)),
                      pl.BlockSpec((None, d, de2), lambda i, be, nu: (be[blk(i, be, nu)], 0, 0)),
                      pl.BlockSpec((None, 1, de2), lambda i, be, nu: (be[blk(i, be, nu)], 0, 0)),
                      pl.BlockSpec((None, de, d), lambda i, be, nu: (be[blk(i, be, nu)], 0, 0)),
                      pl.BlockSpec((None, 1, d), lambda i, be, nu: (be[blk(i, be, nu)], 0, 0))],
            out_specs=pl.BlockSpec((bm, dh), lambda i, be, nu: (blk(i, be, nu), 0)),
        ),
        out_shape=jax.ShapeDtypeStruct((p, dh), U32),
        compiler_params=_cparams("arbitrary"),
        name="experts",
    )(block_e, n_used, xs, w_gu, b_gu.reshape(n_e, 1, de2), w_down, b_down.reshape(n_e, 1, d))


def _combine_body(toff_ref, cnt8_ref, gbase_ref, x1_ref, gts_ref, gtt_ref, gate_ref, eidx_ref, lrank_ref, toffv_ref,
                  lnw_ref, lnb_ref, yb_ref, yp_ref, ys_ref, ybuf, sem, *, top_k, n_ptiles, n_experts, alpha):
    i = pl.program_id(0)
    n_tiles = pl.num_programs(0)
    tm = x1_ref.shape[0]
    n_local = ybuf.shape[1]
    slot = i % 2

    def fetch(tile, sl, act):
        def fn(lo, go, rows):
            act(pltpu.make_async_copy(yb_ref.at[pl.ds(go, rows), :], ybuf.at[sl, pl.ds(lo, rows), :], sem.at[sl]))
        _for_each_run(tile, n_experts, toff_ref, cnt8_ref, gbase_ref, fn)

    def start_fetch(tile, sl):
        ybuf[sl, tm * top_k:, :] = jnp.zeros((n_local - tm * top_k, ybuf.shape[2]), ybuf.dtype)
        fetch(tile, sl, lambda cp: cp.start())

    @pl.when(i == 0)
    def _():
        start_fetch(i, slot)

    @pl.when(i + 1 < n_tiles)
    def _():
        start_fetch(i + 1, 1 - slot)

    gate = gate_ref[...]
    col = lax.broadcasted_iota(I32, (tm, n_local), 1)
    weights = jnp.zeros((tm, n_local), F32)
    for k, pos in enumerate(_local_positions(eidx_ref, lrank_ref, toffv_ref, top_k)):
        weights = weights + jnp.where(col == pos, gate[:, k:k + 1], 0.0)
    weights = weights.astype(BF16)
    fetch(i, slot, lambda cp: cp.wait())
    y_lo, y_hi = _unpack_halves(ybuf[slot])
    ff = jnp.concatenate([jnp.dot(weights, y_lo, preferred_element_type=F32),
                          jnp.dot(weights, y_hi, preferred_element_type=F32)], axis=1)
    gt = _select_mod(i, n_ptiles, gts_ref, gtt_ref)
    y = _layer_norm(alpha * x1_ref[...] + gt * ff, lnw_ref[...], lnb_ref[...])

    @pl.when(i < n_ptiles)
    def _():
        yp_ref[...] = y

    @pl.when(i >= n_ptiles)
    def _():
        ys_ref[...] = y


def _combine(tables, x1, gt_seq, gt_tok, gate, e_idx, lrank, toff_vec, lnw, lnb, yb, *, n_local, n_ptiles,
             tiles_per_seq, n_experts, alpha):
    t, d = x1.shape
    tm = TOKEN_TILE
    nt = t // tm
    n_seq = gt_seq.shape[0]
    body = functools.partial(_combine_body, top_k=TOP_K, n_ptiles=n_ptiles, n_experts=n_experts, alpha=alpha)
    lane_spec = pl.BlockSpec((tm, LANES), lambda i, *_: (i, 0))
    return pl.pallas_call(
        body,
        grid_spec=pltpu.PrefetchScalarGridSpec(
            num_scalar_prefetch=len(tables),
            grid=(nt,),
            in_specs=[pl.BlockSpec((tm, d), lambda i, *_: (i, 0)),
                      pl.BlockSpec((None, 1, d), lambda i, *_: (jnp.minimum(i // tiles_per_seq, n_seq - 1), 0, 0)),
                      pl.BlockSpec((tm, d), lambda i, *_: (jnp.maximum(i - n_ptiles, 0), 0)),
                      lane_spec, lane_spec, lane_spec,
                      pl.BlockSpec((None, 1, LANES), lambda i, *_: (i, 0, 0)),
                      pl.BlockSpec((1, d), lambda i, *_: (0, 0)),
                      pl.BlockSpec((1, d), lambda i, *_: (0, 0)),
                      pl.BlockSpec(memory_space=pl.ANY)],
            out_specs=[pl.BlockSpec((tm, d), lambda i, *_: (jnp.minimum(i, n_ptiles - 1), 0)),
                       pl.BlockSpec((tm, d), lambda i, *_: (jnp.maximum(i - n_ptiles, 0), 0))],
            scratch_shapes=[pltpu.VMEM((2, n_local, d // 2), U32), pltpu.SemaphoreType.DMA((2,))],
        ),
        out_shape=[jax.ShapeDtypeStruct((n_ptiles * tm, d), F32),
                   jax.ShapeDtypeStruct(((nt - n_ptiles) * tm, d), F32)],
        compiler_params=_cparams("arbitrary"),
        name="combine",
    )(*tables, x1, gt_seq, gt_tok, gate, e_idx, lrank, toff_vec, lnw, lnb, yb)


def _rotary_tables(pos, hd, rot_dim):
    half = rot_dim // 2
    inv_freq = jnp.power(jnp.float32(ROPE_THETA), -jnp.arange(half, dtype=F32) * (2.0 / rot_dim))
    ang = pos.astype(F32)[:, None] * inv_freq[None, :]
    cos, sin = jnp.cos(ang), jnp.sin(ang)
    n = pos.shape[0]
    ones = jnp.ones((n, hd - rot_dim), F32)
    zeros = jnp.zeros((n, hd - rot_dim), F32)
    zh = jnp.zeros((n, half), F32)
    ct = jnp.concatenate([cos, cos, ones], axis=1)
    s1 = jnp.concatenate([-sin, zh, zeros], axis=1)
    s2 = jnp.concatenate([zh, sin, zeros], axis=1)
    reps = LANES // hd
    return tuple(jnp.tile(a, (1, reps)) for a in (ct, s1, s2))


def kernel(x_prompt, x_sample, state_win_k, state_win_v, state_conv, state_ssm, c_prompt, c_sample, w_ada, b_ada, w_in, attn_sinks, w_conv, dn_a_log, dn_dt_bias, dn_norm_w, w_proj_attn, w_proj_dn, w_out, ln1_w, ln1_b, w_router, b_router, w_gu, b_gu, w_down, b_down, ln2_w, ln2_b):
    n_p, seq, d = x_prompt.shape
    n_s, l_s, _ = x_sample.shape
    depth = w_ada.shape[0]
    window, n_kv, hd = state_win_k.shape[2:]
    n_q = attn_sinks.shape[1]
    heads, dk, dv = state_ssm.shape[2:]
    conv_w, conv_dim = w_conv.shape[1:]
    n_e = w_router.shape[2]
    qd, kd = n_q * hd, n_kv * hd
    vdim = heads * dv
    rot_dim = hd // 4
    alpha = float((2 * depth) ** 0.25)
    tm = TOKEN_TILE
    t_p, t_s = n_p * seq, n_s * l_s
    t_all = t_p + t_s
    tps = seq // tm
    n_ptiles = t_p // tm
    l_pad = SUBLANES
    assert seq % tm == 0 and t_s % tm == 0 and tm % l_s == 0 and l_s <= l_pad and l_s >= conv_w - 1
    assert 2 * heads <= SUBLANES and n_e <= LANES and hd * 2 == LANES and rot_dim == 2 * SUBLANES

    sizes = [qd, kd, kd, conv_dim, vdim, heads, heads, d, d]
    offs = np.concatenate([[0], np.cumsum(sizes)])
    seg = lambda k: np.arange(offs[k], offs[k + 1])
    perm = np.concatenate([seg(0), seg(1), seg(2), seg(3), seg(4), seg(7), seg(8), seg(5), seg(6)])
    cuts = (qd + 2 * kd, qd + 2 * kd + conv_dim, qd + 2 * kd + conv_dim + vdim, qd + 2 * kd + conv_dim + vdim + 2 * d)
    cuts = cuts + (cuts[-1] + LANES,)

    tabs_p = _rotary_tables(jnp.arange(seq, dtype=I32), hd, rot_dim)
    tabs_s = tuple(jnp.tile(a, (tm // l_s, 1))
                   for a in _rotary_tables(PAST_LEN + jnp.arange(l_s, dtype=I32), hd, rot_dim))

    x_p = x_prompt.reshape(t_p, d)
    x_s = x_sample.reshape(t_s, d)
    c_all = jnp.concatenate([c_prompt, c_sample], axis=0)
    outs = {k: [] for k in ("pwk", "pwv", "pcv", "pss", "swk", "swv", "scv", "sss")}

    for l in range(depth):
        w_in_l = jnp.pad(w_in[l][:, perm], ((0, 0), (0, LANES - 2 * heads))).astype(BF16)
        mod = _ada(c_all, w_ada[l], b_ada[l])
        mod_p = mod[:n_p].reshape(n_p, 6, 1, d)
        mod_s = jnp.repeat(mod[n_p:].reshape(n_s, 6, d), l_s, axis=0)
        sh1p, sc1p, gt1p, sh2p, sc2p, gt2p = [mod_p[:, k] for k in range(6)]
        sh1s, sc1s, gt1s, sh2s, sc2s, gt2s = [mod_s[:, k] for k in range(6)]

        qkva_p, dn_p, z_p, g_p, ba_p, tail_p, kvw_p = _inproj(
            x_p, sh1p, sc1p, tabs_p, w_in_l, cuts, per_token_mod=False, tiles_per_seq=tps, act_dtype=BF16,
            window=window, kv_cols=2 * kd)
        attn_p = _attn_prompt(qkva_p, attn_sinks[l], n_p, seq, n_q, n_kv, hd, window)
        chunk = min(DN_CHUNK, seq)
        nc = seq // chunk
        bat_p = ba_p[:, :SUBLANES].reshape(n_p, nc, chunk, SUBLANES).transpose(0, 1, 3, 2)
        hp = jnp.zeros((SUBLANES, LANES), F32).at[0, :heads].set(dn_a_log[l]).at[1, :heads].set(dn_dt_bias[l])
        nw = dn_norm_w[l].reshape(1, dv)
        o_p, ssm_p = _deltanet(dn_p.reshape(n_p, seq, conv_dim), z_p.reshape(n_p, seq, vdim),
                               ba_p.reshape(n_p, seq, LANES), bat_p,
                               jnp.zeros((n_p, SUBLANES, conv_dim), F32), jnp.zeros((n_p, heads, dk, dv), F32),
                               w_conv[l], hp, nw, chunk=chunk, l_real=chunk, nb=4)
        outs["pwk"].append(kvw_p[:, :, :kd].reshape(n_p, window, n_kv, hd))
        outs["pwv"].append(kvw_p[:, :, kd:].reshape(n_p, window, n_kv, hd))
        outs["pcv"].append(tail_p.reshape(n_p, tps, SUBLANES, conv_dim)[:, -1, SUBLANES - (conv_w - 1):])
        outs["pss"].append(ssm_p)

        qkva_s, dn_s, z_s, g_s, ba_s = _inproj(
            x_s, sh1s, sc1s, tabs_s, w_in_l, cuts, per_token_mod=True, tiles_per_seq=1, act_dtype=F32,
            window=window, kv_cols=2 * kd)
        pad_l = lambda a: jnp.pad(a.reshape(n_s, l_s, a.shape[-1]), ((0, 0), (0, l_pad - l_s), (0, 0)))
        attn_s, wk_s, wv_s = _attn_sample(pad_l(qkva_s), state_win_k[l].reshape(n_s, window, kd),
                                          state_win_v[l].reshape(n_s, window, kd), attn_sinks[l], n_q, n_kv, hd, l_s)
        attn_s = attn_s[:, :l_s].reshape(t_s, qd)
        ba_s3 = pad_l(ba_s)
        bat_s = ba_s3[:, :, :SUBLANES].transpose(0, 2, 1).reshape(n_s, 1, SUBLANES, l_pad)
        cs0 = jnp.pad(state_conv[l], ((0, 0), (SUBLANES - (conv_w - 1), 0), (0, 0)))
        o_s, ssm_s = _deltanet(pad_l(dn_s), pad_l(z_s), ba_s3, bat_s, cs0, state_ssm[l], w_conv[l], hp, nw,
                               chunk=l_pad, l_real=l_s, nb=8)
        o_s = o_s[:, :l_s].reshape(t_s, vdim)
        outs["swk"].append(wk_s.reshape(n_s, window, n_kv, hd))
        outs["swv"].append(wv_s.reshape(n_s, window, n_kv, hd))
        outs["scv"].append(jnp.concatenate([state_conv[l], dn_s.reshape(n_s, l_s, conv_dim)], axis=1)[:, -(conv_w - 1):])
        outs["sss"].append(ssm_s)

        wr = jnp.pad(w_router[l], ((0, 0), (0, LANES - n_e)))
        br = jnp.pad(b_router[l], (0, LANES - n_e), constant_values=NEG_BIG).reshape(1, LANES)
        wr_hi = wr.astype(BF16)
        wr_lo = (wr - wr_hi.astype(F32)).astype(BF16)
        wts = (w_proj_attn[l].astype(BF16), w_proj_dn[l].astype(BF16), w_out[l].astype(BF16),
               ln1_w[l].reshape(1, d), ln1_b[l].reshape(1, d), wr_hi, wr_lo, br)
        res_p = _outproj(attn_p, o_p.reshape(t_p, vdim), g_p, x_p, gt1p, sh2p, sc2p, wts, None,
                         per_token_mod=False, tiles_per_seq=tps, t_total=t_all, tile_off=0, alpha=alpha)
        x1, e_idx, gate, rank, cnt = _outproj(attn_s, o_s, g_s, x_s, gt1s, sh2s, sc2s, wts, res_p,
                                              per_token_mod=True, tiles_per_seq=1, t_total=t_all, tile_off=n_ptiles,
                                              alpha=alpha)

        bm = EXPERT_ROWS
        nt_all = t_all // tm
        cnt8 = (cnt[:, 0, :n_e].astype(I32) + RUN_ALIGN - 1) // RUN_ALIGN * RUN_ALIGN
        tot = jnp.sum(cnt8, axis=0)
        padded = (tot + bm - 1) // bm * bm
        pad_end = jnp.cumsum(padded)
        pad_start = pad_end - padded
        gbase = pad_start[None, :] + jnp.cumsum(cnt8, axis=0) - cnt8
        toff = jnp.cumsum(cnt8, axis=1) - cnt8
        n_local = -(-(tm * TOP_K + n_e * (RUN_ALIGN - 1)) // LANES) * LANES
        n_rows = -(-(t_all * TOP_K + nt_all * n_e * (RUN_ALIGN - 1) + n_e * (bm - 1)) // bm) * bm
        nblk = n_rows // bm
        n_used = jnp.maximum(pad_end[-1:] // bm, 1).astype(I32)
        block_e = jnp.minimum(jnp.sum(pad_end[None, :] <= (jnp.arange(nblk, dtype=I32) * bm)[:, None], axis=1),
                              n_e - 1).astype(I32)
        flat = lambda a: a.astype(I32).reshape(nt_all * n_e)
        run_tables = (flat(toff), flat(cnt8), flat(gbase))
        tail_tables = ((pad_start + tot).astype(I32), (padded - tot).astype(I32))
        toff_vec = jnp.pad(toff.astype(F32), ((0, 0), (0, LANES - n_e))).reshape(nt_all, 1, LANES)

        xs = _dispatch(run_tables + tail_tables, x1, sh2p, sc2p, sh2s, sc2s, e_idx, rank, toff_vec, n_rows=n_rows,
                       n_local=n_local, n_ptiles=n_ptiles, tiles_per_seq=tps, n_experts=n_e)
        yb = _experts(block_e, n_used, xs, w_gu[l].astype(BF16), b_gu[l], w_down[l].astype(BF16), b_down[l])
        x_p, x_s = _combine(run_tables, x1, gt2p, gt2s, gate, e_idx, rank, toff_vec, ln2_w[l].reshape(1, d),
                            ln2_b[l].reshape(1, d), yb, n_local=n_local, n_ptiles=n_ptiles, tiles_per_seq=tps,
                            n_experts=n_e, alpha=alpha)

    st = lambda k: jnp.stack(outs[k])
    return (x_p.reshape(n_p, seq, d), x_s.reshape(n_s, l_s, d), st("pwk"), st("pwv"), st("pcv"), st("pss"),
            st("swk"), st("swv"), st("scv"), st("sss"))
```

```python
import functools

import numpy as np
import jax
import jax.numpy as jnp
from jax import lax
from jax.experimental import pallas as pl
from jax.experimental.pallas import tpu as pltpu

F32 = jnp.float32
BF16 = jnp.bfloat16
I32 = jnp.int32
U32 = jnp.uint32

PAST_LEN = 16384
ROPE_THETA = 500000.0
TOP_K = 4
SWIGLU_LIMIT = 7.0
SWIGLU_ALPHA = 1.702
DN_CHUNK = 64
LN_EPS = 1e-5
RMS_EPS = 1e-6
L2_EPS = 1e-6

LANES = 128
SUBLANES = 8
VMEM_LIMIT_BYTES = 56 * 1024 * 1024

TOKEN_TILE = 256
EXPERT_ROWS = 512
RUN_ALIGN = SUBLANES
OUTPROJ_SUBTILES = 2
N_ROUTE_BUFS = 5
NEG_BIG = -1e30


def _cparams(*sem):
    return pltpu.CompilerParams(dimension_semantics=sem, vmem_limit_bytes=VMEM_LIMIT_BYTES)


def _silu(x):
    return x * jax.nn.sigmoid(x)


def _bdot(a, b):
    return jnp.dot(a.astype(BF16), b.astype(BF16), preferred_element_type=F32)


def _pack_halves(x):
    n = x.shape[1] // 2
    lo = lax.bitcast_convert_type(x[:, :n], U32)
    hi = lax.bitcast_convert_type(x[:, n:], U32)
    return (hi & jnp.uint32(0xFFFF0000)) | (lo >> 16)


def _unpack_halves(w):
    lo = lax.bitcast_convert_type(w << 16, F32).astype(BF16)
    hi = lax.bitcast_convert_type(w & jnp.uint32(0xFFFF0000), F32).astype(BF16)
    return lo, hi


def _ada_body(c_ref, w_ref, b_ref, o_ref):
    o_ref[...] = _bdot(_silu(c_ref[...]), w_ref[...]) + b_ref[...]


def _ada(c_all, w_ada, b_ada):
    n, d = c_all.shape
    dout = w_ada.shape[1]
    tn = d
    return pl.pallas_call(
        _ada_body,
        grid=(dout // tn,),
        in_specs=[pl.BlockSpec((n, d), lambda j: (0, 0)),
                  pl.BlockSpec((d, tn), lambda j: (0, j)),
                  pl.BlockSpec((1, tn), lambda j: (0, j))],
        out_specs=pl.BlockSpec((n, tn), lambda j: (0, j)),
        out_shape=jax.ShapeDtypeStruct((n, dout), F32),
        compiler_params=_cparams("arbitrary"),
        name="ada",
    )(c_all, w_ada, b_ada.reshape(1, dout))


def _inproj_body(x_ref, sh_ref, sc_ref, ct_ref, s1_ref, s2_ref, w_ref,
                 a_ref, dn_ref, z_ref, g_ref, ba_ref, *win_refs, cuts, n_rot_chunks, window):
    h = (x_ref[...] * (1.0 + sc_ref[...]) + sh_ref[...]).astype(BF16)

    def mm(lo, hi):
        return jnp.dot(h, w_ref[:, lo:hi], preferred_element_type=F32)

    c_a, c_dn, c_z, c_g, c_ba = cuts
    qkv = mm(0, c_a)
    ct, s1, s2 = ct_ref[...], s1_ref[...], s2_ref[...]
    cols = []
    for c in range(n_rot_chunks):
        xc = qkv[:, c * LANES:(c + 1) * LANES]
        cols.append(xc * ct + pltpu.roll(xc, LANES - SUBLANES, 1) * s1 + pltpu.roll(xc, SUBLANES, 1) * s2)
    cols.append(qkv[:, n_rot_chunks * LANES:])
    rot = jnp.concatenate(cols, axis=1)
    a_ref[...] = rot.astype(a_ref.dtype)
    dn = mm(c_a, c_dn)
    dn_ref[...] = dn.astype(dn_ref.dtype)
    z_ref[...] = mm(c_dn, c_z).astype(z_ref.dtype)
    g_ref[...] = mm(c_z, c_g).astype(g_ref.dtype)
    ba_ref[...] = mm(c_g, c_ba)
    if win_refs:
        tail_ref, kvw_ref = win_refs
        tm = dn.shape[0]
        tail_ref[...] = dn[tm - SUBLANES:, :]
        kvw_ref[...] = rot[tm - window:, n_rot_chunks * LANES - LANES:]


def _inproj(x, sh, sc, tabs, w_perm, cuts, *, per_token_mod, tiles_per_seq, act_dtype, window, kv_cols):
    t, d = x.shape
    tm = TOKEN_TILE
    nt = t // tm
    c_a, c_dn, c_z, c_g, c_ba = cuts
    n_rot_chunks = (c_a - kv_cols // 2) // LANES
    if per_token_mod:
        mod_spec = pl.BlockSpec((tm, d), lambda i: (i, 0))
        tab_spec = pl.BlockSpec((tm, LANES), lambda i: (0, 0))
    else:
        mod_spec = pl.BlockSpec((None, 1, d), lambda i: (i // tiles_per_seq, 0, 0))
        tab_spec = pl.BlockSpec((tm, LANES), lambda i: (i % tiles_per_seq, 0))
    out_shape = [jax.ShapeDtypeStruct((t, c_a), act_dtype),
                 jax.ShapeDtypeStruct((t, c_dn - c_a), act_dtype),
                 jax.ShapeDtypeStruct((t, c_z - c_dn), act_dtype),
                 jax.ShapeDtypeStruct((t, c_g - c_z), act_dtype),
                 jax.ShapeDtypeStruct((t, c_ba - c_g), F32)]
    out_specs = [pl.BlockSpec((tm, s.shape[1]), lambda i: (i, 0)) for s in out_shape]
    with_win = not per_token_mod
    if with_win:
        n_seq = nt // tiles_per_seq
        out_shape += [jax.ShapeDtypeStruct((nt, SUBLANES, c_dn - c_a), F32),
                      jax.ShapeDtypeStruct((n_seq, window, kv_cols), F32)]
        out_specs += [pl.BlockSpec((None, SUBLANES, c_dn - c_a), lambda i: (i, 0, 0)),
                      pl.BlockSpec((None, window, kv_cols), lambda i: (i // tiles_per_seq, 0, 0))]
    body = functools.partial(_inproj_body, cuts=cuts, n_rot_chunks=n_rot_chunks, window=window)
    return pl.pallas_call(
        body,
        grid=(nt,),
        in_specs=[pl.BlockSpec((tm, d), lambda i: (i, 0)), mod_spec, mod_spec,
                  tab_spec, tab_spec, tab_spec,
                  pl.BlockSpec((d, c_ba), lambda i: (0, 0))],
        out_specs=out_specs,
        out_shape=out_shape,
        compiler_params=_cparams("arbitrary"),
        name="inproj",
    )(x, sh, sc, *tabs, w_perm)


def _softmax_sink_pv(s, valid, sink, v):
    s = jnp.where(valid, s, -jnp.inf)
    m = jnp.maximum(jnp.max(s, axis=-1, keepdims=True), sink)
    p = jnp.exp(s - m)
    denom = jnp.sum(p, axis=-1, keepdims=True) + jnp.exp(sink - m)
    return jnp.dot((p / denom).astype(BF16), v, preferred_element_type=F32)


def _attn_prompt_body(sink_ref, q_ref, kvp_ref, kvc_ref, o_ref, *, n_q, n_kv, hd, window):
    j = pl.program_id(1)
    group = n_q // n_kv
    scale = hd ** -0.5
    assert np.log2(scale) == int(np.log2(scale))
    r = lax.broadcasted_iota(I32, (window, 2 * window), 0)
    c = lax.broadcasted_iota(I32, (window, 2 * window), 1)
    rel = window + r - c
    band = (rel >= 0) & (rel < window)
    kv_cur = kvc_ref[...]
    kv_first = jnp.concatenate([kvp_ref[...], kv_cur[:window]], axis=0)
    for qb, (kv, valid) in enumerate(((kv_first, band & ((c >= window) | (j > 0))), (kv_cur, band))):
        q = q_ref[qb * window:(qb + 1) * window, :] * scale
        outs = []
        for h in range(n_q):
            kvh = h // group
            qh = q[:, h * hd:(h + 1) * hd]
            kh = kv[:, kvh * hd:(kvh + 1) * hd]
            vh = kv[:, (n_kv + kvh) * hd:(n_kv + kvh + 1) * hd]
            s = lax.dot_general(qh, kh, (((1,), (1,)), ((), ())), preferred_element_type=F32)
            outs.append(_softmax_sink_pv(s, valid, sink_ref[h], vh))
        o_ref[qb * window:(qb + 1) * window, :] = jnp.concatenate(outs, axis=1).astype(o_ref.dtype)


def _attn_prompt(qkva, sinks, n_seq, seq, n_q, n_kv, hd, window):
    qd, kvd = n_q * hd, 2 * n_kv * hd
    x3 = qkva.reshape(n_seq, seq, qd + kvd)
    nb = seq // window
    assert nb % 2 == 0 and qd % kvd == 0
    kv_blk = qd // kvd
    body = functools.partial(_attn_prompt_body, n_q=n_q, n_kv=n_kv, hd=hd, window=window)
    out = pl.pallas_call(
        body,
        grid=(n_seq, nb // 2),
        in_specs=[pl.BlockSpec(memory_space=pltpu.SMEM),
                  pl.BlockSpec((None, 2 * window, qd), lambda n, j: (n, j, 0)),
                  pl.BlockSpec((None, window, kvd), lambda n, j: (n, jnp.maximum(2 * j - 1, 0), kv_blk)),
                  pl.BlockSpec((None, 2 * window, kvd), lambda n, j: (n, j, kv_blk))],
        out_specs=pl.BlockSpec((None, 2 * window, qd), lambda n, j: (n, j, 0)),
        out_shape=jax.ShapeDtypeStruct((n_seq, seq, qd), BF16),
        compiler_params=_cparams("arbitrary", "arbitrary"),
        name="attn_prompt",
    )(sinks, x3, x3, x3)
    return out.reshape(n_seq * seq, qd)


def _attn_sample_body(sink_ref, q_ref, wk_ref, wv_ref, o_ref, wko_ref, wvo_ref,
                      *, bs, n_q, n_kv, hd, window, l_new, l_pad):
    group = n_q // n_kv
    qd = n_q * hd
    kd = n_kv * hd
    rows = group * l_pad
    r = lax.broadcasted_iota(I32, (rows, window + l_pad), 0) % l_pad
    c = lax.broadcasted_iota(I32, (rows, window + l_pad), 1)
    rel = window + r - c
    valid = (rel >= 0) & (rel < window) & (c < window + l_new)
    for b in range(bs):
        x = q_ref[b]
        k_new = x[:, qd:qd + kd]
        v_new = x[:, qd + kd:]
        wko_ref[b, 0:window - l_new, :] = wk_ref[b, l_new:window, :]
        wko_ref[b, window - l_new:window, :] = k_new[0:l_new, :]
        wvo_ref[b, 0:window - l_new, :] = wv_ref[b, l_new:window, :]
        wvo_ref[b, window - l_new:window, :] = v_new[0:l_new, :]
        k_all = jnp.concatenate([wk_ref[b], k_new], axis=0).astype(BF16)
        v_all = jnp.concatenate([wv_ref[b], v_new], axis=0).astype(BF16)
        outs = [None] * n_q
        for kvh in range(n_kv):
            qg = jnp.concatenate([x[:, (kvh * group + g) * hd:(kvh * group + g + 1) * hd] for g in range(group)],
                                 axis=0).astype(BF16)
            kh = k_all[:, kvh * hd:(kvh + 1) * hd]
            vh = v_all[:, kvh * hd:(kvh + 1) * hd]
            s = lax.dot_general(qg, kh, (((1,), (1,)), ((), ())), preferred_element_type=F32) * (hd ** -0.5)
            sink = jnp.concatenate([jnp.full((l_pad, 1), sink_ref[kvh * group + g], F32) for g in range(group)], axis=0)
            o = _softmax_sink_pv(s, valid, sink, vh)
            for g in range(group):
                outs[kvh * group + g] = o[g * l_pad:(g + 1) * l_pad, :]
        o_ref[b] = jnp.concatenate(outs, axis=1).astype(o_ref.dtype)


def _attn_sample(qkva_pad, win_k, win_v, sinks, n_q, n_kv, hd, l_new):
    n, l_pad, width = qkva_pad.shape
    window, kd = win_k.shape[1], win_k.shape[2]
    qd = n_q * hd
    bs = 8
    body = functools.partial(_attn_sample_body, bs=bs, n_q=n_q, n_kv=n_kv, hd=hd, window=window,
                             l_new=l_new, l_pad=l_pad)
    return pl.pallas_call(
        body,
        grid=(n // bs,),
        in_specs=[pl.BlockSpec(memory_space=pltpu.SMEM),
                  pl.BlockSpec((bs, l_pad, width), lambda i: (i, 0, 0)),
                  pl.BlockSpec((bs, window, kd), lambda i: (i, 0, 0)),
                  pl.BlockSpec((bs, window, kd), lambda i: (i, 0, 0))],
        out_specs=[pl.BlockSpec((bs, l_pad, qd), lambda i: (i, 0, 0)),
                   pl.BlockSpec((bs, window, kd), lambda i: (i, 0, 0)),
                   pl.BlockSpec((bs, window, kd), lambda i: (i, 0, 0))],
        out_shape=[jax.ShapeDtypeStruct((n, l_pad, qd), BF16),
                   jax.ShapeDtypeStruct((n, window, kd), F32),
                   jax.ShapeDtypeStruct((n, window, kd), F32)],
        compiler_params=_cparams("arbitrary"),
        name="attn_sample",
    )(sinks, qkva_pad, win_k, win_v)


def _split_bf16(x):
    hi = x.astype(BF16)
    return hi, (x - hi.astype(F32)).astype(BF16)


def _tdot(a, b):
    ah, al = _split_bf16(a)
    bh, bl = _split_bf16(b)
    m = a.shape[0]
    t = jnp.dot(jnp.concatenate([ah, al], axis=0), bh, preferred_element_type=F32)
    return t[:m] + t[m:] + jnp.dot(ah, bl, preferred_element_type=F32)


def _dn_body(qkv_ref, z_ref, ba_ref, bat_ref, cs0_ref, s0_ref, wc_ref, hp_ref, nw_ref,
             o_ref, s_ref, xbuf, *, nb, chunk, heads, dk, dv, l_real, conv_w):
    c_idx = pl.program_id(1)
    hc = SUBLANES

    @pl.when(c_idx == 0)
    def _():
        xbuf[:, 0:hc, :] = cs0_ref[...]
        s_ref[...] = s0_ref[...]

    qk_dim = heads * dk
    row = lax.broadcasted_iota(I32, (chunk, chunk), 0)
    col = lax.broadcasted_iota(I32, (chunk, chunk), 1)
    incl = row >= col
    strict = row > col
    eye = (row == col).astype(F32)
    valid_c = row[:, 0:1] < l_real
    valid_r = col[0:1, :] < l_real
    n_levels = max(1, int(np.ceil(np.log2(chunk))))
    wc = wc_ref[...]
    hp = hp_ref[...]
    neg_exp_alog = -jnp.exp(hp[0:1, :])
    dt_bias = hp[1:2, :]
    nw = nw_ref[...]
    chains = [(b, h) for b in range(nb) for h in range(heads)]
    n = len(chains)

    ys = []
    for b in range(nb):
        xbuf[b, hc:hc + chunk, :] = qkv_ref[b].astype(F32)
        y = xbuf[b, hc:hc + chunk, :] * wc[conv_w - 1:conv_w, :]
        for j in range(conv_w - 1):
            off = hc - (conv_w - 1) + j
            y = y + xbuf[b, off:off + chunk, :] * wc[j:j + 1, :]
        ys.append(_silu(y))
        xbuf[b, 0:hc, :] = xbuf[b, chunk:chunk + hc, :]

    qn, kn, kb, vb, decay, e_gc, e_rest, e_last = [], [], [], [], [], [], [], []
    for b, h in chains:
        y = ys[b]
        qh = y[:, h * dk:(h + 1) * dk]
        kh = y[:, qk_dim + h * dk:qk_dim + (h + 1) * dk]
        vh = y[:, 2 * qk_dim + h * dv:2 * qk_dim + (h + 1) * dv]
        ba = ba_ref[b]
        bat = bat_ref[b]
        ne = neg_exp_alog[:, h:h + 1]
        db = dt_bias[:, h:h + 1]
        beta = jnp.where(valid_c, jax.nn.sigmoid(ba[:, h:h + 1]), 0.0)
        g_col = jnp.where(valid_c, ne * jax.nn.softplus(ba[:, heads + h:heads + h + 1] + db), 0.0)
        g_row = jnp.where(valid_r, ne * jax.nn.softplus(bat[heads + h:heads + h + 1, :] + db), 0.0)
        gc_col = jnp.sum(jnp.where(incl, g_row, 0.0), axis=1, keepdims=True)
        gc_row = jnp.sum(jnp.where(row <= col, g_col, 0.0), axis=0, keepdims=True)
        g_last = gc_col[chunk - 1:chunk, :]
        q_ = qh * lax.rsqrt(jnp.sum(qh * qh, -1, keepdims=True) + L2_EPS) * (dk ** -0.5)
        k_ = kh * lax.rsqrt(jnp.sum(kh * kh, -1, keepdims=True) + L2_EPS)
        qn.append(q_)
        kn.append(k_)
        kb.append(k_ * beta)
        vb.append(vh * beta)
        decay.append(jnp.where(incl, jnp.exp(gc_col - gc_row), 0.0))
        e_gc.append(jnp.exp(gc_col))
        e_rest.append(jnp.exp(g_last - gc_col))
        e_last.append(jnp.exp(g_last))

    sc = [lax.dot_general(jnp.concatenate([qn[i], kb[i]], axis=0).astype(BF16), kn[i].astype(BF16),
                          (((1,), (1,)), ((), ())), preferred_element_type=F32) for i in range(n)]
    qk = [sc[i][:chunk] * decay[i] for i in range(n)]
    p = [jnp.where(strict, -(sc[i][chunk:] * decay[i]), 0.0) for i in range(n)]
    t_inv = [eye + p[i] for i in range(n)]
    if n_levels > 1:
        p = [_tdot(p[i], p[i]) for i in range(n)]
    for lvl in range(1, n_levels):
        if lvl < n_levels - 1:
            yp = [_tdot(jnp.concatenate([t_inv[i], p[i]], axis=0), p[i]) for i in range(n)]
            t_inv = [t_inv[i] + yp[i][:chunk] for i in range(n)]
            p = [yp[i][chunk:] for i in range(n)]
        else:
            t_inv = [t_inv[i] + _tdot(t_inv[i], p[i]) for i in range(n)]
    sol = [_tdot(t_inv[i], jnp.concatenate([vb[i], kb[i] * e_gc[i]], axis=1)) for i in range(n)]
    s_old = [s_ref[b, h] for b, h in chains]
    wq = [_bdot(jnp.concatenate([sol[i][:, dv:], qn[i] * e_gc[i]], axis=0), s_old[i]) for i in range(n)]
    v_new = [sol[i][:, :dv] - wq[i][:chunk] for i in range(n)]
    o = [wq[i][chunk:] + _bdot(qk[i], v_new[i]) for i in range(n)]
    for i, (b, h) in enumerate(chains):
        s_ref[b, h] = s_old[i] * e_last[i] + lax.dot_general(
            (kn[i] * e_rest[i]).astype(BF16), v_new[i].astype(BF16), (((0,), (0,)), ((), ())),
            preferred_element_type=F32)
    for b in range(nb):
        zt = z_ref[b].astype(F32)
        outs = []
        for h in range(heads):
            oi = o[b * heads + h]
            on = oi * lax.rsqrt(jnp.mean(oi * oi, -1, keepdims=True) + RMS_EPS) * nw
            outs.append(on * _silu(zt[:, h * dv:(h + 1) * dv]))
        o_ref[b] = jnp.concatenate(outs, axis=1).astype(o_ref.dtype)


def _deltanet(qkv, z, ba, bat, cs0, s0, w_conv, hp, norm_w, *, chunk, l_real, nb):
    n, l, conv_dim = qkv.shape
    heads, dk, dv = s0.shape[1:]
    nc = l // chunk
    assert n % nb == 0 and l % chunk == 0
    conv_w = w_conv.shape[0]
    body = functools.partial(_dn_body, nb=nb, chunk=chunk, heads=heads, dk=dk, dv=dv, l_real=l_real, conv_w=conv_w)
    return pl.pallas_call(
        body,
        grid=(n // nb, nc),
        in_specs=[pl.BlockSpec((nb, chunk, conv_dim), lambda i, c: (i, c, 0)),
                  pl.BlockSpec((nb, chunk, heads * dv), lambda i, c: (i, c, 0)),
                  pl.BlockSpec((nb, chunk, LANES), lambda i, c: (i, c, 0)),
                  pl.BlockSpec((nb, None, SUBLANES, chunk), lambda i, c: (i, c, 0, 0)),
                  pl.BlockSpec((nb, SUBLANES, conv_dim), lambda i, c: (i, 0, 0)),
                  pl.BlockSpec((nb, heads, dk, dv), lambda i, c: (i, 0, 0, 0)),
                  pl.BlockSpec((conv_w, conv_dim), lambda i, c: (0, 0)),
                  pl.BlockSpec((SUBLANES, LANES), lambda i, c: (0, 0)),
                  pl.BlockSpec((1, dv), lambda i, c: (0, 0))],
        out_specs=[pl.BlockSpec((nb, chunk, heads * dv), lambda i, c: (i, c, 0)),
                   pl.BlockSpec((nb, heads, dk, dv), lambda i, c: (i, 0, 0, 0))],
        out_shape=[jax.ShapeDtypeStruct((n, l, heads * dv), BF16),
                   jax.ShapeDtypeStruct((n, heads, dk, dv), F32)],
        scratch_shapes=[pltpu.VMEM((nb, SUBLANES + chunk, conv_dim), F32)],
        compiler_params=_cparams("arbitrary", "arbitrary"),
        name="deltanet",
    )(qkv, z, ba, bat, cs0, s0, w_conv, hp, norm_w)


def _layer_norm(r, w, b):
    mu = jnp.mean(r, -1, keepdims=True)
    var = jnp.mean(jnp.square(r - mu), -1, keepdims=True)
    return (r - mu) * lax.rsqrt(var + LN_EPS) * w + b


def _outproj_body(attn_ref, dn_ref, g_ref, x_ref, gt_ref, sh2_ref, sc2_ref, wpa_ref, wpd_ref, wo_ref,
                  lnw_ref, lnb_ref, wrh_ref, wrl_ref, br_ref, *refs, alpha, top_k, aliased):
    if aliased:
        refs = refs[N_ROUTE_BUFS:]
    x1_ref, eidx_ref, gate_ref, rank_ref, cnt_ref = refs
    for s in range(cnt_ref.shape[0]):
        _outproj_tile(s, attn_ref, dn_ref, g_ref, x_ref, gt_ref, sh2_ref, sc2_ref, wpa_ref, wpd_ref, wo_ref, lnw_ref,
                      lnb_ref, wrh_ref, wrl_ref, br_ref, x1_ref, eidx_ref, gate_ref, rank_ref, cnt_ref,
                      alpha=alpha, top_k=top_k)


def _outproj_tile(s, attn_ref, dn_ref, g_ref, x_ref, gt_ref, sh2_ref, sc2_ref, wpa_ref, wpd_ref, wo_ref, lnw_ref,
                  lnb_ref, wrh_ref, wrl_ref, br_ref, x1_ref, eidx_ref, gate_ref, rank_ref, cnt_ref, *, alpha, top_k):
    d = x_ref.shape[1]
    tm = TOKEN_TILE
    rows = pl.ds(s * tm, tm)

    def mod(ref):
        return ref[...] if ref.shape[0] == 1 else ref[rows, :]

    g = g_ref[rows, :].astype(F32)
    pa = jnp.dot(attn_ref[rows, :], wpa_ref[...], preferred_element_type=F32)
    pd = jnp.dot(dn_ref[rows, :], wpd_ref[...], preferred_element_type=F32)
    merged = jax.nn.sigmoid(g[:, :d]) * pa + jax.nn.sigmoid(g[:, d:]) * pd
    mix = jnp.dot(merged.astype(BF16), wo_ref[...], preferred_element_type=F32)
    x1 = _layer_norm(alpha * x_ref[rows, :] + mod(gt_ref) * mix, lnw_ref[...], lnb_ref[...])
    x1_ref[rows, :] = x1
    h2 = x1 * (1.0 + mod(sc2_ref)) + mod(sh2_ref)
    h_hi, h_lo = _split_bf16(h2)
    lg = jnp.dot(jnp.concatenate([h_hi, h_lo], axis=0), wrh_ref[...], preferred_element_type=F32)
    logits = lg[:tm] + lg[tm:] + jnp.dot(h_hi, wrl_ref[...], preferred_element_type=F32) + br_ref[...]
    lane = lax.broadcasted_iota(I32, (tm, LANES), 1)
    lane_f = lane.astype(F32)
    vals, idxs, sels = [], [], []
    l = logits
    for _ in range(top_k):
        m = jnp.max(l, axis=1, keepdims=True)
        idx = jnp.min(jnp.where(l == m, lane_f, float(LANES)), axis=1, keepdims=True)
        sel = lane_f == idx
        vals.append(m)
        idxs.append(idx)
        sels.append(sel)
        l = jnp.where(sel, -jnp.inf, l)
    ex = [jnp.exp(v - vals[0]) for v in vals]
    den = ex[0]
    for e in ex[1:]:
        den = den + e
    multi_hot = jnp.zeros((tm, LANES), F32)
    for sel in sels:
        multi_hot = multi_hot + jnp.where(sel, 1.0, 0.0)
    r_i = lax.broadcasted_iota(I32, (tm, tm), 0)
    c_i = lax.broadcasted_iota(I32, (tm, tm), 1)
    lower = jnp.where(r_i > c_i, 1.0, 0.0).astype(BF16)
    prefix = jnp.dot(lower, multi_hot.astype(BF16), preferred_element_type=F32)
    e_out = jnp.zeros((tm, LANES), F32)
    g_out = jnp.zeros((tm, LANES), F32)
    r_out = jnp.zeros((tm, LANES), F32)
    for k in range(top_k):
        rank_k = jnp.sum(jnp.where(sels[k], prefix, 0.0), axis=1, keepdims=True)
        e_out = jnp.where(lane == k, idxs[k], e_out)
        g_out = jnp.where(lane == k, ex[k] / den, g_out)
        r_out = jnp.where(lane == k, rank_k, r_out)
    eidx_ref[rows, :] = e_out.astype(I32)
    gate_ref[rows, :] = g_out
    rank_ref[rows, :] = r_out.astype(I32)
    cnt_ref[s] = jnp.broadcast_to(jnp.sum(multi_hot, axis=0, keepdims=True), cnt_ref.shape[1:])


def _outproj(attn, dn, gates, x, gt, sh2, sc2, wts, bufs, *, per_token_mod, tiles_per_seq, t_total, tile_off, alpha):
    t, d = x.shape
    sub = OUTPROJ_SUBTILES
    tm = TOKEN_TILE * sub
    nt = t // tm
    assert t % tm == 0 and tile_off % sub == 0 and (per_token_mod or tiles_per_seq % sub == 0)
    wpa, wpd, wo, lnw, lnb, wrh, wrl, br = wts
    if per_token_mod:
        mod_spec = pl.BlockSpec((tm, d), lambda i: (i, 0))
    else:
        mod_spec = pl.BlockSpec((None, 1, d), lambda i: (i // (tiles_per_seq // sub), 0, 0))

    def row(width):
        return pl.BlockSpec((tm, width), lambda i: (i, 0))

    def full(a):
        return pl.BlockSpec(a.shape, lambda i: (0,) * a.ndim)

    aliased = bufs is not None
    in_specs = [row(attn.shape[1]), row(dn.shape[1]), row(gates.shape[1]), row(d), mod_spec, mod_spec, mod_spec,
                full(wpa), full(wpd), full(wo), full(lnw), full(lnb), full(wrh), full(wrl), full(br)]
    args = [attn, dn, gates, x, gt, sh2, sc2, wpa, wpd, wo, lnw, lnb, wrh, wrl, br]
    io_alias = {}
    if aliased:
        for k, bfr in enumerate(bufs):
            in_specs.append(pl.BlockSpec(memory_space=pl.ANY))
            io_alias[len(args)] = k
            args.append(bfr)
    step_off = tile_off // sub
    out_row = lambda width: pl.BlockSpec((tm, width), lambda i: (i + step_off, 0))
    out_shape = [jax.ShapeDtypeStruct((t_total, d), F32), jax.ShapeDtypeStruct((t_total, LANES), I32),
                 jax.ShapeDtypeStruct((t_total, LANES), F32), jax.ShapeDtypeStruct((t_total, LANES), I32),
                 jax.ShapeDtypeStruct((t_total // TOKEN_TILE, SUBLANES, LANES), F32)]
    assert len(out_shape) == N_ROUTE_BUFS
    out_specs = [out_row(d), out_row(LANES), out_row(LANES), out_row(LANES),
                 pl.BlockSpec((sub, SUBLANES, LANES), lambda i: (i + step_off, 0, 0))]
    body = functools.partial(_outproj_body, alpha=alpha, top_k=TOP_K, aliased=aliased)
    return pl.pallas_call(
        body,
        grid=(nt,),
        in_specs=in_specs,
        out_specs=out_specs,
        out_shape=out_shape,
        input_output_aliases=io_alias,
        compiler_params=_cparams("arbitrary"),
        name="outproj",
    )(*args)


def _select_mod(i, n_ptiles, seq_ref, tok_ref):
    return jnp.where(i < n_ptiles, seq_ref[...], tok_ref[...])


def _for_run(rows, local_off, global_off, fn):
    @pl.when(rows > 0)
    def _():
        fn(pl.multiple_of(local_off, RUN_ALIGN), pl.multiple_of(global_off, RUN_ALIGN),
           pl.multiple_of(rows, RUN_ALIGN))


def _for_each_run(i, n_experts, toff_ref, cnt8_ref, gbase_ref, fn):
    for e in range(n_experts):
        idx = i * n_experts + e
        _for_run(cnt8_ref[idx], toff_ref[idx], gbase_ref[idx], fn)


def _local_positions(eidx_ref, lrank_ref, toffv_ref, top_k):
    eidx = eidx_ref[...]
    lrank = lrank_ref[...]
    toffv = toffv_ref[...]
    lane = lax.broadcasted_iota(I32, eidx.shape, 1)
    pos = []
    for k in range(top_k):
        base = jnp.sum(jnp.where(lane == eidx[:, k:k + 1], toffv, 0.0), axis=1, keepdims=True)
        pos.append(base.astype(I32) + lrank[:, k:k + 1])
    return pos


def _dispatch_body(toff_ref, cnt8_ref, gbase_ref, tstart_ref, trows_ref,
                   x1_ref, shs_ref, scs_ref, sht_ref, sct_ref, eidx_ref, lrank_ref, toffv_ref,
                   xs_ref, lbuf, zbuf, sem, *, top_k, n_tiles, n_ptiles, n_experts):
    i = pl.program_id(0)
    sc = _select_mod(i, n_ptiles, scs_ref, sct_ref)
    sh = _select_mod(i, n_ptiles, shs_ref, sht_ref)
    h2 = (x1_ref[...] * (1.0 + sc) + sh).astype(BF16)
    tm = h2.shape[0]
    n_local = lbuf.shape[1]
    slot = i % 2

    def runs(tile, sl, act):
        def fn(lo, go, rows):
            act(pltpu.make_async_copy(lbuf.at[sl, pl.ds(lo, rows), :], xs_ref.at[pl.ds(go, rows), :], sem.at[sl]))
        _for_each_run(tile, n_experts, toff_ref, cnt8_ref, gbase_ref, fn)

    start = lambda cp: cp.start()
    wait = lambda cp: cp.wait()

    @pl.when(i >= 2)
    def _():
        runs(i - 2, slot, wait)

    col = lax.broadcasted_iota(I32, (tm, n_local), 1)
    onehot = jnp.zeros((tm, n_local), F32)
    for pos in _local_positions(eidx_ref, lrank_ref, toffv_ref, top_k):
        onehot = onehot + jnp.where(col == pos, 1.0, 0.0)
    srt = lax.dot_general(onehot.astype(BF16), h2, (((0,), (0,)), ((), ())), preferred_element_type=F32)
    lbuf[slot] = _pack_halves(srt)
    runs(i, slot, start)

    @pl.when(i == n_tiles - 1)
    def _():
        runs(i - 1, 1 - slot, wait)
        runs(i, slot, wait)
        zbuf[...] = jnp.zeros(zbuf.shape, zbuf.dtype)

        def zero_fill(e, act):
            _for_run(trows_ref[e], 0, tstart_ref[e], lambda lo, go, rows: act(pltpu.make_async_copy(
                zbuf.at[pl.ds(0, rows), :], xs_ref.at[pl.ds(go, rows), :], sem.at[0])))

        for e in range(n_experts):
            zero_fill(e, start)
        for e in range(n_experts):
            zero_fill(e, wait)


def _dispatch(tables, x1, sh_seq, sc_seq, sh_tok, sc_tok, e_idx, lrank, toff_vec, *, n_rows, n_local, n_ptiles,
              tiles_per_seq, n_experts):
    t, d = x1.shape
    tm = TOKEN_TILE
    nt = t // tm
    n_seq = sh_seq.shape[0]
    assert nt >= 2 and d % 2 == 0
    body = functools.partial(_dispatch_body, top_k=TOP_K, n_tiles=nt, n_ptiles=n_ptiles, n_experts=n_experts)
    seq_spec = pl.BlockSpec((None, 1, d), lambda i, *_: (jnp.minimum(i // tiles_per_seq, n_seq - 1), 0, 0))
    tok_spec = pl.BlockSpec((tm, d), lambda i, *_: (jnp.maximum(i - n_ptiles, 0), 0))
    lane_spec = pl.BlockSpec((tm, LANES), lambda i, *_: (i, 0))
    return pl.pallas_call(
        body,
        grid_spec=pltpu.PrefetchScalarGridSpec(
            num_scalar_prefetch=len(tables),
            grid=(nt,),
            in_specs=[pl.BlockSpec((tm, d), lambda i, *_: (i, 0)),
                      seq_spec, seq_spec, tok_spec, tok_spec, lane_spec, lane_spec,
                      pl.BlockSpec((None, 1, LANES), lambda i, *_: (i, 0, 0))],
            out_specs=pl.BlockSpec(memory_space=pl.ANY),
            scratch_shapes=[pltpu.VMEM((2, n_local, d // 2), U32), pltpu.VMEM((EXPERT_ROWS, d // 2), U32),
                            pltpu.SemaphoreType.DMA((2,))],
        ),
        out_shape=jax.ShapeDtypeStruct((n_rows, d // 2), U32),
        compiler_params=_cparams("arbitrary"),
        name="dispatch",
    )(*tables, x1, sh_seq, sc_seq, sh_tok, sc_tok, e_idx, lrank, toff_vec)


def _expert_body(be_ref, nu_ref, first_ref, xs_ref, wgu_ref, bgu_ref, wd_ref, bd_ref, y_ref, wgu_bf, wd_bf, *, de):
    i = pl.program_id(0)
    used = i < nu_ref[0]

    @pl.when(used & (first_ref[i] == 1))
    def _():
        wgu_bf[...] = wgu_ref[...].astype(BF16)
        wd_bf[...] = wd_ref[...].astype(BF16)

    @pl.when(used)
    def _():
        x_lo, x_hi = _unpack_halves(xs_ref[...])
        half = x_lo.shape[1]
        gu = (jnp.dot(x_lo, wgu_bf[:half, :], preferred_element_type=F32)
              + jnp.dot(x_hi, wgu_bf[half:, :], preferred_element_type=F32) + bgu_ref[...])
        glu = jnp.minimum(gu[:, :de], SWIGLU_LIMIT)
        lin = jnp.clip(gu[:, de:], -SWIGLU_LIMIT, SWIGLU_LIMIT)
        act = glu * jax.nn.sigmoid(SWIGLU_ALPHA * glu) * (lin + 1.0)
        y = jnp.dot(act.astype(BF16), wd_bf[...], preferred_element_type=F32) + bd_ref[...]
        y_ref[...] = _pack_halves(y.astype(BF16).astype(F32))


def _experts(block_e, n_used, xs, w_gu, b_gu, w_down, b_down):
    p, dh = xs.shape
    bm = EXPERT_ROWS
    n_e, d, de2 = w_gu.shape
    de = de2 // 2
    nblk = p // bm

    def blk(i, be, nu, *_):
        return jnp.minimum(i, nu[0] - 1)

    first = jnp.concatenate([jnp.ones((1,), I32), (block_e[1:] != block_e[:-1]).astype(I32)])
    body = functools.partial(_expert_body, de=de)
    return pl.pallas_call(
        body,
        grid_spec=pltpu.PrefetchScalarGridSpec(
            num_scalar_prefetch=3,
            grid=(nblk,),
            in_specs=[pl.BlockSpec((bm, dh), lambda i, be, nu, ft: (blk(i, be, nu), 0)),
                      pl.BlockSpec((None, d, de2), lambda i, be, nu, ft: (be[blk(i, be, nu)], 0, 0)),
                      pl.BlockSpec((None, 1, de2), lambda i, be, nu, ft: (be[blk(i, be, nu)], 0, 0)),
                      pl.BlockSpec((None, de, d), lambda i, be, nu, ft: (be[blk(i, be, nu)], 0, 0)),
                      pl.BlockSpec((None, 1, d), lambda i, be, nu, ft: (be[blk(i, be, nu)], 0, 0))],
            out_specs=pl.BlockSpec((bm, dh), lambda i, be, nu, ft: (blk(i, be, nu), 0)),
            scratch_shapes=[pltpu.VMEM((d, de2), BF16), pltpu.VMEM((de, d), BF16)],
        ),
        out_shape=jax.ShapeDtypeStruct((p, dh), U32),
        compiler_params=_cparams("arbitrary"),
        name="experts",
    )(block_e, n_used, first, xs, w_gu, b_gu.reshape(n_e, 1, de2), w_down, b_down.reshape(n_e, 1, d))


def _combine_body(toff_ref, cnt8_ref, gbase_ref, x1_ref, gts_ref, gtt_ref, gate_ref, eidx_ref, lrank_ref, toffv_ref,
                  lnw_ref, lnb_ref, yb_ref, yp_ref, ys_ref, ybuf, sem, *, top_k, n_ptiles, n_experts, alpha):
    i = pl.program_id(0)
    n_tiles = pl.num_programs(0)
    tm = x1_ref.shape[0]
    n_local = ybuf.shape[1]
    slot = i % 2

    def fetch(tile, sl, act):
        def fn(lo, go, rows):
            act(pltpu.make_async_copy(yb_ref.at[pl.ds(go, rows), :], ybuf.at[sl, pl.ds(lo, rows), :], sem.at[sl]))
        _for_each_run(tile, n_experts, toff_ref, cnt8_ref, gbase_ref, fn)

    def start_fetch(tile, sl):
        ybuf[sl, tm * top_k:, :] = jnp.zeros((n_local - tm * top_k, ybuf.shape[2]), ybuf.dtype)
        fetch(tile, sl, lambda cp: cp.start())

    @pl.when(i == 0)
    def _():
        start_fetch(i, slot)

    @pl.when(i + 1 < n_tiles)
    def _():
        start_fetch(i + 1, 1 - slot)

    gate = gate_ref[...]
    col = lax.broadcasted_iota(I32, (tm, n_local), 1)
    weights = jnp.zeros((tm, n_local), F32)
    for k, pos in enumerate(_local_positions(eidx_ref, lrank_ref, toffv_ref, top_k)):
        weights = weights + jnp.where(col == pos, gate[:, k:k + 1], 0.0)
    weights = weights.astype(BF16)
    fetch(i, slot, lambda cp: cp.wait())
    y_lo, y_hi = _unpack_halves(ybuf[slot])
    ff = jnp.concatenate([jnp.dot(weights, y_lo, preferred_element_type=F32),
                          jnp.dot(weights, y_hi, preferred_element_type=F32)], axis=1)
    gt = _select_mod(i, n_ptiles, gts_ref, gtt_ref)
    y = _layer_norm(alpha * x1_ref[...] + gt * ff, lnw_ref[...], lnb_ref[...])

    @pl.when(i < n_ptiles)
    def _():
        yp_ref[...] = y

    @pl.when(i >= n_ptiles)
    def _():
        ys_ref[...] = y


def _combine(tables, x1, gt_seq, gt_tok, gate, e_idx, lrank, toff_vec, lnw, lnb, yb, *, n_local, n_ptiles,
             tiles_per_seq, n_experts, alpha):
    t, d = x1.shape
    tm = TOKEN_TILE
    nt = t // tm
    n_seq = gt_seq.shape[0]
    body = functools.partial(_combine_body, top_k=TOP_K, n_ptiles=n_ptiles, n_experts=n_experts, alpha=alpha)
    lane_spec = pl.BlockSpec((tm, LANES), lambda i, *_: (i, 0))
    return pl.pallas_call(
        body,
        grid_spec=pltpu.PrefetchScalarGridSpec(
            num_scalar_prefetch=len(tables),
            grid=(nt,),
            in_specs=[pl.BlockSpec((tm, d), lambda i, *_: (i, 0)),
                      pl.BlockSpec((None, 1, d), lambda i, *_: (jnp.minimum(i // tiles_per_seq, n_seq - 1), 0, 0)),
                      pl.BlockSpec((tm, d), lambda i, *_: (jnp.maximum(i - n_ptiles, 0), 0)),
                      lane_spec, lane_spec, lane_spec,
                      pl.BlockSpec((None, 1, LANES), lambda i, *_: (i, 0, 0)),
                      pl.BlockSpec((1, d), lambda i, *_: (0, 0)),
                      pl.BlockSpec((1, d), lambda i, *_: (0, 0)),
                      pl.BlockSpec(memory_space=pl.ANY)],
            out_specs=[pl.BlockSpec((tm, d), lambda i, *_: (jnp.minimum(i, n_ptiles - 1), 0)),
                       pl.BlockSpec((tm, d), lambda i, *_: (jnp.maximum(i - n_ptiles, 0), 0))],
            scratch_shapes=[pltpu.VMEM((2, n_local, d // 2), U32), pltpu.SemaphoreType.DMA((2,))],
        ),
        out_shape=[jax.ShapeDtypeStruct((n_ptiles * tm, d), F32),
                   jax.ShapeDtypeStruct(((nt - n_ptiles) * tm, d), F32)],
        compiler_params=_cparams("arbitrary"),
        name="combine",
    )(*tables, x1, gt_seq, gt_tok, gate, e_idx, lrank, toff_vec, lnw, lnb, yb)


def _rotary_tables(pos, hd, rot_dim):
    half = rot_dim // 2
    inv_freq = jnp.power(jnp.float32(ROPE_THETA), -jnp.arange(half, dtype=F32) * (2.0 / rot_dim))
    ang = pos.astype(F32)[:, None] * inv_freq[None, :]
    cos, sin = jnp.cos(ang), jnp.sin(ang)
    n = pos.shape[0]
    ones = jnp.ones((n, hd - rot_dim), F32)
    zeros = jnp.zeros((n, hd - rot_dim), F32)
    zh = jnp.zeros((n, half), F32)
    ct = jnp.concatenate([cos, cos, ones], axis=1)
    s1 = jnp.concatenate([-sin, zh, zeros], axis=1)
    s2 = jnp.concatenate([zh, sin, zeros], axis=1)
    reps = LANES // hd
    return tuple(jnp.tile(a, (1, reps)) for a in (ct, s1, s2))


def kernel(x_prompt, x_sample, state_win_k, state_win_v, state_conv, state_ssm, c_prompt, c_sample, w_ada, b_ada, w_in, attn_sinks, w_conv, dn_a_log, dn_dt_bias, dn_norm_w, w_proj_attn, w_proj_dn, w_out, ln1_w, ln1_b, w_router, b_router, w_gu, b_gu, w_down, b_down, ln2_w, ln2_b):
    n_p, seq, d = x_prompt.shape
    n_s, l_s, _ = x_sample.shape
    depth = w_ada.shape[0]
    window, n_kv, hd = state_win_k.shape[2:]
    n_q = attn_sinks.shape[1]
    heads, dk, dv = state_ssm.shape[2:]
    conv_w, conv_dim = w_conv.shape[1:]
    n_e = w_router.shape[2]
    qd, kd = n_q * hd, n_kv * hd
    vdim = heads * dv
    rot_dim = hd // 4
    alpha = float((2 * depth) ** 0.25)
    tm = TOKEN_TILE
    t_p, t_s = n_p * seq, n_s * l_s
    t_all = t_p + t_s
    tps = seq // tm
    n_ptiles = t_p // tm
    l_pad = SUBLANES
    assert seq % tm == 0 and t_s % tm == 0 and tm % l_s == 0 and l_s <= l_pad and l_s >= conv_w - 1
    assert 2 * heads <= SUBLANES and n_e <= LANES and hd * 2 == LANES and rot_dim == 2 * SUBLANES

    sizes = [qd, kd, kd, conv_dim, vdim, heads, heads, d, d]
    offs = np.concatenate([[0], np.cumsum(sizes)])
    seg = lambda k: np.arange(offs[k], offs[k + 1])
    perm = np.concatenate([seg(0), seg(1), seg(2), seg(3), seg(4), seg(7), seg(8), seg(5), seg(6)])
    cuts = (qd + 2 * kd, qd + 2 * kd + conv_dim, qd + 2 * kd + conv_dim + vdim, qd + 2 * kd + conv_dim + vdim + 2 * d)
    cuts = cuts + (cuts[-1] + LANES,)

    tabs_p = _rotary_tables(jnp.arange(seq, dtype=I32), hd, rot_dim)
    tabs_s = tuple(jnp.tile(a, (tm // l_s, 1))
                   for a in _rotary_tables(PAST_LEN + jnp.arange(l_s, dtype=I32), hd, rot_dim))

    x_p = x_prompt.reshape(t_p, d)
    x_s = x_sample.reshape(t_s, d)
    c_all = jnp.concatenate([c_prompt, c_sample], axis=0)
    outs = {k: [] for k in ("pwk", "pwv", "pcv", "pss", "swk", "swv", "scv", "sss")}

    for l in range(depth):
        w_in_l = jnp.pad(w_in[l][:, perm], ((0, 0), (0, LANES - 2 * heads))).astype(BF16)
        mod = _ada(c_all, w_ada[l], b_ada[l])
        mod_p = mod[:n_p].reshape(n_p, 6, 1, d)
        mod_s = jnp.repeat(mod[n_p:].reshape(n_s, 6, d), l_s, axis=0)
        sh1p, sc1p, gt1p, sh2p, sc2p, gt2p = [mod_p[:, k] for k in range(6)]
        sh1s, sc1s, gt1s, sh2s, sc2s, gt2s = [mod_s[:, k] for k in range(6)]

        qkva_p, dn_p, z_p, g_p, ba_p, tail_p, kvw_p = _inproj(
            x_p, sh1p, sc1p, tabs_p, w_in_l, cuts, per_token_mod=False, tiles_per_seq=tps, act_dtype=BF16,
            window=window, kv_cols=2 * kd)
        attn_p = _attn_prompt(qkva_p, attn_sinks[l], n_p, seq, n_q, n_kv, hd, window)
        chunk = min(DN_CHUNK, seq)
        nc = seq // chunk
        bat_p = ba_p[:, :SUBLANES].reshape(n_p, nc, chunk, SUBLANES).transpose(0, 1, 3, 2)
        hp = jnp.zeros((SUBLANES, LANES), F32).at[0, :heads].set(dn_a_log[l]).at[1, :heads].set(dn_dt_bias[l])
        nw = dn_norm_w[l].reshape(1, dv)
        o_p, ssm_p = _deltanet(dn_p.reshape(n_p, seq, conv_dim), z_p.reshape(n_p, seq, vdim),
                               ba_p.reshape(n_p, seq, LANES), bat_p,
                               jnp.zeros((n_p, SUBLANES, conv_dim), F32), jnp.zeros((n_p, heads, dk, dv), F32),
                               w_conv[l], hp, nw, chunk=chunk, l_real=chunk, nb=4)
        outs["pwk"].append(kvw_p[:, :, :kd].reshape(n_p, window, n_kv, hd))
        outs["pwv"].append(kvw_p[:, :, kd:].reshape(n_p, window, n_kv, hd))
        outs["pcv"].append(tail_p.reshape(n_p, tps, SUBLANES, conv_dim)[:, -1, SUBLANES - (conv_w - 1):])
        outs["pss"].append(ssm_p)

        qkva_s, dn_s, z_s, g_s, ba_s = _inproj(
            x_s, sh1s, sc1s, tabs_s, w_in_l, cuts, per_token_mod=True, tiles_per_seq=1, act_dtype=F32,
            window=window, kv_cols=2 * kd)
        pad_l = lambda a: jnp.pad(a.reshape(n_s, l_s, a.shape[-1]), ((0, 0), (0, l_pad - l_s), (0, 0)))
        attn_s, wk_s, wv_s = _attn_sample(pad_l(qkva_s), state_win_k[l].reshape(n_s, window, kd),
                                          state_win_v[l].reshape(n_s, window, kd), attn_sinks[l], n_q, n_kv, hd, l_s)
        attn_s = attn_s[:, :l_s].reshape(t_s, qd)
        ba_s3 = pad_l(ba_s)
        bat_s = ba_s3[:, :, :SUBLANES].transpose(0, 2, 1).reshape(n_s, 1, SUBLANES, l_pad)
        cs0 = jnp.pad(state_conv[l], ((0, 0), (SUBLANES - (conv_w - 1), 0), (0, 0)))
        o_s, ssm_s = _deltanet(pad_l(dn_s), pad_l(z_s), ba_s3, bat_s, cs0, state_ssm[l], w_conv[l], hp, nw,
                               chunk=l_pad, l_real=l_s, nb=8)
        o_s = o_s[:, :l_s].reshape(t_s, vdim)
        outs["swk"].append(wk_s.reshape(n_s, window, n_kv, hd))
        outs["swv"].append(wv_s.reshape(n_s, window, n_kv, hd))
        outs["scv"].append(jnp.concatenate([state_conv[l], dn_s.reshape(n_s, l_s, conv_dim)], axis=1)[:, -(conv_w - 1):])
        outs["sss"].append(ssm_s)

        wr = jnp.pad(w_router[l], ((0, 0), (0, LANES - n_e)))
        br = jnp.pad(b_router[l], (0, LANES - n_e), constant_values=NEG_BIG).reshape(1, LANES)
        wr_hi = wr.astype(BF16)
        wr_lo = (wr - wr_hi.astype(F32)).astype(BF16)
        wts = (w_proj_attn[l].astype(BF16), w_proj_dn[l].astype(BF16), w_out[l].astype(BF16),
               ln1_w[l].reshape(1, d), ln1_b[l].reshape(1, d), wr_hi, wr_lo, br)
        res_p = _outproj(attn_p, o_p.reshape(t_p, vdim), g_p, x_p, gt1p, sh2p, sc2p, wts, None,
                         per_token_mod=False, tiles_per_seq=tps, t_total=t_all, tile_off=0, alpha=alpha)
        x1, e_idx, gate, rank, cnt = _outproj(attn_s, o_s, g_s, x_s, gt1s, sh2s, sc2s, wts, res_p,
                                              per_token_mod=True, tiles_per_seq=1, t_total=t_all, tile_off=n_ptiles,
                                              alpha=alpha)

        bm = EXPERT_ROWS
        nt_all = t_all // tm
        cnt8 = (cnt[:, 0, :n_e].astype(I32) + RUN_ALIGN - 1) // RUN_ALIGN * RUN_ALIGN
        tot = jnp.sum(cnt8, axis=0)
        padded = (tot + bm - 1) // bm * bm
        pad_end = jnp.cumsum(padded)
        pad_start = pad_end - padded
        gbase = pad_start[None, :] + jnp.cumsum(cnt8, axis=0) - cnt8
        toff = jnp.cumsum(cnt8, axis=1) - cnt8
        n_local = -(-(tm * TOP_K + n_e * (RUN_ALIGN - 1)) // LANES) * LANES
        n_rows = -(-(t_all * TOP_K + nt_all * n_e * (RUN_ALIGN - 1) + n_e * (bm - 1)) // bm) * bm
        nblk = n_rows // bm
        n_used = jnp.maximum(pad_end[-1:] // bm, 1).astype(I32)
        block_e = jnp.minimum(jnp.sum(pad_end[None, :] <= (jnp.arange(nblk, dtype=I32) * bm)[:, None], axis=1),
                              n_e - 1).astype(I32)
        flat = lambda a: a.astype(I32).reshape(nt_all * n_e)
        run_tables = (flat(toff), flat(cnt8), flat(gbase))
        tail_tables = ((pad_start + tot).astype(I32), (padded - tot).astype(I32))
        toff_vec = jnp.pad(toff.astype(F32), ((0, 0), (0, LANES - n_e))).reshape(nt_all, 1, LANES)

        xs = _dispatch(run_tables + tail_tables, x1, sh2p, sc2p, sh2s, sc2s, e_idx, rank, toff_vec, n_rows=n_rows,
                       n_local=n_local, n_ptiles=n_ptiles, tiles_per_seq=tps, n_experts=n_e)
        yb = _experts(block_e, n_used, xs, w_gu[l], b_gu[l], w_down[l], b_down[l])
        x_p, x_s = _combine(run_tables, x1, gt2p, gt2s, gate, e_idx, rank, toff_vec, ln2_w[l].reshape(1, d),
                            ln2_b[l].reshape(1, d), yb, n_local=n_local, n_ptiles=n_ptiles, tiles_per_seq=tps,
                            n_experts=n_e, alpha=alpha)

    st = lambda k: jnp.stack(outs[k])
    return (x_p.reshape(n_p, seq, d), x_s.reshape(n_s, l_s, d), st("pwk"), st("pwv"), st("pcv"), st("pss"),
            st("swk"), st("swv"), st("scv"), st("sss"))
```

```python
import functools

import numpy as np
import jax
import jax.numpy as jnp
from jax import lax
from jax.experimental import pallas as pl
from jax.experimental.pallas import tpu as pltpu

F32 = jnp.float32
BF16 = jnp.bfloat16
I32 = jnp.int32
U32 = jnp.uint32

PAST_LEN = 16384
ROPE_THETA = 500000.0
TOP_K = 4
SWIGLU_LIMIT = 7.0
SWIGLU_ALPHA = 1.702
DN_CHUNK = 64
LN_EPS = 1e-5
RMS_EPS = 1e-6
L2_EPS = 1e-6

LANES = 128
SUBLANES = 8
VMEM_LIMIT_BYTES = 56 * 1024 * 1024

TOKEN_TILE = 256
EXPERT_ROWS = 512
RUN_ALIGN = SUBLANES
OUTPROJ_SUBTILES = 2
N_ROUTE_BUFS = 5
NEG_BIG = -1e30


def _cparams(*sem):
    return pltpu.CompilerParams(dimension_semantics=sem, vmem_limit_bytes=VMEM_LIMIT_BYTES)


def _silu(x):
    return x * jax.nn.sigmoid(x)


def _bdot(a, b):
    return jnp.dot(a.astype(BF16), b.astype(BF16), preferred_element_type=F32)


def _pack_halves(x):
    n = x.shape[1] // 2
    lo = lax.bitcast_convert_type(x[:, :n], U32)
    hi = lax.bitcast_convert_type(x[:, n:], U32)
    return (hi & jnp.uint32(0xFFFF0000)) | (lo >> 16)


def _unpack_halves(w):
    lo = lax.bitcast_convert_type(w << 16, F32).astype(BF16)
    hi = lax.bitcast_convert_type(w & jnp.uint32(0xFFFF0000), F32).astype(BF16)
    return lo, hi


def _ada_body(c_ref, w_ref, b_ref, o_ref):
    o_ref[...] = _bdot(_silu(c_ref[...]), w_ref[...]) + b_ref[...]


def _ada(c_all, w_ada, b_ada):
    n, d = c_all.shape
    dout = w_ada.shape[1]
    tn = d
    return pl.pallas_call(
        _ada_body,
        grid=(dout // tn,),
        in_specs=[pl.BlockSpec((n, d), lambda j: (0, 0)),
                  pl.BlockSpec((d, tn), lambda j: (0, j)),
                  pl.BlockSpec((1, tn), lambda j: (0, j))],
        out_specs=pl.BlockSpec((n, tn), lambda j: (0, j)),
        out_shape=jax.ShapeDtypeStruct((n, dout), F32),
        compiler_params=_cparams("arbitrary"),
        name="ada",
    )(c_all, w_ada, b_ada.reshape(1, dout))


def _inproj_body(x_ref, sh_ref, sc_ref, ct_ref, s1_ref, s2_ref, w_ref,
                 a_ref, dn_ref, z_ref, g_ref, ba_ref, *win_refs, cuts, n_rot_chunks, window):
    h = (x_ref[...] * (1.0 + sc_ref[...]) + sh_ref[...]).astype(BF16)

    def mm(lo, hi):
        return jnp.dot(h, w_ref[:, lo:hi], preferred_element_type=F32)

    c_a, c_dn, c_z, c_g, c_ba = cuts
    qkv = mm(0, c_a)
    ct, s1, s2 = ct_ref[...], s1_ref[...], s2_ref[...]
    cols = []
    for c in range(n_rot_chunks):
        xc = qkv[:, c * LANES:(c + 1) * LANES]
        cols.append(xc * ct + pltpu.roll(xc, LANES - SUBLANES, 1) * s1 + pltpu.roll(xc, SUBLANES, 1) * s2)
    cols.append(qkv[:, n_rot_chunks * LANES:])
    rot = jnp.concatenate(cols, axis=1)
    a_ref[...] = rot.astype(a_ref.dtype)
    dn = mm(c_a, c_dn)
    dn_ref[...] = dn.astype(dn_ref.dtype)
    z_ref[...] = mm(c_dn, c_z).astype(z_ref.dtype)
    g_ref[...] = mm(c_z, c_g).astype(g_ref.dtype)
    ba_ref[...] = mm(c_g, c_ba)
    if win_refs:
        tail_ref, kvw_ref = win_refs
        tm = dn.shape[0]
        tail_ref[...] = dn[tm - SUBLANES:, :]
        kvw_ref[...] = rot[tm - window:, n_rot_chunks * LANES - LANES:]


def _inproj(x, sh, sc, tabs, w_perm, cuts, *, per_token_mod, tiles_per_seq, act_dtype, window, kv_cols):
    t, d = x.shape
    tm = TOKEN_TILE
    nt = t // tm
    c_a, c_dn, c_z, c_g, c_ba = cuts
    n_rot_chunks = (c_a - kv_cols // 2) // LANES
    if per_token_mod:
        mod_spec = pl.BlockSpec((tm, d), lambda i: (i, 0))
        tab_spec = pl.BlockSpec((tm, LANES), lambda i: (0, 0))
    else:
        mod_spec = pl.BlockSpec((None, 1, d), lambda i: (i // tiles_per_seq, 0, 0))
        tab_spec = pl.BlockSpec((tm, LANES), lambda i: (i % tiles_per_seq, 0))
    out_shape = [jax.ShapeDtypeStruct((t, c_a), act_dtype),
                 jax.ShapeDtypeStruct((t, c_dn - c_a), act_dtype),
                 jax.ShapeDtypeStruct((t, c_z - c_dn), act_dtype),
                 jax.ShapeDtypeStruct((t, c_g - c_z), act_dtype),
                 jax.ShapeDtypeStruct((t, c_ba - c_g), F32)]
    out_specs = [pl.BlockSpec((tm, s.shape[1]), lambda i: (i, 0)) for s in out_shape]
    with_win = not per_token_mod
    if with_win:
        n_seq = nt // tiles_per_seq
        out_shape += [jax.ShapeDtypeStruct((nt, SUBLANES, c_dn - c_a), F32),
                      jax.ShapeDtypeStruct((n_seq, window, kv_cols), F32)]
        out_specs += [pl.BlockSpec((None, SUBLANES, c_dn - c_a), lambda i: (i, 0, 0)),
                      pl.BlockSpec((None, window, kv_cols), lambda i: (i // tiles_per_seq, 0, 0))]
    body = functools.partial(_inproj_body, cuts=cuts, n_rot_chunks=n_rot_chunks, window=window)
    return pl.pallas_call(
        body,
        grid=(nt,),
        in_specs=[pl.BlockSpec((tm, d), lambda i: (i, 0)), mod_spec, mod_spec,
                  tab_spec, tab_spec, tab_spec,
                  pl.BlockSpec((d, c_ba), lambda i: (0, 0))],
        out_specs=out_specs,
        out_shape=out_shape,
        compiler_params=_cparams("arbitrary"),
        name="inproj",
    )(x, sh, sc, *tabs, w_perm)


def _softmax_sink_pv(scores, valid, sinks, values):
    ms = [jnp.where(valid, s, -jnp.inf) for s in scores]
    m = [jnp.maximum(jnp.max(x, axis=-1, keepdims=True), sk) for x, sk in zip(ms, sinks)]
    p = [jnp.exp(x - mi) for x, mi in zip(ms, m)]
    den = [jnp.sum(pi, axis=-1, keepdims=True) + jnp.exp(sk - mi) for pi, sk, mi in zip(p, sinks, m)]
    return [jnp.dot((pi / di).astype(BF16), v, preferred_element_type=F32) for pi, di, v in zip(p, den, values)]


def _attn_prompt_body(sink_ref, q_ref, kvp_ref, kvc_ref, o_ref, *, n_q, n_kv, hd, window):
    j = pl.program_id(1)
    group = n_q // n_kv
    scale = hd ** -0.5
    assert np.log2(scale) == int(np.log2(scale))
    r = lax.broadcasted_iota(I32, (window, 2 * window), 0)
    c = lax.broadcasted_iota(I32, (window, 2 * window), 1)
    rel = window + r - c
    band = (rel >= 0) & (rel < window)
    kv_cur = kvc_ref[...]
    kv_first = jnp.concatenate([kvp_ref[...], kv_cur[:window]], axis=0)
    for qb, (kv, valid) in enumerate(((kv_first, band & ((c >= window) | (j > 0))), (kv_cur, band))):
        q = q_ref[qb * window:(qb + 1) * window, :] * scale
        scores = [lax.dot_general(q[:, h * hd:(h + 1) * hd], kv[:, (h // group) * hd:(h // group + 1) * hd],
                                  (((1,), (1,)), ((), ())), preferred_element_type=F32) for h in range(n_q)]
        values = [kv[:, (n_kv + h // group) * hd:(n_kv + h // group + 1) * hd] for h in range(n_q)]
        outs = _softmax_sink_pv(scores, valid, [sink_ref[h] for h in range(n_q)], values)
        o_ref[qb * window:(qb + 1) * window, :] = jnp.concatenate(outs, axis=1).astype(o_ref.dtype)


def _attn_prompt(qkva, sinks, n_seq, seq, n_q, n_kv, hd, window):
    qd, kvd = n_q * hd, 2 * n_kv * hd
    x3 = qkva.reshape(n_seq, seq, qd + kvd)
    nb = seq // window
    assert nb % 2 == 0 and qd % kvd == 0
    kv_blk = qd // kvd
    body = functools.partial(_attn_prompt_body, n_q=n_q, n_kv=n_kv, hd=hd, window=window)
    out = pl.pallas_call(
        body,
        grid=(n_seq, nb // 2),
        in_specs=[pl.BlockSpec(memory_space=pltpu.SMEM),
                  pl.BlockSpec((None, 2 * window, qd), lambda n, j: (n, j, 0)),
                  pl.BlockSpec((None, window, kvd), lambda n, j: (n, jnp.maximum(2 * j - 1, 0), kv_blk)),
                  pl.BlockSpec((None, 2 * window, kvd), lambda n, j: (n, j, kv_blk))],
        out_specs=pl.BlockSpec((None, 2 * window, qd), lambda n, j: (n, j, 0)),
        out_shape=jax.ShapeDtypeStruct((n_seq, seq, qd), BF16),
        compiler_params=_cparams("arbitrary", "arbitrary"),
        name="attn_prompt",
    )(sinks, x3, x3, x3)
    return out.reshape(n_seq * seq, qd)


def _attn_sample_body(sink_ref, q_ref, wk_ref, wv_ref, o_ref, wko_ref, wvo_ref,
                      *, bs, n_q, n_kv, hd, window, l_new, l_pad):
    group = n_q // n_kv
    qd = n_q * hd
    kd = n_kv * hd
    rows = group * l_pad
    r = lax.broadcasted_iota(I32, (rows, window + l_pad), 0) % l_pad
    c = lax.broadcasted_iota(I32, (rows, window + l_pad), 1)
    rel = window + r - c
    valid = (rel >= 0) & (rel < window) & (c < window + l_new)
    sinks = [jnp.concatenate([jnp.full((l_pad, 1), sink_ref[kvh * group + g], F32) for g in range(group)], axis=0)
             for kvh in range(n_kv)]
    qs, ks, vs = [], [], []
    for b in range(bs):
        x = q_ref[b]
        k_new = x[:, qd:qd + kd]
        v_new = x[:, qd + kd:]
        wko_ref[b, 0:window - l_new, :] = wk_ref[b, l_new:window, :]
        wko_ref[b, window - l_new:window, :] = k_new[0:l_new, :]
        wvo_ref[b, 0:window - l_new, :] = wv_ref[b, l_new:window, :]
        wvo_ref[b, window - l_new:window, :] = v_new[0:l_new, :]
        k_all = jnp.concatenate([wk_ref[b], k_new], axis=0).astype(BF16)
        v_all = jnp.concatenate([wv_ref[b], v_new], axis=0).astype(BF16)
        for kvh in range(n_kv):
            qs.append(jnp.concatenate([x[:, (kvh * group + g) * hd:(kvh * group + g + 1) * hd]
                                       for g in range(group)], axis=0).astype(BF16))
            ks.append(k_all[:, kvh * hd:(kvh + 1) * hd])
            vs.append(v_all[:, kvh * hd:(kvh + 1) * hd])
    n = len(qs)
    s = [lax.dot_general(qs[i], ks[i], (((1,), (1,)), ((), ())), preferred_element_type=F32) * (hd ** -0.5)
         for i in range(n)]
    o = _softmax_sink_pv(s, valid, [sinks[i % n_kv] for i in range(n)], vs)
    for b in range(bs):
        outs = [o[b * n_kv + kvh][g * l_pad:(g + 1) * l_pad, :] for kvh in range(n_kv) for g in range(group)]
        o_ref[b] = jnp.concatenate(outs, axis=1).astype(o_ref.dtype)


def _attn_sample(qkva_pad, win_k, win_v, sinks, n_q, n_kv, hd, l_new):
    n, l_pad, width = qkva_pad.shape
    window, kd = win_k.shape[1], win_k.shape[2]
    qd = n_q * hd
    bs = 8
    body = functools.partial(_attn_sample_body, bs=bs, n_q=n_q, n_kv=n_kv, hd=hd, window=window,
                             l_new=l_new, l_pad=l_pad)
    return pl.pallas_call(
        body,
        grid=(n // bs,),
        in_specs=[pl.BlockSpec(memory_space=pltpu.SMEM),
                  pl.BlockSpec((bs, l_pad, width), lambda i: (i, 0, 0)),
                  pl.BlockSpec((bs, window, kd), lambda i: (i, 0, 0)),
                  pl.BlockSpec((bs, window, kd), lambda i: (i, 0, 0))],
        out_specs=[pl.BlockSpec((bs, l_pad, qd), lambda i: (i, 0, 0)),
                   pl.BlockSpec((bs, window, kd), lambda i: (i, 0, 0)),
                   pl.BlockSpec((bs, window, kd), lambda i: (i, 0, 0))],
        out_shape=[jax.ShapeDtypeStruct((n, l_pad, qd), BF16),
                   jax.ShapeDtypeStruct((n, window, kd), F32),
                   jax.ShapeDtypeStruct((n, window, kd), F32)],
        compiler_params=_cparams("arbitrary"),
        name="attn_sample",
    )(sinks, qkva_pad, win_k, win_v)


def _split_bf16(x):
    hi = x.astype(BF16)
    return hi, (x - hi.astype(F32)).astype(BF16)


def _tdot(a, b):
    ah, al = _split_bf16(a)
    bh, bl = _split_bf16(b)
    m = a.shape[0]
    t = jnp.dot(jnp.concatenate([ah, al], axis=0), bh, preferred_element_type=F32)
    return t[:m] + t[m:] + jnp.dot(ah, bl, preferred_element_type=F32)


def _dn_body(qkv_ref, z_ref, ba_ref, bat_ref, cs0_ref, s0_ref, wc_ref, hp_ref, nw_ref,
             o_ref, s_ref, xbuf, *, nb, chunk, heads, dk, dv, l_real, conv_w):
    c_idx = pl.program_id(1)
    hc = SUBLANES

    @pl.when(c_idx == 0)
    def _():
        xbuf[:, 0:hc, :] = cs0_ref[...]
        s_ref[...] = s0_ref[...]

    qk_dim = heads * dk
    row = lax.broadcasted_iota(I32, (chunk, chunk), 0)
    col = lax.broadcasted_iota(I32, (chunk, chunk), 1)
    incl = row >= col
    strict = row > col
    eye = (row == col).astype(F32)
    valid_c = row[:, 0:1] < l_real
    valid_r = col[0:1, :] < l_real
    n_levels = max(1, int(np.ceil(np.log2(chunk))))
    wc = wc_ref[...]
    hp = hp_ref[...]
    neg_exp_alog = -jnp.exp(hp[0:1, :])
    dt_bias = hp[1:2, :]
    nw = nw_ref[...]
    chains = [(b, h) for b in range(nb) for h in range(heads)]
    n = len(chains)

    ys = []
    for b in range(nb):
        xbuf[b, hc:hc + chunk, :] = qkv_ref[b].astype(F32)
        y = xbuf[b, hc:hc + chunk, :] * wc[conv_w - 1:conv_w, :]
        for j in range(conv_w - 1):
            off = hc - (conv_w - 1) + j
            y = y + xbuf[b, off:off + chunk, :] * wc[j:j + 1, :]
        ys.append(_silu(y))
        xbuf[b, 0:hc, :] = xbuf[b, chunk:chunk + hc, :]

    qn, kn, kb, vb, decay, e_gc, e_rest, e_last = [], [], [], [], [], [], [], []
    for b, h in chains:
        y = ys[b]
        qh = y[:, h * dk:(h + 1) * dk]
        kh = y[:, qk_dim + h * dk:qk_dim + (h + 1) * dk]
        vh = y[:, 2 * qk_dim + h * dv:2 * qk_dim + (h + 1) * dv]
        ba = ba_ref[b]
        bat = bat_ref[b]
        ne = neg_exp_alog[:, h:h + 1]
        db = dt_bias[:, h:h + 1]
        beta = jnp.where(valid_c, jax.nn.sigmoid(ba[:, h:h + 1]), 0.0)
        g_col = jnp.where(valid_c, ne * jax.nn.softplus(ba[:, heads + h:heads + h + 1] + db), 0.0)
        g_row = jnp.where(valid_r, ne * jax.nn.softplus(bat[heads + h:heads + h + 1, :] + db), 0.0)
        gc_col = jnp.sum(jnp.where(incl, g_row, 0.0), axis=1, keepdims=True)
        gc_row = jnp.sum(jnp.where(row <= col, g_col, 0.0), axis=0, keepdims=True)
        g_last = gc_col[chunk - 1:chunk, :]
        q_ = qh * lax.rsqrt(jnp.sum(qh * qh, -1, keepdims=True) + L2_EPS) * (dk ** -0.5)
        k_ = kh * lax.rsqrt(jnp.sum(kh * kh, -1, keepdims=True) + L2_EPS)
        qn.append(q_)
        kn.append(k_)
        kb.append(k_ * beta)
        vb.append(vh * beta)
        decay.append(jnp.where(incl, jnp.exp(gc_col - gc_row), 0.0))
        e_gc.append(jnp.exp(gc_col))
        e_rest.append(jnp.exp(g_last - gc_col))
        e_last.append(jnp.exp(g_last))

    sc = [lax.dot_general(jnp.concatenate([qn[i], kb[i]], axis=0).astype(BF16), kn[i].astype(BF16),
                          (((1,), (1,)), ((), ())), preferred_element_type=F32) for i in range(n)]
    qk = [sc[i][:chunk] * decay[i] for i in range(n)]
    p = [jnp.where(strict, -(sc[i][chunk:] * decay[i]), 0.0) for i in range(n)]
    t_inv = [eye + p[i] for i in range(n)]
    if n_levels > 1:
        p = [_tdot(p[i], p[i]) for i in range(n)]
    for lvl in range(1, n_levels):
        if lvl < n_levels - 1:
            yp = [_tdot(jnp.concatenate([t_inv[i], p[i]], axis=0), p[i]) for i in range(n)]
            t_inv = [t_inv[i] + yp[i][:chunk] for i in range(n)]
            p = [yp[i][chunk:] for i in range(n)]
        else:
            t_inv = [t_inv[i] + _tdot(t_inv[i], p[i]) for i in range(n)]
    sol = [_tdot(t_inv[i], jnp.concatenate([vb[i], kb[i] * e_gc[i]], axis=1)) for i in range(n)]
    s_old = [s_ref[b, h] for b, h in chains]
    wq = [_bdot(jnp.concatenate([sol[i][:, dv:], qn[i] * e_gc[i]], axis=0), s_old[i]) for i in range(n)]
    v_new = [sol[i][:, :dv] - wq[i][:chunk] for i in range(n)]
    o = [wq[i][chunk:] + _bdot(qk[i], v_new[i]) for i in range(n)]
    for i, (b, h) in enumerate(chains):
        s_ref[b, h] = s_old[i] * e_last[i] + lax.dot_general(
            (kn[i] * e_rest[i]).astype(BF16), v_new[i].astype(BF16), (((0,), (0,)), ((), ())),
            preferred_element_type=F32)
    for b in range(nb):
        zt = z_ref[b].astype(F32)
        outs = []
        for h in range(heads):
            oi = o[b * heads + h]
            on = oi * lax.rsqrt(jnp.mean(oi * oi, -1, keepdims=True) + RMS_EPS) * nw
            outs.append(on * _silu(zt[:, h * dv:(h + 1) * dv]))
        o_ref[b] = jnp.concatenate(outs, axis=1).astype(o_ref.dtype)


def _deltanet(qkv, z, ba, bat, cs0, s0, w_conv, hp, norm_w, *, chunk, l_real, nb):
    n, l, conv_dim = qkv.shape
    heads, dk, dv = s0.shape[1:]
    nc = l // chunk
    assert n % nb == 0 and l % chunk == 0
    conv_w = w_conv.shape[0]
    body = functools.partial(_dn_body, nb=nb, chunk=chunk, heads=heads, dk=dk, dv=dv, l_real=l_real, conv_w=conv_w)
    return pl.pallas_call(
        body,
        grid=(n // nb, nc),
        in_specs=[pl.BlockSpec((nb, chunk, conv_dim), lambda i, c: (i, c, 0)),
                  pl.BlockSpec((nb, chunk, heads * dv), lambda i, c: (i, c, 0)),
                  pl.BlockSpec((nb, chunk, LANES), lambda i, c: (i, c, 0)),
                  pl.BlockSpec((nb, None, SUBLANES, chunk), lambda i, c: (i, c, 0, 0)),
                  pl.BlockSpec((nb, SUBLANES, conv_dim), lambda i, c: (i, 0, 0)),
                  pl.BlockSpec((nb, heads, dk, dv), lambda i, c: (i, 0, 0, 0)),
                  pl.BlockSpec((conv_w, conv_dim), lambda i, c: (0, 0)),
                  pl.BlockSpec((SUBLANES, LANES), lambda i, c: (0, 0)),
                  pl.BlockSpec((1, dv), lambda i, c: (0, 0))],
        out_specs=[pl.BlockSpec((nb, chunk, heads * dv), lambda i, c: (i, c, 0)),
                   pl.BlockSpec((nb, heads, dk, dv), lambda i, c: (i, 0, 0, 0))],
        out_shape=[jax.ShapeDtypeStruct((n, l, heads * dv), BF16),
                   jax.ShapeDtypeStruct((n, heads, dk, dv), F32)],
        scratch_shapes=[pltpu.VMEM((nb, SUBLANES + chunk, conv_dim), F32)],
        compiler_params=_cparams("arbitrary", "arbitrary"),
        name="deltanet",
    )(qkv, z, ba, bat, cs0, s0, w_conv, hp, norm_w)


def _layer_norm(r, w, b):
    mu = jnp.mean(r, -1, keepdims=True)
    var = jnp.mean(jnp.square(r - mu), -1, keepdims=True)
    return (r - mu) * lax.rsqrt(var + LN_EPS) * w + b


def _outproj_body(attn_ref, dn_ref, g_ref, x_ref, gt_ref, sh2_ref, sc2_ref, wpa_ref, wpd_ref, wo_ref,
                  lnw_ref, lnb_ref, wrh_ref, wrl_ref, br_ref, *refs, alpha, top_k, aliased):
    if aliased:
        refs = refs[N_ROUTE_BUFS:]
    x1_ref, eidx_ref, gate_ref, rank_ref, cnt_ref = refs
    for s in range(cnt_ref.shape[0]):
        _outproj_tile(s, attn_ref, dn_ref, g_ref, x_ref, gt_ref, sh2_ref, sc2_ref, wpa_ref, wpd_ref, wo_ref, lnw_ref,
                      lnb_ref, wrh_ref, wrl_ref, br_ref, x1_ref, eidx_ref, gate_ref, rank_ref, cnt_ref,
                      alpha=alpha, top_k=top_k)


def _outproj_tile(s, attn_ref, dn_ref, g_ref, x_ref, gt_ref, sh2_ref, sc2_ref, wpa_ref, wpd_ref, wo_ref, lnw_ref,
                  lnb_ref, wrh_ref, wrl_ref, br_ref, x1_ref, eidx_ref, gate_ref, rank_ref, cnt_ref, *, alpha, top_k):
    d = x_ref.shape[1]
    tm = TOKEN_TILE
    rows = pl.ds(s * tm, tm)

    def mod(ref):
        return ref[...] if ref.shape[0] == 1 else ref[rows, :]

    g = g_ref[rows, :].astype(F32)
    pa = jnp.dot(attn_ref[rows, :], wpa_ref[...], preferred_element_type=F32)
    pd = jnp.dot(dn_ref[rows, :], wpd_ref[...], preferred_element_type=F32)
    merged = jax.nn.sigmoid(g[:, :d]) * pa + jax.nn.sigmoid(g[:, d:]) * pd
    mix = jnp.dot(merged.astype(BF16), wo_ref[...], preferred_element_type=F32)
    x1 = _layer_norm(alpha * x_ref[rows, :] + mod(gt_ref) * mix, lnw_ref[...], lnb_ref[...])
    x1_ref[rows, :] = x1
    h2 = x1 * (1.0 + mod(sc2_ref)) + mod(sh2_ref)
    h_hi, h_lo = _split_bf16(h2)
    lg = jnp.dot(jnp.concatenate([h_hi, h_lo], axis=0), wrh_ref[...], preferred_element_type=F32)
    logits = lg[:tm] + lg[tm:] + jnp.dot(h_hi, wrl_ref[...], preferred_element_type=F32) + br_ref[...]
    lane = lax.broadcasted_iota(I32, (tm, LANES), 1)
    lane_f = lane.astype(F32)
    vals, idxs, sels = [], [], []
    l = logits
    for _ in range(top_k):
        m = jnp.max(l, axis=1, keepdims=True)
        idx = jnp.min(jnp.where(l == m, lane_f, float(LANES)), axis=1, keepdims=True)
        sel = lane_f == idx
        vals.append(m)
        idxs.append(idx)
        sels.append(sel)
        l = jnp.where(sel, -jnp.inf, l)
    ex = [jnp.exp(v - vals[0]) for v in vals]
    den = ex[0]
    for e in ex[1:]:
        den = den + e
    multi_hot = jnp.zeros((tm, LANES), F32)
    for sel in sels:
        multi_hot = multi_hot + jnp.where(sel, 1.0, 0.0)
    r_i = lax.broadcasted_iota(I32, (tm, tm), 0)
    c_i = lax.broadcasted_iota(I32, (tm, tm), 1)
    lower = jnp.where(r_i > c_i, 1.0, 0.0).astype(BF16)
    prefix = jnp.dot(lower, multi_hot.astype(BF16), preferred_element_type=F32)
    e_out = jnp.zeros((tm, LANES), F32)
    g_out = jnp.zeros((tm, LANES), F32)
    r_out = jnp.zeros((tm, LANES), F32)
    for k in range(top_k):
        rank_k = jnp.sum(jnp.where(sels[k], prefix, 0.0), axis=1, keepdims=True)
        e_out = jnp.where(lane == k, idxs[k], e_out)
        g_out = jnp.where(lane == k, ex[k] / den, g_out)
        r_out = jnp.where(lane == k, rank_k, r_out)
    eidx_ref[rows, :] = e_out.astype(I32)
    gate_ref[rows, :] = g_out
    rank_ref[rows, :] = r_out.astype(I32)
    cnt_ref[s] = jnp.broadcast_to(jnp.sum(multi_hot, axis=0, keepdims=True), cnt_ref.shape[1:])


def _outproj(attn, dn, gates, x, gt, sh2, sc2, wts, bufs, *, per_token_mod, tiles_per_seq, t_total, tile_off, alpha):
    t, d = x.shape
    sub = OUTPROJ_SUBTILES
    tm = TOKEN_TILE * sub
    nt = t // tm
    assert t % tm == 0 and tile_off % sub == 0 and (per_token_mod or tiles_per_seq % sub == 0)
    wpa, wpd, wo, lnw, lnb, wrh, wrl, br = wts
    if per_token_mod:
        mod_spec = pl.BlockSpec((tm, d), lambda i: (i, 0))
    else:
        mod_spec = pl.BlockSpec((None, 1, d), lambda i: (i // (tiles_per_seq // sub), 0, 0))

    def row(width):
        return pl.BlockSpec((tm, width), lambda i: (i, 0))

    def full(a):
        return pl.BlockSpec(a.shape, lambda i: (0,) * a.ndim)

    aliased = bufs is not None
    in_specs = [row(attn.shape[1]), row(dn.shape[1]), row(gates.shape[1]), row(d), mod_spec, mod_spec, mod_spec,
                full(wpa), full(wpd), full(wo), full(lnw), full(lnb), full(wrh), full(wrl), full(br)]
    args = [attn, dn, gates, x, gt, sh2, sc2, wpa, wpd, wo, lnw, lnb, wrh, wrl, br]
    io_alias = {}
    if aliased:
        for k, bfr in enumerate(bufs):
            in_specs.append(pl.BlockSpec(memory_space=pl.ANY))
            io_alias[len(args)] = k
            args.append(bfr)
    step_off = tile_off // sub
    out_row = lambda width: pl.BlockSpec((tm, width), lambda i: (i + step_off, 0))
    out_shape = [jax.ShapeDtypeStruct((t_total, d), F32), jax.ShapeDtypeStruct((t_total, LANES), I32),
                 jax.ShapeDtypeStruct((t_total, LANES), F32), jax.ShapeDtypeStruct((t_total, LANES), I32),
                 jax.ShapeDtypeStruct((t_total // TOKEN_TILE, SUBLANES, LANES), F32)]
    assert len(out_shape) == N_ROUTE_BUFS
    out_specs = [out_row(d), out_row(LANES), out_row(LANES), out_row(LANES),
                 pl.BlockSpec((sub, SUBLANES, LANES), lambda i: (i + step_off, 0, 0))]
    body = functools.partial(_outproj_body, alpha=alpha, top_k=TOP_K, aliased=aliased)
    return pl.pallas_call(
        body,
        grid=(nt,),
        in_specs=in_specs,
        out_specs=out_specs,
        out_shape=out_shape,
        input_output_aliases=io_alias,
        compiler_params=_cparams("arbitrary"),
        name="outproj",
    )(*args)


def _select_mod(i, n_ptiles, seq_ref, tok_ref):
    return jnp.where(i < n_ptiles, seq_ref[...], tok_ref[...])


def _for_run(rows, local_off, global_off, fn):
    @pl.when(rows > 0)
    def _():
        fn(pl.multiple_of(local_off, RUN_ALIGN), pl.multiple_of(global_off, RUN_ALIGN),
           pl.multiple_of(rows, RUN_ALIGN))


def _for_each_run(i, n_experts, toff_ref, cnt8_ref, gbase_ref, fn):
    for e in range(n_experts):
        idx = i * n_experts + e
        _for_run(cnt8_ref[idx], toff_ref[idx], gbase_ref[idx], fn)


def _scatter_matrix(pos_ref, n_local, values):
    pos = pos_ref[...]
    col = lax.broadcasted_iota(I32, (pos.shape[0], n_local), 1)
    out = jnp.zeros((pos.shape[0], n_local), F32)
    for k in range(pos.shape[1]):
        out = jnp.where(col == pos[:, k:k + 1], 1.0 if values is None else values[:, k:k + 1], out)
    return out


def _dispatch_body(toff_ref, cnt8_ref, gbase_ref, tstart_ref, trows_ref,
                   x1_ref, shs_ref, scs_ref, sht_ref, sct_ref, pos_ref,
                   xs_ref, lbuf, zbuf, sem, *, n_tiles, n_ptiles, n_experts):
    i = pl.program_id(0)
    sc = _select_mod(i, n_ptiles, scs_ref, sct_ref)
    sh = _select_mod(i, n_ptiles, shs_ref, sht_ref)
    h2 = (x1_ref[...] * (1.0 + sc) + sh).astype(BF16)
    n_local = lbuf.shape[1]
    slot = i % 2

    def runs(tile, sl, act):
        def fn(lo, go, rows):
            act(pltpu.make_async_copy(lbuf.at[sl, pl.ds(lo, rows), :], xs_ref.at[pl.ds(go, rows), :], sem.at[sl]))
        _for_each_run(tile, n_experts, toff_ref, cnt8_ref, gbase_ref, fn)

    start = lambda cp: cp.start()
    wait = lambda cp: cp.wait()

    @pl.when(i >= 2)
    def _():
        runs(i - 2, slot, wait)

    onehot = _scatter_matrix(pos_ref, n_local, None)
    srt = lax.dot_general(onehot.astype(BF16), h2, (((0,), (0,)), ((), ())), preferred_element_type=F32)
    lbuf[slot] = _pack_halves(srt)
    runs(i, slot, start)

    @pl.when(i == n_tiles - 1)
    def _():
        runs(i - 1, 1 - slot, wait)
        runs(i, slot, wait)
        zbuf[...] = jnp.zeros(zbuf.shape, zbuf.dtype)

        def zero_fill(e, act):
            _for_run(trows_ref[e], 0, tstart_ref[e], lambda lo, go, rows: act(pltpu.make_async_copy(
                zbuf.at[pl.ds(0, rows), :], xs_ref.at[pl.ds(go, rows), :], sem.at[0])))

        for e in range(n_experts):
            zero_fill(e, start)
        for e in range(n_experts):
            zero_fill(e, wait)


def _dispatch(tables, x1, sh_seq, sc_seq, sh_tok, sc_tok, pos, *, n_rows, n_local, n_ptiles, tiles_per_seq,
              n_experts):
    t, d = x1.shape
    tm = TOKEN_TILE
    nt = t // tm
    n_seq = sh_seq.shape[0]
    assert nt >= 2 and d % 2 == 0
    body = functools.partial(_dispatch_body, n_tiles=nt, n_ptiles=n_ptiles, n_experts=n_experts)
    seq_spec = pl.BlockSpec((None, 1, d), lambda i, *_: (jnp.minimum(i // tiles_per_seq, n_seq - 1), 0, 0))
    tok_spec = pl.BlockSpec((tm, d), lambda i, *_: (jnp.maximum(i - n_ptiles, 0), 0))
    return pl.pallas_call(
        body,
        grid_spec=pltpu.PrefetchScalarGridSpec(
            num_scalar_prefetch=len(tables),
            grid=(nt,),
            in_specs=[pl.BlockSpec((tm, d), lambda i, *_: (i, 0)),
                      seq_spec, seq_spec, tok_spec, tok_spec,
                      pl.BlockSpec((tm, pos.shape[1]), lambda i, *_: (i, 0))],
            out_specs=pl.BlockSpec(memory_space=pl.ANY),
            scratch_shapes=[pltpu.VMEM((2, n_local, d // 2), U32), pltpu.VMEM((EXPERT_ROWS, d // 2), U32),
                            pltpu.SemaphoreType.DMA((2,))],
        ),
        out_shape=jax.ShapeDtypeStruct((n_rows, d // 2), U32),
        compiler_params=_cparams("arbitrary"),
        name="dispatch",
    )(*tables, x1, sh_seq, sc_seq, sh_tok, sc_tok, pos)


def _expert_body(be_ref, nu_ref, first_ref, xs_ref, wgu_ref, bgu_ref, wd_ref, bd_ref, y_ref, wgu_bf, wd_bf, *, de):
    i = pl.program_id(0)
    used = i < nu_ref[0]

    @pl.when(used & (first_ref[i] == 1))
    def _():
        wgu_bf[...] = wgu_ref[...].astype(BF16)
        wd_bf[...] = wd_ref[...].astype(BF16)

    @pl.when(used)
    def _():
        x_lo, x_hi = _unpack_halves(xs_ref[...])
        half = x_lo.shape[1]
        gu = (jnp.dot(x_lo, wgu_bf[:half, :], preferred_element_type=F32)
              + jnp.dot(x_hi, wgu_bf[half:, :], preferred_element_type=F32) + bgu_ref[...])
        glu = jnp.minimum(gu[:, :de], SWIGLU_LIMIT)
        lin = jnp.clip(gu[:, de:], -SWIGLU_LIMIT, SWIGLU_LIMIT)
        act = glu * jax.nn.sigmoid(SWIGLU_ALPHA * glu) * (lin + 1.0)
        y = jnp.dot(act.astype(BF16), wd_bf[...], preferred_element_type=F32) + bd_ref[...]
        y_ref[...] = _pack_halves(y.astype(BF16).astype(F32))


def _experts(block_e, n_used, xs, w_gu, b_gu, w_down, b_down):
    p, dh = xs.shape
    bm = EXPERT_ROWS
    n_e, d, de2 = w_gu.shape
    de = de2 // 2
    nblk = p // bm

    def blk(i, be, nu, *_):
        return jnp.minimum(i, nu[0] - 1)

    first = jnp.concatenate([jnp.ones((1,), I32), (block_e[1:] != block_e[:-1]).astype(I32)])
    body = functools.partial(_expert_body, de=de)
    return pl.pallas_call(
        body,
        grid_spec=pltpu.PrefetchScalarGridSpec(
            num_scalar_prefetch=3,
            grid=(nblk,),
            in_specs=[pl.BlockSpec((bm, dh), lambda i, be, nu, ft: (blk(i, be, nu), 0)),
                      pl.BlockSpec((None, d, de2), lambda i, be, nu, ft: (be[blk(i, be, nu)], 0, 0)),
                      pl.BlockSpec((None, 1, de2), lambda i, be, nu, ft: (be[blk(i, be, nu)], 0, 0)),
                      pl.BlockSpec((None, de, d), lambda i, be, nu, ft: (be[blk(i, be, nu)], 0, 0)),
                      pl.BlockSpec((None, 1, d), lambda i, be, nu, ft: (be[blk(i, be, nu)], 0, 0))],
            out_specs=pl.BlockSpec((bm, dh), lambda i, be, nu, ft: (blk(i, be, nu), 0)),
            scratch_shapes=[pltpu.VMEM((d, de2), BF16), pltpu.VMEM((de, d), BF16)],
        ),
        out_shape=jax.ShapeDtypeStruct((p, dh), U32),
        compiler_params=_cparams("arbitrary"),
        name="experts",
    )(block_e, n_used, first, xs, w_gu, b_gu.reshape(n_e, 1, de2), w_down, b_down.reshape(n_e, 1, d))


def _combine_body(toff_ref, cnt8_ref, gbase_ref, x1_ref, gts_ref, gtt_ref, gate_ref, pos_ref,
                  lnw_ref, lnb_ref, yb_ref, yp_ref, ys_ref, ybuf, sem, *, top_k, n_ptiles, n_experts, alpha):
    i = pl.program_id(0)
    n_tiles = pl.num_programs(0)
    tm = x1_ref.shape[0]
    n_local = ybuf.shape[1]
    slot = i % 2

    def fetch(tile, sl, act):
        def fn(lo, go, rows):
            act(pltpu.make_async_copy(yb_ref.at[pl.ds(go, rows), :], ybuf.at[sl, pl.ds(lo, rows), :], sem.at[sl]))
        _for_each_run(tile, n_experts, toff_ref, cnt8_ref, gbase_ref, fn)

    def start_fetch(tile, sl):
        ybuf[sl, tm * top_k:, :] = jnp.zeros((n_local - tm * top_k, ybuf.shape[2]), ybuf.dtype)
        fetch(tile, sl, lambda cp: cp.start())

    @pl.when(i == 0)
    def _():
        start_fetch(i, slot)

    @pl.when(i + 1 < n_tiles)
    def _():
        start_fetch(i + 1, 1 - slot)

    weights = _scatter_matrix(pos_ref, n_local, gate_ref[...]).astype(BF16)
    fetch(i, slot, lambda cp: cp.wait())
    y_lo, y_hi = _unpack_halves(ybuf[slot])
    ff = jnp.concatenate([jnp.dot(weights, y_lo, preferred_element_type=F32),
                          jnp.dot(weights, y_hi, preferred_element_type=F32)], axis=1)
    gt = _select_mod(i, n_ptiles, gts_ref, gtt_ref)
    y = _layer_norm(alpha * x1_ref[...] + gt * ff, lnw_ref[...], lnb_ref[...])

    @pl.when(i < n_ptiles)
    def _():
        yp_ref[...] = y

    @pl.when(i >= n_ptiles)
    def _():
        ys_ref[...] = y


def _combine(tables, x1, gt_seq, gt_tok, gate, pos, lnw, lnb, yb, *, n_local, n_ptiles, tiles_per_seq, n_experts,
             alpha):
    t, d = x1.shape
    tm = TOKEN_TILE
    nt = t // tm
    n_seq = gt_seq.shape[0]
    body = functools.partial(_combine_body, top_k=TOP_K, n_ptiles=n_ptiles, n_experts=n_experts, alpha=alpha)
    lane_spec = pl.BlockSpec((tm, LANES), lambda i, *_: (i, 0))
    return pl.pallas_call(
        body,
        grid_spec=pltpu.PrefetchScalarGridSpec(
            num_scalar_prefetch=len(tables),
            grid=(nt,),
            in_specs=[pl.BlockSpec((tm, d), lambda i, *_: (i, 0)),
                      pl.BlockSpec((None, 1, d), lambda i, *_: (jnp.minimum(i // tiles_per_seq, n_seq - 1), 0, 0)),
                      pl.BlockSpec((tm, d), lambda i, *_: (jnp.maximum(i - n_ptiles, 0), 0)),
                      lane_spec,
                      pl.BlockSpec((tm, pos.shape[1]), lambda i, *_: (i, 0)),
                      pl.BlockSpec((1, d), lambda i, *_: (0, 0)),
                      pl.BlockSpec((1, d), lambda i, *_: (0, 0)),
                      pl.BlockSpec(memory_space=pl.ANY)],
            out_specs=[pl.BlockSpec((tm, d), lambda i, *_: (jnp.minimum(i, n_ptiles - 1), 0)),
                       pl.BlockSpec((tm, d), lambda i, *_: (jnp.maximum(i - n_ptiles, 0), 0))],
            scratch_shapes=[pltpu.VMEM((2, n_local, d // 2), U32), pltpu.SemaphoreType.DMA((2,))],
        ),
        out_shape=[jax.ShapeDtypeStruct((n_ptiles * tm, d), F32),
                   jax.ShapeDtypeStruct(((nt - n_ptiles) * tm, d), F32)],
        compiler_params=_cparams("arbitrary"),
        name="combine",
    )(*tables, x1, gt_seq, gt_tok, gate, pos, lnw, lnb, yb)


def _rotary_tables(pos, hd, rot_dim):
    half = rot_dim // 2
    inv_freq = jnp.power(jnp.float32(ROPE_THETA), -jnp.arange(half, dtype=F32) * (2.0 / rot_dim))
    ang = pos.astype(F32)[:, None] * inv_freq[None, :]
    cos, sin = jnp.cos(ang), jnp.sin(ang)
    n = pos.shape[0]
    ones = jnp.ones((n, hd - rot_dim), F32)
    zeros = jnp.zeros((n, hd - rot_dim), F32)
    zh = jnp.zeros((n, half), F32)
    ct = jnp.concatenate([cos, cos, ones], axis=1)
    s1 = jnp.concatenate([-sin, zh, zeros], axis=1)
    s2 = jnp.concatenate([zh, sin, zeros], axis=1)
    reps = LANES // hd
    return tuple(jnp.tile(a, (1, reps)) for a in (ct, s1, s2))


def kernel(x_prompt, x_sample, state_win_k, state_win_v, state_conv, state_ssm, c_prompt, c_sample, w_ada, b_ada, w_in, attn_sinks, w_conv, dn_a_log, dn_dt_bias, dn_norm_w, w_proj_attn, w_proj_dn, w_out, ln1_w, ln1_b, w_router, b_router, w_gu, b_gu, w_down, b_down, ln2_w, ln2_b):
    n_p, seq, d = x_prompt.shape
    n_s, l_s, _ = x_sample.shape
    depth = w_ada.shape[0]
    window, n_kv, hd = state_win_k.shape[2:]
    n_q = attn_sinks.shape[1]
    heads, dk, dv = state_ssm.shape[2:]
    conv_w, conv_dim = w_conv.shape[1:]
    n_e = w_router.shape[2]
    qd, kd = n_q * hd, n_kv * hd
    vdim = heads * dv
    rot_dim = hd // 4
    alpha = float((2 * depth) ** 0.25)
    tm = TOKEN_TILE
    t_p, t_s = n_p * seq, n_s * l_s
    t_all = t_p + t_s
    tps = seq // tm
    n_ptiles = t_p // tm
    l_pad = SUBLANES
    assert seq % tm == 0 and t_s % tm == 0 and tm % l_s == 0 and l_s <= l_pad and l_s >= conv_w - 1
    assert 2 * heads <= SUBLANES and n_e <= LANES and hd * 2 == LANES and rot_dim == 2 * SUBLANES

    sizes = [qd, kd, kd, conv_dim, vdim, heads, heads, d, d]
    offs = [int(o) for o in np.concatenate([[0], np.cumsum(sizes)])]
    cuts = (qd + 2 * kd, qd + 2 * kd + conv_dim, qd + 2 * kd + conv_dim + vdim, qd + 2 * kd + conv_dim + vdim + 2 * d)
    cuts = cuts + (cuts[-1] + LANES,)

    tabs_p = _rotary_tables(jnp.arange(seq, dtype=I32), hd, rot_dim)
    tabs_s = tuple(jnp.tile(a, (tm // l_s, 1))
                   for a in _rotary_tables(PAST_LEN + jnp.arange(l_s, dtype=I32), hd, rot_dim))

    x_p = x_prompt.reshape(t_p, d)
    x_s = x_sample.reshape(t_s, d)
    c_all = jnp.concatenate([c_prompt, c_sample], axis=0)
    outs = {k: [] for k in ("pwk", "pwv", "pcv", "pss", "swk", "swv", "scv", "sss")}

    for l in range(depth):
        w_in_l = jnp.concatenate([w_in[l][:, :offs[5]], w_in[l][:, offs[7]:], w_in[l][:, offs[5]:offs[7]],
                                  jnp.zeros((d, LANES - 2 * heads), F32)], axis=1).astype(BF16)
        mod = _ada(c_all, w_ada[l], b_ada[l])
        mod_p = mod[:n_p].reshape(n_p, 6, 1, d)
        mod_s = jnp.repeat(mod[n_p:].reshape(n_s, 6, d), l_s, axis=0)
        sh1p, sc1p, gt1p, sh2p, sc2p, gt2p = [mod_p[:, k] for k in range(6)]
        sh1s, sc1s, gt1s, sh2s, sc2s, gt2s = [mod_s[:, k] for k in range(6)]

        qkva_p, dn_p, z_p, g_p, ba_p, tail_p, kvw_p = _inproj(
            x_p, sh1p, sc1p, tabs_p, w_in_l, cuts, per_token_mod=False, tiles_per_seq=tps, act_dtype=BF16,
            window=window, kv_cols=2 * kd)
        attn_p = _attn_prompt(qkva_p, attn_sinks[l], n_p, seq, n_q, n_kv, hd, window)
        chunk = min(DN_CHUNK, seq)
        nc = seq // chunk
        bat_p = ba_p[:, :SUBLANES].reshape(n_p, nc, chunk, SUBLANES).transpose(0, 1, 3, 2)
        hp = jnp.zeros((SUBLANES, LANES), F32).at[0, :heads].set(dn_a_log[l]).at[1, :heads].set(dn_dt_bias[l])
        nw = dn_norm_w[l].reshape(1, dv)
        o_p, ssm_p = _deltanet(dn_p.reshape(n_p, seq, conv_dim), z_p.reshape(n_p, seq, vdim),
                               ba_p.reshape(n_p, seq, LANES), bat_p,
                               jnp.zeros((n_p, SUBLANES, conv_dim), F32), jnp.zeros((n_p, heads, dk, dv), F32),
                               w_conv[l], hp, nw, chunk=chunk, l_real=chunk, nb=4)
        outs["pwk"].append(kvw_p[:, :, :kd].reshape(n_p, window, n_kv, hd))
        outs["pwv"].append(kvw_p[:, :, kd:].reshape(n_p, window, n_kv, hd))
        outs["pcv"].append(tail_p.reshape(n_p, tps, SUBLANES, conv_dim)[:, -1, SUBLANES - (conv_w - 1):])
        outs["pss"].append(ssm_p)

        qkva_s, dn_s, z_s, g_s, ba_s = _inproj(
            x_s, sh1s, sc1s, tabs_s, w_in_l, cuts, per_token_mod=True, tiles_per_seq=1, act_dtype=F32,
            window=window, kv_cols=2 * kd)
        pad_l = lambda a: jnp.pad(a.reshape(n_s, l_s, a.shape[-1]), ((0, 0), (0, l_pad - l_s), (0, 0)))
        attn_s, wk_s, wv_s = _attn_sample(pad_l(qkva_s), state_win_k[l].reshape(n_s, window, kd),
                                          state_win_v[l].reshape(n_s, window, kd), attn_sinks[l], n_q, n_kv, hd, l_s)
        attn_s = attn_s[:, :l_s].reshape(t_s, qd)
        ba_s3 = pad_l(ba_s)
        bat_s = ba_s3[:, :, :SUBLANES].transpose(0, 2, 1).reshape(n_s, 1, SUBLANES, l_pad)
        cs0 = jnp.pad(state_conv[l], ((0, 0), (SUBLANES - (conv_w - 1), 0), (0, 0)))
        o_s, ssm_s = _deltanet(pad_l(dn_s), pad_l(z_s), ba_s3, bat_s, cs0, state_ssm[l], w_conv[l], hp, nw,
                               chunk=l_pad, l_real=l_s, nb=8)
        o_s = o_s[:, :l_s].reshape(t_s, vdim)
        outs["swk"].append(wk_s.reshape(n_s, window, n_kv, hd))
        outs["swv"].append(wv_s.reshape(n_s, window, n_kv, hd))
        outs["scv"].append(jnp.concatenate([state_conv[l], dn_s.reshape(n_s, l_s, conv_dim)], axis=1)[:, -(conv_w - 1):])
        outs["sss"].append(ssm_s)

        wr = jnp.pad(w_router[l], ((0, 0), (0, LANES - n_e)))
        br = jnp.pad(b_router[l], (0, LANES - n_e), constant_values=NEG_BIG).reshape(1, LANES)
        wr_hi = wr.astype(BF16)
        wr_lo = (wr - wr_hi.astype(F32)).astype(BF16)
        wts = (w_proj_attn[l].astype(BF16), w_proj_dn[l].astype(BF16), w_out[l].astype(BF16),
               ln1_w[l].reshape(1, d), ln1_b[l].reshape(1, d), wr_hi, wr_lo, br)
        res_p = _outproj(attn_p, o_p.reshape(t_p, vdim), g_p, x_p, gt1p, sh2p, sc2p, wts, None,
                         per_token_mod=False, tiles_per_seq=tps, t_total=t_all, tile_off=0, alpha=alpha)
        x1, e_idx, gate, rank, cnt = _outproj(attn_s, o_s, g_s, x_s, gt1s, sh2s, sc2s, wts, res_p,
                                              per_token_mod=True, tiles_per_seq=1, t_total=t_all, tile_off=n_ptiles,
                                              alpha=alpha)

        bm = EXPERT_ROWS
        nt_all = t_all // tm
        cnt8 = (cnt[:, 0, :n_e].astype(I32) + RUN_ALIGN - 1) // RUN_ALIGN * RUN_ALIGN
        tot = jnp.sum(cnt8, axis=0)
        padded = (tot + bm - 1) // bm * bm
        pad_end = jnp.cumsum(padded)
        pad_start = pad_end - padded
        gbase = pad_start[None, :] + jnp.cumsum(cnt8, axis=0) - cnt8
        toff = jnp.cumsum(cnt8, axis=1) - cnt8
        n_local = -(-(tm * TOP_K + n_e * (RUN_ALIGN - 1)) // LANES) * LANES
        n_rows = -(-(t_all * TOP_K + nt_all * n_e * (RUN_ALIGN - 1) + n_e * (bm - 1)) // bm) * bm
        nblk = n_rows // bm
        n_used = jnp.maximum(pad_end[-1:] // bm, 1).astype(I32)
        block_e = jnp.minimum(jnp.sum(pad_end[None, :] <= (jnp.arange(nblk, dtype=I32) * bm)[:, None], axis=1),
                              n_e - 1).astype(I32)
        flat = lambda a: a.astype(I32).reshape(nt_all * n_e)
        run_tables = (flat(toff), flat(cnt8), flat(gbase))
        tail_tables = ((pad_start + tot).astype(I32), (padded - tot).astype(I32))
        tile_of = (jnp.arange(t_all, dtype=I32) // tm)[:, None]
        pos = (toff[tile_of, e_idx[:, :TOP_K]] + rank[:, :TOP_K]).astype(I32)

        xs = _dispatch(run_tables + tail_tables, x1, sh2p, sc2p, sh2s, sc2s, pos, n_rows=n_rows, n_local=n_local,
                       n_ptiles=n_ptiles, tiles_per_seq=tps, n_experts=n_e)
        yb = _experts(block_e, n_used, xs, w_gu[l], b_gu[l], w_down[l], b_down[l])
        x_p, x_s = _combine(run_tables, x1, gt2p, gt2s, gate, pos, ln2_w[l].reshape(1, d), ln2_b[l].reshape(1, d),
                            yb, n_local=n_local, n_ptiles=n_ptiles, tiles_per_seq=tps, n_experts=n_e, alpha=alpha)

    st = lambda k: jnp.stack(outs[k])
    return (x_p.reshape(n_p, seq, d), x_s.reshape(n_s, l_s, d), st("pwk"), st("pwv"), st("pcv"), st("pss"),
            st("swk"), st("swv"), st("scv"), st("sss"))
```

```python
import functools

import numpy as np
import jax
import jax.numpy as jnp
from jax import lax
from jax.experimental import pallas as pl
from jax.experimental.pallas import tpu as pltpu

F32 = jnp.float32
BF16 = jnp.bfloat16
I32 = jnp.int32
U32 = jnp.uint32

PAST_LEN = 16384
ROPE_THETA = 500000.0
TOP_K = 4
SWIGLU_LIMIT = 7.0
SWIGLU_ALPHA = 1.702
DN_CHUNK = 64
LN_EPS = 1e-5
RMS_EPS = 1e-6
L2_EPS = 1e-6

LANES = 128
SUBLANES = 8
VMEM_LIMIT_BYTES = 56 * 1024 * 1024

TOKEN_TILE = 256
EXPERT_ROWS = 512
RUN_ALIGN = SUBLANES
OUTPROJ_SUBTILES = 2
N_ROUTE_BUFS = 5
NEG_BIG = -1e30


def _cparams(*sem):
    return pltpu.CompilerParams(dimension_semantics=sem, vmem_limit_bytes=VMEM_LIMIT_BYTES)


def _silu(x):
    return x * jax.nn.sigmoid(x)


def _bdot(a, b):
    return jnp.dot(a.astype(BF16), b.astype(BF16), preferred_element_type=F32)


def _pack_halves(x):
    n = x.shape[1] // 2
    lo = lax.bitcast_convert_type(x[:, :n], U32)
    hi = lax.bitcast_convert_type(x[:, n:], U32)
    return (hi & jnp.uint32(0xFFFF0000)) | (lo >> 16)


def _unpack_halves(w):
    lo = lax.bitcast_convert_type(w << 16, F32).astype(BF16)
    hi = lax.bitcast_convert_type(w & jnp.uint32(0xFFFF0000), F32).astype(BF16)
    return lo, hi


def _ada_body(c_ref, w_ref, b_ref, o_ref):
    o_ref[...] = _bdot(_silu(c_ref[...]), w_ref[...]) + b_ref[...]


def _ada(c_all, w_ada, b_ada):
    n, d = c_all.shape
    dout = w_ada.shape[1]
    tn = d
    return pl.pallas_call(
        _ada_body,
        grid=(dout // tn,),
        in_specs=[pl.BlockSpec((n, d), lambda j: (0, 0)),
                  pl.BlockSpec((d, tn), lambda j: (0, j)),
                  pl.BlockSpec((1, tn), lambda j: (0, j))],
        out_specs=pl.BlockSpec((n, tn), lambda j: (0, j)),
        out_shape=jax.ShapeDtypeStruct((n, dout), F32),
        compiler_params=_cparams("arbitrary"),
        name="ada",
    )(c_all, w_ada, b_ada.reshape(1, dout))


def _inproj_body(x_ref, sh_ref, sc_ref, ct_ref, s1_ref, s2_ref, w_ref,
                 a_ref, dn_ref, z_ref, g_ref, ba_ref, *win_refs, cuts, n_rot_chunks, window):
    h = (x_ref[...] * (1.0 + sc_ref[...]) + sh_ref[...]).astype(BF16)

    def mm(lo, hi):
        return jnp.dot(h, w_ref[:, lo:hi], preferred_element_type=F32)

    c_a, c_dn, c_z, c_g, c_ba = cuts
    qkv = mm(0, c_a)
    ct, s1, s2 = ct_ref[...], s1_ref[...], s2_ref[...]
    cols = []
    for c in range(n_rot_chunks):
        xc = qkv[:, c * LANES:(c + 1) * LANES]
        cols.append(xc * ct + pltpu.roll(xc, LANES - SUBLANES, 1) * s1 + pltpu.roll(xc, SUBLANES, 1) * s2)
    cols.append(qkv[:, n_rot_chunks * LANES:])
    rot = jnp.concatenate(cols, axis=1)
    a_ref[...] = rot.astype(a_ref.dtype)
    dn = mm(c_a, c_dn)
    dn_ref[...] = dn.astype(dn_ref.dtype)
    z_ref[...] = mm(c_dn, c_z).astype(z_ref.dtype)
    g_ref[...] = mm(c_z, c_g).astype(g_ref.dtype)
    ba_ref[...] = mm(c_g, c_ba)
    if win_refs:
        tail_ref, kvw_ref = win_refs
        tm = dn.shape[0]
        tail_ref[...] = dn[tm - SUBLANES:, :]
        kvw_ref[...] = rot[tm - window:, n_rot_chunks * LANES - LANES:]


def _inproj(x, sh, sc, tabs, w_perm, cuts, *, per_token_mod, tiles_per_seq, act_dtype, window, kv_cols):
    t, d = x.shape
    tm = TOKEN_TILE
    nt = t // tm
    c_a, c_dn, c_z, c_g, c_ba = cuts
    n_rot_chunks = (c_a - kv_cols // 2) // LANES
    if per_token_mod:
        mod_spec = pl.BlockSpec((tm, d), lambda i: (i, 0))
        tab_spec = pl.BlockSpec((tm, LANES), lambda i: (0, 0))
    else:
        mod_spec = pl.BlockSpec((None, 1, d), lambda i: (i // tiles_per_seq, 0, 0))
        tab_spec = pl.BlockSpec((tm, LANES), lambda i: (i % tiles_per_seq, 0))
    out_shape = [jax.ShapeDtypeStruct((t, c_a), act_dtype),
                 jax.ShapeDtypeStruct((t, c_dn - c_a), act_dtype),
                 jax.ShapeDtypeStruct((t, c_z - c_dn), act_dtype),
                 jax.ShapeDtypeStruct((t, c_g - c_z), act_dtype),
                 jax.ShapeDtypeStruct((t, c_ba - c_g), F32)]
    out_specs = [pl.BlockSpec((tm, s.shape[1]), lambda i: (i, 0)) for s in out_shape]
    with_win = not per_token_mod
    if with_win:
        n_seq = nt // tiles_per_seq
        out_shape += [jax.ShapeDtypeStruct((nt, SUBLANES, c_dn - c_a), F32),
                      jax.ShapeDtypeStruct((n_seq, window, kv_cols), F32)]
        out_specs += [pl.BlockSpec((None, SUBLANES, c_dn - c_a), lambda i: (i, 0, 0)),
                      pl.BlockSpec((None, window, kv_cols), lambda i: (i // tiles_per_seq, 0, 0))]
    body = functools.partial(_inproj_body, cuts=cuts, n_rot_chunks=n_rot_chunks, window=window)
    return pl.pallas_call(
        body,
        grid=(nt,),
        in_specs=[pl.BlockSpec((tm, d), lambda i: (i, 0)), mod_spec, mod_spec,
                  tab_spec, tab_spec, tab_spec,
                  pl.BlockSpec((d, c_ba), lambda i: (0, 0))],
        out_specs=out_specs,
        out_shape=out_shape,
        compiler_params=_cparams("arbitrary"),
        name="inproj",
    )(x, sh, sc, *tabs, w_perm)


def _softmax_sink_pv(scores, valid, sinks, values):
    ms = [jnp.where(valid, s, -jnp.inf) for s in scores]
    m = [jnp.maximum(jnp.max(x, axis=-1, keepdims=True), sk) for x, sk in zip(ms, sinks)]
    p = [jnp.exp(x - mi) for x, mi in zip(ms, m)]
    den = [jnp.sum(pi, axis=-1, keepdims=True) + jnp.exp(sk - mi) for pi, sk, mi in zip(p, sinks, m)]
    return [jnp.dot((pi / di).astype(BF16), v, preferred_element_type=F32) for pi, di, v in zip(p, den, values)]


def _attn_prompt_body(sink_ref, q_ref, kvp_ref, kvc_ref, o_ref, *, n_q, n_kv, hd, window):
    j = pl.program_id(1)
    group = n_q // n_kv
    scale = hd ** -0.5
    assert np.log2(scale) == int(np.log2(scale))
    r = lax.broadcasted_iota(I32, (window, 2 * window), 0)
    c = lax.broadcasted_iota(I32, (window, 2 * window), 1)
    rel = window + r - c
    band = (rel >= 0) & (rel < window)
    kv_cur = kvc_ref[...]
    kv_first = jnp.concatenate([kvp_ref[...], kv_cur[:window]], axis=0)
    for qb, (kv, valid) in enumerate(((kv_first, band & ((c >= window) | (j > 0))), (kv_cur, band))):
        q = q_ref[qb * window:(qb + 1) * window, :] * scale
        scores = [lax.dot_general(q[:, h * hd:(h + 1) * hd], kv[:, (h // group) * hd:(h // group + 1) * hd],
                                  (((1,), (1,)), ((), ())), preferred_element_type=F32) for h in range(n_q)]
        values = [kv[:, (n_kv + h // group) * hd:(n_kv + h // group + 1) * hd] for h in range(n_q)]
        outs = _softmax_sink_pv(scores, valid, [sink_ref[h] for h in range(n_q)], values)
        o_ref[qb * window:(qb + 1) * window, :] = jnp.concatenate(outs, axis=1).astype(o_ref.dtype)


def _attn_prompt(qkva, sinks, n_seq, seq, n_q, n_kv, hd, window):
    qd, kvd = n_q * hd, 2 * n_kv * hd
    x3 = qkva.reshape(n_seq, seq, qd + kvd)
    nb = seq // window
    assert nb % 2 == 0 and qd % kvd == 0
    kv_blk = qd // kvd
    body = functools.partial(_attn_prompt_body, n_q=n_q, n_kv=n_kv, hd=hd, window=window)
    out = pl.pallas_call(
        body,
        grid=(n_seq, nb // 2),
        in_specs=[pl.BlockSpec(memory_space=pltpu.SMEM),
                  pl.BlockSpec((None, 2 * window, qd), lambda n, j: (n, j, 0)),
                  pl.BlockSpec((None, window, kvd), lambda n, j: (n, jnp.maximum(2 * j - 1, 0), kv_blk)),
                  pl.BlockSpec((None, 2 * window, kvd), lambda n, j: (n, j, kv_blk))],
        out_specs=pl.BlockSpec((None, 2 * window, qd), lambda n, j: (n, j, 0)),
        out_shape=jax.ShapeDtypeStruct((n_seq, seq, qd), BF16),
        compiler_params=_cparams("arbitrary", "arbitrary"),
        name="attn_prompt",
    )(sinks, x3, x3, x3)
    return out.reshape(n_seq * seq, qd)


def _attn_sample_body(sink_ref, q_ref, wk_ref, wv_ref, o_ref, wko_ref, wvo_ref,
                      *, bs, n_q, n_kv, hd, window, l_new, l_pad):
    group = n_q // n_kv
    qd = n_q * hd
    kd = n_kv * hd
    rows = group * l_pad
    r = lax.broadcasted_iota(I32, (rows, window + l_pad), 0) % l_pad
    c = lax.broadcasted_iota(I32, (rows, window + l_pad), 1)
    rel = window + r - c
    valid = (rel >= 0) & (rel < window) & (c < window + l_new)
    sinks = [jnp.concatenate([jnp.full((l_pad, 1), sink_ref[kvh * group + g], F32) for g in range(group)], axis=0)
             for kvh in range(n_kv)]
    qs, ks, vs = [], [], []
    for b in range(bs):
        x = q_ref[b]
        k_new = x[:, qd:qd + kd]
        v_new = x[:, qd + kd:]
        wko_ref[b, 0:window - l_new, :] = wk_ref[b, l_new:window, :]
        wko_ref[b, window - l_new:window, :] = k_new[0:l_new, :]
        wvo_ref[b, 0:window - l_new, :] = wv_ref[b, l_new:window, :]
        wvo_ref[b, window - l_new:window, :] = v_new[0:l_new, :]
        k_all = jnp.concatenate([wk_ref[b], k_new], axis=0).astype(BF16)
        v_all = jnp.concatenate([wv_ref[b], v_new], axis=0).astype(BF16)
        for kvh in range(n_kv):
            qs.append(jnp.concatenate([x[:, (kvh * group + g) * hd:(kvh * group + g + 1) * hd]
                                       for g in range(group)], axis=0).astype(BF16))
            ks.append(k_all[:, kvh * hd:(kvh + 1) * hd])
            vs.append(v_all[:, kvh * hd:(kvh + 1) * hd])
    n = len(qs)
    s = [lax.dot_general(qs[i], ks[i], (((1,), (1,)), ((), ())), preferred_element_type=F32) * (hd ** -0.5)
         for i in range(n)]
    o = _softmax_sink_pv(s, valid, [sinks[i % n_kv] for i in range(n)], vs)
    for b in range(bs):
        outs = [o[b * n_kv + kvh][g * l_pad:(g + 1) * l_pad, :] for kvh in range(n_kv) for g in range(group)]
        o_ref[b] = jnp.concatenate(outs, axis=1).astype(o_ref.dtype)


def _attn_sample(qkva_pad, win_k, win_v, sinks, n_q, n_kv, hd, l_new):
    n, l_pad, width = qkva_pad.shape
    window, kd = win_k.shape[1], win_k.shape[2]
    qd = n_q * hd
    bs = 8
    body = functools.partial(_attn_sample_body, bs=bs, n_q=n_q, n_kv=n_kv, hd=hd, window=window,
                             l_new=l_new, l_pad=l_pad)
    return pl.pallas_call(
        body,
        grid=(n // bs,),
        in_specs=[pl.BlockSpec(memory_space=pltpu.SMEM),
                  pl.BlockSpec((bs, l_pad, width), lambda i: (i, 0, 0)),
                  pl.BlockSpec((bs, window, kd), lambda i: (i, 0, 0)),
                  pl.BlockSpec((bs, window, kd), lambda i: (i, 0, 0))],
        out_specs=[pl.BlockSpec((bs, l_pad, qd), lambda i: (i, 0, 0)),
                   pl.BlockSpec((bs, window, kd), lambda i: (i, 0, 0)),
                   pl.BlockSpec((bs, window, kd), lambda i: (i, 0, 0))],
        out_shape=[jax.ShapeDtypeStruct((n, l_pad, qd), BF16),
                   jax.ShapeDtypeStruct((n, window, kd), F32),
                   jax.ShapeDtypeStruct((n, window, kd), F32)],
        compiler_params=_cparams("arbitrary"),
        name="attn_sample",
    )(sinks, qkva_pad, win_k, win_v)


def _split_bf16(x):
    hi = x.astype(BF16)
    return hi, (x - hi.astype(F32)).astype(BF16)


def _tdot(a, b):
    ah, al = _split_bf16(a)
    bh, bl = _split_bf16(b)
    m = a.shape[0]
    t = jnp.dot(jnp.concatenate([ah, al], axis=0), bh, preferred_element_type=F32)
    return t[:m] + t[m:] + jnp.dot(ah, bl, preferred_element_type=F32)


def _dn_body(qkv_ref, z_ref, ba_ref, bat_ref, cs0_ref, s0_ref, wc_ref, hp_ref, nw_ref,
             o_ref, s_ref, xbuf, *, nb, chunk, heads, dk, dv, l_real, conv_w):
    c_idx = pl.program_id(1)
    hc = SUBLANES

    @pl.when(c_idx == 0)
    def _():
        xbuf[:, 0:hc, :] = cs0_ref[...]
        s_ref[...] = s0_ref[...]

    qk_dim = heads * dk
    row = lax.broadcasted_iota(I32, (chunk, chunk), 0)
    col = lax.broadcasted_iota(I32, (chunk, chunk), 1)
    incl = row >= col
    strict = row > col
    eye = (row == col).astype(F32)
    valid_c = row[:, 0:1] < l_real
    valid_r = col[0:1, :] < l_real
    n_levels = max(1, int(np.ceil(np.log2(chunk))))
    wc = wc_ref[...]
    hp = hp_ref[...]
    neg_exp_alog = -jnp.exp(hp[0:1, :])
    dt_bias = hp[1:2, :]
    nw = nw_ref[...]
    chains = [(b, h) for b in range(nb) for h in range(heads)]
    n = len(chains)

    ys = []
    for b in range(nb):
        xbuf[b, hc:hc + chunk, :] = qkv_ref[b].astype(F32)
        y = xbuf[b, hc:hc + chunk, :] * wc[conv_w - 1:conv_w, :]
        for j in range(conv_w - 1):
            off = hc - (conv_w - 1) + j
            y = y + xbuf[b, off:off + chunk, :] * wc[j:j + 1, :]
        ys.append(_silu(y))
        xbuf[b, 0:hc, :] = xbuf[b, chunk:chunk + hc, :]

    qn, kn, kb, vb, decay, e_gc, e_rest, e_last = [], [], [], [], [], [], [], []
    for b, h in chains:
        y = ys[b]
        qh = y[:, h * dk:(h + 1) * dk]
        kh = y[:, qk_dim + h * dk:qk_dim + (h + 1) * dk]
        vh = y[:, 2 * qk_dim + h * dv:2 * qk_dim + (h + 1) * dv]
        ba = ba_ref[b]
        bat = bat_ref[b]
        ne = neg_exp_alog[:, h:h + 1]
        db = dt_bias[:, h:h + 1]
        beta = jnp.where(valid_c, jax.nn.sigmoid(ba[:, h:h + 1]), 0.0)
        g_col = jnp.where(valid_c, ne * jax.nn.softplus(ba[:, heads + h:heads + h + 1] + db), 0.0)
        g_row = jnp.where(valid_r, ne * jax.nn.softplus(bat[heads + h:heads + h + 1, :] + db), 0.0)
        gc_col = jnp.sum(jnp.where(incl, g_row, 0.0), axis=1, keepdims=True)
        gc_row = jnp.sum(jnp.where(row <= col, g_col, 0.0), axis=0, keepdims=True)
        g_last = gc_col[chunk - 1:chunk, :]
        q_ = qh * lax.rsqrt(jnp.sum(qh * qh, -1, keepdims=True) + L2_EPS) * (dk ** -0.5)
        k_ = kh * lax.rsqrt(jnp.sum(kh * kh, -1, keepdims=True) + L2_EPS)
        qn.append(q_)
        kn.append(k_)
        kb.append(k_ * beta)
        vb.append(vh * beta)
        decay.append(jnp.where(incl, jnp.exp(gc_col - gc_row), 0.0))
        e_gc.append(jnp.exp(gc_col))
        e_rest.append(jnp.exp(g_last - gc_col))
        e_last.append(jnp.exp(g_last))

    sc = [lax.dot_general(jnp.concatenate([qn[i], kb[i]], axis=0).astype(BF16), kn[i].astype(BF16),
                          (((1,), (1,)), ((), ())), preferred_element_type=F32) for i in range(n)]
    qk = [sc[i][:chunk] * decay[i] for i in range(n)]
    p = [jnp.where(strict, -(sc[i][chunk:] * decay[i]), 0.0) for i in range(n)]
    t_inv = [eye + p[i] for i in range(n)]
    if n_levels > 1:
        p = [_tdot(p[i], p[i]) for i in range(n)]
    for lvl in range(1, n_levels):
        if lvl < n_levels - 1:
            yp = [_tdot(jnp.concatenate([t_inv[i], p[i]], axis=0), p[i]) for i in range(n)]
            t_inv = [t_inv[i] + yp[i][:chunk] for i in range(n)]
            p = [yp[i][chunk:] for i in range(n)]
        else:
            t_inv = [t_inv[i] + _tdot(t_inv[i], p[i]) for i in range(n)]
    sol = [_tdot(t_inv[i], jnp.concatenate([vb[i], kb[i] * e_gc[i]], axis=1)) for i in range(n)]
    s_old = [s_ref[b, h] for b, h in chains]
    wq = [_bdot(jnp.concatenate([sol[i][:, dv:], qn[i] * e_gc[i]], axis=0), s_old[i]) for i in range(n)]
    v_new = [sol[i][:, :dv] - wq[i][:chunk] for i in range(n)]
    o = [wq[i][chunk:] + _bdot(qk[i], v_new[i]) for i in range(n)]
    for i, (b, h) in enumerate(chains):
        s_ref[b, h] = s_old[i] * e_last[i] + lax.dot_general(
            (kn[i] * e_rest[i]).astype(BF16), v_new[i].astype(BF16), (((0,), (0,)), ((), ())),
            preferred_element_type=F32)
    for b in range(nb):
        zt = z_ref[b].astype(F32)
        outs = []
        for h in range(heads):
            oi = o[b * heads + h]
            on = oi * lax.rsqrt(jnp.mean(oi * oi, -1, keepdims=True) + RMS_EPS) * nw
            outs.append(on * _silu(zt[:, h * dv:(h + 1) * dv]))
        o_ref[b] = jnp.concatenate(outs, axis=1).astype(o_ref.dtype)


def _deltanet(qkv, z, ba, bat, cs0, s0, w_conv, hp, norm_w, *, chunk, l_real, nb):
    n, l, conv_dim = qkv.shape
    heads, dk, dv = s0.shape[1:]
    nc = l // chunk
    assert n % nb == 0 and l % chunk == 0
    conv_w = w_conv.shape[0]
    hist = pltpu.VMEM((nb, SUBLANES + chunk, conv_dim), F32)
    body = functools.partial(_dn_body, nb=nb, chunk=chunk, heads=heads, dk=dk, dv=dv, l_real=l_real, conv_w=conv_w)
    return pl.pallas_call(
        body,
        grid=(n // nb, nc),
        in_specs=[pl.BlockSpec((nb, chunk, conv_dim), lambda i, c: (i, c, 0)),
                  pl.BlockSpec((nb, chunk, heads * dv), lambda i, c: (i, c, 0)),
                  pl.BlockSpec((nb, chunk, LANES), lambda i, c: (i, c, 0)),
                  pl.BlockSpec((nb, None, SUBLANES, chunk), lambda i, c: (i, c, 0, 0)),
                  pl.BlockSpec((nb, SUBLANES, conv_dim), lambda i, c: (i, 0, 0)),
                  pl.BlockSpec((nb, heads, dk, dv), lambda i, c: (i, 0, 0, 0)),
                  pl.BlockSpec((conv_w, conv_dim), lambda i, c: (0, 0)),
                  pl.BlockSpec((SUBLANES, LANES), lambda i, c: (0, 0)),
                  pl.BlockSpec((1, dv), lambda i, c: (0, 0))],
        out_specs=[pl.BlockSpec((nb, chunk, heads * dv), lambda i, c: (i, c, 0)),
                   pl.BlockSpec((nb, heads, dk, dv), lambda i, c: (i, 0, 0, 0))],
        out_shape=[jax.ShapeDtypeStruct((n, l, heads * dv), BF16),
                   jax.ShapeDtypeStruct((n, heads, dk, dv), F32)],
        scratch_shapes=[hist],
        compiler_params=_cparams("arbitrary", "arbitrary"),
        name="deltanet",
    )(qkv, z, ba, bat, cs0, s0, w_conv, hp, norm_w)


def _layer_norm(r, w, b):
    mu = jnp.mean(r, -1, keepdims=True)
    var = jnp.mean(jnp.square(r - mu), -1, keepdims=True)
    return (r - mu) * lax.rsqrt(var + LN_EPS) * w + b


def _outproj_body(attn_ref, dn_ref, g_ref, x_ref, gt_ref, sh2_ref, sc2_ref, wpa_ref, wpd_ref, wo_ref,
                  lnw_ref, lnb_ref, wrh_ref, wrl_ref, br_ref, *refs, alpha, top_k, aliased):
    if aliased:
        refs = refs[N_ROUTE_BUFS:]
    x1_ref, eidx_ref, gate_ref, rank_ref, cnt_ref = refs
    for s in range(cnt_ref.shape[0]):
        _outproj_tile(s, attn_ref, dn_ref, g_ref, x_ref, gt_ref, sh2_ref, sc2_ref, wpa_ref, wpd_ref, wo_ref, lnw_ref,
                      lnb_ref, wrh_ref, wrl_ref, br_ref, x1_ref, eidx_ref, gate_ref, rank_ref, cnt_ref,
                      alpha=alpha, top_k=top_k)


def _outproj_tile(s, attn_ref, dn_ref, g_ref, x_ref, gt_ref, sh2_ref, sc2_ref, wpa_ref, wpd_ref, wo_ref, lnw_ref,
                  lnb_ref, wrh_ref, wrl_ref, br_ref, x1_ref, eidx_ref, gate_ref, rank_ref, cnt_ref, *, alpha, top_k):
    d = x_ref.shape[1]
    tm = TOKEN_TILE
    rows = pl.ds(s * tm, tm)

    def mod(ref):
        return ref[...] if ref.shape[0] == 1 else ref[rows, :]

    g = g_ref[rows, :].astype(F32)
    pa = jnp.dot(attn_ref[rows, :], wpa_ref[...], preferred_element_type=F32)
    pd = jnp.dot(dn_ref[rows, :], wpd_ref[...], preferred_element_type=F32)
    merged = jax.nn.sigmoid(g[:, :d]) * pa + jax.nn.sigmoid(g[:, d:]) * pd
    mix = jnp.dot(merged.astype(BF16), wo_ref[...], preferred_element_type=F32)
    x1 = _layer_norm(alpha * x_ref[rows, :] + mod(gt_ref) * mix, lnw_ref[...], lnb_ref[...])
    x1_ref[rows, :] = x1
    h2 = x1 * (1.0 + mod(sc2_ref)) + mod(sh2_ref)
    h_hi, h_lo = _split_bf16(h2)
    lg = jnp.dot(jnp.concatenate([h_hi, h_lo], axis=0), wrh_ref[...], preferred_element_type=F32)
    logits = lg[:tm] + lg[tm:] + jnp.dot(h_hi, wrl_ref[...], preferred_element_type=F32) + br_ref[...]
    lane = lax.broadcasted_iota(I32, (tm, LANES), 1)
    lane_f = lane.astype(F32)
    vals, idxs, sels = [], [], []
    l = logits
    for _ in range(top_k):
        m = jnp.max(l, axis=1, keepdims=True)
        idx = jnp.min(jnp.where(l == m, lane_f, float(LANES)), axis=1, keepdims=True)
        sel = lane_f == idx
        vals.append(m)
        idxs.append(idx)
        sels.append(sel)
        l = jnp.where(sel, -jnp.inf, l)
    ex = [jnp.exp(v - vals[0]) for v in vals]
    den = ex[0]
    for e in ex[1:]:
        den = den + e
    multi_hot = jnp.zeros((tm, LANES), F32)
    for sel in sels:
        multi_hot = multi_hot + jnp.where(sel, 1.0, 0.0)
    r_i = lax.broadcasted_iota(I32, (tm, tm), 0)
    c_i = lax.broadcasted_iota(I32, (tm, tm), 1)
    lower = jnp.where(r_i > c_i, 1.0, 0.0).astype(BF16)
    prefix = jnp.dot(lower, multi_hot.astype(BF16), preferred_element_type=F32)
    e_out = jnp.zeros((tm, LANES), F32)
    g_out = jnp.zeros((tm, LANES), F32)
    r_out = jnp.zeros((tm, LANES), F32)
    for k in range(top_k):
        rank_k = jnp.sum(jnp.where(sels[k], prefix, 0.0), axis=1, keepdims=True)
        e_out = jnp.where(lane == k, idxs[k], e_out)
        g_out = jnp.where(lane == k, ex[k] / den, g_out)
        r_out = jnp.where(lane == k, rank_k, r_out)
    eidx_ref[rows, :] = e_out.astype(I32)
    gate_ref[rows, :] = g_out
    rank_ref[rows, :] = r_out.astype(I32)
    cnt_ref[s] = jnp.broadcast_to(jnp.sum(multi_hot, axis=0, keepdims=True), cnt_ref.shape[1:])


def _outproj(attn, dn, gates, x, gt, sh2, sc2, wts, bufs, *, per_token_mod, tiles_per_seq, t_total, tile_off, alpha):
    t, d = x.shape
    sub = OUTPROJ_SUBTILES
    tm = TOKEN_TILE * sub
    nt = t // tm
    assert t % tm == 0 and tile_off % sub == 0 and (per_token_mod or tiles_per_seq % sub == 0)
    wpa, wpd, wo, lnw, lnb, wrh, wrl, br = wts
    if per_token_mod:
        mod_spec = pl.BlockSpec((tm, d), lambda i: (i, 0))
    else:
        mod_spec = pl.BlockSpec((None, 1, d), lambda i: (i // (tiles_per_seq // sub), 0, 0))

    def row(width):
        return pl.BlockSpec((tm, width), lambda i: (i, 0))

    def full(a):
        return pl.BlockSpec(a.shape, lambda i: (0,) * a.ndim)

    aliased = bufs is not None
    in_specs = [row(attn.shape[1]), row(dn.shape[1]), row(gates.shape[1]), row(d), mod_spec, mod_spec, mod_spec,
                full(wpa), full(wpd), full(wo), full(lnw), full(lnb), full(wrh), full(wrl), full(br)]
    args = [attn, dn, gates, x, gt, sh2, sc2, wpa, wpd, wo, lnw, lnb, wrh, wrl, br]
    io_alias = {}
    if aliased:
        for k, bfr in enumerate(bufs):
            in_specs.append(pl.BlockSpec(memory_space=pl.ANY))
            io_alias[len(args)] = k
            args.append(bfr)
    step_off = tile_off // sub
    out_row = lambda width: pl.BlockSpec((tm, width), lambda i: (i + step_off, 0))
    out_shape = [jax.ShapeDtypeStruct((t_total, d), F32), jax.ShapeDtypeStruct((t_total, LANES), I32),
                 jax.ShapeDtypeStruct((t_total, LANES), F32), jax.ShapeDtypeStruct((t_total, LANES), I32),
                 jax.ShapeDtypeStruct((t_total // TOKEN_TILE, SUBLANES, LANES), F32)]
    assert len(out_shape) == N_ROUTE_BUFS
    out_specs = [out_row(d), out_row(LANES), out_row(LANES), out_row(LANES),
                 pl.BlockSpec((sub, SUBLANES, LANES), lambda i: (i + step_off, 0, 0))]
    body = functools.partial(_outproj_body, alpha=alpha, top_k=TOP_K, aliased=aliased)
    return pl.pallas_call(
        body,
        grid=(nt,),
        in_specs=in_specs,
        out_specs=out_specs,
        out_shape=out_shape,
        input_output_aliases=io_alias,
        compiler_params=_cparams("arbitrary"),
        name="outproj",
    )(*args)


def _select_mod(i, n_ptiles, seq_ref, tok_ref):
    return jnp.where(i < n_ptiles, seq_ref[...], tok_ref[...])


def _for_run(rows, local_off, global_off, fn):
    @pl.when(rows > 0)
    def _():
        fn(pl.multiple_of(local_off, RUN_ALIGN), pl.multiple_of(global_off, RUN_ALIGN),
           pl.multiple_of(rows, RUN_ALIGN))


def _for_each_run(i, n_experts, toff_ref, cnt8_ref, gbase_ref, fn):
    for e in range(n_experts):
        idx = i * n_experts + e
        _for_run(cnt8_ref[idx], toff_ref[idx], gbase_ref[idx], fn)


def _scatter_matrix(pos_ref, n_local, values):
    pos = pos_ref[...]
    col = lax.broadcasted_iota(I32, (pos.shape[0], n_local), 1)
    out = jnp.zeros((pos.shape[0], n_local), F32)
    for k in range(pos.shape[1]):
        out = jnp.where(col == pos[:, k:k + 1], 1.0 if values is None else values[:, k:k + 1], out)
    return out


def _dispatch_body(toff_ref, cnt8_ref, gbase_ref, tstart_ref, trows_ref,
                   x1_ref, shs_ref, scs_ref, sht_ref, sct_ref, pos_ref,
                   xs_ref, lbuf, zbuf, sem, *, n_tiles, n_ptiles, n_experts):
    i = pl.program_id(0)
    sc = _select_mod(i, n_ptiles, scs_ref, sct_ref)
    sh = _select_mod(i, n_ptiles, shs_ref, sht_ref)
    h2 = (x1_ref[...] * (1.0 + sc) + sh).astype(BF16)
    n_local = lbuf.shape[1]
    slot = i % 2

    def runs(tile, sl, act):
        def fn(lo, go, rows):
            act(pltpu.make_async_copy(lbuf.at[sl, pl.ds(lo, rows), :], xs_ref.at[pl.ds(go, rows), :], sem.at[sl]))
        _for_each_run(tile, n_experts, toff_ref, cnt8_ref, gbase_ref, fn)

    start = lambda cp: cp.start()
    wait = lambda cp: cp.wait()

    @pl.when(i >= 2)
    def _():
        runs(i - 2, slot, wait)

    onehot = _scatter_matrix(pos_ref, n_local, None)
    srt = lax.dot_general(onehot.astype(BF16), h2, (((0,), (0,)), ((), ())), preferred_element_type=F32)
    lbuf[slot] = _pack_halves(srt)
    runs(i, slot, start)

    @pl.when(i == n_tiles - 1)
    def _():
        runs(i - 1, 1 - slot, wait)
        runs(i, slot, wait)
        zbuf[...] = jnp.zeros(zbuf.shape, zbuf.dtype)

        def zero_fill(e, act):
            _for_run(trows_ref[e], 0, tstart_ref[e], lambda lo, go, rows: act(pltpu.make_async_copy(
                zbuf.at[pl.ds(0, rows), :], xs_ref.at[pl.ds(go, rows), :], sem.at[0])))

        for e in range(n_experts):
            zero_fill(e, start)
        for e in range(n_experts):
            zero_fill(e, wait)


def _dispatch(tables, x1, sh_seq, sc_seq, sh_tok, sc_tok, pos, *, n_rows, n_local, n_ptiles, tiles_per_seq,
              n_experts):
    t, d = x1.shape
    tm = TOKEN_TILE
    nt = t // tm
    n_seq = sh_seq.shape[0]
    assert nt >= 2 and d % 2 == 0
    body = functools.partial(_dispatch_body, n_tiles=nt, n_ptiles=n_ptiles, n_experts=n_experts)
    seq_spec = pl.BlockSpec((None, 1, d), lambda i, *_: (jnp.minimum(i // tiles_per_seq, n_seq - 1), 0, 0))
    tok_spec = pl.BlockSpec((tm, d), lambda i, *_: (jnp.maximum(i - n_ptiles, 0), 0))
    return pl.pallas_call(
        body,
        grid_spec=pltpu.PrefetchScalarGridSpec(
            num_scalar_prefetch=len(tables),
            grid=(nt,),
            in_specs=[pl.BlockSpec((tm, d), lambda i, *_: (i, 0)),
                      seq_spec, seq_spec, tok_spec, tok_spec,
                      pl.BlockSpec((tm, pos.shape[1]), lambda i, *_: (i, 0))],
            out_specs=pl.BlockSpec(memory_space=pl.ANY),
            scratch_shapes=[pltpu.VMEM((2, n_local, d // 2), U32), pltpu.VMEM((EXPERT_ROWS, d // 2), U32),
                            pltpu.SemaphoreType.DMA((2,))],
        ),
        out_shape=jax.ShapeDtypeStruct((n_rows, d // 2), U32),
        compiler_params=_cparams("arbitrary"),
        name="dispatch",
    )(*tables, x1, sh_seq, sc_seq, sh_tok, sc_tok, pos)


def _expert_body(be_ref, nu_ref, first_ref, xs_ref, wgu_ref, bgu_ref, wd_ref, bd_ref, y_ref, wgu_bf, wd_bf, *, de):
    i = pl.program_id(0)
    used = i < nu_ref[0]

    @pl.when(used & (first_ref[i] == 1))
    def _():
        wgu_bf[...] = wgu_ref[...].astype(BF16)
        wd_bf[...] = wd_ref[...].astype(BF16)

    @pl.when(used)
    def _():
        x_lo, x_hi = _unpack_halves(xs_ref[...])
        half = x_lo.shape[1]
        gu = (jnp.dot(x_lo, wgu_bf[:half, :], preferred_element_type=F32)
              + jnp.dot(x_hi, wgu_bf[half:, :], preferred_element_type=F32) + bgu_ref[...])
        glu = jnp.minimum(gu[:, :de], SWIGLU_LIMIT)
        lin = jnp.clip(gu[:, de:], -SWIGLU_LIMIT, SWIGLU_LIMIT)
        act = glu * jax.nn.sigmoid(SWIGLU_ALPHA * glu) * (lin + 1.0)
        y = jnp.dot(act.astype(BF16), wd_bf[...], preferred_element_type=F32) + bd_ref[...]
        y_ref[...] = _pack_halves(y.astype(BF16).astype(F32))


def _experts(block_e, n_used, xs, w_gu, b_gu, w_down, b_down):
    p, dh = xs.shape
    bm = EXPERT_ROWS
    n_e, d, de2 = w_gu.shape
    de = de2 // 2
    nblk = p // bm

    def blk(i, be, nu, *_):
        return jnp.minimum(i, nu[0] - 1)

    first = jnp.concatenate([jnp.ones((1,), I32), (block_e[1:] != block_e[:-1]).astype(I32)])
    body = functools.partial(_expert_body, de=de)
    return pl.pallas_call(
        body,
        grid_spec=pltpu.PrefetchScalarGridSpec(
            num_scalar_prefetch=3,
            grid=(nblk,),
            in_specs=[pl.BlockSpec((bm, dh), lambda i, be, nu, ft: (blk(i, be, nu), 0)),
                      pl.BlockSpec((None, d, de2), lambda i, be, nu, ft: (be[blk(i, be, nu)], 0, 0)),
                      pl.BlockSpec((None, 1, de2), lambda i, be, nu, ft: (be[blk(i, be, nu)], 0, 0)),
                      pl.BlockSpec((None, de, d), lambda i, be, nu, ft: (be[blk(i, be, nu)], 0, 0)),
                      pl.BlockSpec((None, 1, d), lambda i, be, nu, ft: (be[blk(i, be, nu)], 0, 0))],
            out_specs=pl.BlockSpec((bm, dh), lambda i, be, nu, ft: (blk(i, be, nu), 0)),
            scratch_shapes=[pltpu.VMEM((d, de2), BF16), pltpu.VMEM((de, d), BF16)],
        ),
        out_shape=jax.ShapeDtypeStruct((p, dh), U32),
        compiler_params=_cparams("arbitrary"),
        name="experts",
    )(block_e, n_used, first, xs, w_gu, b_gu.reshape(n_e, 1, de2), w_down, b_down.reshape(n_e, 1, d))


def _combine_body(toff_ref, cnt8_ref, gbase_ref, x1_ref, gts_ref, gtt_ref, gate_ref, pos_ref,
                  lnw_ref, lnb_ref, yb_ref, yp_ref, ys_ref, ybuf, sem, *, top_k, n_ptiles, n_experts, alpha):
    i = pl.program_id(0)
    n_tiles = pl.num_programs(0)
    tm = x1_ref.shape[0]
    n_local = ybuf.shape[1]
    slot = i % 2

    def fetch(tile, sl, act):
        def fn(lo, go, rows):
            act(pltpu.make_async_copy(yb_ref.at[pl.ds(go, rows), :], ybuf.at[sl, pl.ds(lo, rows), :], sem.at[sl]))
        _for_each_run(tile, n_experts, toff_ref, cnt8_ref, gbase_ref, fn)

    def start_fetch(tile, sl):
        ybuf[sl, tm * top_k:, :] = jnp.zeros((n_local - tm * top_k, ybuf.shape[2]), ybuf.dtype)
        fetch(tile, sl, lambda cp: cp.start())

    @pl.when(i == 0)
    def _():
        start_fetch(i, slot)

    @pl.when(i + 1 < n_tiles)
    def _():
        start_fetch(i + 1, 1 - slot)

    weights = _scatter_matrix(pos_ref, n_local, gate_ref[...]).astype(BF16)
    fetch(i, slot, lambda cp: cp.wait())
    y_lo, y_hi = _unpack_halves(ybuf[slot])
    ff = jnp.concatenate([jnp.dot(weights, y_lo, preferred_element_type=F32),
                          jnp.dot(weights, y_hi, preferred_element_type=F32)], axis=1)
    gt = _select_mod(i, n_ptiles, gts_ref, gtt_ref)
    y = _layer_norm(alpha * x1_ref[...] + gt * ff, lnw_ref[...], lnb_ref[...])

    @pl.when(i < n_ptiles)
    def _():
        yp_ref[...] = y

    @pl.when(i >= n_ptiles)
    def _():
        ys_ref[...] = y


def _combine(tables, x1, gt_seq, gt_tok, gate, pos, lnw, lnb, yb, *, n_local, n_ptiles, tiles_per_seq, n_experts,
             alpha):
    t, d = x1.shape
    tm = TOKEN_TILE
    nt = t // tm
    n_seq = gt_seq.shape[0]
    body = functools.partial(_combine_body, top_k=TOP_K, n_ptiles=n_ptiles, n_experts=n_experts, alpha=alpha)
    lane_spec = pl.BlockSpec((tm, LANES), lambda i, *_: (i, 0))
    return pl.pallas_call(
        body,
        grid_spec=pltpu.PrefetchScalarGridSpec(
            num_scalar_prefetch=len(tables),
            grid=(nt,),
            in_specs=[pl.BlockSpec((tm, d), lambda i, *_: (i, 0)),
                      pl.BlockSpec((None, 1, d), lambda i, *_: (jnp.minimum(i // tiles_per_seq, n_seq - 1), 0, 0)),
                      pl.BlockSpec((tm, d), lambda i, *_: (jnp.maximum(i - n_ptiles, 0), 0)),
                      lane_spec,
                      pl.BlockSpec((tm, pos.shape[1]), lambda i, *_: (i, 0)),
                      pl.BlockSpec((1, d), lambda i, *_: (0, 0)),
                      pl.BlockSpec((1, d), lambda i, *_: (0, 0)),
                      pl.BlockSpec(memory_space=pl.ANY)],
            out_specs=[pl.BlockSpec((tm, d), lambda i, *_: (jnp.minimum(i, n_ptiles - 1), 0)),
                       pl.BlockSpec((tm, d), lambda i, *_: (jnp.maximum(i - n_ptiles, 0), 0))],
            scratch_shapes=[pltpu.VMEM((2, n_local, d // 2), U32), pltpu.SemaphoreType.DMA((2,))],
        ),
        out_shape=[jax.ShapeDtypeStruct((n_ptiles * tm, d), F32),
                   jax.ShapeDtypeStruct(((nt - n_ptiles) * tm, d), F32)],
        compiler_params=_cparams("arbitrary"),
        name="combine",
    )(*tables, x1, gt_seq, gt_tok, gate, pos, lnw, lnb, yb)


def _rotary_tables(pos, hd, rot_dim):
    half = rot_dim // 2
    inv_freq = jnp.power(jnp.float32(ROPE_THETA), -jnp.arange(half, dtype=F32) * (2.0 / rot_dim))
    ang = pos.astype(F32)[:, None] * inv_freq[None, :]
    cos, sin = jnp.cos(ang), jnp.sin(ang)
    n = pos.shape[0]
    ones = jnp.ones((n, hd - rot_dim), F32)
    zeros = jnp.zeros((n, hd - rot_dim), F32)
    zh = jnp.zeros((n, half), F32)
    ct = jnp.concatenate([cos, cos, ones], axis=1)
    s1 = jnp.concatenate([-sin, zh, zeros], axis=1)
    s2 = jnp.concatenate([zh, sin, zeros], axis=1)
    reps = LANES // hd
    return tuple(jnp.tile(a, (1, reps)) for a in (ct, s1, s2))


def kernel(x_prompt, x_sample, state_win_k, state_win_v, state_conv, state_ssm, c_prompt, c_sample, w_ada, b_ada, w_in, attn_sinks, w_conv, dn_a_log, dn_dt_bias, dn_norm_w, w_proj_attn, w_proj_dn, w_out, ln1_w, ln1_b, w_router, b_router, w_gu, b_gu, w_down, b_down, ln2_w, ln2_b):
    n_p, seq, d = x_prompt.shape
    n_s, l_s, _ = x_sample.shape
    depth = w_ada.shape[0]
    window, n_kv, hd = state_win_k.shape[2:]
    n_q = attn_sinks.shape[1]
    heads, dk, dv = state_ssm.shape[2:]
    conv_w, conv_dim = w_conv.shape[1:]
    n_e = w_router.shape[2]
    qd, kd = n_q * hd, n_kv * hd
    vdim = heads * dv
    rot_dim = hd // 4
    alpha = float((2 * depth) ** 0.25)
    tm = TOKEN_TILE
    t_p, t_s = n_p * seq, n_s * l_s
    t_all = t_p + t_s
    tps = seq // tm
    n_ptiles = t_p // tm
    l_pad = SUBLANES
    assert seq % tm == 0 and t_s % tm == 0 and tm % l_s == 0 and l_s <= l_pad and l_s >= conv_w - 1
    assert 2 * heads <= SUBLANES and n_e <= LANES and hd * 2 == LANES and rot_dim == 2 * SUBLANES

    sizes = [qd, kd, kd, conv_dim, vdim, heads, heads, d, d]
    offs = [int(o) for o in np.concatenate([[0], np.cumsum(sizes)])]
    cuts = (qd + 2 * kd, qd + 2 * kd + conv_dim, qd + 2 * kd + conv_dim + vdim, qd + 2 * kd + conv_dim + vdim + 2 * d)
    cuts = cuts + (cuts[-1] + LANES,)

    tabs_p = _rotary_tables(jnp.arange(seq, dtype=I32), hd, rot_dim)
    tabs_s = tuple(jnp.tile(a, (tm // l_s, 1))
                   for a in _rotary_tables(PAST_LEN + jnp.arange(l_s, dtype=I32), hd, rot_dim))

    x_p = x_prompt.reshape(t_p, d)
    x_s = x_sample.reshape(t_s, d)
    c_all = jnp.concatenate([c_prompt, c_sample], axis=0)
    outs = {k: [] for k in ("pwk", "pwv", "pcv", "pss", "swk", "swv", "scv", "sss")}

    for l in range(depth):
        w_in_l = jnp.concatenate([w_in[l][:, :offs[5]], w_in[l][:, offs[7]:], w_in[l][:, offs[5]:offs[7]],
                                  jnp.zeros((d, LANES - 2 * heads), F32)], axis=1).astype(BF16)
        mod = _ada(c_all, w_ada[l], b_ada[l])
        mod_p = mod[:n_p].reshape(n_p, 6, 1, d)
        mod_s = jnp.repeat(mod[n_p:].reshape(n_s, 6, d), l_s, axis=0)
        sh1p, sc1p, gt1p, sh2p, sc2p, gt2p = [mod_p[:, k] for k in range(6)]
        sh1s, sc1s, gt1s, sh2s, sc2s, gt2s = [mod_s[:, k] for k in range(6)]

        qkva_p, dn_p, z_p, g_p, ba_p, tail_p, kvw_p = _inproj(
            x_p, sh1p, sc1p, tabs_p, w_in_l, cuts, per_token_mod=False, tiles_per_seq=tps, act_dtype=BF16,
            window=window, kv_cols=2 * kd)
        attn_p = _attn_prompt(qkva_p, attn_sinks[l], n_p, seq, n_q, n_kv, hd, window)
        chunk = min(DN_CHUNK, seq)
        nc = seq // chunk
        bat_p = ba_p[:, :SUBLANES].reshape(n_p, nc, chunk, SUBLANES).transpose(0, 1, 3, 2)
        hp = jnp.zeros((SUBLANES, LANES), F32).at[0, :heads].set(dn_a_log[l]).at[1, :heads].set(dn_dt_bias[l])
        nw = dn_norm_w[l].reshape(1, dv)
        o_p, ssm_p = _deltanet(dn_p.reshape(n_p, seq, conv_dim), z_p.reshape(n_p, seq, vdim),
                               ba_p.reshape(n_p, seq, LANES), bat_p,
                               jnp.zeros((n_p, SUBLANES, conv_dim), F32), jnp.zeros((n_p, heads, dk, dv), F32),
                               w_conv[l], hp, nw, chunk=chunk, l_real=chunk, nb=4)
        outs["pwk"].append(kvw_p[:, :, :kd].reshape(n_p, window, n_kv, hd))
        outs["pwv"].append(kvw_p[:, :, kd:].reshape(n_p, window, n_kv, hd))
        outs["pcv"].append(tail_p.reshape(n_p, tps, SUBLANES, conv_dim)[:, -1, SUBLANES - (conv_w - 1):])
        outs["pss"].append(ssm_p)

        qkva_s, dn_s, z_s, g_s, ba_s = _inproj(
            x_s, sh1s, sc1s, tabs_s, w_in_l, cuts, per_token_mod=True, tiles_per_seq=1, act_dtype=F32,
            window=window, kv_cols=2 * kd)
        pad_l = lambda a: jnp.pad(a.reshape(n_s, l_s, a.shape[-1]), ((0, 0), (0, l_pad - l_s), (0, 0)))
        attn_s, wk_s, wv_s = _attn_sample(pad_l(qkva_s), state_win_k[l].reshape(n_s, window, kd),
                                          state_win_v[l].reshape(n_s, window, kd), attn_sinks[l], n_q, n_kv, hd, l_s)
        attn_s = attn_s[:, :l_s].reshape(t_s, qd)
        ba_s3 = pad_l(ba_s)
        bat_s = ba_s3[:, :, :SUBLANES].transpose(0, 2, 1).reshape(n_s, 1, SUBLANES, l_pad)
        cs0 = jnp.pad(state_conv[l], ((0, 0), (SUBLANES - (conv_w - 1), 0), (0, 0)))
        o_s, ssm_s = _deltanet(pad_l(dn_s), pad_l(z_s), ba_s3, bat_s, cs0, state_ssm[l], w_conv[l], hp, nw,
                               chunk=l_pad, l_real=l_s, nb=8)
        o_s = o_s[:, :l_s].reshape(t_s, vdim)
        outs["swk"].append(wk_s.reshape(n_s, window, n_kv, hd))
        outs["swv"].append(wv_s.reshape(n_s, window, n_kv, hd))
        outs["scv"].append(jnp.concatenate([state_conv[l], dn_s.reshape(n_s, l_s, conv_dim)], axis=1)[:, -(conv_w - 1):])
        outs["sss"].append(ssm_s)

        wr = jnp.pad(w_router[l], ((0, 0), (0, LANES - n_e)))
        br = jnp.pad(b_router[l], (0, LANES - n_e), constant_values=NEG_BIG).reshape(1, LANES)
        wr_hi = wr.astype(BF16)
        wr_lo = (wr - wr_hi.astype(F32)).astype(BF16)
        wts = (w_proj_attn[l].astype(BF16), w_proj_dn[l].astype(BF16), w_out[l].astype(BF16),
               ln1_w[l].reshape(1, d), ln1_b[l].reshape(1, d), wr_hi, wr_lo, br)
        res_p = _outproj(attn_p, o_p.reshape(t_p, vdim), g_p, x_p, gt1p, sh2p, sc2p, wts, None,
                         per_token_mod=False, tiles_per_seq=tps, t_total=t_all, tile_off=0, alpha=alpha)
        x1, e_idx, gate, rank, cnt = _outproj(attn_s, o_s, g_s, x_s, gt1s, sh2s, sc2s, wts, res_p,
                                              per_token_mod=True, tiles_per_seq=1, t_total=t_all, tile_off=n_ptiles,
                                              alpha=alpha)

        bm = EXPERT_ROWS
        nt_all = t_all // tm
        cnt8 = (cnt[:, 0, :n_e].astype(I32) + RUN_ALIGN - 1) // RUN_ALIGN * RUN_ALIGN
        tot = jnp.sum(cnt8, axis=0)
        padded = (tot + bm - 1) // bm * bm
        pad_end = jnp.cumsum(padded)
        pad_start = pad_end - padded
        gbase = pad_start[None, :] + jnp.cumsum(cnt8, axis=0) - cnt8
        toff = jnp.cumsum(cnt8, axis=1) - cnt8
        n_local = -(-(tm * TOP_K + n_e * (RUN_ALIGN - 1)) // LANES) * LANES
        n_rows = -(-(t_all * TOP_K + nt_all * n_e * (RUN_ALIGN - 1) + n_e * (bm - 1)) // bm) * bm
        nblk = n_rows // bm
        n_used = jnp.maximum(pad_end[-1:] // bm, 1).astype(I32)
        block_e = jnp.minimum(jnp.sum(pad_end[None, :] <= (jnp.arange(nblk, dtype=I32) * bm)[:, None], axis=1),
                              n_e - 1).astype(I32)
        flat = lambda a: a.astype(I32).reshape(nt_all * n_e)
        run_tables = (flat(toff), flat(cnt8), flat(gbase))
        tail_tables = ((pad_start + tot).astype(I32), (padded - tot).astype(I32))
        toff_tok = jnp.broadcast_to(toff[:, None, None, :], (nt_all, tm, 1, n_e)).reshape(t_all, 1, n_e)
        chosen = e_idx[:, :TOP_K, None] == jnp.arange(n_e, dtype=I32)[None, None, :]
        pos = (jnp.sum(jnp.where(chosen, toff_tok, 0), axis=-1) + rank[:, :TOP_K]).astype(I32)

        xs = _dispatch(run_tables + tail_tables, x1, sh2p, sc2p, sh2s, sc2s, pos, n_rows=n_rows, n_local=n_local,
                       n_ptiles=n_ptiles, tiles_per_seq=tps, n_experts=n_e)
        yb = _experts(block_e, n_used, xs, w_gu[l], b_gu[l], w_down[l], b_down[l])
        x_p, x_s = _combine(run_tables, x1, gt2p, gt2s, gate, pos, ln2_w[l].reshape(1, d), ln2_b[l].reshape(1, d),
                            yb, n_local=n_local, n_ptiles=n_ptiles, tiles_per_seq=tps, n_experts=n_e, alpha=alpha)

    st = lambda k: jnp.stack(outs[k])
    return (x_p.reshape(n_p, seq, d), x_s.reshape(n_s, l_s, d), st("pwk"), st("pwv"), st("pcv"), st("pss"),
            st("swk"), st("swv"), st("scv"), st("sss"))
```

```python
import functools

import numpy as np
import jax
import jax.numpy as jnp
from jax import lax
from jax.experimental import pallas as pl
from jax.experimental.pallas import tpu as pltpu

F32 = jnp.float32
BF16 = jnp.bfloat16
I32 = jnp.int32
U32 = jnp.uint32

PAST_LEN = 16384
ROPE_THETA = 500000.0
TOP_K = 4
SWIGLU_LIMIT = 7.0
SWIGLU_ALPHA = 1.702
DN_CHUNK = 64
LN_EPS = 1e-5
RMS_EPS = 1e-6
L2_EPS = 1e-6

LANES = 128
SUBLANES = 8
VMEM_LIMIT_BYTES = 56 * 1024 * 1024

TOKEN_TILE = 256
INPROJ_TILE = 512
EXPERT_ROWS = 512
RUN_ALIGN = SUBLANES
OUTPROJ_SUBTILES = 2
N_ROUTE_BUFS = 5
NEG_BIG = -1e30


def _cparams(*sem):
    return pltpu.CompilerParams(dimension_semantics=sem, vmem_limit_bytes=VMEM_LIMIT_BYTES)


def _silu(x):
    return x * jax.nn.sigmoid(x)


def _bdot(a, b):
    return jnp.dot(a.astype(BF16), b.astype(BF16), preferred_element_type=F32)


def _pack_halves(x):
    n = x.shape[1] // 2
    lo = lax.bitcast_convert_type(x[:, :n], U32)
    hi = lax.bitcast_convert_type(x[:, n:], U32)
    return (hi & jnp.uint32(0xFFFF0000)) | (lo >> 16)


def _unpack_halves(w):
    lo = lax.bitcast_convert_type(w << 16, F32).astype(BF16)
    hi = lax.bitcast_convert_type(w & jnp.uint32(0xFFFF0000), F32).astype(BF16)
    return lo, hi


def _ada_body(c_ref, w_ref, b_ref, o_ref):
    o_ref[...] = _bdot(_silu(c_ref[...]), w_ref[...]) + b_ref[...]


def _ada(c_all, w_ada, b_ada):
    n, d = c_all.shape
    dout = w_ada.shape[1]
    tn = d
    return pl.pallas_call(
        _ada_body,
        grid=(dout // tn,),
        in_specs=[pl.BlockSpec((n, d), lambda j: (0, 0)),
                  pl.BlockSpec((d, tn), lambda j: (0, j)),
                  pl.BlockSpec((1, tn), lambda j: (0, j))],
        out_specs=pl.BlockSpec((n, tn), lambda j: (0, j)),
        out_shape=jax.ShapeDtypeStruct((n, dout), F32),
        compiler_params=_cparams("arbitrary"),
        name="ada",
    )(c_all, w_ada, b_ada.reshape(1, dout))


def _inproj_body(x_ref, sh_ref, sc_ref, ct_ref, s1_ref, s2_ref, w_ref,
                 a_ref, dn_ref, z_ref, g_ref, ba_ref, *win_refs, cuts, n_rot_chunks, window):
    h = (x_ref[...] * (1.0 + sc_ref[...]) + sh_ref[...]).astype(BF16)

    def mm(lo, hi):
        return jnp.dot(h, w_ref[:, lo:hi], preferred_element_type=F32)

    c_a, c_dn, c_z, c_g, c_ba = cuts
    qkv = mm(0, c_a)
    ct, s1, s2 = ct_ref[...], s1_ref[...], s2_ref[...]
    cols = []
    for c in range(n_rot_chunks):
        xc = qkv[:, c * LANES:(c + 1) * LANES]
        cols.append(xc * ct + pltpu.roll(xc, LANES - SUBLANES, 1) * s1 + pltpu.roll(xc, SUBLANES, 1) * s2)
    cols.append(qkv[:, n_rot_chunks * LANES:])
    rot = jnp.concatenate(cols, axis=1)
    a_ref[...] = rot.astype(a_ref.dtype)
    dn = mm(c_a, c_dn)
    dn_ref[...] = dn.astype(dn_ref.dtype)
    z_ref[...] = mm(c_dn, c_z).astype(z_ref.dtype)
    g_ref[...] = mm(c_z, c_g).astype(g_ref.dtype)
    ba_ref[...] = mm(c_g, c_ba)
    if win_refs:
        tail_ref, kvw_ref = win_refs
        tm = dn.shape[0]
        tail_ref[...] = dn[tm - SUBLANES:, :]
        kvw_ref[...] = rot[tm - window:, n_rot_chunks * LANES - LANES:]


def _inproj(x, sh, sc, tabs, w_perm, cuts, *, per_token_mod, tiles_per_seq, act_dtype, window, kv_cols):
    t, d = x.shape
    tm = INPROJ_TILE
    nt = t // tm
    assert t % tm == 0 and tm >= window and (per_token_mod or nt % tiles_per_seq == 0)
    c_a, c_dn, c_z, c_g, c_ba = cuts
    n_rot_chunks = (c_a - kv_cols // 2) // LANES
    if per_token_mod:
        mod_spec = pl.BlockSpec((tm, d), lambda i: (i, 0))
        tab_spec = pl.BlockSpec((tm, LANES), lambda i: (0, 0))
    else:
        mod_spec = pl.BlockSpec((None, 1, d), lambda i: (i // tiles_per_seq, 0, 0))
        tab_spec = pl.BlockSpec((tm, LANES), lambda i: (i % tiles_per_seq, 0))
    out_shape = [jax.ShapeDtypeStruct((t, c_a), act_dtype),
                 jax.ShapeDtypeStruct((t, c_dn - c_a), act_dtype),
                 jax.ShapeDtypeStruct((t, c_z - c_dn), act_dtype),
                 jax.ShapeDtypeStruct((t, c_g - c_z), act_dtype),
                 jax.ShapeDtypeStruct((t, c_ba - c_g), F32)]
    out_specs = [pl.BlockSpec((tm, s.shape[1]), lambda i: (i, 0)) for s in out_shape]
    with_win = not per_token_mod
    if with_win:
        n_seq = nt // tiles_per_seq
        out_shape += [jax.ShapeDtypeStruct((nt, SUBLANES, c_dn - c_a), F32),
                      jax.ShapeDtypeStruct((n_seq, window, kv_cols), F32)]
        out_specs += [pl.BlockSpec((None, SUBLANES, c_dn - c_a), lambda i: (i, 0, 0)),
                      pl.BlockSpec((None, window, kv_cols), lambda i: (i // tiles_per_seq, 0, 0))]
    body = functools.partial(_inproj_body, cuts=cuts, n_rot_chunks=n_rot_chunks, window=window)
    return pl.pallas_call(
        body,
        grid=(nt,),
        in_specs=[pl.BlockSpec((tm, d), lambda i: (i, 0)), mod_spec, mod_spec,
                  tab_spec, tab_spec, tab_spec,
                  pl.BlockSpec((d, c_ba), lambda i: (0, 0))],
        out_specs=out_specs,
        out_shape=out_shape,
        compiler_params=_cparams("arbitrary"),
        name="inproj",
    )(x, sh, sc, *tabs, w_perm)


def _softmax_sink_pv(scores, valid, sinks, values):
    ms = [jnp.where(valid, s, -jnp.inf) for s in scores]
    m = [jnp.maximum(jnp.max(x, axis=-1, keepdims=True), sk) for x, sk in zip(ms, sinks)]
    p = [jnp.exp(x - mi) for x, mi in zip(ms, m)]
    den = [jnp.sum(pi, axis=-1, keepdims=True) + jnp.exp(sk - mi) for pi, sk, mi in zip(p, sinks, m)]
    return [jnp.dot((pi / di).astype(BF16), v, preferred_element_type=F32) for pi, di, v in zip(p, den, values)]


def _attn_prompt_body(sink_ref, q_ref, kvp_ref, kvc_ref, o_ref, *, n_q, n_kv, hd, window):
    j = pl.program_id(1)
    group = n_q // n_kv
    scale = hd ** -0.5
    assert np.log2(scale) == int(np.log2(scale))
    r = lax.broadcasted_iota(I32, (window, 2 * window), 0)
    c = lax.broadcasted_iota(I32, (window, 2 * window), 1)
    rel = window + r - c
    band = (rel >= 0) & (rel < window)
    kv_cur = kvc_ref[...]
    kv_first = jnp.concatenate([kvp_ref[...], kv_cur[:window]], axis=0)
    for qb, (kv, valid) in enumerate(((kv_first, band & ((c >= window) | (j > 0))), (kv_cur, band))):
        q = q_ref[qb * window:(qb + 1) * window, :] * scale
        scores = [lax.dot_general(q[:, h * hd:(h + 1) * hd], kv[:, (h // group) * hd:(h // group + 1) * hd],
                                  (((1,), (1,)), ((), ())), preferred_element_type=F32) for h in range(n_q)]
        values = [kv[:, (n_kv + h // group) * hd:(n_kv + h // group + 1) * hd] for h in range(n_q)]
        outs = _softmax_sink_pv(scores, valid, [sink_ref[h] for h in range(n_q)], values)
        o_ref[qb * window:(qb + 1) * window, :] = jnp.concatenate(outs, axis=1).astype(o_ref.dtype)


def _attn_prompt(qkva, sinks, n_seq, seq, n_q, n_kv, hd, window):
    qd, kvd = n_q * hd, 2 * n_kv * hd
    x3 = qkva.reshape(n_seq, seq, qd + kvd)
    nb = seq // window
    assert nb % 2 == 0 and qd % kvd == 0
    kv_blk = qd // kvd
    body = functools.partial(_attn_prompt_body, n_q=n_q, n_kv=n_kv, hd=hd, window=window)
    out = pl.pallas_call(
        body,
        grid=(n_seq, nb // 2),
        in_specs=[pl.BlockSpec(memory_space=pltpu.SMEM),
                  pl.BlockSpec((None, 2 * window, qd), lambda n, j: (n, j, 0)),
                  pl.BlockSpec((None, window, kvd), lambda n, j: (n, jnp.maximum(2 * j - 1, 0), kv_blk)),
                  pl.BlockSpec((None, 2 * window, kvd), lambda n, j: (n, j, kv_blk))],
        out_specs=pl.BlockSpec((None, 2 * window, qd), lambda n, j: (n, j, 0)),
        out_shape=jax.ShapeDtypeStruct((n_seq, seq, qd), BF16),
        compiler_params=_cparams("arbitrary", "arbitrary"),
        name="attn_prompt",
    )(sinks, x3, x3, x3)
    return out.reshape(n_seq * seq, qd)


def _attn_sample_body(sink_ref, q_ref, wk_ref, wv_ref, o_ref, wko_ref, wvo_ref,
                      *, bs, n_q, n_kv, hd, window, l_new, l_pad):
    group = n_q // n_kv
    qd = n_q * hd
    kd = n_kv * hd
    rows = group * l_pad
    r = lax.broadcasted_iota(I32, (rows, window + l_pad), 0) % l_pad
    c = lax.broadcasted_iota(I32, (rows, window + l_pad), 1)
    rel = window + r - c
    valid = (rel >= 0) & (rel < window) & (c < window + l_new)
    sinks = [jnp.concatenate([jnp.full((l_pad, 1), sink_ref[kvh * group + g], F32) for g in range(group)], axis=0)
             for kvh in range(n_kv)]
    qs, ks, vs = [], [], []
    for b in range(bs):
        x = q_ref[b]
        k_new = x[:, qd:qd + kd]
        v_new = x[:, qd + kd:]
        wko_ref[b, 0:window - l_new, :] = wk_ref[b, l_new:window, :]
        wko_ref[b, window - l_new:window, :] = k_new[0:l_new, :]
        wvo_ref[b, 0:window - l_new, :] = wv_ref[b, l_new:window, :]
        wvo_ref[b, window - l_new:window, :] = v_new[0:l_new, :]
        k_all = jnp.concatenate([wk_ref[b], k_new], axis=0).astype(BF16)
        v_all = jnp.concatenate([wv_ref[b], v_new], axis=0).astype(BF16)
        for kvh in range(n_kv):
            qs.append(jnp.concatenate([x[:, (kvh * group + g) * hd:(kvh * group + g + 1) * hd]
                                       for g in range(group)], axis=0).astype(BF16))
            ks.append(k_all[:, kvh * hd:(kvh + 1) * hd])
            vs.append(v_all[:, kvh * hd:(kvh + 1) * hd])
    n = len(qs)
    s = [lax.dot_general(qs[i], ks[i], (((1,), (1,)), ((), ())), preferred_element_type=F32) * (hd ** -0.5)
         for i in range(n)]
    o = _softmax_sink_pv(s, valid, [sinks[i % n_kv] for i in range(n)], vs)
    for b in range(bs):
        outs = [o[b * n_kv + kvh][g * l_pad:(g + 1) * l_pad, :] for kvh in range(n_kv) for g in range(group)]
        o_ref[b] = jnp.concatenate(outs, axis=1).astype(o_ref.dtype)


def _attn_sample(qkva_pad, win_k, win_v, sinks, n_q, n_kv, hd, l_new):
    n, l_pad, width = qkva_pad.shape
    window, kd = win_k.shape[1], win_k.shape[2]
    qd = n_q * hd
    bs = 8
    body = functools.partial(_attn_sample_body, bs=bs, n_q=n_q, n_kv=n_kv, hd=hd, window=window,
                             l_new=l_new, l_pad=l_pad)
    return pl.pallas_call(
        body,
        grid=(n // bs,),
        in_specs=[pl.BlockSpec(memory_space=pltpu.SMEM),
                  pl.BlockSpec((bs, l_pad, width), lambda i: (i, 0, 0)),
                  pl.BlockSpec((bs, window, kd), lambda i: (i, 0, 0)),
                  pl.BlockSpec((bs, window, kd), lambda i: (i, 0, 0))],
        out_specs=[pl.BlockSpec((bs, l_pad, qd), lambda i: (i, 0, 0)),
                   pl.BlockSpec((bs, window, kd), lambda i: (i, 0, 0)),
                   pl.BlockSpec((bs, window, kd), lambda i: (i, 0, 0))],
        out_shape=[jax.ShapeDtypeStruct((n, l_pad, qd), BF16),
                   jax.ShapeDtypeStruct((n, window, kd), F32),
                   jax.ShapeDtypeStruct((n, window, kd), F32)],
        compiler_params=_cparams("arbitrary"),
        name="attn_sample",
    )(sinks, qkva_pad, win_k, win_v)


def _split_bf16(x):
    hi = x.astype(BF16)
    return hi, (x - hi.astype(F32)).astype(BF16)


def _tdot(a, b):
    ah, al = _split_bf16(a)
    bh, bl = _split_bf16(b)
    m = a.shape[0]
    t = jnp.dot(jnp.concatenate([ah, al], axis=0), bh, preferred_element_type=F32)
    return t[:m] + t[m:] + jnp.dot(ah, bl, preferred_element_type=F32)


def _dn_body(qkv_ref, z_ref, ba_ref, bat_ref, cs0_ref, s0_ref, wc_ref, hp_ref, nw_ref,
             o_ref, s_ref, xbuf, *, nb, chunk, heads, dk, dv, l_real, conv_w):
    c_idx = pl.program_id(1)
    hc = SUBLANES

    @pl.when(c_idx == 0)
    def _():
        xbuf[:, 0:hc, :] = cs0_ref[...]
        s_ref[...] = s0_ref[...]

    qk_dim = heads * dk
    row = lax.broadcasted_iota(I32, (chunk, chunk), 0)
    col = lax.broadcasted_iota(I32, (chunk, chunk), 1)
    incl = row >= col
    strict = row > col
    eye = (row == col).astype(F32)
    valid_c = row[:, 0:1] < l_real
    valid_r = col[0:1, :] < l_real
    n_levels = max(1, int(np.ceil(np.log2(chunk))))
    wc = wc_ref[...]
    hp = hp_ref[...]
    neg_exp_alog = -jnp.exp(hp[0:1, :])
    dt_bias = hp[1:2, :]
    nw = nw_ref[...]
    chains = [(b, h) for b in range(nb) for h in range(heads)]
    n = len(chains)

    ys = []
    for b in range(nb):
        xbuf[b, hc:hc + chunk, :] = qkv_ref[b].astype(F32)
        y = xbuf[b, hc:hc + chunk, :] * wc[conv_w - 1:conv_w, :]
        for j in range(conv_w - 1):
            off = hc - (conv_w - 1) + j
            y = y + xbuf[b, off:off + chunk, :] * wc[j:j + 1, :]
        ys.append(_silu(y))
        xbuf[b, 0:hc, :] = xbuf[b, chunk:chunk + hc, :]

    qn, kn, kb, vb, decay, e_gc, e_rest, e_last = [], [], [], [], [], [], [], []
    for b, h in chains:
        y = ys[b]
        qh = y[:, h * dk:(h + 1) * dk]
        kh = y[:, qk_dim + h * dk:qk_dim + (h + 1) * dk]
        vh = y[:, 2 * qk_dim + h * dv:2 * qk_dim + (h + 1) * dv]
        ba = ba_ref[b]
        bat = bat_ref[b]
        ne = neg_exp_alog[:, h:h + 1]
        db = dt_bias[:, h:h + 1]
        beta = jnp.where(valid_c, jax.nn.sigmoid(ba[:, h:h + 1]), 0.0)
        g_col = jnp.where(valid_c, ne * jax.nn.softplus(ba[:, heads + h:heads + h + 1] + db), 0.0)
        g_row = jnp.where(valid_r, ne * jax.nn.softplus(bat[heads + h:heads + h + 1, :] + db), 0.0)
        gc_col = jnp.sum(jnp.where(incl, g_row, 0.0), axis=1, keepdims=True)
        gc_row = jnp.sum(jnp.where(row <= col, g_col, 0.0), axis=0, keepdims=True)
        g_last = gc_col[chunk - 1:chunk, :]
        q_ = qh * lax.rsqrt(jnp.sum(qh * qh, -1, keepdims=True) + L2_EPS) * (dk ** -0.5)
        k_ = kh * lax.rsqrt(jnp.sum(kh * kh, -1, keepdims=True) + L2_EPS)
        qn.append(q_)
        kn.append(k_)
        kb.append(k_ * beta)
        vb.append(vh * beta)
        decay.append(jnp.where(incl, jnp.exp(gc_col - gc_row), 0.0))
        e_gc.append(jnp.exp(gc_col))
        e_rest.append(jnp.exp(g_last - gc_col))
        e_last.append(jnp.exp(g_last))

    sc = [lax.dot_general(jnp.concatenate([qn[i], kb[i]], axis=0).astype(BF16), kn[i].astype(BF16),
                          (((1,), (1,)), ((), ())), preferred_element_type=F32) for i in range(n)]
    qk = [sc[i][:chunk] * decay[i] for i in range(n)]
    p = [jnp.where(strict, -(sc[i][chunk:] * decay[i]), 0.0) for i in range(n)]
    t_inv = [eye + p[i] for i in range(n)]
    if n_levels > 1:
        p = [_tdot(p[i], p[i]) for i in range(n)]
    for lvl in range(1, n_levels):
        if lvl < n_levels - 1:
            yp = [_tdot(jnp.concatenate([t_inv[i], p[i]], axis=0), p[i]) for i in range(n)]
            t_inv = [t_inv[i] + yp[i][:chunk] for i in range(n)]
            p = [yp[i][chunk:] for i in range(n)]
        else:
            t_inv = [t_inv[i] + _tdot(t_inv[i], p[i]) for i in range(n)]
    sol = [_tdot(t_inv[i], jnp.concatenate([vb[i], kb[i] * e_gc[i]], axis=1)) for i in range(n)]
    s_old = [s_ref[b, h] for b, h in chains]
    wq = [_bdot(jnp.concatenate([sol[i][:, dv:], qn[i] * e_gc[i]], axis=0), s_old[i]) for i in range(n)]
    v_new = [sol[i][:, :dv] - wq[i][:chunk] for i in range(n)]
    o = [wq[i][chunk:] + _bdot(qk[i], v_new[i]) for i in range(n)]
    for i, (b, h) in enumerate(chains):
        s_ref[b, h] = s_old[i] * e_last[i] + lax.dot_general(
            (kn[i] * e_rest[i]).astype(BF16), v_new[i].astype(BF16), (((0,), (0,)), ((), ())),
            preferred_element_type=F32)
    for b in range(nb):
        zt = z_ref[b].astype(F32)
        outs = []
        for h in range(heads):
            oi = o[b * heads + h]
            on = oi * lax.rsqrt(jnp.mean(oi * oi, -1, keepdims=True) + RMS_EPS) * nw
            outs.append(on * _silu(zt[:, h * dv:(h + 1) * dv]))
        o_ref[b] = jnp.concatenate(outs, axis=1).astype(o_ref.dtype)


def _deltanet(qkv, z, ba, bat, cs0, s0, w_conv, hp, norm_w, *, chunk, l_real, nb):
    n, l, conv_dim = qkv.shape
    heads, dk, dv = s0.shape[1:]
    nc = l // chunk
    assert n % nb == 0 and l % chunk == 0
    conv_w = w_conv.shape[0]
    hist = pltpu.VMEM((nb, SUBLANES + chunk, conv_dim), F32)
    body = functools.partial(_dn_body, nb=nb, chunk=chunk, heads=heads, dk=dk, dv=dv, l_real=l_real, conv_w=conv_w)
    return pl.pallas_call(
        body,
        grid=(n // nb, nc),
        in_specs=[pl.BlockSpec((nb, chunk, conv_dim), lambda i, c: (i, c, 0)),
                  pl.BlockSpec((nb, chunk, heads * dv), lambda i, c: (i, c, 0)),
                  pl.BlockSpec((nb, chunk, LANES), lambda i, c: (i, c, 0)),
                  pl.BlockSpec((nb, None, SUBLANES, chunk), lambda i, c: (i, c, 0, 0)),
                  pl.BlockSpec((nb, SUBLANES, conv_dim), lambda i, c: (i, 0, 0)),
                  pl.BlockSpec((nb, heads, dk, dv), lambda i, c: (i, 0, 0, 0)),
                  pl.BlockSpec((conv_w, conv_dim), lambda i, c: (0, 0)),
                  pl.BlockSpec((SUBLANES, LANES), lambda i, c: (0, 0)),
                  pl.BlockSpec((1, dv), lambda i, c: (0, 0))],
        out_specs=[pl.BlockSpec((nb, chunk, heads * dv), lambda i, c: (i, c, 0)),
                   pl.BlockSpec((nb, heads, dk, dv), lambda i, c: (i, 0, 0, 0))],
        out_shape=[jax.ShapeDtypeStruct((n, l, heads * dv), BF16),
                   jax.ShapeDtypeStruct((n, heads, dk, dv), F32)],
        scratch_shapes=[hist],
        compiler_params=_cparams("arbitrary", "arbitrary"),
        name="deltanet",
    )(qkv, z, ba, bat, cs0, s0, w_conv, hp, norm_w)


def _layer_norm(r, w, b):
    mu = jnp.mean(r, -1, keepdims=True)
    var = jnp.mean(jnp.square(r - mu), -1, keepdims=True)
    return (r - mu) * lax.rsqrt(var + LN_EPS) * w + b


def _outproj_body(attn_ref, dn_ref, g_ref, x_ref, gt_ref, sh2_ref, sc2_ref, wpa_ref, wpd_ref, wo_ref,
                  lnw_ref, lnb_ref, wrh_ref, wrl_ref, br_ref, *refs, alpha, top_k, aliased):
    if aliased:
        refs = refs[N_ROUTE_BUFS:]
    x1_ref, eidx_ref, gate_ref, rank_ref, cnt_ref = refs
    for s in range(cnt_ref.shape[0]):
        _outproj_tile(s, attn_ref, dn_ref, g_ref, x_ref, gt_ref, sh2_ref, sc2_ref, wpa_ref, wpd_ref, wo_ref, lnw_ref,
                      lnb_ref, wrh_ref, wrl_ref, br_ref, x1_ref, eidx_ref, gate_ref, rank_ref, cnt_ref,
                      alpha=alpha, top_k=top_k)


def _outproj_tile(s, attn_ref, dn_ref, g_ref, x_ref, gt_ref, sh2_ref, sc2_ref, wpa_ref, wpd_ref, wo_ref, lnw_ref,
                  lnb_ref, wrh_ref, wrl_ref, br_ref, x1_ref, eidx_ref, gate_ref, rank_ref, cnt_ref, *, alpha, top_k):
    d = x_ref.shape[1]
    tm = TOKEN_TILE
    rows = pl.ds(s * tm, tm)

    def mod(ref):
        return ref[...] if ref.shape[0] == 1 else ref[rows, :]

    g = g_ref[rows, :].astype(F32)
    pa = jnp.dot(attn_ref[rows, :], wpa_ref[...], preferred_element_type=F32)
    pd = jnp.dot(dn_ref[rows, :], wpd_ref[...], preferred_element_type=F32)
    merged = jax.nn.sigmoid(g[:, :d]) * pa + jax.nn.sigmoid(g[:, d:]) * pd
    mix = jnp.dot(merged.astype(BF16), wo_ref[...], preferred_element_type=F32)
    x1 = _layer_norm(alpha * x_ref[rows, :] + mod(gt_ref) * mix, lnw_ref[...], lnb_ref[...])
    x1_ref[rows, :] = x1
    h2 = x1 * (1.0 + mod(sc2_ref)) + mod(sh2_ref)
    h_hi, h_lo = _split_bf16(h2)
    lg = jnp.dot(jnp.concatenate([h_hi, h_lo], axis=0), wrh_ref[...], preferred_element_type=F32)
    logits = lg[:tm] + lg[tm:] + jnp.dot(h_hi, wrl_ref[...], preferred_element_type=F32) + br_ref[...]
    lane = lax.broadcasted_iota(I32, (tm, LANES), 1)
    lane_f = lane.astype(F32)
    vals, idxs, sels = [], [], []
    l = logits
    for _ in range(top_k):
        m = jnp.max(l, axis=1, keepdims=True)
        idx = jnp.min(jnp.where(l == m, lane_f, float(LANES)), axis=1, keepdims=True)
        sel = lane_f == idx
        vals.append(m)
        idxs.append(idx)
        sels.append(sel)
        l = jnp.where(sel, -jnp.inf, l)
    ex = [jnp.exp(v - vals[0]) for v in vals]
    den = ex[0]
    for e in ex[1:]:
        den = den + e
    multi_hot = jnp.zeros((tm, LANES), F32)
    for sel in sels:
        multi_hot = multi_hot + jnp.where(sel, 1.0, 0.0)
    r_i = lax.broadcasted_iota(I32, (tm, tm), 0)
    c_i = lax.broadcasted_iota(I32, (tm, tm), 1)
    lower = jnp.where(r_i > c_i, 1.0, 0.0).astype(BF16)
    prefix = jnp.dot(lower, multi_hot.astype(BF16), preferred_element_type=F32)
    e_out = jnp.zeros((tm, LANES), F32)
    g_out = jnp.zeros((tm, LANES), F32)
    r_out = jnp.zeros((tm, LANES), F32)
    for k in range(top_k):
        rank_k = jnp.sum(jnp.where(sels[k], prefix, 0.0), axis=1, keepdims=True)
        e_out = jnp.where(lane == k, idxs[k], e_out)
        g_out = jnp.where(lane == k, ex[k] / den, g_out)
        r_out = jnp.where(lane == k, rank_k, r_out)
    eidx_ref[rows, :] = e_out.astype(I32)
    gate_ref[rows, :] = g_out
    rank_ref[rows, :] = r_out.astype(I32)
    cnt_ref[s] = jnp.broadcast_to(jnp.sum(multi_hot, axis=0, keepdims=True), cnt_ref.shape[1:])


def _outproj(attn, dn, gates, x, gt, sh2, sc2, wts, bufs, *, per_token_mod, tiles_per_seq, t_total, tile_off, alpha):
    t, d = x.shape
    sub = OUTPROJ_SUBTILES
    tm = TOKEN_TILE * sub
    nt = t // tm
    assert t % tm == 0 and tile_off % sub == 0 and (per_token_mod or tiles_per_seq % sub == 0)
    wpa, wpd, wo, lnw, lnb, wrh, wrl, br = wts
    if per_token_mod:
        mod_spec = pl.BlockSpec((tm, d), lambda i: (i, 0))
    else:
        mod_spec = pl.BlockSpec((None, 1, d), lambda i: (i // (tiles_per_seq // sub), 0, 0))

    def row(width):
        return pl.BlockSpec((tm, width), lambda i: (i, 0))

    def full(a):
        return pl.BlockSpec(a.shape, lambda i: (0,) * a.ndim)

    aliased = bufs is not None
    in_specs = [row(attn.shape[1]), row(dn.shape[1]), row(gates.shape[1]), row(d), mod_spec, mod_spec, mod_spec,
                full(wpa), full(wpd), full(wo), full(lnw), full(lnb), full(wrh), full(wrl), full(br)]
    args = [attn, dn, gates, x, gt, sh2, sc2, wpa, wpd, wo, lnw, lnb, wrh, wrl, br]
    io_alias = {}
    if aliased:
        for k, bfr in enumerate(bufs):
            in_specs.append(pl.BlockSpec(memory_space=pl.ANY))
            io_alias[len(args)] = k
            args.append(bfr)
    step_off = tile_off // sub
    out_row = lambda width: pl.BlockSpec((tm, width), lambda i: (i + step_off, 0))
    out_shape = [jax.ShapeDtypeStruct((t_total, d), F32), jax.ShapeDtypeStruct((t_total, LANES), I32),
                 jax.ShapeDtypeStruct((t_total, LANES), F32), jax.ShapeDtypeStruct((t_total, LANES), I32),
                 jax.ShapeDtypeStruct((t_total // TOKEN_TILE, SUBLANES, LANES), F32)]
    assert len(out_shape) == N_ROUTE_BUFS
    out_specs = [out_row(d), out_row(LANES), out_row(LANES), out_row(LANES),
                 pl.BlockSpec((sub, SUBLANES, LANES), lambda i: (i + step_off, 0, 0))]
    body = functools.partial(_outproj_body, alpha=alpha, top_k=TOP_K, aliased=aliased)
    return pl.pallas_call(
        body,
        grid=(nt,),
        in_specs=in_specs,
        out_specs=out_specs,
        out_shape=out_shape,
        input_output_aliases=io_alias,
        compiler_params=_cparams("arbitrary"),
        name="outproj",
    )(*args)


def _select_mod(i, n_ptiles, seq_ref, tok_ref):
    return jnp.where(i < n_ptiles, seq_ref[...], tok_ref[...])


def _for_run(rows, local_off, global_off, fn):
    @pl.when(rows > 0)
    def _():
        fn(pl.multiple_of(local_off, RUN_ALIGN), pl.multiple_of(global_off, RUN_ALIGN),
           pl.multiple_of(rows, RUN_ALIGN))


def _for_each_run(i, n_experts, toff_ref, cnt8_ref, gbase_ref, fn):
    for e in range(n_experts):
        idx = i * n_experts + e
        _for_run(cnt8_ref[idx], toff_ref[idx], gbase_ref[idx], fn)


def _scatter_matrix(pos_ref, n_local, values):
    pos = pos_ref[...]
    col = lax.broadcasted_iota(I32, (pos.shape[0], n_local), 1)
    out = jnp.zeros((pos.shape[0], n_local), F32)
    for k in range(pos.shape[1]):
        out = jnp.where(col == pos[:, k:k + 1], 1.0 if values is None else values[:, k:k + 1], out)
    return out


def _dispatch_body(toff_ref, cnt8_ref, gbase_ref, tstart_ref, trows_ref,
                   x1_ref, shs_ref, scs_ref, sht_ref, sct_ref, pos_ref,
                   xs_ref, lbuf, zbuf, sem, *, n_tiles, n_ptiles, n_experts):
    i = pl.program_id(0)
    sc = _select_mod(i, n_ptiles, scs_ref, sct_ref)
    sh = _select_mod(i, n_ptiles, shs_ref, sht_ref)
    h2 = (x1_ref[...] * (1.0 + sc) + sh).astype(BF16)
    n_local = lbuf.shape[1]
    slot = i % 2

    def runs(tile, sl, act):
        def fn(lo, go, rows):
            act(pltpu.make_async_copy(lbuf.at[sl, pl.ds(lo, rows), :], xs_ref.at[pl.ds(go, rows), :], sem.at[sl]))
        _for_each_run(tile, n_experts, toff_ref, cnt8_ref, gbase_ref, fn)

    start = lambda cp: cp.start()
    wait = lambda cp: cp.wait()

    @pl.when(i >= 2)
    def _():
        runs(i - 2, slot, wait)

    onehot = _scatter_matrix(pos_ref, n_local, None)
    srt = lax.dot_general(onehot.astype(BF16), h2, (((0,), (0,)), ((), ())), preferred_element_type=F32)
    lbuf[slot] = _pack_halves(srt)
    runs(i, slot, start)

    @pl.when(i == n_tiles - 1)
    def _():
        runs(i - 1, 1 - slot, wait)
        runs(i, slot, wait)
        zbuf[...] = jnp.zeros(zbuf.shape, zbuf.dtype)

        def zero_fill(e, act):
            _for_run(trows_ref[e], 0, tstart_ref[e], lambda lo, go, rows: act(pltpu.make_async_copy(
                zbuf.at[pl.ds(0, rows), :], xs_ref.at[pl.ds(go, rows), :], sem.at[0])))

        for e in range(n_experts):
            zero_fill(e, start)
        for e in range(n_experts):
            zero_fill(e, wait)


def _dispatch(tables, x1, sh_seq, sc_seq, sh_tok, sc_tok, pos, *, n_rows, n_local, n_ptiles, tiles_per_seq,
              n_experts):
    t, d = x1.shape
    tm = TOKEN_TILE
    nt = t // tm
    n_seq = sh_seq.shape[0]
    assert nt >= 2 and d % 2 == 0
    body = functools.partial(_dispatch_body, n_tiles=nt, n_ptiles=n_ptiles, n_experts=n_experts)
    seq_spec = pl.BlockSpec((None, 1, d), lambda i, *_: (jnp.minimum(i // tiles_per_seq, n_seq - 1), 0, 0))
    tok_spec = pl.BlockSpec((tm, d), lambda i, *_: (jnp.maximum(i - n_ptiles, 0), 0))
    return pl.pallas_call(
        body,
        grid_spec=pltpu.PrefetchScalarGridSpec(
            num_scalar_prefetch=len(tables),
            grid=(nt,),
            in_specs=[pl.BlockSpec((tm, d), lambda i, *_: (i, 0)),
                      seq_spec, seq_spec, tok_spec, tok_spec,
                      pl.BlockSpec((tm, pos.shape[1]), lambda i, *_: (i, 0))],
            out_specs=pl.BlockSpec(memory_space=pl.ANY),
            scratch_shapes=[pltpu.VMEM((2, n_local, d // 2), U32), pltpu.VMEM((EXPERT_ROWS, d // 2), U32),
                            pltpu.SemaphoreType.DMA((2,))],
        ),
        out_shape=jax.ShapeDtypeStruct((n_rows, d // 2), U32),
        compiler_params=_cparams("arbitrary"),
        name="dispatch",
    )(*tables, x1, sh_seq, sc_seq, sh_tok, sc_tok, pos)


def _expert_body(be_ref, nu_ref, first_ref, xs_ref, wgu_ref, bgu_ref, wd_ref, bd_ref, y_ref, wgu_bf, wd_bf, *, de):
    i = pl.program_id(0)
    used = i < nu_ref[0]

    @pl.when(used & (first_ref[i] == 1))
    def _():
        wgu_bf[...] = wgu_ref[...].astype(BF16)
        wd_bf[...] = wd_ref[...].astype(BF16)

    @pl.when(used)
    def _():
        x = jnp.concatenate(_unpack_halves(xs_ref[...]), axis=1)
        gu = jnp.dot(x, wgu_bf[...], preferred_element_type=F32) + bgu_ref[...]
        glu = jnp.minimum(gu[:, :de], SWIGLU_LIMIT)
        lin = jnp.clip(gu[:, de:], -SWIGLU_LIMIT, SWIGLU_LIMIT)
        act = glu * jax.nn.sigmoid(SWIGLU_ALPHA * glu) * (lin + 1.0)
        y = jnp.dot(act.astype(BF16), wd_bf[...], preferred_element_type=F32) + bd_ref[...]
        y_ref[...] = _pack_halves(y.astype(BF16).astype(F32))


def _experts(block_e, n_used, xs, w_gu, b_gu, w_down, b_down):
    p, dh = xs.shape
    bm = EXPERT_ROWS
    n_e, d, de2 = w_gu.shape
    de = de2 // 2
    nblk = p // bm

    def blk(i, be, nu, *_):
        return jnp.minimum(i, nu[0] - 1)

    first = jnp.concatenate([jnp.ones((1,), I32), (block_e[1:] != block_e[:-1]).astype(I32)])
    body = functools.partial(_expert_body, de=de)
    return pl.pallas_call(
        body,
        grid_spec=pltpu.PrefetchScalarGridSpec(
            num_scalar_prefetch=3,
            grid=(nblk,),
            in_specs=[pl.BlockSpec((bm, dh), lambda i, be, nu, ft: (blk(i, be, nu), 0)),
                      pl.BlockSpec((None, d, de2), lambda i, be, nu, ft: (be[blk(i, be, nu)], 0, 0)),
                      pl.BlockSpec((None, 1, de2), lambda i, be, nu, ft: (be[blk(i, be, nu)], 0, 0)),
                      pl.BlockSpec((None, de, d), lambda i, be, nu, ft: (be[blk(i, be, nu)], 0, 0)),
                      pl.BlockSpec((None, 1, d), lambda i, be, nu, ft: (be[blk(i, be, nu)], 0, 0))],
            out_specs=pl.BlockSpec((bm, dh), lambda i, be, nu, ft: (blk(i, be, nu), 0)),
            scratch_shapes=[pltpu.VMEM((d, de2), BF16), pltpu.VMEM((de, d), BF16)],
        ),
        out_shape=jax.ShapeDtypeStruct((p, dh), U32),
        compiler_params=_cparams("arbitrary"),
        name="experts",
    )(block_e, n_used, first, xs, w_gu, b_gu.reshape(n_e, 1, de2), w_down, b_down.reshape(n_e, 1, d))


def _combine_body(toff_ref, cnt8_ref, gbase_ref, x1_ref, gts_ref, gtt_ref, gate_ref, pos_ref,
                  lnw_ref, lnb_ref, yb_ref, yp_ref, ys_ref, ybuf, sem, *, top_k, n_ptiles, n_experts, alpha):
    i = pl.program_id(0)
    n_tiles = pl.num_programs(0)
    tm = x1_ref.shape[0]
    n_local = ybuf.shape[1]
    slot = i % 2

    def fetch(tile, sl, act):
        def fn(lo, go, rows):
            act(pltpu.make_async_copy(yb_ref.at[pl.ds(go, rows), :], ybuf.at[sl, pl.ds(lo, rows), :], sem.at[sl]))
        _for_each_run(tile, n_experts, toff_ref, cnt8_ref, gbase_ref, fn)

    def start_fetch(tile, sl):
        ybuf[sl, tm * top_k:, :] = jnp.zeros((n_local - tm * top_k, ybuf.shape[2]), ybuf.dtype)
        fetch(tile, sl, lambda cp: cp.start())

    @pl.when(i == 0)
    def _():
        start_fetch(i, slot)

    @pl.when(i + 1 < n_tiles)
    def _():
        start_fetch(i + 1, 1 - slot)

    weights = _scatter_matrix(pos_ref, n_local, gate_ref[...]).astype(BF16)
    fetch(i, slot, lambda cp: cp.wait())
    y_lo, y_hi = _unpack_halves(ybuf[slot])
    ff = jnp.concatenate([jnp.dot(weights, y_lo, preferred_element_type=F32),
                          jnp.dot(weights, y_hi, preferred_element_type=F32)], axis=1)
    gt = _select_mod(i, n_ptiles, gts_ref, gtt_ref)
    y = _layer_norm(alpha * x1_ref[...] + gt * ff, lnw_ref[...], lnb_ref[...])

    @pl.when(i < n_ptiles)
    def _():
        yp_ref[...] = y

    @pl.when(i >= n_ptiles)
    def _():
        ys_ref[...] = y


def _combine(tables, x1, gt_seq, gt_tok, gate, pos, lnw, lnb, yb, *, n_local, n_ptiles, tiles_per_seq, n_experts,
             alpha):
    t, d = x1.shape
    tm = TOKEN_TILE
    nt = t // tm
    n_seq = gt_seq.shape[0]
    body = functools.partial(_combine_body, top_k=TOP_K, n_ptiles=n_ptiles, n_experts=n_experts, alpha=alpha)
    lane_spec = pl.BlockSpec((tm, LANES), lambda i, *_: (i, 0))
    return pl.pallas_call(
        body,
        grid_spec=pltpu.PrefetchScalarGridSpec(
            num_scalar_prefetch=len(tables),
            grid=(nt,),
            in_specs=[pl.BlockSpec((tm, d), lambda i, *_: (i, 0)),
                      pl.BlockSpec((None, 1, d), lambda i, *_: (jnp.minimum(i // tiles_per_seq, n_seq - 1), 0, 0)),
                      pl.BlockSpec((tm, d), lambda i, *_: (jnp.maximum(i - n_ptiles, 0), 0)),
                      lane_spec,
                      pl.BlockSpec((tm, pos.shape[1]), lambda i, *_: (i, 0)),
                      pl.BlockSpec((1, d), lambda i, *_: (0, 0)),
                      pl.BlockSpec((1, d), lambda i, *_: (0, 0)),
                      pl.BlockSpec(memory_space=pl.ANY)],
            out_specs=[pl.BlockSpec((tm, d), lambda i, *_: (jnp.minimum(i, n_ptiles - 1), 0)),
                       pl.BlockSpec((tm, d), lambda i, *_: (jnp.maximum(i - n_ptiles, 0), 0))],
            scratch_shapes=[pltpu.VMEM((2, n_local, d // 2), U32), pltpu.SemaphoreType.DMA((2,))],
        ),
        out_shape=[jax.ShapeDtypeStruct((n_ptiles * tm, d), F32),
                   jax.ShapeDtypeStruct(((nt - n_ptiles) * tm, d), F32)],
        compiler_params=_cparams("arbitrary"),
        name="combine",
    )(*tables, x1, gt_seq, gt_tok, gate, pos, lnw, lnb, yb)


def _rotary_tables(pos, hd, rot_dim):
    half = rot_dim // 2
    inv_freq = jnp.power(jnp.float32(ROPE_THETA), -jnp.arange(half, dtype=F32) * (2.0 / rot_dim))
    ang = pos.astype(F32)[:, None] * inv_freq[None, :]
    cos, sin = jnp.cos(ang), jnp.sin(ang)
    n = pos.shape[0]
    ones = jnp.ones((n, hd - rot_dim), F32)
    zeros = jnp.zeros((n, hd - rot_dim), F32)
    zh = jnp.zeros((n, half), F32)
    ct = jnp.concatenate([cos, cos, ones], axis=1)
    s1 = jnp.concatenate([-sin, zh, zeros], axis=1)
    s2 = jnp.concatenate([zh, sin, zeros], axis=1)
    reps = LANES // hd
    return tuple(jnp.tile(a, (1, reps)) for a in (ct, s1, s2))


def kernel(x_prompt, x_sample, state_win_k, state_win_v, state_conv, state_ssm, c_prompt, c_sample, w_ada, b_ada, w_in, attn_sinks, w_conv, dn_a_log, dn_dt_bias, dn_norm_w, w_proj_attn, w_proj_dn, w_out, ln1_w, ln1_b, w_router, b_router, w_gu, b_gu, w_down, b_down, ln2_w, ln2_b):
    n_p, seq, d = x_prompt.shape
    n_s, l_s, _ = x_sample.shape
    depth = w_ada.shape[0]
    window, n_kv, hd = state_win_k.shape[2:]
    n_q = attn_sinks.shape[1]
    heads, dk, dv = state_ssm.shape[2:]
    conv_w, conv_dim = w_conv.shape[1:]
    n_e = w_router.shape[2]
    qd, kd = n_q * hd, n_kv * hd
    vdim = heads * dv
    rot_dim = hd // 4
    alpha = float((2 * depth) ** 0.25)
    tm = TOKEN_TILE
    t_p, t_s = n_p * seq, n_s * l_s
    t_all = t_p + t_s
    tps = seq // tm
    n_ptiles = t_p // tm
    l_pad = SUBLANES
    assert seq % tm == 0 and t_s % tm == 0 and tm % l_s == 0 and l_s <= l_pad and l_s >= conv_w - 1
    assert 2 * heads <= SUBLANES and n_e <= LANES and hd * 2 == LANES and rot_dim == 2 * SUBLANES

    sizes = [qd, kd, kd, conv_dim, vdim, heads, heads, d, d]
    offs = [int(o) for o in np.concatenate([[0], np.cumsum(sizes)])]
    cuts = (qd + 2 * kd, qd + 2 * kd + conv_dim, qd + 2 * kd + conv_dim + vdim, qd + 2 * kd + conv_dim + vdim + 2 * d)
    cuts = cuts + (cuts[-1] + LANES,)

    tabs_p = _rotary_tables(jnp.arange(seq, dtype=I32), hd, rot_dim)
    tabs_s = tuple(jnp.tile(a, (INPROJ_TILE // l_s, 1))
                   for a in _rotary_tables(PAST_LEN + jnp.arange(l_s, dtype=I32), hd, rot_dim))

    x_p = x_prompt.reshape(t_p, d)
    x_s = x_sample.reshape(t_s, d)
    c_all = jnp.concatenate([c_prompt, c_sample], axis=0)
    outs = {k: [] for k in ("pwk", "pwv", "pcv", "pss", "swk", "swv", "scv", "sss")}

    for l in range(depth):
        w_in_l = jnp.concatenate([w_in[l][:, :offs[5]], w_in[l][:, offs[7]:], w_in[l][:, offs[5]:offs[7]],
                                  jnp.zeros((d, LANES - 2 * heads), F32)], axis=1).astype(BF16)
        mod = _ada(c_all, w_ada[l], b_ada[l])
        mod_p = mod[:n_p].reshape(n_p, 6, 1, d)
        mod_s = jnp.repeat(mod[n_p:].reshape(n_s, 6, d), l_s, axis=0)
        sh1p, sc1p, gt1p, sh2p, sc2p, gt2p = [mod_p[:, k] for k in range(6)]
        sh1s, sc1s, gt1s, sh2s, sc2s, gt2s = [mod_s[:, k] for k in range(6)]

        qkva_p, dn_p, z_p, g_p, ba_p, tail_p, kvw_p = _inproj(
            x_p, sh1p, sc1p, tabs_p, w_in_l, cuts, per_token_mod=False, tiles_per_seq=seq // INPROJ_TILE, act_dtype=BF16,
            window=window, kv_cols=2 * kd)
        attn_p = _attn_prompt(qkva_p, attn_sinks[l], n_p, seq, n_q, n_kv, hd, window)
        chunk = min(DN_CHUNK, seq)
        nc = seq // chunk
        bat_p = ba_p[:, :SUBLANES].reshape(n_p, nc, chunk, SUBLANES).transpose(0, 1, 3, 2)
        hp = jnp.zeros((SUBLANES, LANES), F32).at[0, :heads].set(dn_a_log[l]).at[1, :heads].set(dn_dt_bias[l])
        nw = dn_norm_w[l].reshape(1, dv)
        o_p, ssm_p = _deltanet(dn_p.reshape(n_p, seq, conv_dim), z_p.reshape(n_p, seq, vdim),
                               ba_p.reshape(n_p, seq, LANES), bat_p,
                               jnp.zeros((n_p, SUBLANES, conv_dim), F32), jnp.zeros((n_p, heads, dk, dv), F32),
                               w_conv[l], hp, nw, chunk=chunk, l_real=chunk, nb=4)
        outs["pwk"].append(kvw_p[:, :, :kd].reshape(n_p, window, n_kv, hd))
        outs["pwv"].append(kvw_p[:, :, kd:].reshape(n_p, window, n_kv, hd))
        outs["pcv"].append(tail_p.reshape(n_p, seq // INPROJ_TILE, SUBLANES, conv_dim)[:, -1, SUBLANES - (conv_w - 1):])
        outs["pss"].append(ssm_p)

        qkva_s, dn_s, z_s, g_s, ba_s = _inproj(
            x_s, sh1s, sc1s, tabs_s, w_in_l, cuts, per_token_mod=True, tiles_per_seq=1, act_dtype=F32,
            window=window, kv_cols=2 * kd)
        pad_l = lambda a: jnp.pad(a.reshape(n_s, l_s, a.shape[-1]), ((0, 0), (0, l_pad - l_s), (0, 0)))
        attn_s, wk_s, wv_s = _attn_sample(pad_l(qkva_s), state_win_k[l].reshape(n_s, window, kd),
                                          state_win_v[l].reshape(n_s, window, kd), attn_sinks[l], n_q, n_kv, hd, l_s)
        attn_s = attn_s[:, :l_s].reshape(t_s, qd)
        ba_s3 = pad_l(ba_s)
        bat_s = ba_s3[:, :, :SUBLANES].transpose(0, 2, 1).reshape(n_s, 1, SUBLANES, l_pad)
        cs0 = jnp.pad(state_conv[l], ((0, 0), (SUBLANES - (conv_w - 1), 0), (0, 0)))
        o_s, ssm_s = _deltanet(pad_l(dn_s), pad_l(z_s), ba_s3, bat_s, cs0, state_ssm[l], w_conv[l], hp, nw,
                               chunk=l_pad, l_real=l_s, nb=8)
        o_s = o_s[:, :l_s].reshape(t_s, vdim)
        outs["swk"].append(wk_s.reshape(n_s, window, n_kv, hd))
        outs["swv"].append(wv_s.reshape(n_s, window, n_kv, hd))
        outs["scv"].append(jnp.concatenate([state_conv[l], dn_s.reshape(n_s, l_s, conv_dim)], axis=1)[:, -(conv_w - 1):])
        outs["sss"].append(ssm_s)

        wr = jnp.pad(w_router[l], ((0, 0), (0, LANES - n_e)))
        br = jnp.pad(b_router[l], (0, LANES - n_e), constant_values=NEG_BIG).reshape(1, LANES)
        wr_hi = wr.astype(BF16)
        wr_lo = (wr - wr_hi.astype(F32)).astype(BF16)
        wts = (w_proj_attn[l].astype(BF16), w_proj_dn[l].astype(BF16), w_out[l].astype(BF16),
               ln1_w[l].reshape(1, d), ln1_b[l].reshape(1, d), wr_hi, wr_lo, br)
        res_p = _outproj(attn_p, o_p.reshape(t_p, vdim), g_p, x_p, gt1p, sh2p, sc2p, wts, None,
                         per_token_mod=False, tiles_per_seq=tps, t_total=t_all, tile_off=0, alpha=alpha)
        x1, e_idx, gate, rank, cnt = _outproj(attn_s, o_s, g_s, x_s, gt1s, sh2s, sc2s, wts, res_p,
                                              per_token_mod=True, tiles_per_seq=1, t_total=t_all, tile_off=n_ptiles,
                                              alpha=alpha)

        bm = EXPERT_ROWS
        nt_all = t_all // tm
        cnt8 = (cnt[:, 0, :n_e].astype(I32) + RUN_ALIGN - 1) // RUN_ALIGN * RUN_ALIGN
        tot = jnp.sum(cnt8, axis=0)
        padded = (tot + bm - 1) // bm * bm
        pad_end = jnp.cumsum(padded)
        pad_start = pad_end - padded
        gbase = pad_start[None, :] + jnp.cumsum(cnt8, axis=0) - cnt8
        toff = jnp.cumsum(cnt8, axis=1) - cnt8
        n_local = -(-(tm * TOP_K + n_e * (RUN_ALIGN - 1)) // LANES) * LANES
        n_rows = -(-(t_all * TOP_K + nt_all * n_e * (RUN_ALIGN - 1) + n_e * (bm - 1)) // bm) * bm
        nblk = n_rows // bm
        n_used = jnp.maximum(pad_end[-1:] // bm, 1).astype(I32)
        block_e = jnp.minimum(jnp.sum(pad_end[None, :] <= (jnp.arange(nblk, dtype=I32) * bm)[:, None], axis=1),
                              n_e - 1).astype(I32)
        flat = lambda a: a.astype(I32).reshape(nt_all * n_e)
        run_tables = (flat(toff), flat(cnt8), flat(gbase))
        tail_tables = ((pad_start + tot).astype(I32), (padded - tot).astype(I32))
        toff_tok = jnp.broadcast_to(toff[:, None, None, :], (nt_all, tm, 1, n_e)).reshape(t_all, 1, n_e)
        chosen = e_idx[:, :TOP_K, None] == jnp.arange(n_e, dtype=I32)[None, None, :]
        pos = (jnp.sum(jnp.where(chosen, toff_tok, 0), axis=-1) + rank[:, :TOP_K]).astype(I32)

        xs = _dispatch(run_tables + tail_tables, x1, sh2p, sc2p, sh2s, sc2s, pos, n_rows=n_rows, n_local=n_local,
                       n_ptiles=n_ptiles, tiles_per_seq=tps, n_experts=n_e)
        yb = _experts(block_e, n_used, xs, w_gu[l], b_gu[l], w_down[l], b_down[l])
        x_p, x_s = _combine(run_tables, x1, gt2p, gt2s, gate, pos, ln2_w[l].reshape(1, d), ln2_b[l].reshape(1, d),
                            yb, n_local=n_local, n_ptiles=n_ptiles, tiles_per_seq=tps, n_experts=n_e, alpha=alpha)

    st = lambda k: outs[k][0][None] if depth == 1 else jnp.stack(outs[k])
    return (x_p.reshape(n_p, seq, d), x_s.reshape(n_s, l_s, d), st("pwk"), st("pwv"), st("pcv"), st("pss"),
            st("swk"), st("swv"), st("scv"), st("sss"))
```

```python
import functools

import numpy as np
import jax
import jax.numpy as jnp
from jax import lax
from jax.experimental import pallas as pl
from jax.experimental.pallas import tpu as pltpu

F32 = jnp.float32
BF16 = jnp.bfloat16
I32 = jnp.int32
U32 = jnp.uint32

PAST_LEN = 16384
ROPE_THETA = 500000.0
TOP_K = 4
SWIGLU_LIMIT = 7.0
SWIGLU_ALPHA = 1.702
DN_CHUNK = 64
LN_EPS = 1e-5
RMS_EPS = 1e-6
L2_EPS = 1e-6

LANES = 128
SUBLANES = 8
VMEM_LIMIT_BYTES = 56 * 1024 * 1024

TOKEN_TILE = 256
INPROJ_TILE = 512
EXPERT_ROWS = 512
RUN_ALIGN = SUBLANES
OUTPROJ_SUBTILES = 2
N_ROUTE_BUFS = 4
NEG_BIG = -1e30


def _cparams(*sem):
    return pltpu.CompilerParams(dimension_semantics=sem, vmem_limit_bytes=VMEM_LIMIT_BYTES)


def _silu(x):
    return x * jax.nn.sigmoid(x)


def _bdot(a, b):
    return jnp.dot(a.astype(BF16), b.astype(BF16), preferred_element_type=F32)


def _pack_halves(x):
    n = x.shape[1] // 2
    lo = lax.bitcast_convert_type(x[:, :n], U32)
    hi = lax.bitcast_convert_type(x[:, n:], U32)
    return (hi & jnp.uint32(0xFFFF0000)) | (lo >> 16)


def _unpack_halves(w):
    lo = lax.bitcast_convert_type(w << 16, F32).astype(BF16)
    hi = lax.bitcast_convert_type(w & jnp.uint32(0xFFFF0000), F32).astype(BF16)
    return lo, hi


def _ada_body(c_ref, w_ref, b_ref, o_ref):
    o_ref[...] = _bdot(_silu(c_ref[...]), w_ref[...]) + b_ref[...]


def _ada(c_all, w_ada, b_ada):
    n, d = c_all.shape
    dout = w_ada.shape[1]
    tn = d
    return pl.pallas_call(
        _ada_body,
        grid=(dout // tn,),
        in_specs=[pl.BlockSpec((n, d), lambda j: (0, 0)),
                  pl.BlockSpec((d, tn), lambda j: (0, j)),
                  pl.BlockSpec((1, tn), lambda j: (0, j))],
        out_specs=pl.BlockSpec((n, tn), lambda j: (0, j)),
        out_shape=jax.ShapeDtypeStruct((n, dout), F32),
        compiler_params=_cparams("arbitrary"),
        name="ada",
    )(c_all, w_ada, b_ada.reshape(1, dout))


def _inproj_body(x_ref, sh_ref, sc_ref, ct_ref, s1_ref, s2_ref, w_ref,
                 a_ref, dn_ref, z_ref, g_ref, ba_ref, *win_refs, cuts, n_rot_chunks, window):
    h = (x_ref[...] * (1.0 + sc_ref[...]) + sh_ref[...]).astype(BF16)

    def mm(lo, hi):
        return jnp.dot(h, w_ref[:, lo:hi], preferred_element_type=F32)

    c_a, c_dn, c_z, c_g, c_ba = cuts
    qkv = mm(0, c_a)
    ct, s1, s2 = ct_ref[...], s1_ref[...], s2_ref[...]
    cols = []
    for c in range(n_rot_chunks):
        xc = qkv[:, c * LANES:(c + 1) * LANES]
        cols.append(xc * ct + pltpu.roll(xc, LANES - SUBLANES, 1) * s1 + pltpu.roll(xc, SUBLANES, 1) * s2)
    cols.append(qkv[:, n_rot_chunks * LANES:])
    rot = jnp.concatenate(cols, axis=1)
    a_ref[...] = rot.astype(a_ref.dtype)
    dn = mm(c_a, c_dn)
    dn_ref[...] = dn.astype(dn_ref.dtype)
    z_ref[...] = mm(c_dn, c_z).astype(z_ref.dtype)
    g_ref[...] = mm(c_z, c_g).astype(g_ref.dtype)
    ba_ref[...] = mm(c_g, c_ba)
    if win_refs:
        tail_ref, kvw_ref = win_refs
        tm = dn.shape[0]
        tail_ref[...] = dn[tm - SUBLANES:, :]
        kvw_ref[...] = rot[tm - window:, n_rot_chunks * LANES - LANES:]


def _inproj(x, sh, sc, tabs, w_perm, cuts, *, per_token_mod, tiles_per_seq, act_dtype, window, kv_cols):
    t, d = x.shape
    tm = INPROJ_TILE
    nt = t // tm
    assert t % tm == 0 and tm >= window and (per_token_mod or nt % tiles_per_seq == 0)
    c_a, c_dn, c_z, c_g, c_ba = cuts
    n_rot_chunks = (c_a - kv_cols // 2) // LANES
    if per_token_mod:
        mod_spec = pl.BlockSpec((tm, d), lambda i: (i, 0))
        tab_spec = pl.BlockSpec((tm, LANES), lambda i: (0, 0))
    else:
        mod_spec = pl.BlockSpec((None, 1, d), lambda i: (i // tiles_per_seq, 0, 0))
        tab_spec = pl.BlockSpec((tm, LANES), lambda i: (i % tiles_per_seq, 0))
    out_shape = [jax.ShapeDtypeStruct((t, c_a), act_dtype),
                 jax.ShapeDtypeStruct((t, c_dn - c_a), act_dtype),
                 jax.ShapeDtypeStruct((t, c_z - c_dn), act_dtype),
                 jax.ShapeDtypeStruct((t, c_g - c_z), act_dtype),
                 jax.ShapeDtypeStruct((t, c_ba - c_g), F32)]
    out_specs = [pl.BlockSpec((tm, s.shape[1]), lambda i: (i, 0)) for s in out_shape]
    with_win = not per_token_mod
    if with_win:
        n_seq = nt // tiles_per_seq
        out_shape += [jax.ShapeDtypeStruct((nt, SUBLANES, c_dn - c_a), F32),
                      jax.ShapeDtypeStruct((n_seq, window, kv_cols), F32)]
        out_specs += [pl.BlockSpec((None, SUBLANES, c_dn - c_a), lambda i: (i, 0, 0)),
                      pl.BlockSpec((None, window, kv_cols), lambda i: (i // tiles_per_seq, 0, 0))]
    body = functools.partial(_inproj_body, cuts=cuts, n_rot_chunks=n_rot_chunks, window=window)
    return pl.pallas_call(
        body,
        grid=(nt,),
        in_specs=[pl.BlockSpec((tm, d), lambda i: (i, 0)), mod_spec, mod_spec,
                  tab_spec, tab_spec, tab_spec,
                  pl.BlockSpec((d, c_ba), lambda i: (0, 0))],
        out_specs=out_specs,
        out_shape=out_shape,
        compiler_params=_cparams("arbitrary"),
        name="inproj",
    )(x, sh, sc, *tabs, w_perm)


def _softmax_sink_pv(scores, valid, sinks, values):
    ms = [jnp.where(valid, s, -jnp.inf) for s in scores]
    m = [jnp.maximum(jnp.max(x, axis=-1, keepdims=True), sk) for x, sk in zip(ms, sinks)]
    p = [jnp.exp(x - mi) for x, mi in zip(ms, m)]
    den = [jnp.sum(pi, axis=-1, keepdims=True) + jnp.exp(sk - mi) for pi, sk, mi in zip(p, sinks, m)]
    return [jnp.dot((pi / di).astype(BF16), v, preferred_element_type=F32) for pi, di, v in zip(p, den, values)]


def _attn_prompt_body(sink_ref, q_ref, kvp_ref, kvc_ref, o_ref, *, n_q, n_kv, hd, window):
    j = pl.program_id(1)
    group = n_q // n_kv
    scale = hd ** -0.5
    assert np.log2(scale) == int(np.log2(scale))
    r = lax.broadcasted_iota(I32, (window, 2 * window), 0)
    c = lax.broadcasted_iota(I32, (window, 2 * window), 1)
    rel = window + r - c
    band = (rel >= 0) & (rel < window)
    kv_cur = kvc_ref[...]
    kv_first = jnp.concatenate([kvp_ref[...], kv_cur[:window]], axis=0)
    for qb, (kv, valid) in enumerate(((kv_first, band & ((c >= window) | (j > 0))), (kv_cur, band))):
        q = q_ref[qb * window:(qb + 1) * window, :] * scale
        scores = [lax.dot_general(q[:, h * hd:(h + 1) * hd], kv[:, (h // group) * hd:(h // group + 1) * hd],
                                  (((1,), (1,)), ((), ())), preferred_element_type=F32) for h in range(n_q)]
        values = [kv[:, (n_kv + h // group) * hd:(n_kv + h // group + 1) * hd] for h in range(n_q)]
        outs = _softmax_sink_pv(scores, valid, [sink_ref[h] for h in range(n_q)], values)
        o_ref[qb * window:(qb + 1) * window, :] = jnp.concatenate(outs, axis=1).astype(o_ref.dtype)


def _attn_prompt(qkva, sinks, n_seq, seq, n_q, n_kv, hd, window):
    qd, kvd = n_q * hd, 2 * n_kv * hd
    x3 = qkva.reshape(n_seq, seq, qd + kvd)
    nb = seq // window
    assert nb % 2 == 0 and qd % kvd == 0
    kv_blk = qd // kvd
    body = functools.partial(_attn_prompt_body, n_q=n_q, n_kv=n_kv, hd=hd, window=window)
    out = pl.pallas_call(
        body,
        grid=(n_seq, nb // 2),
        in_specs=[pl.BlockSpec(memory_space=pltpu.SMEM),
                  pl.BlockSpec((None, 2 * window, qd), lambda n, j: (n, j, 0)),
                  pl.BlockSpec((None, window, kvd), lambda n, j: (n, jnp.maximum(2 * j - 1, 0), kv_blk)),
                  pl.BlockSpec((None, 2 * window, kvd), lambda n, j: (n, j, kv_blk))],
        out_specs=pl.BlockSpec((None, 2 * window, qd), lambda n, j: (n, j, 0)),
        out_shape=jax.ShapeDtypeStruct((n_seq, seq, qd), BF16),
        compiler_params=_cparams("arbitrary", "arbitrary"),
        name="attn_prompt",
    )(sinks, x3, x3, x3)
    return out.reshape(n_seq * seq, qd)


def _attn_sample_body(sink_ref, q_ref, wk_ref, wv_ref, o_ref, wko_ref, wvo_ref,
                      *, bs, n_q, n_kv, hd, window, l_new, l_pad):
    group = n_q // n_kv
    qd = n_q * hd
    kd = n_kv * hd
    rows = group * l_pad
    r = lax.broadcasted_iota(I32, (rows, window + l_pad), 0) % l_pad
    c = lax.broadcasted_iota(I32, (rows, window + l_pad), 1)
    rel = window + r - c
    valid = (rel >= 0) & (rel < window) & (c < window + l_new)
    sinks = [jnp.concatenate([jnp.full((l_pad, 1), sink_ref[kvh * group + g], F32) for g in range(group)], axis=0)
             for kvh in range(n_kv)]
    qs, ks, vs = [], [], []
    for b in range(bs):
        x = q_ref[b]
        k_new = x[:, qd:qd + kd]
        v_new = x[:, qd + kd:]
        wko_ref[b, 0:window - l_new, :] = wk_ref[b, l_new:window, :]
        wko_ref[b, window - l_new:window, :] = k_new[0:l_new, :]
        wvo_ref[b, 0:window - l_new, :] = wv_ref[b, l_new:window, :]
        wvo_ref[b, window - l_new:window, :] = v_new[0:l_new, :]
        k_all = jnp.concatenate([wk_ref[b], k_new], axis=0).astype(BF16)
        v_all = jnp.concatenate([wv_ref[b], v_new], axis=0).astype(BF16)
        for kvh in range(n_kv):
            qs.append(jnp.concatenate([x[:, (kvh * group + g) * hd:(kvh * group + g + 1) * hd]
                                       for g in range(group)], axis=0).astype(BF16))
            ks.append(k_all[:, kvh * hd:(kvh + 1) * hd])
            vs.append(v_all[:, kvh * hd:(kvh + 1) * hd])
    n = len(qs)
    s = [lax.dot_general(qs[i], ks[i], (((1,), (1,)), ((), ())), preferred_element_type=F32) * (hd ** -0.5)
         for i in range(n)]
    o = _softmax_sink_pv(s, valid, [sinks[i % n_kv] for i in range(n)], vs)
    for b in range(bs):
        outs = [o[b * n_kv + kvh][g * l_pad:(g + 1) * l_pad, :] for kvh in range(n_kv) for g in range(group)]
        o_ref[b] = jnp.concatenate(outs, axis=1).astype(o_ref.dtype)


def _attn_sample(qkva_pad, win_k, win_v, sinks, n_q, n_kv, hd, l_new):
    n, l_pad, width = qkva_pad.shape
    window, kd = win_k.shape[1], win_k.shape[2]
    qd = n_q * hd
    bs = 8
    body = functools.partial(_attn_sample_body, bs=bs, n_q=n_q, n_kv=n_kv, hd=hd, window=window,
                             l_new=l_new, l_pad=l_pad)
    return pl.pallas_call(
        body,
        grid=(n // bs,),
        in_specs=[pl.BlockSpec(memory_space=pltpu.SMEM),
                  pl.BlockSpec((bs, l_pad, width), lambda i: (i, 0, 0)),
                  pl.BlockSpec((bs, window, kd), lambda i: (i, 0, 0)),
                  pl.BlockSpec((bs, window, kd), lambda i: (i, 0, 0))],
        out_specs=[pl.BlockSpec((bs, l_pad, qd), lambda i: (i, 0, 0)),
                   pl.BlockSpec((bs, window, kd), lambda i: (i, 0, 0)),
                   pl.BlockSpec((bs, window, kd), lambda i: (i, 0, 0))],
        out_shape=[jax.ShapeDtypeStruct((n, l_pad, qd), BF16),
                   jax.ShapeDtypeStruct((n, window, kd), F32),
                   jax.ShapeDtypeStruct((n, window, kd), F32)],
        compiler_params=_cparams("arbitrary"),
        name="attn_sample",
    )(sinks, qkva_pad, win_k, win_v)


def _split_bf16(x):
    hi = x.astype(BF16)
    return hi, (x - hi.astype(F32)).astype(BF16)


def _tdot(a, b):
    ah, al = _split_bf16(a)
    bh, bl = _split_bf16(b)
    m = a.shape[0]
    t = jnp.dot(jnp.concatenate([ah, al], axis=0), bh, preferred_element_type=F32)
    return t[:m] + t[m:] + jnp.dot(ah, bl, preferred_element_type=F32)


def _dn_body(qkv_ref, z_ref, ba_ref, bat_ref, cs0_ref, s0_ref, wc_ref, hp_ref, nw_ref,
             o_ref, s_ref, xbuf, *, nb, chunk, heads, dk, dv, l_real, conv_w):
    c_idx = pl.program_id(1)
    hc = SUBLANES

    @pl.when(c_idx == 0)
    def _():
        xbuf[:, 0:hc, :] = cs0_ref[...]
        s_ref[...] = s0_ref[...]

    qk_dim = heads * dk
    row = lax.broadcasted_iota(I32, (chunk, chunk), 0)
    col = lax.broadcasted_iota(I32, (chunk, chunk), 1)
    incl = row >= col
    strict = row > col
    eye = (row == col).astype(F32)
    valid_c = row[:, 0:1] < l_real
    valid_r = col[0:1, :] < l_real
    n_levels = max(1, int(np.ceil(np.log2(chunk))))
    wc = wc_ref[...]
    hp = hp_ref[...]
    neg_exp_alog = -jnp.exp(hp[0:1, :])
    dt_bias = hp[1:2, :]
    nw = nw_ref[...]
    chains = [(b, h) for b in range(nb) for h in range(heads)]
    n = len(chains)

    ys = []
    for b in range(nb):
        xbuf[b, hc:hc + chunk, :] = qkv_ref[b].astype(F32)
        y = xbuf[b, hc:hc + chunk, :] * wc[conv_w - 1:conv_w, :]
        for j in range(conv_w - 1):
            off = hc - (conv_w - 1) + j
            y = y + xbuf[b, off:off + chunk, :] * wc[j:j + 1, :]
        ys.append(_silu(y))
        xbuf[b, 0:hc, :] = xbuf[b, chunk:chunk + hc, :]

    qn, kn, kb, vb, decay, e_gc, e_rest, e_last = [], [], [], [], [], [], [], []
    for b, h in chains:
        y = ys[b]
        qh = y[:, h * dk:(h + 1) * dk]
        kh = y[:, qk_dim + h * dk:qk_dim + (h + 1) * dk]
        vh = y[:, 2 * qk_dim + h * dv:2 * qk_dim + (h + 1) * dv]
        ba = ba_ref[b]
        bat = bat_ref[b]
        ne = neg_exp_alog[:, h:h + 1]
        db = dt_bias[:, h:h + 1]
        beta = jnp.where(valid_c, jax.nn.sigmoid(ba[:, h:h + 1]), 0.0)
        g_col = jnp.where(valid_c, ne * jax.nn.softplus(ba[:, heads + h:heads + h + 1] + db), 0.0)
        g_row = jnp.where(valid_r, ne * jax.nn.softplus(bat[heads + h:heads + h + 1, :] + db), 0.0)
        gc_col = jnp.sum(jnp.where(incl, g_row, 0.0), axis=1, keepdims=True)
        gc_row = jnp.sum(jnp.where(row <= col, g_col, 0.0), axis=0, keepdims=True)
        g_last = gc_col[chunk - 1:chunk, :]
        q_ = qh * lax.rsqrt(jnp.sum(qh * qh, -1, keepdims=True) + L2_EPS) * (dk ** -0.5)
        k_ = kh * lax.rsqrt(jnp.sum(kh * kh, -1, keepdims=True) + L2_EPS)
        qn.append(q_)
        kn.append(k_)
        kb.append(k_ * beta)
        vb.append(vh * beta)
        decay.append(jnp.where(incl, jnp.exp(gc_col - gc_row), 0.0))
        e_gc.append(jnp.exp(gc_col))
        e_rest.append(jnp.exp(g_last - gc_col))
        e_last.append(jnp.exp(g_last))

    sc = [lax.dot_general(jnp.concatenate([qn[i], kb[i]], axis=0).astype(BF16), kn[i].astype(BF16),
                          (((1,), (1,)), ((), ())), preferred_element_type=F32) for i in range(n)]
    qk = [sc[i][:chunk] * decay[i] for i in range(n)]
    p = [jnp.where(strict, -(sc[i][chunk:] * decay[i]), 0.0) for i in range(n)]
    t_inv = [eye + p[i] for i in range(n)]
    if n_levels > 1:
        p = [_tdot(p[i], p[i]) for i in range(n)]
    for lvl in range(1, n_levels):
        if lvl < n_levels - 1:
            yp = [_tdot(jnp.concatenate([t_inv[i], p[i]], axis=0), p[i]) for i in range(n)]
            t_inv = [t_inv[i] + yp[i][:chunk] for i in range(n)]
            p = [yp[i][chunk:] for i in range(n)]
        else:
            t_inv = [t_inv[i] + _tdot(t_inv[i], p[i]) for i in range(n)]
    sol = [_tdot(t_inv[i], jnp.concatenate([vb[i], kb[i] * e_gc[i]], axis=1)) for i in range(n)]
    s_old = [s_ref[b, h] for b, h in chains]
    wq = [_bdot(jnp.concatenate([sol[i][:, dv:], qn[i] * e_gc[i]], axis=0), s_old[i]) for i in range(n)]
    v_new = [sol[i][:, :dv] - wq[i][:chunk] for i in range(n)]
    o = [wq[i][chunk:] + _bdot(qk[i], v_new[i]) for i in range(n)]
    for i, (b, h) in enumerate(chains):
        s_ref[b, h] = s_old[i] * e_last[i] + lax.dot_general(
            (kn[i] * e_rest[i]).astype(BF16), v_new[i].astype(BF16), (((0,), (0,)), ((), ())),
            preferred_element_type=F32)
    for b in range(nb):
        zt = z_ref[b].astype(F32)
        outs = []
        for h in range(heads):
            oi = o[b * heads + h]
            on = oi * lax.rsqrt(jnp.mean(oi * oi, -1, keepdims=True) + RMS_EPS) * nw
            outs.append(on * _silu(zt[:, h * dv:(h + 1) * dv]))
        o_ref[b] = jnp.concatenate(outs, axis=1).astype(o_ref.dtype)


def _deltanet(qkv, z, ba, bat, cs0, s0, w_conv, hp, norm_w, *, chunk, l_real, nb):
    n, l, conv_dim = qkv.shape
    heads, dk, dv = s0.shape[1:]
    nc = l // chunk
    assert n % nb == 0 and l % chunk == 0
    conv_w = w_conv.shape[0]
    hist = pltpu.VMEM((nb, SUBLANES + chunk, conv_dim), F32)
    body = functools.partial(_dn_body, nb=nb, chunk=chunk, heads=heads, dk=dk, dv=dv, l_real=l_real, conv_w=conv_w)
    return pl.pallas_call(
        body,
        grid=(n // nb, nc),
        in_specs=[pl.BlockSpec((nb, chunk, conv_dim), lambda i, c: (i, c, 0)),
                  pl.BlockSpec((nb, chunk, heads * dv), lambda i, c: (i, c, 0)),
                  pl.BlockSpec((nb, chunk, LANES), lambda i, c: (i, c, 0)),
                  pl.BlockSpec((nb, None, SUBLANES, chunk), lambda i, c: (i, c, 0, 0)),
                  pl.BlockSpec((nb, SUBLANES, conv_dim), lambda i, c: (i, 0, 0)),
                  pl.BlockSpec((nb, heads, dk, dv), lambda i, c: (i, 0, 0, 0)),
                  pl.BlockSpec((conv_w, conv_dim), lambda i, c: (0, 0)),
                  pl.BlockSpec((SUBLANES, LANES), lambda i, c: (0, 0)),
                  pl.BlockSpec((1, dv), lambda i, c: (0, 0))],
        out_specs=[pl.BlockSpec((nb, chunk, heads * dv), lambda i, c: (i, c, 0)),
                   pl.BlockSpec((nb, heads, dk, dv), lambda i, c: (i, 0, 0, 0))],
        out_shape=[jax.ShapeDtypeStruct((n, l, heads * dv), BF16),
                   jax.ShapeDtypeStruct((n, heads, dk, dv), F32)],
        scratch_shapes=[hist],
        compiler_params=_cparams("arbitrary", "arbitrary"),
        name="deltanet",
    )(qkv, z, ba, bat, cs0, s0, w_conv, hp, norm_w)


def _layer_norm(r, w, b):
    mu = jnp.mean(r, -1, keepdims=True)
    var = jnp.mean(jnp.square(r - mu), -1, keepdims=True)
    return (r - mu) * lax.rsqrt(var + LN_EPS) * w + b


def _outproj_body(attn_ref, dn_ref, g_ref, x_ref, gt_ref, sh2_ref, sc2_ref, wpa_ref, wpd_ref, wo_ref,
                  lnw_ref, lnb_ref, wrh_ref, wrl_ref, br_ref, *refs, alpha, top_k, aliased):
    if aliased:
        refs = refs[N_ROUTE_BUFS:]
    x1_ref, pos_ref, gate_ref, cnt_ref = refs
    for s in range(cnt_ref.shape[0]):
        _outproj_tile(s, attn_ref, dn_ref, g_ref, x_ref, gt_ref, sh2_ref, sc2_ref, wpa_ref, wpd_ref, wo_ref, lnw_ref,
                      lnb_ref, wrh_ref, wrl_ref, br_ref, x1_ref, pos_ref, gate_ref, cnt_ref,
                      alpha=alpha, top_k=top_k)


def _outproj_tile(s, attn_ref, dn_ref, g_ref, x_ref, gt_ref, sh2_ref, sc2_ref, wpa_ref, wpd_ref, wo_ref, lnw_ref,
                  lnb_ref, wrh_ref, wrl_ref, br_ref, x1_ref, pos_ref, gate_ref, cnt_ref, *, alpha, top_k):
    d = x_ref.shape[1]
    tm = TOKEN_TILE
    rows = pl.ds(s * tm, tm)

    def mod(ref):
        return ref[...] if ref.shape[0] == 1 else ref[rows, :]

    g = g_ref[rows, :].astype(F32)
    pa = jnp.dot(attn_ref[rows, :], wpa_ref[...], preferred_element_type=F32)
    pd = jnp.dot(dn_ref[rows, :], wpd_ref[...], preferred_element_type=F32)
    merged = jax.nn.sigmoid(g[:, :d]) * pa + jax.nn.sigmoid(g[:, d:]) * pd
    mix = jnp.dot(merged.astype(BF16), wo_ref[...], preferred_element_type=F32)
    x1 = _layer_norm(alpha * x_ref[rows, :] + mod(gt_ref) * mix, lnw_ref[...], lnb_ref[...])
    x1_ref[rows, :] = x1
    h2 = x1 * (1.0 + mod(sc2_ref)) + mod(sh2_ref)
    h_hi, h_lo = _split_bf16(h2)
    lg = jnp.dot(jnp.concatenate([h_hi, h_lo], axis=0), wrh_ref[...], preferred_element_type=F32)
    logits = lg[:tm] + lg[tm:] + jnp.dot(h_hi, wrl_ref[...], preferred_element_type=F32) + br_ref[...]
    lane = lax.broadcasted_iota(I32, (tm, LANES), 1)
    lane_f = lane.astype(F32)
    vals, sels = [], []
    l = logits
    for _ in range(top_k):
        m = jnp.max(l, axis=1, keepdims=True)
        idx = jnp.min(jnp.where(l == m, lane_f, float(LANES)), axis=1, keepdims=True)
        sel = lane_f == idx
        vals.append(m)
        sels.append(sel)
        l = jnp.where(sel, -jnp.inf, l)
    ex = [jnp.exp(v - vals[0]) for v in vals]
    den = ex[0]
    for e in ex[1:]:
        den = den + e
    multi_hot = jnp.zeros((tm, LANES), F32)
    for sel in sels:
        multi_hot = multi_hot + jnp.where(sel, 1.0, 0.0)
    r_i = lax.broadcasted_iota(I32, (tm, tm), 0)
    c_i = lax.broadcasted_iota(I32, (tm, tm), 1)
    lower = jnp.where(r_i > c_i, 1.0, 0.0).astype(BF16)
    prefix = jnp.dot(lower, multi_hot.astype(BF16), preferred_element_type=F32)
    counts = jnp.sum(multi_hot, axis=0, keepdims=True)
    cnt_pad = jnp.floor((counts + (RUN_ALIGN - 1)) * (1.0 / RUN_ALIGN)) * RUN_ALIGN
    e_r = lax.broadcasted_iota(I32, (LANES, LANES), 0)
    e_c = lax.broadcasted_iota(I32, (LANES, LANES), 1)
    before = jnp.where(e_r < e_c, 1.0, 0.0).astype(BF16)
    run_off = jnp.dot(jnp.broadcast_to(cnt_pad, (SUBLANES, LANES)).astype(BF16), before,
                      preferred_element_type=F32)[0:1, :]
    g_out = jnp.zeros((tm, LANES), F32)
    p_out = jnp.zeros((tm, LANES), F32)
    for k in range(top_k):
        pos_k = jnp.sum(jnp.where(sels[k], prefix + run_off, 0.0), axis=1, keepdims=True)
        g_out = jnp.where(lane == k, ex[k] / den, g_out)
        p_out = jnp.where(lane == k, pos_k, p_out)
    pos_ref[rows, :] = p_out.astype(I32)
    gate_ref[rows, :] = g_out
    cnt_ref[s] = jnp.broadcast_to(counts, cnt_ref.shape[1:])


def _outproj(attn, dn, gates, x, gt, sh2, sc2, wts, bufs, *, per_token_mod, tiles_per_seq, t_total, tile_off, alpha):
    t, d = x.shape
    sub = OUTPROJ_SUBTILES
    tm = TOKEN_TILE * sub
    nt = t // tm
    assert t % tm == 0 and tile_off % sub == 0 and (per_token_mod or tiles_per_seq % sub == 0)
    wpa, wpd, wo, lnw, lnb, wrh, wrl, br = wts
    if per_token_mod:
        mod_spec = pl.BlockSpec((tm, d), lambda i: (i, 0))
    else:
        mod_spec = pl.BlockSpec((None, 1, d), lambda i: (i // (tiles_per_seq // sub), 0, 0))

    def row(width):
        return pl.BlockSpec((tm, width), lambda i: (i, 0))

    def full(a):
        return pl.BlockSpec(a.shape, lambda i: (0,) * a.ndim)

    aliased = bufs is not None
    in_specs = [row(attn.shape[1]), row(dn.shape[1]), row(gates.shape[1]), row(d), mod_spec, mod_spec, mod_spec,
                full(wpa), full(wpd), full(wo), full(lnw), full(lnb), full(wrh), full(wrl), full(br)]
    args = [attn, dn, gates, x, gt, sh2, sc2, wpa, wpd, wo, lnw, lnb, wrh, wrl, br]
    io_alias = {}
    if aliased:
        for k, bfr in enumerate(bufs):
            in_specs.append(pl.BlockSpec(memory_space=pl.ANY))
            io_alias[len(args)] = k
            args.append(bfr)
    step_off = tile_off // sub
    out_row = lambda width: pl.BlockSpec((tm, width), lambda i: (i + step_off, 0))
    out_shape = [jax.ShapeDtypeStruct((t_total, d), F32), jax.ShapeDtypeStruct((t_total, LANES), I32),
                 jax.ShapeDtypeStruct((t_total, LANES), F32),
                 jax.ShapeDtypeStruct((t_total // TOKEN_TILE, SUBLANES, LANES), F32)]
    assert len(out_shape) == N_ROUTE_BUFS and TOKEN_TILE <= 256
    out_specs = [out_row(d), out_row(LANES), out_row(LANES),
                 pl.BlockSpec((sub, SUBLANES, LANES), lambda i: (i + step_off, 0, 0))]
    body = functools.partial(_outproj_body, alpha=alpha, top_k=TOP_K, aliased=aliased)
    return pl.pallas_call(
        body,
        grid=(nt,),
        in_specs=in_specs,
        out_specs=out_specs,
        out_shape=out_shape,
        input_output_aliases=io_alias,
        compiler_params=_cparams("arbitrary"),
        name="outproj",
    )(*args)


def _select_mod(i, n_ptiles, seq_ref, tok_ref):
    return jnp.where(i < n_ptiles, seq_ref[...], tok_ref[...])


def _for_run(rows, local_off, global_off, fn):
    @pl.when(rows > 0)
    def _():
        fn(pl.multiple_of(local_off, RUN_ALIGN), pl.multiple_of(global_off, RUN_ALIGN),
           pl.multiple_of(rows, RUN_ALIGN))


def _for_each_run(i, n_experts, toff_ref, cnt8_ref, gbase_ref, fn):
    for e in range(n_experts):
        idx = i * n_experts + e
        _for_run(cnt8_ref[idx], toff_ref[idx], gbase_ref[idx], fn)


def _scatter_matrix(pos_ref, n_local, values):
    pos = pos_ref[...]
    col = lax.broadcasted_iota(I32, (pos.shape[0], n_local), 1)
    out = jnp.zeros((pos.shape[0], n_local), F32)
    for k in range(TOP_K):
        out = jnp.where(col == pos[:, k:k + 1], 1.0 if values is None else values[:, k:k + 1], out)
    return out


def _dispatch_body(toff_ref, cnt8_ref, gbase_ref, tstart_ref, trows_ref,
                   x1_ref, shs_ref, scs_ref, sht_ref, sct_ref, pos_ref,
                   xs_ref, lbuf, zbuf, sem, *, n_tiles, n_ptiles, n_experts):
    i = pl.program_id(0)
    sc = _select_mod(i, n_ptiles, scs_ref, sct_ref)
    sh = _select_mod(i, n_ptiles, shs_ref, sht_ref)
    h2 = (x1_ref[...] * (1.0 + sc) + sh).astype(BF16)
    n_local = lbuf.shape[1]
    slot = i % 2

    def runs(tile, sl, act):
        def fn(lo, go, rows):
            act(pltpu.make_async_copy(lbuf.at[sl, pl.ds(lo, rows), :], xs_ref.at[pl.ds(go, rows), :], sem.at[sl]))
        _for_each_run(tile, n_experts, toff_ref, cnt8_ref, gbase_ref, fn)

    start = lambda cp: cp.start()
    wait = lambda cp: cp.wait()

    @pl.when(i >= 2)
    def _():
        runs(i - 2, slot, wait)

    onehot = _scatter_matrix(pos_ref, n_local, None)
    srt = lax.dot_general(onehot.astype(BF16), h2, (((0,), (0,)), ((), ())), preferred_element_type=F32)
    lbuf[slot] = _pack_halves(srt)
    runs(i, slot, start)

    @pl.when(i == n_tiles - 1)
    def _():
        runs(i - 1, 1 - slot, wait)
        runs(i, slot, wait)
        zbuf[...] = jnp.zeros(zbuf.shape, zbuf.dtype)

        def zero_fill(e, act):
            _for_run(trows_ref[e], 0, tstart_ref[e], lambda lo, go, rows: act(pltpu.make_async_copy(
                zbuf.at[pl.ds(0, rows), :], xs_ref.at[pl.ds(go, rows), :], sem.at[0])))

        for e in range(n_experts):
            zero_fill(e, start)
        for e in range(n_experts):
            zero_fill(e, wait)


def _dispatch(tables, x1, sh_seq, sc_seq, sh_tok, sc_tok, pos, *, n_rows, n_local, n_ptiles, tiles_per_seq,
              n_experts):
    t, d = x1.shape
    tm = TOKEN_TILE
    nt = t // tm
    n_seq = sh_seq.shape[0]
    assert nt >= 2 and d % 2 == 0
    body = functools.partial(_dispatch_body, n_tiles=nt, n_ptiles=n_ptiles, n_experts=n_experts)
    seq_spec = pl.BlockSpec((None, 1, d), lambda i, *_: (jnp.minimum(i // tiles_per_seq, n_seq - 1), 0, 0))
    tok_spec = pl.BlockSpec((tm, d), lambda i, *_: (jnp.maximum(i - n_ptiles, 0), 0))
    return pl.pallas_call(
        body,
        grid_spec=pltpu.PrefetchScalarGridSpec(
            num_scalar_prefetch=len(tables),
            grid=(nt,),
            in_specs=[pl.BlockSpec((tm, d), lambda i, *_: (i, 0)),
                      seq_spec, seq_spec, tok_spec, tok_spec,
                      pl.BlockSpec((tm, LANES), lambda i, *_: (i, 0))],
            out_specs=pl.BlockSpec(memory_space=pl.ANY),
            scratch_shapes=[pltpu.VMEM((2, n_local, d // 2), U32), pltpu.VMEM((EXPERT_ROWS, d // 2), U32),
                            pltpu.SemaphoreType.DMA((2,))],
        ),
        out_shape=jax.ShapeDtypeStruct((n_rows, d // 2), U32),
        compiler_params=_cparams("arbitrary"),
        name="dispatch",
    )(*tables, x1, sh_seq, sc_seq, sh_tok, sc_tok, pos)


def _expert_body(be_ref, nu_ref, first_ref, xs_ref, wgu_ref, bgu_ref, wd_ref, bd_ref, y_ref, wgu_bf, wd_bf, *, de):
    i = pl.program_id(0)
    used = i < nu_ref[0]

    @pl.when(used & (first_ref[i] == 1))
    def _():
        wgu_bf[...] = wgu_ref[...].astype(BF16)
        wd_bf[...] = wd_ref[...].astype(BF16)

    @pl.when(used)
    def _():
        x = jnp.concatenate(_unpack_halves(xs_ref[...]), axis=1)
        gu = jnp.dot(x, wgu_bf[...], preferred_element_type=F32) + bgu_ref[...]
        glu = jnp.minimum(gu[:, :de], SWIGLU_LIMIT)
        lin = jnp.clip(gu[:, de:], -SWIGLU_LIMIT, SWIGLU_LIMIT)
        act = glu * jax.nn.sigmoid(SWIGLU_ALPHA * glu) * (lin + 1.0)
        y = jnp.dot(act.astype(BF16), wd_bf[...], preferred_element_type=F32) + bd_ref[...]
        y_ref[...] = _pack_halves(y.astype(BF16).astype(F32))


def _experts(block_e, n_used, xs, w_gu, b_gu, w_down, b_down):
    p, dh = xs.shape
    bm = EXPERT_ROWS
    n_e, d, de2 = w_gu.shape
    de = de2 // 2
    nblk = p // bm

    def blk(i, be, nu, *_):
        return jnp.minimum(i, nu[0] - 1)

    first = jnp.concatenate([jnp.ones((1,), I32), (block_e[1:] != block_e[:-1]).astype(I32)])
    body = functools.partial(_expert_body, de=de)
    return pl.pallas_call(
        body,
        grid_spec=pltpu.PrefetchScalarGridSpec(
            num_scalar_prefetch=3,
            grid=(nblk,),
            in_specs=[pl.BlockSpec((bm, dh), lambda i, be, nu, ft: (blk(i, be, nu), 0)),
                      pl.BlockSpec((None, d, de2), lambda i, be, nu, ft: (be[blk(i, be, nu)], 0, 0)),
                      pl.BlockSpec((None, 1, de2), lambda i, be, nu, ft: (be[blk(i, be, nu)], 0, 0)),
                      pl.BlockSpec((None, de, d), lambda i, be, nu, ft: (be[blk(i, be, nu)], 0, 0)),
                      pl.BlockSpec((None, 1, d), lambda i, be, nu, ft: (be[blk(i, be, nu)], 0, 0))],
            out_specs=pl.BlockSpec((bm, dh), lambda i, be, nu, ft: (blk(i, be, nu), 0)),
            scratch_shapes=[pltpu.VMEM((d, de2), BF16), pltpu.VMEM((de, d), BF16)],
        ),
        out_shape=jax.ShapeDtypeStruct((p, dh), U32),
        compiler_params=_cparams("arbitrary"),
        name="experts",
    )(block_e, n_used, first, xs, w_gu, b_gu.reshape(n_e, 1, de2), w_down, b_down.reshape(n_e, 1, d))


def _combine_body(toff_ref, cnt8_ref, gbase_ref, x1_ref, gts_ref, gtt_ref, gate_ref, pos_ref,
                  lnw_ref, lnb_ref, yb_ref, yp_ref, ys_ref, ybuf, sem, *, top_k, n_ptiles, n_experts, alpha):
    i = pl.program_id(0)
    n_tiles = pl.num_programs(0)
    tm = x1_ref.shape[0]
    n_local = ybuf.shape[1]
    slot = i % 2

    def fetch(tile, sl, act):
        def fn(lo, go, rows):
            act(pltpu.make_async_copy(yb_ref.at[pl.ds(go, rows), :], ybuf.at[sl, pl.ds(lo, rows), :], sem.at[sl]))
        _for_each_run(tile, n_experts, toff_ref, cnt8_ref, gbase_ref, fn)

    def start_fetch(tile, sl):
        ybuf[sl, tm * top_k:, :] = jnp.zeros((n_local - tm * top_k, ybuf.shape[2]), ybuf.dtype)
        fetch(tile, sl, lambda cp: cp.start())

    @pl.when(i == 0)
    def _():
        start_fetch(i, slot)

    @pl.when(i + 1 < n_tiles)
    def _():
        start_fetch(i + 1, 1 - slot)

    weights = _scatter_matrix(pos_ref, n_local, gate_ref[...]).astype(BF16)
    fetch(i, slot, lambda cp: cp.wait())
    y_lo, y_hi = _unpack_halves(ybuf[slot])
    ff = jnp.concatenate([jnp.dot(weights, y_lo, preferred_element_type=F32),
                          jnp.dot(weights, y_hi, preferred_element_type=F32)], axis=1)
    gt = _select_mod(i, n_ptiles, gts_ref, gtt_ref)
    y = _layer_norm(alpha * x1_ref[...] + gt * ff, lnw_ref[...], lnb_ref[...])

    @pl.when(i < n_ptiles)
    def _():
        yp_ref[...] = y

    @pl.when(i >= n_ptiles)
    def _():
        ys_ref[...] = y


def _combine(tables, x1, gt_seq, gt_tok, gate, pos, lnw, lnb, yb, *, n_local, n_ptiles, tiles_per_seq, n_experts,
             alpha):
    t, d = x1.shape
    tm = TOKEN_TILE
    nt = t // tm
    n_seq = gt_seq.shape[0]
    body = functools.partial(_combine_body, top_k=TOP_K, n_ptiles=n_ptiles, n_experts=n_experts, alpha=alpha)
    lane_spec = pl.BlockSpec((tm, LANES), lambda i, *_: (i, 0))
    return pl.pallas_call(
        body,
        grid_spec=pltpu.PrefetchScalarGridSpec(
            num_scalar_prefetch=len(tables),
            grid=(nt,),
            in_specs=[pl.BlockSpec((tm, d), lambda i, *_: (i, 0)),
                      pl.BlockSpec((None, 1, d), lambda i, *_: (jnp.minimum(i // tiles_per_seq, n_seq - 1), 0, 0)),
                      pl.BlockSpec((tm, d), lambda i, *_: (jnp.maximum(i - n_ptiles, 0), 0)),
                      lane_spec,
                      pl.BlockSpec((tm, LANES), lambda i, *_: (i, 0)),
                      pl.BlockSpec((1, d), lambda i, *_: (0, 0)),
                      pl.BlockSpec((1, d), lambda i, *_: (0, 0)),
                      pl.BlockSpec(memory_space=pl.ANY)],
            out_specs=[pl.BlockSpec((tm, d), lambda i, *_: (jnp.minimum(i, n_ptiles - 1), 0)),
                       pl.BlockSpec((tm, d), lambda i, *_: (jnp.maximum(i - n_ptiles, 0), 0))],
            scratch_shapes=[pltpu.VMEM((2, n_local, d // 2), U32), pltpu.SemaphoreType.DMA((2,))],
        ),
        out_shape=[jax.ShapeDtypeStruct((n_ptiles * tm, d), F32),
                   jax.ShapeDtypeStruct(((nt - n_ptiles) * tm, d), F32)],
        compiler_params=_cparams("arbitrary"),
        name="combine",
    )(*tables, x1, gt_seq, gt_tok, gate, pos, lnw, lnb, yb)


def _rotary_tables(pos, hd, rot_dim):
    half = rot_dim // 2
    inv_freq = jnp.power(jnp.float32(ROPE_THETA), -jnp.arange(half, dtype=F32) * (2.0 / rot_dim))
    ang = pos.astype(F32)[:, None] * inv_freq[None, :]
    cos, sin = jnp.cos(ang), jnp.sin(ang)
    n = pos.shape[0]
    ones = jnp.ones((n, hd - rot_dim), F32)
    zeros = jnp.zeros((n, hd - rot_dim), F32)
    zh = jnp.zeros((n, half), F32)
    ct = jnp.concatenate([cos, cos, ones], axis=1)
    s1 = jnp.concatenate([-sin, zh, zeros], axis=1)
    s2 = jnp.concatenate([zh, sin, zeros], axis=1)
    reps = LANES // hd
    return tuple(jnp.tile(a, (1, reps)) for a in (ct, s1, s2))


def kernel(x_prompt, x_sample, state_win_k, state_win_v, state_conv, state_ssm, c_prompt, c_sample, w_ada, b_ada, w_in, attn_sinks, w_conv, dn_a_log, dn_dt_bias, dn_norm_w, w_proj_attn, w_proj_dn, w_out, ln1_w, ln1_b, w_router, b_router, w_gu, b_gu, w_down, b_down, ln2_w, ln2_b):
    n_p, seq, d = x_prompt.shape
    n_s, l_s, _ = x_sample.shape
    depth = w_ada.shape[0]
    window, n_kv, hd = state_win_k.shape[2:]
    n_q = attn_sinks.shape[1]
    heads, dk, dv = state_ssm.shape[2:]
    conv_w, conv_dim = w_conv.shape[1:]
    n_e = w_router.shape[2]
    qd, kd = n_q * hd, n_kv * hd
    vdim = heads * dv
    rot_dim = hd // 4
    alpha = float((2 * depth) ** 0.25)
    tm = TOKEN_TILE
    t_p, t_s = n_p * seq, n_s * l_s
    t_all = t_p + t_s
    tps = seq // tm
    n_ptiles = t_p // tm
    l_pad = SUBLANES
    assert seq % tm == 0 and t_s % tm == 0 and tm % l_s == 0 and l_s <= l_pad and l_s >= conv_w - 1
    assert 2 * heads <= SUBLANES and n_e <= LANES and hd * 2 == LANES and rot_dim == 2 * SUBLANES

    sizes = [qd, kd, kd, conv_dim, vdim, heads, heads, d, d]
    offs = [int(o) for o in np.concatenate([[0], np.cumsum(sizes)])]
    cuts = (qd + 2 * kd, qd + 2 * kd + conv_dim, qd + 2 * kd + conv_dim + vdim, qd + 2 * kd + conv_dim + vdim + 2 * d)
    cuts = cuts + (cuts[-1] + LANES,)

    tabs_p = _rotary_tables(jnp.arange(seq, dtype=I32), hd, rot_dim)
    tabs_s = tuple(jnp.tile(a, (INPROJ_TILE // l_s, 1))
                   for a in _rotary_tables(PAST_LEN + jnp.arange(l_s, dtype=I32), hd, rot_dim))

    x_p = x_prompt.reshape(t_p, d)
    x_s = x_sample.reshape(t_s, d)
    c_all = jnp.concatenate([c_prompt, c_sample], axis=0)
    outs = {k: [] for k in ("pwk", "pwv", "pcv", "pss", "swk", "swv", "scv", "sss")}

    for l in range(depth):
        w_in_l = jnp.concatenate([w_in[l][:, :offs[5]], w_in[l][:, offs[7]:], w_in[l][:, offs[5]:offs[7]],
                                  jnp.zeros((d, LANES - 2 * heads), F32)], axis=1).astype(BF16)
        mod = _ada(c_all, w_ada[l], b_ada[l])
        mod_p = mod[:n_p].reshape(n_p, 6, 1, d)
        mod_s = jnp.repeat(mod[n_p:].reshape(n_s, 6, d).transpose(1, 0, 2), l_s, axis=1)
        sh1p, sc1p, gt1p, sh2p, sc2p, gt2p = [mod_p[:, k] for k in range(6)]
        sh1s, sc1s, gt1s, sh2s, sc2s, gt2s = [mod_s[k] for k in range(6)]

        qkva_p, dn_p, z_p, g_p, ba_p, tail_p, kvw_p = _inproj(
            x_p, sh1p, sc1p, tabs_p, w_in_l, cuts, per_token_mod=False, tiles_per_seq=seq // INPROJ_TILE, act_dtype=BF16,
            window=window, kv_cols=2 * kd)
        attn_p = _attn_prompt(qkva_p, attn_sinks[l], n_p, seq, n_q, n_kv, hd, window)
        chunk = min(DN_CHUNK, seq)
        nc = seq // chunk
        bat_p = ba_p[:, :SUBLANES].reshape(n_p, nc, chunk, SUBLANES).transpose(0, 1, 3, 2)
        hp = jnp.zeros((SUBLANES, LANES), F32).at[0, :heads].set(dn_a_log[l]).at[1, :heads].set(dn_dt_bias[l])
        nw = dn_norm_w[l].reshape(1, dv)
        o_p, ssm_p = _deltanet(dn_p.reshape(n_p, seq, conv_dim), z_p.reshape(n_p, seq, vdim),
                               ba_p.reshape(n_p, seq, LANES), bat_p,
                               jnp.zeros((n_p, SUBLANES, conv_dim), F32), jnp.zeros((n_p, heads, dk, dv), F32),
                               w_conv[l], hp, nw, chunk=chunk, l_real=chunk, nb=4)
        outs["pwk"].append(kvw_p[:, :, :kd].reshape(n_p, window, n_kv, hd))
        outs["pwv"].append(kvw_p[:, :, kd:].reshape(n_p, window, n_kv, hd))
        outs["pcv"].append(tail_p.reshape(n_p, seq // INPROJ_TILE, SUBLANES, conv_dim)[:, -1, SUBLANES - (conv_w - 1):])
        outs["pss"].append(ssm_p)

        qkva_s, dn_s, z_s, g_s, ba_s = _inproj(
            x_s, sh1s, sc1s, tabs_s, w_in_l, cuts, per_token_mod=True, tiles_per_seq=1, act_dtype=F32,
            window=window, kv_cols=2 * kd)
        pad_l = lambda a: jnp.pad(a.reshape(n_s, l_s, a.shape[-1]), ((0, 0), (0, l_pad - l_s), (0, 0)))
        attn_s, wk_s, wv_s = _attn_sample(pad_l(qkva_s), state_win_k[l].reshape(n_s, window, kd),
                                          state_win_v[l].reshape(n_s, window, kd), attn_sinks[l], n_q, n_kv, hd, l_s)
        attn_s = attn_s[:, :l_s].reshape(t_s, qd)
        ba_s3 = pad_l(ba_s)
        bat_s = ba_s3[:, :, :SUBLANES].transpose(0, 2, 1).reshape(n_s, 1, SUBLANES, l_pad)
        cs0 = jnp.pad(state_conv[l], ((0, 0), (SUBLANES - (conv_w - 1), 0), (0, 0)))
        o_s, ssm_s = _deltanet(pad_l(dn_s), pad_l(z_s), ba_s3, bat_s, cs0, state_ssm[l], w_conv[l], hp, nw,
                               chunk=l_pad, l_real=l_s, nb=8)
        o_s = o_s[:, :l_s].reshape(t_s, vdim)
        outs["swk"].append(wk_s.reshape(n_s, window, n_kv, hd))
        outs["swv"].append(wv_s.reshape(n_s, window, n_kv, hd))
        outs["scv"].append(jnp.concatenate([state_conv[l], dn_s.reshape(n_s, l_s, conv_dim)], axis=1)[:, -(conv_w - 1):])
        outs["sss"].append(ssm_s)

        wr = jnp.pad(w_router[l], ((0, 0), (0, LANES - n_e)))
        br = jnp.pad(b_router[l], (0, LANES - n_e), constant_values=NEG_BIG).reshape(1, LANES)
        wr_hi = wr.astype(BF16)
        wr_lo = (wr - wr_hi.astype(F32)).astype(BF16)
        wts = (w_proj_attn[l].astype(BF16), w_proj_dn[l].astype(BF16), w_out[l].astype(BF16),
               ln1_w[l].reshape(1, d), ln1_b[l].reshape(1, d), wr_hi, wr_lo, br)
        res_p = _outproj(attn_p, o_p.reshape(t_p, vdim), g_p, x_p, gt1p, sh2p, sc2p, wts, None,
                         per_token_mod=False, tiles_per_seq=tps, t_total=t_all, tile_off=0, alpha=alpha)
        x1, pos, gate, cnt = _outproj(attn_s, o_s, g_s, x_s, gt1s, sh2s, sc2s, wts, res_p, per_token_mod=True,
                                      tiles_per_seq=1, t_total=t_all, tile_off=n_ptiles, alpha=alpha)

        bm = EXPERT_ROWS
        nt_all = t_all // tm
        cnt8 = (cnt[:, 0, :n_e].astype(I32) + RUN_ALIGN - 1) // RUN_ALIGN * RUN_ALIGN
        tot = jnp.sum(cnt8, axis=0)
        padded = (tot + bm - 1) // bm * bm
        pad_end = jnp.cumsum(padded)
        pad_start = pad_end - padded
        gbase = pad_start[None, :] + jnp.cumsum(cnt8, axis=0) - cnt8
        toff = jnp.cumsum(cnt8, axis=1) - cnt8
        n_local = -(-(tm * TOP_K + n_e * (RUN_ALIGN - 1)) // LANES) * LANES
        n_rows = -(-(t_all * TOP_K + nt_all * n_e * (RUN_ALIGN - 1) + n_e * (bm - 1)) // bm) * bm
        nblk = n_rows // bm
        n_used = jnp.maximum(pad_end[-1:] // bm, 1).astype(I32)
        block_e = jnp.minimum(jnp.sum(pad_end[None, :] <= (jnp.arange(nblk, dtype=I32) * bm)[:, None], axis=1),
                              n_e - 1).astype(I32)
        flat = lambda a: a.astype(I32).reshape(nt_all * n_e)
        run_tables = (flat(toff), flat(cnt8), flat(gbase))
        tail_tables = ((pad_start + tot).astype(I32), (padded - tot).astype(I32))

        xs = _dispatch(run_tables + tail_tables, x1, sh2p, sc2p, sh2s, sc2s, pos, n_rows=n_rows, n_local=n_local,
                       n_ptiles=n_ptiles, tiles_per_seq=tps, n_experts=n_e)
        yb = _experts(block_e, n_used, xs, w_gu[l], b_gu[l], w_down[l], b_down[l])
        x_p, x_s = _combine(run_tables, x1, gt2p, gt2s, gate, pos, ln2_w[l].reshape(1, d), ln2_b[l].reshape(1, d),
                            yb, n_local=n_local, n_ptiles=n_ptiles, tiles_per_seq=tps, n_experts=n_e, alpha=alpha)

    st = lambda k: outs[k][0][None] if depth == 1 else jnp.stack(outs[k])
    return (x_p.reshape(n_p, seq, d), x_s.reshape(n_s, l_s, d), st("pwk"), st("pwv"), st("pcv"), st("pss"),
            st("swk"), st("swv"), st("scv"), st("sss"))
```

```python
import functools

import numpy as np
import jax
import jax.numpy as jnp
from jax import lax
from jax.experimental import pallas as pl
from jax.experimental.pallas import tpu as pltpu

F32 = jnp.float32
BF16 = jnp.bfloat16
I32 = jnp.int32
U32 = jnp.uint32

PAST_LEN = 16384
ROPE_THETA = 500000.0
TOP_K = 4
SWIGLU_LIMIT = 7.0
SWIGLU_ALPHA = 1.702
DN_CHUNK = 64
LN_EPS = 1e-5
RMS_EPS = 1e-6
L2_EPS = 1e-6

LANES = 128
SUBLANES = 8
VMEM_LIMIT_BYTES = 56 * 1024 * 1024

TOKEN_TILE = 256
INPROJ_TILE = 512
EXPERT_ROWS = 512
RUN_ALIGN = SUBLANES
OUTPROJ_SUBTILES = 2
N_ROUTE_BUFS = 4
NEG_BIG = -1e30


def _cparams(*sem):
    return pltpu.CompilerParams(dimension_semantics=sem, vmem_limit_bytes=VMEM_LIMIT_BYTES)


def _silu(x):
    return x * jax.nn.sigmoid(x)


def _bdot(a, b):
    return jnp.dot(a.astype(BF16), b.astype(BF16), preferred_element_type=F32)


def _pack_halves(x):
    n = x.shape[1] // 2
    lo = lax.bitcast_convert_type(x[:, :n], U32)
    hi = lax.bitcast_convert_type(x[:, n:], U32)
    return (hi & jnp.uint32(0xFFFF0000)) | (lo >> 16)


def _unpack_halves(w):
    lo = lax.bitcast_convert_type(w << 16, F32).astype(BF16)
    hi = lax.bitcast_convert_type(w & jnp.uint32(0xFFFF0000), F32).astype(BF16)
    return lo, hi


def _ada_body(c_ref, w_ref, b_ref, o_ref):
    o_ref[...] = _bdot(_silu(c_ref[...]), w_ref[...]) + b_ref[...]


def _ada(c_all, w_ada, b_ada):
    n, d = c_all.shape
    dout = w_ada.shape[1]
    tn = d
    return pl.pallas_call(
        _ada_body,
        grid=(dout // tn,),
        in_specs=[pl.BlockSpec((n, d), lambda j: (0, 0)),
                  pl.BlockSpec((d, tn), lambda j: (0, j)),
                  pl.BlockSpec((1, tn), lambda j: (0, j))],
        out_specs=pl.BlockSpec((n, tn), lambda j: (0, j)),
        out_shape=jax.ShapeDtypeStruct((n, dout), F32),
        compiler_params=_cparams("arbitrary"),
        name="ada",
    )(c_all, w_ada, b_ada.reshape(1, dout))


def _inproj_body(x_ref, sh_ref, sc_ref, ct_ref, s1_ref, s2_ref, w_ref,
                 a_ref, dn_ref, z_ref, g_ref, ba_ref, *win_refs, cuts, n_rot_chunks, half_rot, window):
    h = (x_ref[...] * (1.0 + sc_ref[...]) + sh_ref[...]).astype(BF16)

    def mm(lo, hi):
        return jnp.dot(h, w_ref[:, lo:hi], preferred_element_type=F32)

    c_a, c_dn, c_z, c_g, c_ba = cuts
    qkv = mm(0, c_a)
    ct, s1, s2 = ct_ref[...], s1_ref[...], s2_ref[...]
    cols = []
    for c in range(n_rot_chunks):
        xc = qkv[:, c * LANES:(c + 1) * LANES]
        cols.append(xc * ct + pltpu.roll(xc, LANES - half_rot, 1) * s1 + pltpu.roll(xc, half_rot, 1) * s2)
    cols.append(qkv[:, n_rot_chunks * LANES:])
    rot = jnp.concatenate(cols, axis=1)
    a_ref[...] = rot.astype(a_ref.dtype)
    dn = mm(c_a, c_dn)
    dn_ref[...] = dn.astype(dn_ref.dtype)
    z_ref[...] = mm(c_dn, c_z).astype(z_ref.dtype)
    g_ref[...] = mm(c_z, c_g).astype(g_ref.dtype)
    ba_ref[...] = mm(c_g, c_ba)
    if win_refs:
        tail_ref, kvw_ref = win_refs
        tm = dn.shape[0]
        tail_ref[...] = dn[tm - SUBLANES:, :]
        kvw_ref[...] = rot[tm - window:, n_rot_chunks * LANES - LANES:]


def _inproj(x, sh, sc, tabs, w_perm, cuts, *, per_token_mod, tiles_per_seq, act_dtype, window, kv_cols, half_rot):
    t, d = x.shape
    tm = INPROJ_TILE
    nt = t // tm
    assert t % tm == 0 and tm >= window and (per_token_mod or nt % tiles_per_seq == 0)
    c_a, c_dn, c_z, c_g, c_ba = cuts
    n_rot_chunks = (c_a - kv_cols // 2) // LANES
    if per_token_mod:
        mod_spec = pl.BlockSpec((tm, d), lambda i: (i, 0))
        tab_spec = pl.BlockSpec((tm, LANES), lambda i: (0, 0))
    else:
        mod_spec = pl.BlockSpec((None, 1, d), lambda i: (i // tiles_per_seq, 0, 0))
        tab_spec = pl.BlockSpec((tm, LANES), lambda i: (i % tiles_per_seq, 0))
    out_shape = [jax.ShapeDtypeStruct((t, c_a), act_dtype),
                 jax.ShapeDtypeStruct((t, c_dn - c_a), act_dtype),
                 jax.ShapeDtypeStruct((t, c_z - c_dn), act_dtype),
                 jax.ShapeDtypeStruct((t, c_g - c_z), act_dtype),
                 jax.ShapeDtypeStruct((t, c_ba - c_g), F32)]
    out_specs = [pl.BlockSpec((tm, s.shape[1]), lambda i: (i, 0)) for s in out_shape]
    with_win = not per_token_mod
    if with_win:
        n_seq = nt // tiles_per_seq
        out_shape += [jax.ShapeDtypeStruct((nt, SUBLANES, c_dn - c_a), F32),
                      jax.ShapeDtypeStruct((n_seq, window, kv_cols), F32)]
        out_specs += [pl.BlockSpec((None, SUBLANES, c_dn - c_a), lambda i: (i, 0, 0)),
                      pl.BlockSpec((None, window, kv_cols), lambda i: (i // tiles_per_seq, 0, 0))]
    body = functools.partial(_inproj_body, cuts=cuts, n_rot_chunks=n_rot_chunks, half_rot=half_rot, window=window)
    return pl.pallas_call(
        body,
        grid=(nt,),
        in_specs=[pl.BlockSpec((tm, d), lambda i: (i, 0)), mod_spec, mod_spec,
                  tab_spec, tab_spec, tab_spec,
                  pl.BlockSpec((d, c_ba), lambda i: (0, 0))],
        out_specs=out_specs,
        out_shape=out_shape,
        compiler_params=_cparams("arbitrary"),
        name="inproj",
    )(x, sh, sc, *tabs, w_perm)


def _softmax_sink_pv(scores, valid, sinks, values):
    ms = [jnp.where(valid, s, -jnp.inf) for s in scores]
    m = [jnp.maximum(jnp.max(x, axis=-1, keepdims=True), sk) for x, sk in zip(ms, sinks)]
    p = [jnp.exp(x - mi) for x, mi in zip(ms, m)]
    den = [jnp.sum(pi, axis=-1, keepdims=True) + jnp.exp(sk - mi) for pi, sk, mi in zip(p, sinks, m)]
    return [jnp.dot((pi / di).astype(BF16), v, preferred_element_type=F32) for pi, di, v in zip(p, den, values)]


def _attn_prompt_body(sink_ref, q_ref, kvp_ref, kvc_ref, o_ref, *, n_q, n_kv, hd, window):
    j = pl.program_id(1)
    group = n_q // n_kv
    scale = hd ** -0.5
    assert np.log2(scale) == int(np.log2(scale))
    r = lax.broadcasted_iota(I32, (window, 2 * window), 0)
    c = lax.broadcasted_iota(I32, (window, 2 * window), 1)
    rel = window + r - c
    band = (rel >= 0) & (rel < window)
    kv_cur = kvc_ref[...]
    kv_first = jnp.concatenate([kvp_ref[...], kv_cur[:window]], axis=0)
    for qb, (kv, valid) in enumerate(((kv_first, band & ((c >= window) | (j > 0))), (kv_cur, band))):
        q = q_ref[qb * window:(qb + 1) * window, :] * scale
        scores = [lax.dot_general(q[:, h * hd:(h + 1) * hd], kv[:, (h // group) * hd:(h // group + 1) * hd],
                                  (((1,), (1,)), ((), ())), preferred_element_type=F32) for h in range(n_q)]
        values = [kv[:, (n_kv + h // group) * hd:(n_kv + h // group + 1) * hd] for h in range(n_q)]
        outs = _softmax_sink_pv(scores, valid, [sink_ref[h] for h in range(n_q)], values)
        o_ref[qb * window:(qb + 1) * window, :] = jnp.concatenate(outs, axis=1).astype(o_ref.dtype)


def _attn_prompt(qkva, sinks, n_seq, seq, n_q, n_kv, hd, window):
    qd, kvd = n_q * hd, 2 * n_kv * hd
    x3 = qkva.reshape(n_seq, seq, qd + kvd)
    nb = seq // window
    assert nb % 2 == 0 and qd % kvd == 0
    kv_blk = qd // kvd
    body = functools.partial(_attn_prompt_body, n_q=n_q, n_kv=n_kv, hd=hd, window=window)
    out = pl.pallas_call(
        body,
        grid=(n_seq, nb // 2),
        in_specs=[pl.BlockSpec(memory_space=pltpu.SMEM),
                  pl.BlockSpec((None, 2 * window, qd), lambda n, j: (n, j, 0)),
                  pl.BlockSpec((None, window, kvd), lambda n, j: (n, jnp.maximum(2 * j - 1, 0), kv_blk)),
                  pl.BlockSpec((None, 2 * window, kvd), lambda n, j: (n, j, kv_blk))],
        out_specs=pl.BlockSpec((None, 2 * window, qd), lambda n, j: (n, j, 0)),
        out_shape=jax.ShapeDtypeStruct((n_seq, seq, qd), BF16),
        compiler_params=_cparams("arbitrary", "arbitrary"),
        name="attn_prompt",
    )(sinks, x3, x3, x3)
    return out.reshape(n_seq * seq, qd)


def _attn_sample_body(sink_ref, q_ref, wk_ref, wv_ref, o_ref, wko_ref, wvo_ref,
                      *, bs, n_q, n_kv, hd, window, l_new, l_pad):
    group = n_q // n_kv
    qd = n_q * hd
    kd = n_kv * hd
    rows = group * l_pad
    r = lax.broadcasted_iota(I32, (rows, window + l_pad), 0) % l_pad
    c = lax.broadcasted_iota(I32, (rows, window + l_pad), 1)
    rel = window + r - c
    valid = (rel >= 0) & (rel < window) & (c < window + l_new)
    sinks = [jnp.concatenate([jnp.full((l_pad, 1), sink_ref[kvh * group + g], F32) for g in range(group)], axis=0)
             for kvh in range(n_kv)]
    qs, ks, vs = [], [], []
    for b in range(bs):
        x = q_ref[b]
        k_new = x[:, qd:qd + kd]
        v_new = x[:, qd + kd:]
        wko_ref[b, 0:window - l_new, :] = wk_ref[b, l_new:window, :]
        wko_ref[b, window - l_new:window, :] = k_new[0:l_new, :]
        wvo_ref[b, 0:window - l_new, :] = wv_ref[b, l_new:window, :]
        wvo_ref[b, window - l_new:window, :] = v_new[0:l_new, :]
        k_all = jnp.concatenate([wk_ref[b], k_new], axis=0).astype(BF16)
        v_all = jnp.concatenate([wv_ref[b], v_new], axis=0).astype(BF16)
        for kvh in range(n_kv):
            qs.append(jnp.concatenate([x[:, (kvh * group + g) * hd:(kvh * group + g + 1) * hd]
                                       for g in range(group)], axis=0).astype(BF16))
            ks.append(k_all[:, kvh * hd:(kvh + 1) * hd])
            vs.append(v_all[:, kvh * hd:(kvh + 1) * hd])
    n = len(qs)
    s = [lax.dot_general(qs[i], ks[i], (((1,), (1,)), ((), ())), preferred_element_type=F32) * (hd ** -0.5)
         for i in range(n)]
    o = _softmax_sink_pv(s, valid, [sinks[i % n_kv] for i in range(n)], vs)
    for b in range(bs):
        outs = [o[b * n_kv + kvh][g * l_pad:(g + 1) * l_pad, :] for kvh in range(n_kv) for g in range(group)]
        o_ref[b] = jnp.concatenate(outs, axis=1).astype(o_ref.dtype)


def _attn_sample(qkva_pad, win_k, win_v, sinks, n_q, n_kv, hd, l_new):
    n, l_pad, width = qkva_pad.shape
    window, kd = win_k.shape[1], win_k.shape[2]
    qd = n_q * hd
    bs = 8
    body = functools.partial(_attn_sample_body, bs=bs, n_q=n_q, n_kv=n_kv, hd=hd, window=window,
                             l_new=l_new, l_pad=l_pad)
    return pl.pallas_call(
        body,
        grid=(n // bs,),
        in_specs=[pl.BlockSpec(memory_space=pltpu.SMEM),
                  pl.BlockSpec((bs, l_pad, width), lambda i: (i, 0, 0)),
                  pl.BlockSpec((bs, window, kd), lambda i: (i, 0, 0)),
                  pl.BlockSpec((bs, window, kd), lambda i: (i, 0, 0))],
        out_specs=[pl.BlockSpec((bs, l_pad, qd), lambda i: (i, 0, 0)),
                   pl.BlockSpec((bs, window, kd), lambda i: (i, 0, 0)),
                   pl.BlockSpec((bs, window, kd), lambda i: (i, 0, 0))],
        out_shape=[jax.ShapeDtypeStruct((n, l_pad, qd), BF16),
                   jax.ShapeDtypeStruct((n, window, kd), F32),
                   jax.ShapeDtypeStruct((n, window, kd), F32)],
        compiler_params=_cparams("arbitrary"),
        name="attn_sample",
    )(sinks, qkva_pad, win_k, win_v)


def _split_bf16(x):
    hi = x.astype(BF16)
    return hi, (x - hi.astype(F32)).astype(BF16)


def _tdot(a, b):
    ah, al = _split_bf16(a)
    bh, bl = _split_bf16(b)
    m = a.shape[0]
    t = jnp.dot(jnp.concatenate([ah, al], axis=0), bh, preferred_element_type=F32)
    return t[:m] + t[m:] + jnp.dot(ah, bl, preferred_element_type=F32)


def _dn_body(qkv_ref, z_ref, ba_ref, bat_ref, cs0_ref, s0_ref, wc_ref, hp_ref, nw_ref,
             o_ref, s_ref, xbuf, *, nb, chunk, heads, dk, dv, l_real, conv_w):
    c_idx = pl.program_id(1)
    hc = SUBLANES

    @pl.when(c_idx == 0)
    def _():
        xbuf[:, 0:hc, :] = cs0_ref[...]
        s_ref[...] = s0_ref[...]

    qk_dim = heads * dk
    row = lax.broadcasted_iota(I32, (chunk, chunk), 0)
    col = lax.broadcasted_iota(I32, (chunk, chunk), 1)
    incl = row >= col
    strict = row > col
    eye = (row == col).astype(F32)
    valid_c = row[:, 0:1] < l_real
    valid_r = col[0:1, :] < l_real
    n_levels = max(1, int(np.ceil(np.log2(chunk))))
    wc = wc_ref[...]
    hp = hp_ref[...]
    neg_exp_alog = -jnp.exp(hp[0:1, :])
    dt_bias = hp[1:2, :]
    nw = nw_ref[...]
    chains = [(b, h) for b in range(nb) for h in range(heads)]
    n = len(chains)

    ys = []
    for b in range(nb):
        xbuf[b, hc:hc + chunk, :] = qkv_ref[b].astype(F32)
        y = xbuf[b, hc:hc + chunk, :] * wc[conv_w - 1:conv_w, :]
        for j in range(conv_w - 1):
            off = hc - (conv_w - 1) + j
            y = y + xbuf[b, off:off + chunk, :] * wc[j:j + 1, :]
        ys.append(_silu(y))
        xbuf[b, 0:hc, :] = xbuf[b, chunk:chunk + hc, :]

    qn, kn, kb, vb, decay, e_gc, e_rest, e_last = [], [], [], [], [], [], [], []
    for b, h in chains:
        y = ys[b]
        qh = y[:, h * dk:(h + 1) * dk]
        kh = y[:, qk_dim + h * dk:qk_dim + (h + 1) * dk]
        vh = y[:, 2 * qk_dim + h * dv:2 * qk_dim + (h + 1) * dv]
        ba = ba_ref[b]
        bat = bat_ref[b]
        ne = neg_exp_alog[:, h:h + 1]
        db = dt_bias[:, h:h + 1]
        beta = jnp.where(valid_c, jax.nn.sigmoid(ba[:, h:h + 1]), 0.0)
        g_col = jnp.where(valid_c, ne * jax.nn.softplus(ba[:, heads + h:heads + h + 1] + db), 0.0)
        g_row = jnp.where(valid_r, ne * jax.nn.softplus(bat[heads + h:heads + h + 1, :] + db), 0.0)
        gc_col = jnp.sum(jnp.where(incl, g_row, 0.0), axis=1, keepdims=True)
        gc_row = jnp.sum(jnp.where(row <= col, g_col, 0.0), axis=0, keepdims=True)
        g_last = gc_col[chunk - 1:chunk, :]
        q_ = qh * lax.rsqrt(jnp.sum(qh * qh, -1, keepdims=True) + L2_EPS) * (dk ** -0.5)
        k_ = kh * lax.rsqrt(jnp.sum(kh * kh, -1, keepdims=True) + L2_EPS)
        qn.append(q_)
        kn.append(k_)
        kb.append(k_ * beta)
        vb.append(vh * beta)
        decay.append(jnp.where(incl, jnp.exp(gc_col - gc_row), 0.0))
        e_gc.append(jnp.exp(gc_col))
        e_rest.append(jnp.exp(g_last - gc_col))
        e_last.append(jnp.exp(g_last))

    sc = [lax.dot_general(jnp.concatenate([qn[i], kb[i]], axis=0).astype(BF16), kn[i].astype(BF16),
                          (((1,), (1,)), ((), ())), preferred_element_type=F32) for i in range(n)]
    qk = [sc[i][:chunk] * decay[i] for i in range(n)]
    p = [jnp.where(strict, -(sc[i][chunk:] * decay[i]), 0.0) for i in range(n)]
    t_inv = [eye + p[i] for i in range(n)]
    if n_levels > 1:
        p = [_tdot(p[i], p[i]) for i in range(n)]
    for lvl in range(1, n_levels):
        if lvl < n_levels - 1:
            yp = [_tdot(jnp.concatenate([t_inv[i], p[i]], axis=0), p[i]) for i in range(n)]
            t_inv = [t_inv[i] + yp[i][:chunk] for i in range(n)]
            p = [yp[i][chunk:] for i in range(n)]
        else:
            t_inv = [t_inv[i] + _tdot(t_inv[i], p[i]) for i in range(n)]
    sol = [_tdot(t_inv[i], jnp.concatenate([vb[i], kb[i] * e_gc[i]], axis=1)) for i in range(n)]
    s_old = [s_ref[b, h] for b, h in chains]
    wq = [_bdot(jnp.concatenate([sol[i][:, dv:], qn[i] * e_gc[i]], axis=0), s_old[i]) for i in range(n)]
    v_new = [sol[i][:, :dv] - wq[i][:chunk] for i in range(n)]
    o = [wq[i][chunk:] + _bdot(qk[i], v_new[i]) for i in range(n)]
    for i, (b, h) in enumerate(chains):
        s_ref[b, h] = s_old[i] * e_last[i] + lax.dot_general(
            (kn[i] * e_rest[i]).astype(BF16), v_new[i].astype(BF16), (((0,), (0,)), ((), ())),
            preferred_element_type=F32)
    for b in range(nb):
        zt = z_ref[b].astype(F32)
        outs = []
        for h in range(heads):
            oi = o[b * heads + h]
            on = oi * lax.rsqrt(jnp.mean(oi * oi, -1, keepdims=True) + RMS_EPS) * nw
            outs.append(on * _silu(zt[:, h * dv:(h + 1) * dv]))
        o_ref[b] = jnp.concatenate(outs, axis=1).astype(o_ref.dtype)


def _deltanet(qkv, z, ba, bat, cs0, s0, w_conv, hp, norm_w, *, chunk, l_real, nb):
    n, l, conv_dim = qkv.shape
    heads, dk, dv = s0.shape[1:]
    nc = l // chunk
    assert n % nb == 0 and l % chunk == 0
    conv_w = w_conv.shape[0]
    hist = pltpu.VMEM((nb, SUBLANES + chunk, conv_dim), F32)
    body = functools.partial(_dn_body, nb=nb, chunk=chunk, heads=heads, dk=dk, dv=dv, l_real=l_real, conv_w=conv_w)
    return pl.pallas_call(
        body,
        grid=(n // nb, nc),
        in_specs=[pl.BlockSpec((nb, chunk, conv_dim), lambda i, c: (i, c, 0)),
                  pl.BlockSpec((nb, chunk, heads * dv), lambda i, c: (i, c, 0)),
                  pl.BlockSpec((nb, chunk, LANES), lambda i, c: (i, c, 0)),
                  pl.BlockSpec((nb, None, SUBLANES, chunk), lambda i, c: (i, c, 0, 0)),
                  pl.BlockSpec((nb, SUBLANES, conv_dim), lambda i, c: (i, 0, 0)),
                  pl.BlockSpec((nb, heads, dk, dv), lambda i, c: (i, 0, 0, 0)),
                  pl.BlockSpec((conv_w, conv_dim), lambda i, c: (0, 0)),
                  pl.BlockSpec((SUBLANES, LANES), lambda i, c: (0, 0)),
                  pl.BlockSpec((1, dv), lambda i, c: (0, 0))],
        out_specs=[pl.BlockSpec((nb, chunk, heads * dv), lambda i, c: (i, c, 0)),
                   pl.BlockSpec((nb, heads, dk, dv), lambda i, c: (i, 0, 0, 0))],
        out_shape=[jax.ShapeDtypeStruct((n, l, heads * dv), BF16),
                   jax.ShapeDtypeStruct((n, heads, dk, dv), F32)],
        scratch_shapes=[hist],
        compiler_params=_cparams("arbitrary", "arbitrary"),
        name="deltanet",
    )(qkv, z, ba, bat, cs0, s0, w_conv, hp, norm_w)


def _layer_norm(r, w, b):
    mu = jnp.mean(r, -1, keepdims=True)
    var = jnp.mean(jnp.square(r - mu), -1, keepdims=True)
    return (r - mu) * lax.rsqrt(var + LN_EPS) * w + b


def _outproj_body(attn_ref, dn_ref, g_ref, x_ref, gt_ref, sh2_ref, sc2_ref, wpa_ref, wpd_ref, wo_ref,
                  lnw_ref, lnb_ref, wrh_ref, wrl_ref, br_ref, *refs, alpha, top_k, aliased):
    if aliased:
        refs = refs[N_ROUTE_BUFS:]
    x1_ref, pos_ref, gate_ref, cnt_ref = refs
    for s in range(cnt_ref.shape[0]):
        _outproj_tile(s, attn_ref, dn_ref, g_ref, x_ref, gt_ref, sh2_ref, sc2_ref, wpa_ref, wpd_ref, wo_ref, lnw_ref,
                      lnb_ref, wrh_ref, wrl_ref, br_ref, x1_ref, pos_ref, gate_ref, cnt_ref,
                      alpha=alpha, top_k=top_k)


def _outproj_tile(s, attn_ref, dn_ref, g_ref, x_ref, gt_ref, sh2_ref, sc2_ref, wpa_ref, wpd_ref, wo_ref, lnw_ref,
                  lnb_ref, wrh_ref, wrl_ref, br_ref, x1_ref, pos_ref, gate_ref, cnt_ref, *, alpha, top_k):
    d = x_ref.shape[1]
    tm = TOKEN_TILE
    rows = pl.ds(s * tm, tm)

    def mod(ref):
        return ref[...] if ref.shape[0] == 1 else ref[rows, :]

    g = g_ref[rows, :].astype(F32)
    pa = jnp.dot(attn_ref[rows, :], wpa_ref[...], preferred_element_type=F32)
    pd = jnp.dot(dn_ref[rows, :], wpd_ref[...], preferred_element_type=F32)
    merged = jax.nn.sigmoid(g[:, :d]) * pa + jax.nn.sigmoid(g[:, d:]) * pd
    mix = jnp.dot(merged.astype(BF16), wo_ref[...], preferred_element_type=F32)
    x1 = _layer_norm(alpha * x_ref[rows, :] + mod(gt_ref) * mix, lnw_ref[...], lnb_ref[...])
    x1_ref[rows, :] = x1
    h2 = x1 * (1.0 + mod(sc2_ref)) + mod(sh2_ref)
    h_hi, h_lo = _split_bf16(h2)
    lg = jnp.dot(jnp.concatenate([h_hi, h_lo], axis=0), wrh_ref[...], preferred_element_type=F32)
    logits = lg[:tm] + lg[tm:] + jnp.dot(h_hi, wrl_ref[...], preferred_element_type=F32) + br_ref[...]
    lane = lax.broadcasted_iota(I32, (tm, LANES), 1)
    lane_f = lane.astype(F32)
    vals, sels = [], []
    l = logits
    for _ in range(top_k):
        m = jnp.max(l, axis=1, keepdims=True)
        idx = jnp.min(jnp.where(l == m, lane_f, float(LANES)), axis=1, keepdims=True)
        sel = lane_f == idx
        vals.append(m)
        sels.append(sel)
        l = jnp.where(sel, -jnp.inf, l)
    ex = [jnp.exp(v - vals[0]) for v in vals]
    den = ex[0]
    for e in ex[1:]:
        den = den + e
    multi_hot = jnp.zeros((tm, LANES), F32)
    for sel in sels:
        multi_hot = multi_hot + jnp.where(sel, 1.0, 0.0)
    r_i = lax.broadcasted_iota(I32, (tm, tm), 0)
    c_i = lax.broadcasted_iota(I32, (tm, tm), 1)
    lower = jnp.where(r_i > c_i, 1.0, 0.0).astype(BF16)
    prefix = jnp.dot(lower, multi_hot.astype(BF16), preferred_element_type=F32)
    counts = jnp.sum(multi_hot, axis=0, keepdims=True)
    cnt_pad = jnp.floor((counts + (RUN_ALIGN - 1)) * (1.0 / RUN_ALIGN)) * RUN_ALIGN
    e_r = lax.broadcasted_iota(I32, (LANES, LANES), 0)
    e_c = lax.broadcasted_iota(I32, (LANES, LANES), 1)
    before = jnp.where(e_r < e_c, 1.0, 0.0).astype(BF16)
    run_off = jnp.dot(jnp.broadcast_to(cnt_pad, (SUBLANES, LANES)).astype(BF16), before,
                      preferred_element_type=F32)[0:1, :]
    g_out = jnp.zeros((tm, LANES), F32)
    p_out = jnp.zeros((tm, LANES), F32)
    for k in range(top_k):
        pos_k = jnp.sum(jnp.where(sels[k], prefix + run_off, 0.0), axis=1, keepdims=True)
        g_out = jnp.where(lane == k, ex[k] / den, g_out)
        p_out = jnp.where(lane == k, pos_k, p_out)
    pos_ref[rows, :] = p_out.astype(I32)
    gate_ref[rows, :] = g_out
    cnt_ref[s] = jnp.broadcast_to(counts, cnt_ref.shape[1:])


def _outproj(attn, dn, gates, x, gt, sh2, sc2, wts, bufs, *, per_token_mod, tiles_per_seq, t_total, tile_off, alpha):
    t, d = x.shape
    sub = OUTPROJ_SUBTILES
    tm = TOKEN_TILE * sub
    nt = t // tm
    assert t % tm == 0 and tile_off % sub == 0 and (per_token_mod or tiles_per_seq % sub == 0)
    wpa, wpd, wo, lnw, lnb, wrh, wrl, br = wts
    if per_token_mod:
        mod_spec = pl.BlockSpec((tm, d), lambda i: (i, 0))
    else:
        mod_spec = pl.BlockSpec((None, 1, d), lambda i: (i // (tiles_per_seq // sub), 0, 0))

    def row(width):
        return pl.BlockSpec((tm, width), lambda i: (i, 0))

    def full(a):
        return pl.BlockSpec(a.shape, lambda i: (0,) * a.ndim)

    aliased = bufs is not None
    in_specs = [row(attn.shape[1]), row(dn.shape[1]), row(gates.shape[1]), row(d), mod_spec, mod_spec, mod_spec,
                full(wpa), full(wpd), full(wo), full(lnw), full(lnb), full(wrh), full(wrl), full(br)]
    args = [attn, dn, gates, x, gt, sh2, sc2, wpa, wpd, wo, lnw, lnb, wrh, wrl, br]
    io_alias = {}
    if aliased:
        for k, bfr in enumerate(bufs):
            in_specs.append(pl.BlockSpec(memory_space=pl.ANY))
            io_alias[len(args)] = k
            args.append(bfr)
    step_off = tile_off // sub
    out_row = lambda width: pl.BlockSpec((tm, width), lambda i: (i + step_off, 0))
    out_shape = [jax.ShapeDtypeStruct((t_total, d), F32), jax.ShapeDtypeStruct((t_total, LANES), I32),
                 jax.ShapeDtypeStruct((t_total, LANES), F32),
                 jax.ShapeDtypeStruct((t_total // TOKEN_TILE, SUBLANES, LANES), F32)]
    assert len(out_shape) == N_ROUTE_BUFS and TOKEN_TILE <= 256
    out_specs = [out_row(d), out_row(LANES), out_row(LANES),
                 pl.BlockSpec((sub, SUBLANES, LANES), lambda i: (i + step_off, 0, 0))]
    body = functools.partial(_outproj_body, alpha=alpha, top_k=TOP_K, aliased=aliased)
    return pl.pallas_call(
        body,
        grid=(nt,),
        in_specs=in_specs,
        out_specs=out_specs,
        out_shape=out_shape,
        input_output_aliases=io_alias,
        compiler_params=_cparams("arbitrary"),
        name="outproj",
    )(*args)


def _select_mod(i, n_ptiles, seq_ref, tok_ref):
    return jnp.where(i < n_ptiles, seq_ref[...], tok_ref[...])


def _for_run(rows, local_off, global_off, fn):
    @pl.when(rows > 0)
    def _():
        fn(pl.multiple_of(local_off, RUN_ALIGN), pl.multiple_of(global_off, RUN_ALIGN),
           pl.multiple_of(rows, RUN_ALIGN))


def _for_each_run(i, n_experts, toff_ref, cnt8_ref, gbase_ref, fn):
    for e in range(n_experts):
        idx = i * n_experts + e
        _for_run(cnt8_ref[idx], toff_ref[idx], gbase_ref[idx], fn)


def _tile_rows(tile, n_experts, toff_ref, cnt8_ref):
    last = tile * n_experts + n_experts - 1
    return pl.multiple_of(toff_ref[last] + cnt8_ref[last], RUN_ALIGN)


def _scatter_matrix(pos_ref, n_local, values):
    pos = pos_ref[...]
    col = lax.broadcasted_iota(I32, (pos.shape[0], n_local), 1)
    out = jnp.zeros((pos.shape[0], n_local), F32)
    for k in range(TOP_K):
        out = jnp.where(col == pos[:, k:k + 1], 1.0 if values is None else values[:, k:k + 1], out)
    return out


def _dispatch_body(toff_ref, cnt8_ref, gbase_ref, tstart_ref, trows_ref,
                   x1_ref, shs_ref, scs_ref, sht_ref, sct_ref, pos_ref,
                   xs_ref, lbuf, zbuf, sem, *, n_tiles, n_ptiles, n_experts):
    i = pl.program_id(0)
    sc = _select_mod(i, n_ptiles, scs_ref, sct_ref)
    sh = _select_mod(i, n_ptiles, shs_ref, sht_ref)
    h2 = (x1_ref[...] * (1.0 + sc) + sh).astype(BF16)
    n_local = lbuf.shape[1]
    slot = i % 2

    def runs(tile, sl, act):
        def fn(lo, go, rows):
            act(pltpu.make_async_copy(lbuf.at[sl, pl.ds(lo, rows), :], xs_ref.at[pl.ds(go, rows), :], sem.at[sl]))
        _for_each_run(tile, n_experts, toff_ref, cnt8_ref, gbase_ref, fn)

    start = lambda cp: cp.start()
    wait = lambda cp: cp.wait()

    def wait_runs(tile, sl):
        rows = _tile_rows(tile, n_experts, toff_ref, cnt8_ref)
        pltpu.make_async_copy(lbuf.at[sl, pl.ds(0, rows), :], xs_ref.at[pl.ds(0, rows), :], sem.at[sl]).wait()

    @pl.when(i >= 2)
    def _():
        wait_runs(i - 2, slot)

    onehot = _scatter_matrix(pos_ref, n_local, None)
    srt = lax.dot_general(onehot.astype(BF16), h2, (((0,), (0,)), ((), ())), preferred_element_type=F32)
    lbuf[slot] = _pack_halves(srt)
    runs(i, slot, start)

    @pl.when(i == n_tiles - 1)
    def _():
        wait_runs(i - 1, 1 - slot)
        wait_runs(i, slot)
        zbuf[...] = jnp.zeros(zbuf.shape, zbuf.dtype)

        def zero_fill(e, act):
            _for_run(trows_ref[e], 0, tstart_ref[e], lambda lo, go, rows: act(pltpu.make_async_copy(
                zbuf.at[pl.ds(0, rows), :], xs_ref.at[pl.ds(go, rows), :], sem.at[0])))

        for e in range(n_experts):
            zero_fill(e, start)
        for e in range(n_experts):
            zero_fill(e, wait)


def _dispatch(tables, x1, sh_seq, sc_seq, sh_tok, sc_tok, pos, *, n_rows, n_local, n_ptiles, tiles_per_seq,
              n_experts):
    t, d = x1.shape
    tm = TOKEN_TILE
    nt = t // tm
    n_seq = sh_seq.shape[0]
    assert nt >= 2 and d % 2 == 0
    body = functools.partial(_dispatch_body, n_tiles=nt, n_ptiles=n_ptiles, n_experts=n_experts)
    seq_spec = pl.BlockSpec((None, 1, d), lambda i, *_: (jnp.minimum(i // tiles_per_seq, n_seq - 1), 0, 0))
    tok_spec = pl.BlockSpec((tm, d), lambda i, *_: (jnp.maximum(i - n_ptiles, 0), 0))
    return pl.pallas_call(
        body,
        grid_spec=pltpu.PrefetchScalarGridSpec(
            num_scalar_prefetch=len(tables),
            grid=(nt,),
            in_specs=[pl.BlockSpec((tm, d), lambda i, *_: (i, 0)),
                      seq_spec, seq_spec, tok_spec, tok_spec,
                      pl.BlockSpec((tm, LANES), lambda i, *_: (i, 0))],
            out_specs=pl.BlockSpec(memory_space=pl.ANY),
            scratch_shapes=[pltpu.VMEM((2, n_local, d // 2), U32), pltpu.VMEM((EXPERT_ROWS, d // 2), U32),
                            pltpu.SemaphoreType.DMA((2,))],
        ),
        out_shape=jax.ShapeDtypeStruct((n_rows, d // 2), U32),
        compiler_params=_cparams("arbitrary"),
        name="dispatch",
    )(*tables, x1, sh_seq, sc_seq, sh_tok, sc_tok, pos)


def _expert_body(be_ref, nu_ref, first_ref, xs_ref, wgu_ref, bgu_ref, wd_ref, bd_ref, y_ref, wgu_bf, wd_bf, *, de):
    i = pl.program_id(0)
    used = i < nu_ref[0]

    @pl.when(used & (first_ref[i] == 1))
    def _():
        wgu_bf[...] = wgu_ref[...].astype(BF16)
        wd_bf[...] = wd_ref[...].astype(BF16)

    @pl.when(used)
    def _():
        x = jnp.concatenate(_unpack_halves(xs_ref[...]), axis=1)
        gu = jnp.dot(x, wgu_bf[...], preferred_element_type=F32) + bgu_ref[...]
        glu = jnp.minimum(gu[:, :de], SWIGLU_LIMIT)
        lin = jnp.clip(gu[:, de:], -SWIGLU_LIMIT, SWIGLU_LIMIT)
        act = glu * jax.nn.sigmoid(SWIGLU_ALPHA * glu) * (lin + 1.0)
        y = jnp.dot(act.astype(BF16), wd_bf[...], preferred_element_type=F32) + bd_ref[...]
        y_ref[...] = _pack_halves(y.astype(BF16).astype(F32))


def _experts(block_e, n_used, xs, w_gu, b_gu, w_down, b_down):
    p, dh = xs.shape
    bm = EXPERT_ROWS
    n_e, d, de2 = w_gu.shape
    de = de2 // 2
    nblk = p // bm

    def blk(i, be, nu, *_):
        return jnp.minimum(i, nu[0] - 1)

    first = jnp.concatenate([jnp.ones((1,), I32), (block_e[1:] != block_e[:-1]).astype(I32)])
    body = functools.partial(_expert_body, de=de)
    return pl.pallas_call(
        body,
        grid_spec=pltpu.PrefetchScalarGridSpec(
            num_scalar_prefetch=3,
            grid=(nblk,),
            in_specs=[pl.BlockSpec((bm, dh), lambda i, be, nu, ft: (blk(i, be, nu), 0)),
                      pl.BlockSpec((None, d, de2), lambda i, be, nu, ft: (be[blk(i, be, nu)], 0, 0)),
                      pl.BlockSpec((None, 1, de2), lambda i, be, nu, ft: (be[blk(i, be, nu)], 0, 0)),
                      pl.BlockSpec((None, de, d), lambda i, be, nu, ft: (be[blk(i, be, nu)], 0, 0)),
                      pl.BlockSpec((None, 1, d), lambda i, be, nu, ft: (be[blk(i, be, nu)], 0, 0))],
            out_specs=pl.BlockSpec((bm, dh), lambda i, be, nu, ft: (blk(i, be, nu), 0)),
            scratch_shapes=[pltpu.VMEM((d, de2), BF16), pltpu.VMEM((de, d), BF16)],
        ),
        out_shape=jax.ShapeDtypeStruct((p, dh), U32),
        compiler_params=_cparams("arbitrary"),
        name="experts",
    )(block_e, n_used, first, xs, w_gu, b_gu.reshape(n_e, 1, de2), w_down, b_down.reshape(n_e, 1, d))


def _combine_body(toff_ref, cnt8_ref, gbase_ref, x1_ref, gts_ref, gtt_ref, gate_ref, pos_ref,
                  lnw_ref, lnb_ref, yb_ref, yp_ref, ys_ref, ybuf, sem, *, top_k, n_ptiles, n_experts, alpha):
    i = pl.program_id(0)
    n_tiles = pl.num_programs(0)
    tm = x1_ref.shape[0]
    n_local = ybuf.shape[1]
    slot = i % 2

    def fetch(tile, sl, act):
        def fn(lo, go, rows):
            act(pltpu.make_async_copy(yb_ref.at[pl.ds(go, rows), :], ybuf.at[sl, pl.ds(lo, rows), :], sem.at[sl]))
        _for_each_run(tile, n_experts, toff_ref, cnt8_ref, gbase_ref, fn)

    def start_fetch(tile, sl):
        ybuf[sl, tm * top_k:, :] = jnp.zeros((n_local - tm * top_k, ybuf.shape[2]), ybuf.dtype)
        fetch(tile, sl, lambda cp: cp.start())

    @pl.when(i == 0)
    def _():
        start_fetch(i, slot)

    @pl.when(i + 1 < n_tiles)
    def _():
        start_fetch(i + 1, 1 - slot)

    weights = _scatter_matrix(pos_ref, n_local, gate_ref[...]).astype(BF16)
    rows = _tile_rows(i, n_experts, toff_ref, cnt8_ref)
    pltpu.make_async_copy(yb_ref.at[pl.ds(0, rows), :], ybuf.at[slot, pl.ds(0, rows), :], sem.at[slot]).wait()
    y_lo, y_hi = _unpack_halves(ybuf[slot])
    ff = jnp.concatenate([jnp.dot(weights, y_lo, preferred_element_type=F32),
                          jnp.dot(weights, y_hi, preferred_element_type=F32)], axis=1)
    gt = _select_mod(i, n_ptiles, gts_ref, gtt_ref)
    y = _layer_norm(alpha * x1_ref[...] + gt * ff, lnw_ref[...], lnb_ref[...])

    @pl.when(i < n_ptiles)
    def _():
        yp_ref[...] = y

    @pl.when(i >= n_ptiles)
    def _():
        ys_ref[...] = y


def _combine(tables, x1, gt_seq, gt_tok, gate, pos, lnw, lnb, yb, *, n_local, n_ptiles, tiles_per_seq, n_experts,
             alpha):
    t, d = x1.shape
    tm = TOKEN_TILE
    nt = t // tm
    n_seq = gt_seq.shape[0]
    body = functools.partial(_combine_body, top_k=TOP_K, n_ptiles=n_ptiles, n_experts=n_experts, alpha=alpha)
    lane_spec = pl.BlockSpec((tm, LANES), lambda i, *_: (i, 0))
    return pl.pallas_call(
        body,
        grid_spec=pltpu.PrefetchScalarGridSpec(
            num_scalar_prefetch=len(tables),
            grid=(nt,),
            in_specs=[pl.BlockSpec((tm, d), lambda i, *_: (i, 0)),
                      pl.BlockSpec((None, 1, d), lambda i, *_: (jnp.minimum(i // tiles_per_seq, n_seq - 1), 0, 0)),
                      pl.BlockSpec((tm, d), lambda i, *_: (jnp.maximum(i - n_ptiles, 0), 0)),
                      lane_spec,
                      pl.BlockSpec((tm, LANES), lambda i, *_: (i, 0)),
                      pl.BlockSpec((1, d), lambda i, *_: (0, 0)),
                      pl.BlockSpec((1, d), lambda i, *_: (0, 0)),
                      pl.BlockSpec(memory_space=pl.ANY)],
            out_specs=[pl.BlockSpec((tm, d), lambda i, *_: (jnp.minimum(i, n_ptiles - 1), 0)),
                       pl.BlockSpec((tm, d), lambda i, *_: (jnp.maximum(i - n_ptiles, 0), 0))],
            scratch_shapes=[pltpu.VMEM((2, n_local, d // 2), U32), pltpu.SemaphoreType.DMA((2,))],
        ),
        out_shape=[jax.ShapeDtypeStruct((n_ptiles * tm, d), F32),
                   jax.ShapeDtypeStruct(((nt - n_ptiles) * tm, d), F32)],
        compiler_params=_cparams("arbitrary"),
        name="combine",
    )(*tables, x1, gt_seq, gt_tok, gate, pos, lnw, lnb, yb)


def _rotary_tables(pos, hd, rot_dim):
    half = rot_dim // 2
    inv_freq = jnp.power(jnp.float32(ROPE_THETA), -jnp.arange(half, dtype=F32) * (2.0 / rot_dim))
    ang = pos.astype(F32)[:, None] * inv_freq[None, :]
    cos, sin = jnp.cos(ang), jnp.sin(ang)
    n = pos.shape[0]
    ones = jnp.ones((n, hd - rot_dim), F32)
    zeros = jnp.zeros((n, hd - rot_dim), F32)
    zh = jnp.zeros((n, half), F32)
    ct = jnp.concatenate([cos, cos, ones], axis=1)
    s1 = jnp.concatenate([-sin, zh, zeros], axis=1)
    s2 = jnp.concatenate([zh, sin, zeros], axis=1)
    reps = LANES // hd
    return tuple(jnp.tile(a, (1, reps)) for a in (ct, s1, s2))


def kernel(x_prompt, x_sample, state_win_k, state_win_v, state_conv, state_ssm, c_prompt, c_sample, w_ada, b_ada, w_in, attn_sinks, w_conv, dn_a_log, dn_dt_bias, dn_norm_w, w_proj_attn, w_proj_dn, w_out, ln1_w, ln1_b, w_router, b_router, w_gu, b_gu, w_down, b_down, ln2_w, ln2_b):
    n_p, seq, d = x_prompt.shape
    n_s, l_s, _ = x_sample.shape
    depth = w_ada.shape[0]
    window, n_kv, hd = state_win_k.shape[2:]
    n_q = attn_sinks.shape[1]
    heads, dk, dv = state_ssm.shape[2:]
    conv_w, conv_dim = w_conv.shape[1:]
    n_e = w_router.shape[2]
    qd, kd = n_q * hd, n_kv * hd
    vdim = heads * dv
    rot_dim = hd // 4
    alpha = float((2 * depth) ** 0.25)
    tm = TOKEN_TILE
    t_p, t_s = n_p * seq, n_s * l_s
    t_all = t_p + t_s
    tps = seq // tm
    n_ptiles = t_p // tm
    l_pad = SUBLANES
    assert seq % tm == 0 and t_s % tm == 0 and tm % l_s == 0 and l_s <= l_pad and l_s >= conv_w - 1
    assert 2 * heads <= SUBLANES and n_e <= LANES and hd * 2 == LANES and rot_dim % 2 == 0

    sizes = [qd, kd, kd, conv_dim, vdim, heads, heads, d, d]
    offs = [int(o) for o in np.concatenate([[0], np.cumsum(sizes)])]
    cuts = (qd + 2 * kd, qd + 2 * kd + conv_dim, qd + 2 * kd + conv_dim + vdim, qd + 2 * kd + conv_dim + vdim + 2 * d)
    cuts = cuts + (cuts[-1] + LANES,)

    tabs_p = _rotary_tables(jnp.arange(seq, dtype=I32), hd, rot_dim)
    tabs_s = tuple(jnp.tile(a, (INPROJ_TILE // l_s, 1))
                   for a in _rotary_tables(PAST_LEN + jnp.arange(l_s, dtype=I32), hd, rot_dim))

    x_p = x_prompt.reshape(t_p, d)
    x_s = x_sample.reshape(t_s, d)
    c_all = jnp.concatenate([c_prompt, c_sample], axis=0)
    outs = {k: [] for k in ("pwk", "pwv", "pcv", "pss", "swk", "swv", "scv", "sss")}

    for l in range(depth):
        w_in_l = jnp.concatenate([w_in[l][:, :offs[5]], w_in[l][:, offs[7]:], w_in[l][:, offs[5]:offs[7]],
                                  jnp.zeros((d, LANES - 2 * heads), F32)], axis=1).astype(BF16)
        mod = _ada(c_all, w_ada[l], b_ada[l])
        mod_p = mod[:n_p].reshape(n_p, 6, 1, d)
        mod_s = jnp.repeat(mod[n_p:].reshape(n_s, 6, d).transpose(1, 0, 2), l_s, axis=1)
        sh1p, sc1p, gt1p, sh2p, sc2p, gt2p = [mod_p[:, k] for k in range(6)]
        sh1s, sc1s, gt1s, sh2s, sc2s, gt2s = [mod_s[k] for k in range(6)]

        qkva_p, dn_p, z_p, g_p, ba_p, tail_p, kvw_p = _inproj(
            x_p, sh1p, sc1p, tabs_p, w_in_l, cuts, per_token_mod=False, tiles_per_seq=seq // INPROJ_TILE, act_dtype=BF16,
            window=window, kv_cols=2 * kd, half_rot=rot_dim // 2)
        attn_p = _attn_prompt(qkva_p, attn_sinks[l], n_p, seq, n_q, n_kv, hd, window)
        chunk = min(DN_CHUNK, seq)
        nc = seq // chunk
        bat_p = ba_p[:, :SUBLANES].reshape(n_p, nc, chunk, SUBLANES).transpose(0, 1, 3, 2)
        hp = jnp.zeros((SUBLANES, LANES), F32).at[0, :heads].set(dn_a_log[l]).at[1, :heads].set(dn_dt_bias[l])
        nw = dn_norm_w[l].reshape(1, dv)
        o_p, ssm_p = _deltanet(dn_p.reshape(n_p, seq, conv_dim), z_p.reshape(n_p, seq, vdim),
                               ba_p.reshape(n_p, seq, LANES), bat_p,
                               jnp.zeros((n_p, SUBLANES, conv_dim), F32), jnp.zeros((n_p, heads, dk, dv), F32),
                               w_conv[l], hp, nw, chunk=chunk, l_real=chunk, nb=4)
        outs["pwk"].append(kvw_p[:, :, :kd].reshape(n_p, window, n_kv, hd))
        outs["pwv"].append(kvw_p[:, :, kd:].reshape(n_p, window, n_kv, hd))
        outs["pcv"].append(tail_p.reshape(n_p, seq // INPROJ_TILE, SUBLANES, conv_dim)[:, -1, SUBLANES - (conv_w - 1):])
        outs["pss"].append(ssm_p)

        qkva_s, dn_s, z_s, g_s, ba_s = _inproj(
            x_s, sh1s, sc1s, tabs_s, w_in_l, cuts, per_token_mod=True, tiles_per_seq=1, act_dtype=F32,
            window=window, kv_cols=2 * kd, half_rot=rot_dim // 2)
        pad_l = lambda a: jnp.pad(a.reshape(n_s, l_s, a.shape[-1]), ((0, 0), (0, l_pad - l_s), (0, 0)))
        attn_s, wk_s, wv_s = _attn_sample(pad_l(qkva_s), state_win_k[l].reshape(n_s, window, kd),
                                          state_win_v[l].reshape(n_s, window, kd), attn_sinks[l], n_q, n_kv, hd, l_s)
        attn_s = attn_s[:, :l_s].reshape(t_s, qd)
        ba_s3 = pad_l(ba_s)
        bat_s = ba_s3[:, :, :SUBLANES].transpose(0, 2, 1).reshape(n_s, 1, SUBLANES, l_pad)
        cs0 = jnp.pad(state_conv[l], ((0, 0), (SUBLANES - (conv_w - 1), 0), (0, 0)))
        o_s, ssm_s = _deltanet(pad_l(dn_s), pad_l(z_s), ba_s3, bat_s, cs0, state_ssm[l], w_conv[l], hp, nw,
                               chunk=l_pad, l_real=l_s, nb=8)
        o_s = o_s[:, :l_s].reshape(t_s, vdim)
        outs["swk"].append(wk_s.reshape(n_s, window, n_kv, hd))
        outs["swv"].append(wv_s.reshape(n_s, window, n_kv, hd))
        outs["scv"].append(jnp.concatenate([state_conv[l], dn_s.reshape(n_s, l_s, conv_dim)], axis=1)[:, -(conv_w - 1):])
        outs["sss"].append(ssm_s)

        wr = jnp.pad(w_router[l], ((0, 0), (0, LANES - n_e)))
        br = jnp.pad(b_router[l], (0, LANES - n_e), constant_values=NEG_BIG).reshape(1, LANES)
        wr_hi = wr.astype(BF16)
        wr_lo = (wr - wr_hi.astype(F32)).astype(BF16)
        wts = (w_proj_attn[l].astype(BF16), w_proj_dn[l].astype(BF16), w_out[l].astype(BF16),
               ln1_w[l].reshape(1, d), ln1_b[l].reshape(1, d), wr_hi, wr_lo, br)
        res_p = _outproj(attn_p, o_p.reshape(t_p, vdim), g_p, x_p, gt1p, sh2p, sc2p, wts, None,
                         per_token_mod=False, tiles_per_seq=tps, t_total=t_all, tile_off=0, alpha=alpha)
        x1, pos, gate, cnt = _outproj(attn_s, o_s, g_s, x_s, gt1s, sh2s, sc2s, wts, res_p, per_token_mod=True,
                                      tiles_per_seq=1, t_total=t_all, tile_off=n_ptiles, alpha=alpha)

        bm = EXPERT_ROWS
        nt_all = t_all // tm
        cnt8 = (cnt[:, 0, :n_e].astype(I32) + RUN_ALIGN - 1) // RUN_ALIGN * RUN_ALIGN
        tot = jnp.sum(cnt8, axis=0)
        padded = (tot + bm - 1) // bm * bm
        pad_end = jnp.cumsum(padded)
        pad_start = pad_end - padded
        gbase = pad_start[None, :] + jnp.cumsum(cnt8, axis=0) - cnt8
        toff = jnp.cumsum(cnt8, axis=1) - cnt8
        n_local = -(-(tm * TOP_K + n_e * (RUN_ALIGN - 1)) // LANES) * LANES
        n_rows = -(-(t_all * TOP_K + nt_all * n_e * (RUN_ALIGN - 1) + n_e * (bm - 1)) // bm) * bm
        nblk = n_rows // bm
        n_used = jnp.maximum(pad_end[-1:] // bm, 1).astype(I32)
        block_e = jnp.minimum(jnp.sum(pad_end[None, :] <= (jnp.arange(nblk, dtype=I32) * bm)[:, None], axis=1),
                              n_e - 1).astype(I32)
        flat = lambda a: a.astype(I32).reshape(nt_all * n_e)
        run_tables = (flat(toff), flat(cnt8), flat(gbase))
        tail_tables = ((pad_start + tot).astype(I32), (padded - tot).astype(I32))

        xs = _dispatch(run_tables + tail_tables, x1, sh2p, sc2p, sh2s, sc2s, pos, n_rows=n_rows, n_local=n_local,
                       n_ptiles=n_ptiles, tiles_per_seq=tps, n_experts=n_e)
        yb = _experts(block_e, n_used, xs, w_gu[l], b_gu[l], w_down[l], b_down[l])
        x_p, x_s = _combine(run_tables, x1, gt2p, gt2s, gate, pos, ln2_w[l].reshape(1, d), ln2_b[l].reshape(1, d),
                            yb, n_local=n_local, n_ptiles=n_ptiles, tiles_per_seq=tps, n_experts=n_e, alpha=alpha)

    st = lambda k: outs[k][0][None] if depth == 1 else jnp.stack(outs[k])
    return (x_p.reshape(n_p, seq, d), x_s.reshape(n_s, l_s, d), st("pwk"), st("pwv"), st("pcv"), st("pss"),
            st("swk"), st("swv"), st("scv"), st("sss"))
```

```python
import functools

import numpy as np
import jax
import jax.numpy as jnp
from jax import lax
from jax.experimental import pallas as pl
from jax.experimental.pallas import tpu as pltpu

F32 = jnp.float32
BF16 = jnp.bfloat16
I32 = jnp.int32
U32 = jnp.uint32

PAST_LEN = 16384
ROPE_THETA = 500000.0
TOP_K = 4
SWIGLU_LIMIT = 7.0
SWIGLU_ALPHA = 1.702
DN_CHUNK = 64
LN_EPS = 1e-5
RMS_EPS = 1e-6
L2_EPS = 1e-6

LANES = 128
SUBLANES = 8
VMEM_LIMIT_BYTES = 56 * 1024 * 1024

TOKEN_TILE = 256
INPROJ_TILE = 512
EXPERT_ROWS = 512
RUN_ALIGN = SUBLANES
OUTPROJ_SUBTILES = 2
N_ROUTE_BUFS = 4
NEG_BIG = -1e30


def _cparams(*sem):
    return pltpu.CompilerParams(dimension_semantics=sem, vmem_limit_bytes=VMEM_LIMIT_BYTES)


def _silu(x):
    return x * jax.nn.sigmoid(x)


def _bdot(a, b):
    return jnp.dot(a.astype(BF16), b.astype(BF16), preferred_element_type=F32)


def _pack_halves(x):
    n = x.shape[1] // 2
    lo = lax.bitcast_convert_type(x[:, :n], U32)
    hi = lax.bitcast_convert_type(x[:, n:], U32)
    return (hi & jnp.uint32(0xFFFF0000)) | (lo >> 16)


def _unpack_halves(w):
    lo = lax.bitcast_convert_type(w << 16, F32).astype(BF16)
    hi = lax.bitcast_convert_type(w & jnp.uint32(0xFFFF0000), F32).astype(BF16)
    return lo, hi


def _ada_body(c_ref, w_ref, b_ref, o_ref):
    o_ref[...] = _bdot(_silu(c_ref[...]), w_ref[...]) + b_ref[...]


def _ada(c_all, w_ada, b_ada):
    n, d = c_all.shape
    dout = w_ada.shape[1]
    tn = d
    return pl.pallas_call(
        _ada_body,
        grid=(dout // tn,),
        in_specs=[pl.BlockSpec((n, d), lambda j: (0, 0)),
                  pl.BlockSpec((d, tn), lambda j: (0, j)),
                  pl.BlockSpec((1, tn), lambda j: (0, j))],
        out_specs=pl.BlockSpec((n, tn), lambda j: (0, j)),
        out_shape=jax.ShapeDtypeStruct((n, dout), F32),
        compiler_params=_cparams("arbitrary"),
        name="ada",
    )(c_all, w_ada, b_ada.reshape(1, dout))


def _inproj_body(x_ref, sh_ref, sc_ref, ct_ref, s1_ref, s2_ref, w_ref,
                 a_ref, dn_ref, z_ref, g_ref, ba_ref, *win_refs, cuts, n_rot_chunks, half_rot, window):
    h = (x_ref[...] * (1.0 + sc_ref[...]) + sh_ref[...]).astype(BF16)

    def mm(lo, hi):
        return jnp.dot(h, w_ref[:, lo:hi], preferred_element_type=F32)

    c_a, c_dn, c_z, c_g, c_ba = cuts
    qkv = mm(0, c_a)
    ct, s1, s2 = ct_ref[...], s1_ref[...], s2_ref[...]
    cols = []
    for c in range(n_rot_chunks):
        xc = qkv[:, c * LANES:(c + 1) * LANES]
        cols.append(xc * ct + pltpu.roll(xc, LANES - half_rot, 1) * s1 + pltpu.roll(xc, half_rot, 1) * s2)
    cols.append(qkv[:, n_rot_chunks * LANES:])
    rot = jnp.concatenate(cols, axis=1)
    a_ref[...] = rot.astype(a_ref.dtype)
    dn = mm(c_a, c_dn)
    dn_ref[...] = dn.astype(dn_ref.dtype)
    z_ref[...] = mm(c_dn, c_z).astype(z_ref.dtype)
    g_ref[...] = mm(c_z, c_g).astype(g_ref.dtype)
    ba_ref[...] = mm(c_g, c_ba)
    if win_refs:
        tail_ref, kvw_ref = win_refs
        tm = dn.shape[0]
        tail_ref[...] = dn[tm - SUBLANES:, :]
        kvw_ref[...] = rot[tm - window:, n_rot_chunks * LANES - LANES:]


def _inproj(x, sh, sc, tabs, w_perm, cuts, *, per_token_mod, tiles_per_seq, act_dtype, window, kv_cols, half_rot):
    t, d = x.shape
    tm = INPROJ_TILE
    nt = t // tm
    assert t % tm == 0 and tm >= window and (per_token_mod or nt % tiles_per_seq == 0)
    c_a, c_dn, c_z, c_g, c_ba = cuts
    n_rot_chunks = (c_a - kv_cols // 2) // LANES
    if per_token_mod:
        mod_spec = pl.BlockSpec((tm, d), lambda i: (i, 0))
        tab_spec = pl.BlockSpec((tm, LANES), lambda i: (0, 0))
    else:
        mod_spec = pl.BlockSpec((None, 1, d), lambda i: (i // tiles_per_seq, 0, 0))
        tab_spec = pl.BlockSpec((tm, LANES), lambda i: (i % tiles_per_seq, 0))
    out_shape = [jax.ShapeDtypeStruct((t, c_a), act_dtype),
                 jax.ShapeDtypeStruct((t, c_dn - c_a), act_dtype),
                 jax.ShapeDtypeStruct((t, c_z - c_dn), act_dtype),
                 jax.ShapeDtypeStruct((t, c_g - c_z), act_dtype),
                 jax.ShapeDtypeStruct((t, c_ba - c_g), F32)]
    out_specs = [pl.BlockSpec((tm, s.shape[1]), lambda i: (i, 0)) for s in out_shape]
    with_win = not per_token_mod
    if with_win:
        n_seq = nt // tiles_per_seq
        out_shape += [jax.ShapeDtypeStruct((nt, SUBLANES, c_dn - c_a), F32),
                      jax.ShapeDtypeStruct((n_seq, window, kv_cols), F32)]
        out_specs += [pl.BlockSpec((None, SUBLANES, c_dn - c_a), lambda i: (i, 0, 0)),
                      pl.BlockSpec((None, window, kv_cols), lambda i: (i // tiles_per_seq, 0, 0))]
    body = functools.partial(_inproj_body, cuts=cuts, n_rot_chunks=n_rot_chunks, half_rot=half_rot, window=window)
    return pl.pallas_call(
        body,
        grid=(nt,),
        in_specs=[pl.BlockSpec((tm, d), lambda i: (i, 0)), mod_spec, mod_spec,
                  tab_spec, tab_spec, tab_spec,
                  pl.BlockSpec((d, c_ba), lambda i: (0, 0))],
        out_specs=out_specs,
        out_shape=out_shape,
        compiler_params=_cparams("arbitrary"),
        name="inproj",
    )(x, sh, sc, *tabs, w_perm)


def _softmax_sink_pv(scores, valid, sinks, values):
    ms = [jnp.where(valid, s, -jnp.inf) for s in scores]
    m = [jnp.maximum(jnp.max(x, axis=-1, keepdims=True), sk) for x, sk in zip(ms, sinks)]
    p = [jnp.exp(x - mi) for x, mi in zip(ms, m)]
    den = [jnp.sum(pi, axis=-1, keepdims=True) + jnp.exp(sk - mi) for pi, sk, mi in zip(p, sinks, m)]
    return [jnp.dot((pi / di).astype(BF16), v, preferred_element_type=F32) for pi, di, v in zip(p, den, values)]


def _attn_prompt_body(sink_ref, q_ref, kvp_ref, kvc_ref, o_ref, *, n_q, n_kv, hd, window):
    j = pl.program_id(1)
    group = n_q // n_kv
    scale = hd ** -0.5
    assert np.log2(scale) == int(np.log2(scale))
    r = lax.broadcasted_iota(I32, (window, 2 * window), 0)
    c = lax.broadcasted_iota(I32, (window, 2 * window), 1)
    rel = window + r - c
    band = (rel >= 0) & (rel < window)
    kv_cur = kvc_ref[...]
    kv_first = jnp.concatenate([kvp_ref[...], kv_cur[:window]], axis=0)
    for qb, (kv, valid) in enumerate(((kv_first, band & ((c >= window) | (j > 0))), (kv_cur, band))):
        q = q_ref[qb * window:(qb + 1) * window, :] * scale
        scores = [lax.dot_general(q[:, h * hd:(h + 1) * hd], kv[:, (h // group) * hd:(h // group + 1) * hd],
                                  (((1,), (1,)), ((), ())), preferred_element_type=F32) for h in range(n_q)]
        values = [kv[:, (n_kv + h // group) * hd:(n_kv + h // group + 1) * hd] for h in range(n_q)]
        outs = _softmax_sink_pv(scores, valid, [sink_ref[h] for h in range(n_q)], values)
        o_ref[qb * window:(qb + 1) * window, :] = jnp.concatenate(outs, axis=1).astype(o_ref.dtype)


def _attn_prompt(qkva, sinks, n_seq, seq, n_q, n_kv, hd, window):
    qd, kvd = n_q * hd, 2 * n_kv * hd
    x3 = qkva.reshape(n_seq, seq, qd + kvd)
    nb = seq // window
    assert nb % 2 == 0 and qd % kvd == 0
    kv_blk = qd // kvd
    body = functools.partial(_attn_prompt_body, n_q=n_q, n_kv=n_kv, hd=hd, window=window)
    out = pl.pallas_call(
        body,
        grid=(n_seq, nb // 2),
        in_specs=[pl.BlockSpec(memory_space=pltpu.SMEM),
                  pl.BlockSpec((None, 2 * window, qd), lambda n, j: (n, j, 0)),
                  pl.BlockSpec((None, window, kvd), lambda n, j: (n, jnp.maximum(2 * j - 1, 0), kv_blk)),
                  pl.BlockSpec((None, 2 * window, kvd), lambda n, j: (n, j, kv_blk))],
        out_specs=pl.BlockSpec((None, 2 * window, qd), lambda n, j: (n, j, 0)),
        out_shape=jax.ShapeDtypeStruct((n_seq, seq, qd), BF16),
        compiler_params=_cparams("arbitrary", "arbitrary"),
        name="attn_prompt",
    )(sinks, x3, x3, x3)
    return out.reshape(n_seq * seq, qd)


def _attn_sample_body(sink_ref, q_ref, wk_ref, wv_ref, o_ref, wko_ref, wvo_ref,
                      *, bs, n_q, n_kv, hd, window, l_new, l_pad):
    group = n_q // n_kv
    qd = n_q * hd
    kd = n_kv * hd
    rows = group * l_pad
    r = lax.broadcasted_iota(I32, (rows, window + l_pad), 0) % l_pad
    c = lax.broadcasted_iota(I32, (rows, window + l_pad), 1)
    rel = window + r - c
    valid = (rel >= 0) & (rel < window) & (c < window + l_new)
    sinks = [jnp.concatenate([jnp.full((l_pad, 1), sink_ref[kvh * group + g], F32) for g in range(group)], axis=0)
             for kvh in range(n_kv)]
    qs, ks, vs = [], [], []
    for b in range(bs):
        x = q_ref[b]
        k_new = x[:, qd:qd + kd]
        v_new = x[:, qd + kd:]
        wko_ref[b, 0:window - l_new, :] = wk_ref[b, l_new:window, :]
        wko_ref[b, window - l_new:window, :] = k_new[0:l_new, :]
        wvo_ref[b, 0:window - l_new, :] = wv_ref[b, l_new:window, :]
        wvo_ref[b, window - l_new:window, :] = v_new[0:l_new, :]
        k_all = jnp.concatenate([wk_ref[b], k_new], axis=0).astype(BF16)
        v_all = jnp.concatenate([wv_ref[b], v_new], axis=0).astype(BF16)
        for kvh in range(n_kv):
            qs.append(jnp.concatenate([x[:, (kvh * group + g) * hd:(kvh * group + g + 1) * hd]
                                       for g in range(group)], axis=0).astype(BF16))
            ks.append(k_all[:, kvh * hd:(kvh + 1) * hd])
            vs.append(v_all[:, kvh * hd:(kvh + 1) * hd])
    n = len(qs)
    s = [lax.dot_general(qs[i], ks[i], (((1,), (1,)), ((), ())), preferred_element_type=F32) * (hd ** -0.5)
         for i in range(n)]
    o = _softmax_sink_pv(s, valid, [sinks[i % n_kv] for i in range(n)], vs)
    for b in range(bs):
        outs = [o[b * n_kv + kvh][g * l_pad:(g + 1) * l_pad, :] for kvh in range(n_kv) for g in range(group)]
        o_ref[b] = jnp.concatenate(outs, axis=1).astype(o_ref.dtype)


def _attn_sample(qkva_pad, win_k, win_v, sinks, n_q, n_kv, hd, l_new):
    n, l_pad, width = qkva_pad.shape
    window, kd = win_k.shape[1], win_k.shape[2]
    qd = n_q * hd
    bs = 8
    body = functools.partial(_attn_sample_body, bs=bs, n_q=n_q, n_kv=n_kv, hd=hd, window=window,
                             l_new=l_new, l_pad=l_pad)
    return pl.pallas_call(
        body,
        grid=(n // bs,),
        in_specs=[pl.BlockSpec(memory_space=pltpu.SMEM),
                  pl.BlockSpec((bs, l_pad, width), lambda i: (i, 0, 0)),
                  pl.BlockSpec((bs, window, kd), lambda i: (i, 0, 0)),
                  pl.BlockSpec((bs, window, kd), lambda i: (i, 0, 0))],
        out_specs=[pl.BlockSpec((bs, l_pad, qd), lambda i: (i, 0, 0)),
                   pl.BlockSpec((bs, window, kd), lambda i: (i, 0, 0)),
                   pl.BlockSpec((bs, window, kd), lambda i: (i, 0, 0))],
        out_shape=[jax.ShapeDtypeStruct((n, l_pad, qd), BF16),
                   jax.ShapeDtypeStruct((n, window, kd), F32),
                   jax.ShapeDtypeStruct((n, window, kd), F32)],
        compiler_params=_cparams("arbitrary"),
        name="attn_sample",
    )(sinks, qkva_pad, win_k, win_v)


def _split_bf16(x):
    hi = x.astype(BF16)
    return hi, (x - hi.astype(F32)).astype(BF16)


def _tdot(a, b):
    ah, al = _split_bf16(a)
    bh, bl = _split_bf16(b)
    m = a.shape[0]
    t = jnp.dot(jnp.concatenate([ah, al], axis=0), bh, preferred_element_type=F32)
    return t[:m] + t[m:] + jnp.dot(ah, bl, preferred_element_type=F32)


def _dn_body(qkv_ref, z_ref, ba_ref, bat_ref, cs0_ref, s0_ref, wc_ref, hp_ref, nw_ref,
             o_ref, s_ref, xbuf, *, nb, chunk, heads, dk, dv, l_real, conv_w):
    c_idx = pl.program_id(1)
    hc = SUBLANES

    @pl.when(c_idx == 0)
    def _():
        xbuf[:, 0:hc, :] = cs0_ref[...]
        s_ref[...] = s0_ref[...]

    qk_dim = heads * dk
    row = lax.broadcasted_iota(I32, (chunk, chunk), 0)
    col = lax.broadcasted_iota(I32, (chunk, chunk), 1)
    incl = row >= col
    strict = row > col
    eye = (row == col).astype(F32)
    valid_c = row[:, 0:1] < l_real
    valid_r = col[0:1, :] < l_real
    n_levels = max(1, int(np.ceil(np.log2(chunk))))
    wc = wc_ref[...]
    hp = hp_ref[...]
    neg_exp_alog = -jnp.exp(hp[0:1, :])
    dt_bias = hp[1:2, :]
    nw = nw_ref[...]
    chains = [(b, h) for b in range(nb) for h in range(heads)]
    n = len(chains)

    ys = []
    for b in range(nb):
        xbuf[b, hc:hc + chunk, :] = qkv_ref[b].astype(F32)
        y = xbuf[b, hc:hc + chunk, :] * wc[conv_w - 1:conv_w, :]
        for j in range(conv_w - 1):
            off = hc - (conv_w - 1) + j
            y = y + xbuf[b, off:off + chunk, :] * wc[j:j + 1, :]
        ys.append(_silu(y))
        xbuf[b, 0:hc, :] = xbuf[b, chunk:chunk + hc, :]

    qn, kn, kb, vb, decay, e_gc, e_rest, e_last = [], [], [], [], [], [], [], []
    for b, h in chains:
        y = ys[b]
        qh = y[:, h * dk:(h + 1) * dk]
        kh = y[:, qk_dim + h * dk:qk_dim + (h + 1) * dk]
        vh = y[:, 2 * qk_dim + h * dv:2 * qk_dim + (h + 1) * dv]
        ba = ba_ref[b]
        bat = bat_ref[b]
        ne = neg_exp_alog[:, h:h + 1]
        db = dt_bias[:, h:h + 1]
        beta = jnp.where(valid_c, jax.nn.sigmoid(ba[:, h:h + 1]), 0.0)
        g_col = jnp.where(valid_c, ne * jax.nn.softplus(ba[:, heads + h:heads + h + 1] + db), 0.0)
        g_row = jnp.where(valid_r, ne * jax.nn.softplus(bat[heads + h:heads + h + 1, :] + db), 0.0)
        gc_col = jnp.sum(jnp.where(incl, g_row, 0.0), axis=1, keepdims=True)
        gc_row = jnp.sum(jnp.where(row <= col, g_col, 0.0), axis=0, keepdims=True)
        g_last = gc_col[chunk - 1:chunk, :]
        q_ = qh * lax.rsqrt(jnp.sum(qh * qh, -1, keepdims=True) + L2_EPS) * (dk ** -0.5)
        k_ = kh * lax.rsqrt(jnp.sum(kh * kh, -1, keepdims=True) + L2_EPS)
        qn.append(q_)
        kn.append(k_)
        kb.append(k_ * beta)
        vb.append(vh * beta)
        decay.append(jnp.where(incl, jnp.exp(gc_col - gc_row), 0.0))
        e_gc.append(jnp.exp(gc_col))
        e_rest.append(jnp.exp(g_last - gc_col))
        e_last.append(jnp.exp(g_last))

    sc = [lax.dot_general(jnp.concatenate([qn[i], kb[i]], axis=0).astype(BF16), kn[i].astype(BF16),
                          (((1,), (1,)), ((), ())), preferred_element_type=F32) for i in range(n)]
    qk = [sc[i][:chunk] * decay[i] for i in range(n)]
    p = [jnp.where(strict, -(sc[i][chunk:] * decay[i]), 0.0) for i in range(n)]
    t_inv = [eye + p[i] for i in range(n)]
    if n_levels > 1:
        p = [_tdot(p[i], p[i]) for i in range(n)]
    for lvl in range(1, n_levels):
        if lvl < n_levels - 1:
            yp = [_tdot(jnp.concatenate([t_inv[i], p[i]], axis=0), p[i]) for i in range(n)]
            t_inv = [t_inv[i] + yp[i][:chunk] for i in range(n)]
            p = [yp[i][chunk:] for i in range(n)]
        else:
            t_inv = [t_inv[i] + _tdot(t_inv[i], p[i]) for i in range(n)]
    sol = [_tdot(t_inv[i], jnp.concatenate([vb[i], kb[i] * e_gc[i]], axis=1)) for i in range(n)]
    s_old = [s_ref[b, h] for b, h in chains]
    wq = [_bdot(jnp.concatenate([sol[i][:, dv:], qn[i] * e_gc[i]], axis=0), s_old[i]) for i in range(n)]
    v_new = [sol[i][:, :dv] - wq[i][:chunk] for i in range(n)]
    o = [wq[i][chunk:] + _bdot(qk[i], v_new[i]) for i in range(n)]
    for i, (b, h) in enumerate(chains):
        s_ref[b, h] = s_old[i] * e_last[i] + lax.dot_general(
            (kn[i] * e_rest[i]).astype(BF16), v_new[i].astype(BF16), (((0,), (0,)), ((), ())),
            preferred_element_type=F32)
    for b in range(nb):
        zt = z_ref[b].astype(F32)
        outs = []
        for h in range(heads):
            oi = o[b * heads + h]
            on = oi * lax.rsqrt(jnp.mean(oi * oi, -1, keepdims=True) + RMS_EPS) * nw
            outs.append(on * _silu(zt[:, h * dv:(h + 1) * dv]))
        o_ref[b] = jnp.concatenate(outs, axis=1).astype(o_ref.dtype)


def _deltanet(qkv, z, ba, bat, cs0, s0, w_conv, hp, norm_w, *, chunk, l_real, nb):
    n, l, conv_dim = qkv.shape
    heads, dk, dv = s0.shape[1:]
    nc = l // chunk
    assert n % nb == 0 and l % chunk == 0
    conv_w = w_conv.shape[0]
    hist = pltpu.VMEM((nb, SUBLANES + chunk, conv_dim), F32)
    body = functools.partial(_dn_body, nb=nb, chunk=chunk, heads=heads, dk=dk, dv=dv, l_real=l_real, conv_w=conv_w)
    return pl.pallas_call(
        body,
        grid=(n // nb, nc),
        in_specs=[pl.BlockSpec((nb, chunk, conv_dim), lambda i, c: (i, c, 0)),
                  pl.BlockSpec((nb, chunk, heads * dv), lambda i, c: (i, c, 0)),
                  pl.BlockSpec((nb, chunk, LANES), lambda i, c: (i, c, 0)),
                  pl.BlockSpec((nb, None, SUBLANES, chunk), lambda i, c: (i, c, 0, 0)),
                  pl.BlockSpec((nb, SUBLANES, conv_dim), lambda i, c: (i, 0, 0)),
                  pl.BlockSpec((nb, heads, dk, dv), lambda i, c: (i, 0, 0, 0)),
                  pl.BlockSpec((conv_w, conv_dim), lambda i, c: (0, 0)),
                  pl.BlockSpec((SUBLANES, LANES), lambda i, c: (0, 0)),
                  pl.BlockSpec((1, dv), lambda i, c: (0, 0))],
        out_specs=[pl.BlockSpec((nb, chunk, heads * dv), lambda i, c: (i, c, 0)),
                   pl.BlockSpec((nb, heads, dk, dv), lambda i, c: (i, 0, 0, 0))],
        out_shape=[jax.ShapeDtypeStruct((n, l, heads * dv), BF16),
                   jax.ShapeDtypeStruct((n, heads, dk, dv), F32)],
        scratch_shapes=[hist],
        compiler_params=_cparams("arbitrary", "arbitrary"),
        name="deltanet",
    )(qkv, z, ba, bat, cs0, s0, w_conv, hp, norm_w)


def _layer_norm(r, w, b):
    mu = jnp.mean(r, -1, keepdims=True)
    var = jnp.mean(jnp.square(r - mu), -1, keepdims=True)
    return (r - mu) * lax.rsqrt(var + LN_EPS) * w + b


def _outproj_body(attn_ref, dn_ref, g_ref, x_ref, gt_ref, sh2_ref, sc2_ref, wpa_ref, wpd_ref, wo_ref,
                  lnw_ref, lnb_ref, wrh_ref, wrl_ref, br_ref, *refs, alpha, top_k, aliased):
    if aliased:
        refs = refs[N_ROUTE_BUFS:]
    x1_ref, pos_ref, gate_ref, cnt_ref = refs
    for s in range(cnt_ref.shape[0]):
        _outproj_tile(s, attn_ref, dn_ref, g_ref, x_ref, gt_ref, sh2_ref, sc2_ref, wpa_ref, wpd_ref, wo_ref, lnw_ref,
                      lnb_ref, wrh_ref, wrl_ref, br_ref, x1_ref, pos_ref, gate_ref, cnt_ref,
                      alpha=alpha, top_k=top_k)


def _outproj_tile(s, attn_ref, dn_ref, g_ref, x_ref, gt_ref, sh2_ref, sc2_ref, wpa_ref, wpd_ref, wo_ref, lnw_ref,
                  lnb_ref, wrh_ref, wrl_ref, br_ref, x1_ref, pos_ref, gate_ref, cnt_ref, *, alpha, top_k):
    d = x_ref.shape[1]
    tm = TOKEN_TILE
    rows = pl.ds(s * tm, tm)

    def mod(ref):
        return ref[...] if ref.shape[0] == 1 else ref[rows, :]

    g = g_ref[rows, :].astype(F32)
    pa = jnp.dot(attn_ref[rows, :], wpa_ref[...], preferred_element_type=F32)
    pd = jnp.dot(dn_ref[rows, :], wpd_ref[...], preferred_element_type=F32)
    merged = jax.nn.sigmoid(g[:, :d]) * pa + jax.nn.sigmoid(g[:, d:]) * pd
    mix = jnp.dot(merged.astype(BF16), wo_ref[...], preferred_element_type=F32)
    x1 = _layer_norm(alpha * x_ref[rows, :] + mod(gt_ref) * mix, lnw_ref[...], lnb_ref[...])
    x1_ref[rows, :] = x1
    h2 = x1 * (1.0 + mod(sc2_ref)) + mod(sh2_ref)
    h_hi, h_lo = _split_bf16(h2)
    lg = jnp.dot(jnp.concatenate([h_hi, h_lo], axis=0), wrh_ref[...], preferred_element_type=F32)
    logits = lg[:tm] + lg[tm:] + jnp.dot(h_hi, wrl_ref[...], preferred_element_type=F32) + br_ref[...]
    lane = lax.broadcasted_iota(I32, (tm, LANES), 1)
    lane_f = lane.astype(F32)
    vals, sels = [], []
    l = logits
    for _ in range(top_k):
        m = jnp.max(l, axis=1, keepdims=True)
        idx = jnp.min(jnp.where(l == m, lane_f, float(LANES)), axis=1, keepdims=True)
        sel = lane_f == idx
        vals.append(m)
        sels.append(sel)
        l = jnp.where(sel, -jnp.inf, l)
    ex = [jnp.exp(v - vals[0]) for v in vals]
    den = ex[0]
    for e in ex[1:]:
        den = den + e
    multi_hot = jnp.zeros((tm, LANES), F32)
    for sel in sels:
        multi_hot = multi_hot + jnp.where(sel, 1.0, 0.0)
    r_i = lax.broadcasted_iota(I32, (tm, tm), 0)
    c_i = lax.broadcasted_iota(I32, (tm, tm), 1)
    lower = jnp.where(r_i > c_i, 1.0, 0.0).astype(BF16)
    prefix = jnp.dot(lower, multi_hot.astype(BF16), preferred_element_type=F32)
    counts = jnp.sum(multi_hot, axis=0, keepdims=True)
    cnt_pad = jnp.floor((counts + (RUN_ALIGN - 1)) * (1.0 / RUN_ALIGN)) * RUN_ALIGN
    e_r = lax.broadcasted_iota(I32, (LANES, LANES), 0)
    e_c = lax.broadcasted_iota(I32, (LANES, LANES), 1)
    before = jnp.where(e_r < e_c, 1.0, 0.0).astype(BF16)
    run_off = jnp.dot(jnp.broadcast_to(cnt_pad, (SUBLANES, LANES)).astype(BF16), before,
                      preferred_element_type=F32)[0:1, :]
    g_out = jnp.zeros((tm, LANES), F32)
    p_out = jnp.zeros((tm, LANES), F32)
    for k in range(top_k):
        pos_k = jnp.sum(jnp.where(sels[k], prefix + run_off, 0.0), axis=1, keepdims=True)
        g_out = jnp.where(lane == k, ex[k] / den, g_out)
        p_out = jnp.where(lane == k, pos_k, p_out)
    pos_ref[rows, :] = p_out.astype(I32)
    gate_ref[rows, :] = g_out
    cnt_ref[s] = jnp.broadcast_to(counts, cnt_ref.shape[1:])


def _outproj(attn, dn, gates, x, gt, sh2, sc2, wts, bufs, *, per_token_mod, tiles_per_seq, t_total, tile_off, alpha):
    t, d = x.shape
    sub = OUTPROJ_SUBTILES
    tm = TOKEN_TILE * sub
    nt = t // tm
    assert t % tm == 0 and tile_off % sub == 0 and (per_token_mod or tiles_per_seq % sub == 0)
    wpa, wpd, wo, lnw, lnb, wrh, wrl, br = wts
    if per_token_mod:
        mod_spec = pl.BlockSpec((tm, d), lambda i: (i, 0))
    else:
        mod_spec = pl.BlockSpec((None, 1, d), lambda i: (i // (tiles_per_seq // sub), 0, 0))

    def row(width):
        return pl.BlockSpec((tm, width), lambda i: (i, 0))

    def full(a):
        return pl.BlockSpec(a.shape, lambda i: (0,) * a.ndim)

    aliased = bufs is not None
    in_specs = [row(attn.shape[1]), row(dn.shape[1]), row(gates.shape[1]), row(d), mod_spec, mod_spec, mod_spec,
                full(wpa), full(wpd), full(wo), full(lnw), full(lnb), full(wrh), full(wrl), full(br)]
    args = [attn, dn, gates, x, gt, sh2, sc2, wpa, wpd, wo, lnw, lnb, wrh, wrl, br]
    io_alias = {}
    if aliased:
        for k, bfr in enumerate(bufs):
            in_specs.append(pl.BlockSpec(memory_space=pl.ANY))
            io_alias[len(args)] = k
            args.append(bfr)
    step_off = tile_off // sub
    out_row = lambda width: pl.BlockSpec((tm, width), lambda i: (i + step_off, 0))
    out_shape = [jax.ShapeDtypeStruct((t_total, d), F32), jax.ShapeDtypeStruct((t_total, LANES), I32),
                 jax.ShapeDtypeStruct((t_total, LANES), F32),
                 jax.ShapeDtypeStruct((t_total // TOKEN_TILE, SUBLANES, LANES), F32)]
    assert len(out_shape) == N_ROUTE_BUFS and TOKEN_TILE <= 256
    out_specs = [out_row(d), out_row(LANES), out_row(LANES),
                 pl.BlockSpec((sub, SUBLANES, LANES), lambda i: (i + step_off, 0, 0))]
    body = functools.partial(_outproj_body, alpha=alpha, top_k=TOP_K, aliased=aliased)
    return pl.pallas_call(
        body,
        grid=(nt,),
        in_specs=in_specs,
        out_specs=out_specs,
        out_shape=out_shape,
        input_output_aliases=io_alias,
        compiler_params=_cparams("arbitrary"),
        name="outproj",
    )(*args)


def _select_mod(i, n_ptiles, seq_ref, tok_ref):
    return jnp.where(i < n_ptiles, seq_ref[...], tok_ref[...])


def _for_run(rows, local_off, global_off, fn):
    @pl.when(rows > 0)
    def _():
        fn(pl.multiple_of(local_off, RUN_ALIGN), pl.multiple_of(global_off, RUN_ALIGN),
           pl.multiple_of(rows, RUN_ALIGN))


def _for_each_run(i, n_experts, toff_ref, cnt8_ref, gbase_ref, fn):
    for e in range(n_experts):
        idx = i * n_experts + e
        _for_run(cnt8_ref[idx], toff_ref[idx], gbase_ref[idx], fn)


def _tile_rows(tile, n_experts, toff_ref, cnt8_ref):
    last = tile * n_experts + n_experts - 1
    return pl.multiple_of(toff_ref[last] + cnt8_ref[last], RUN_ALIGN)


def _scatter_matrix(pos_ref, n_local, values):
    pos = pos_ref[...]
    col = lax.broadcasted_iota(I32, (pos.shape[0], n_local), 1)
    out = jnp.zeros((pos.shape[0], n_local), F32)
    for k in range(TOP_K):
        out = jnp.where(col == pos[:, k:k + 1], 1.0 if values is None else values[:, k:k + 1], out)
    return out


def _dispatch_body(toff_ref, cnt8_ref, gbase_ref, tstart_ref, trows_ref,
                   x1_ref, shs_ref, scs_ref, sht_ref, sct_ref, pos_ref,
                   xs_ref, lbuf, zbuf, sem, *, n_tiles, n_ptiles, n_experts):
    i = pl.program_id(0)
    sc = _select_mod(i, n_ptiles, scs_ref, sct_ref)
    sh = _select_mod(i, n_ptiles, shs_ref, sht_ref)
    h2 = (x1_ref[...] * (1.0 + sc) + sh).astype(BF16)
    n_local = lbuf.shape[1]
    slot = i % 2

    def runs(tile, sl, act):
        def fn(lo, go, rows):
            act(pltpu.make_async_copy(lbuf.at[sl, pl.ds(lo, rows), :], xs_ref.at[pl.ds(go, rows), :], sem.at[sl]))
        _for_each_run(tile, n_experts, toff_ref, cnt8_ref, gbase_ref, fn)

    start = lambda cp: cp.start()
    wait = lambda cp: cp.wait()

    def wait_runs(tile, sl):
        rows = _tile_rows(tile, n_experts, toff_ref, cnt8_ref)
        pltpu.make_async_copy(lbuf.at[sl, pl.ds(0, rows), :], xs_ref.at[pl.ds(0, rows), :], sem.at[sl]).wait()

    @pl.when(i >= 2)
    def _():
        wait_runs(i - 2, slot)

    onehot = _scatter_matrix(pos_ref, n_local, None)
    srt = lax.dot_general(onehot.astype(BF16), h2, (((0,), (0,)), ((), ())), preferred_element_type=F32)
    lbuf[slot] = _pack_halves(srt)
    runs(i, slot, start)

    @pl.when(i == n_tiles - 1)
    def _():
        wait_runs(i - 1, 1 - slot)
        wait_runs(i, slot)
        zbuf[...] = jnp.zeros(zbuf.shape, zbuf.dtype)

        def zero_fill(e, act):
            _for_run(trows_ref[e], 0, tstart_ref[e], lambda lo, go, rows: act(pltpu.make_async_copy(
                zbuf.at[pl.ds(0, rows), :], xs_ref.at[pl.ds(go, rows), :], sem.at[0])))

        for e in range(n_experts):
            zero_fill(e, start)
        for e in range(n_experts):
            zero_fill(e, wait)


def _dispatch(tables, x1, sh_seq, sc_seq, sh_tok, sc_tok, pos, *, n_rows, n_local, n_ptiles, tiles_per_seq,
              n_experts):
    t, d = x1.shape
    tm = TOKEN_TILE
    nt = t // tm
    n_seq = sh_seq.shape[0]
    assert nt >= 2 and d % 2 == 0
    body = functools.partial(_dispatch_body, n_tiles=nt, n_ptiles=n_ptiles, n_experts=n_experts)
    seq_spec = pl.BlockSpec((None, 1, d), lambda i, *_: (jnp.minimum(i // tiles_per_seq, n_seq - 1), 0, 0))
    tok_spec = pl.BlockSpec((tm, d), lambda i, *_: (jnp.maximum(i - n_ptiles, 0), 0))
    return pl.pallas_call(
        body,
        grid_spec=pltpu.PrefetchScalarGridSpec(
            num_scalar_prefetch=len(tables),
            grid=(nt,),
            in_specs=[pl.BlockSpec((tm, d), lambda i, *_: (i, 0)),
                      seq_spec, seq_spec, tok_spec, tok_spec,
                      pl.BlockSpec((tm, LANES), lambda i, *_: (i, 0))],
            out_specs=pl.BlockSpec(memory_space=pl.ANY),
            scratch_shapes=[pltpu.VMEM((2, n_local, d // 2), U32), pltpu.VMEM((EXPERT_ROWS, d // 2), U32),
                            pltpu.SemaphoreType.DMA((2,))],
        ),
        out_shape=jax.ShapeDtypeStruct((n_rows, d // 2), U32),
        compiler_params=_cparams("arbitrary"),
        name="dispatch",
    )(*tables, x1, sh_seq, sc_seq, sh_tok, sc_tok, pos)


def _expert_body(be_ref, nu_ref, first_ref, next_ref, xs_ref, wgu_hbm, bgu_ref, wd_hbm, bd_ref, y_ref,
                 stage_gu, stage_dn, wgu_bf, wd_bf, sem, *, de):
    i = pl.program_id(0)
    used = i < nu_ref[0]
    expert = be_ref[i]

    def weights(e, act):
        act(pltpu.make_async_copy(wgu_hbm.at[e], stage_gu, sem.at[0]))
        act(pltpu.make_async_copy(wd_hbm.at[e], stage_dn, sem.at[1]))

    @pl.when(i == 0)
    def _():
        weights(expert, lambda cp: cp.start())

    @pl.when(used & (first_ref[i] == 1))
    def _():
        weights(expert, lambda cp: cp.wait())
        wgu_bf[...] = stage_gu[...].astype(BF16)
        wd_bf[...] = stage_dn[...].astype(BF16)
        nxt = next_ref[expert]

        @pl.when(nxt >= 0)
        def _():
            weights(nxt, lambda cp: cp.start())

    @pl.when(used)
    def _():
        x = jnp.concatenate(_unpack_halves(xs_ref[...]), axis=1)
        gu = jnp.dot(x, wgu_bf[...], preferred_element_type=F32) + bgu_ref[...]
        glu = jnp.minimum(gu[:, :de], SWIGLU_LIMIT)
        lin = jnp.clip(gu[:, de:], -SWIGLU_LIMIT, SWIGLU_LIMIT)
        act = glu * jax.nn.sigmoid(SWIGLU_ALPHA * glu) * (lin + 1.0)
        y = jnp.dot(act.astype(BF16), wd_bf[...], preferred_element_type=F32) + bd_ref[...]
        y_ref[...] = _pack_halves(y.astype(BF16).astype(F32))


def _experts(block_e, n_used, next_e, xs, w_gu, b_gu, w_down, b_down):
    p, dh = xs.shape
    bm = EXPERT_ROWS
    n_e, d, de2 = w_gu.shape
    de = de2 // 2
    nblk = p // bm

    def blk(i, be, nu, *_):
        return jnp.minimum(i, nu[0] - 1)

    first = jnp.concatenate([jnp.ones((1,), I32), (block_e[1:] != block_e[:-1]).astype(I32)])
    body = functools.partial(_expert_body, de=de)
    return pl.pallas_call(
        body,
        grid_spec=pltpu.PrefetchScalarGridSpec(
            num_scalar_prefetch=4,
            grid=(nblk,),
            in_specs=[pl.BlockSpec((bm, dh), lambda i, be, nu, *_: (blk(i, be, nu), 0)),
                      pl.BlockSpec(memory_space=pl.ANY),
                      pl.BlockSpec((None, 1, de2), lambda i, be, nu, *_: (be[blk(i, be, nu)], 0, 0)),
                      pl.BlockSpec(memory_space=pl.ANY),
                      pl.BlockSpec((None, 1, d), lambda i, be, nu, *_: (be[blk(i, be, nu)], 0, 0))],
            out_specs=pl.BlockSpec((bm, dh), lambda i, be, nu, *_: (blk(i, be, nu), 0)),
            scratch_shapes=[pltpu.VMEM((d, de2), F32), pltpu.VMEM((de, d), F32),
                            pltpu.VMEM((d, de2), BF16), pltpu.VMEM((de, d), BF16), pltpu.SemaphoreType.DMA((2,))],
        ),
        out_shape=jax.ShapeDtypeStruct((p, dh), U32),
        compiler_params=_cparams("arbitrary"),
        name="experts",
    )(block_e, n_used, first, next_e, xs, w_gu, b_gu.reshape(n_e, 1, de2), w_down, b_down.reshape(n_e, 1, d))


def _combine_body(toff_ref, cnt8_ref, gbase_ref, x1_ref, gts_ref, gtt_ref, gate_ref, pos_ref,
                  lnw_ref, lnb_ref, yb_ref, yp_ref, ys_ref, ybuf, sem, *, top_k, n_ptiles, n_experts, alpha):
    i = pl.program_id(0)
    n_tiles = pl.num_programs(0)
    tm = x1_ref.shape[0]
    n_local = ybuf.shape[1]
    slot = i % 2

    def fetch(tile, sl, act):
        def fn(lo, go, rows):
            act(pltpu.make_async_copy(yb_ref.at[pl.ds(go, rows), :], ybuf.at[sl, pl.ds(lo, rows), :], sem.at[sl]))
        _for_each_run(tile, n_experts, toff_ref, cnt8_ref, gbase_ref, fn)

    def start_fetch(tile, sl):
        ybuf[sl, tm * top_k:, :] = jnp.zeros((n_local - tm * top_k, ybuf.shape[2]), ybuf.dtype)
        fetch(tile, sl, lambda cp: cp.start())

    @pl.when(i == 0)
    def _():
        start_fetch(i, slot)

    @pl.when(i + 1 < n_tiles)
    def _():
        start_fetch(i + 1, 1 - slot)

    weights = _scatter_matrix(pos_ref, n_local, gate_ref[...]).astype(BF16)
    rows = _tile_rows(i, n_experts, toff_ref, cnt8_ref)
    pltpu.make_async_copy(yb_ref.at[pl.ds(0, rows), :], ybuf.at[slot, pl.ds(0, rows), :], sem.at[slot]).wait()
    y_lo, y_hi = _unpack_halves(ybuf[slot])
    ff = jnp.concatenate([jnp.dot(weights, y_lo, preferred_element_type=F32),
                          jnp.dot(weights, y_hi, preferred_element_type=F32)], axis=1)
    gt = _select_mod(i, n_ptiles, gts_ref, gtt_ref)
    y = _layer_norm(alpha * x1_ref[...] + gt * ff, lnw_ref[...], lnb_ref[...])

    @pl.when(i < n_ptiles)
    def _():
        yp_ref[...] = y

    @pl.when(i >= n_ptiles)
    def _():
        ys_ref[...] = y


def _combine(tables, x1, gt_seq, gt_tok, gate, pos, lnw, lnb, yb, *, n_local, n_ptiles, tiles_per_seq, n_experts,
             alpha):
    t, d = x1.shape
    tm = TOKEN_TILE
    nt = t // tm
    n_seq = gt_seq.shape[0]
    body = functools.partial(_combine_body, top_k=TOP_K, n_ptiles=n_ptiles, n_experts=n_experts, alpha=alpha)
    lane_spec = pl.BlockSpec((tm, LANES), lambda i, *_: (i, 0))
    return pl.pallas_call(
        body,
        grid_spec=pltpu.PrefetchScalarGridSpec(
            num_scalar_prefetch=len(tables),
            grid=(nt,),
            in_specs=[pl.BlockSpec((tm, d), lambda i, *_: (i, 0)),
                      pl.BlockSpec((None, 1, d), lambda i, *_: (jnp.minimum(i // tiles_per_seq, n_seq - 1), 0, 0)),
                      pl.BlockSpec((tm, d), lambda i, *_: (jnp.maximum(i - n_ptiles, 0), 0)),
                      lane_spec,
                      pl.BlockSpec((tm, LANES), lambda i, *_: (i, 0)),
                      pl.BlockSpec((1, d), lambda i, *_: (0, 0)),
                      pl.BlockSpec((1, d), lambda i, *_: (0, 0)),
                      pl.BlockSpec(memory_space=pl.ANY)],
            out_specs=[pl.BlockSpec((tm, d), lambda i, *_: (jnp.minimum(i, n_ptiles - 1), 0)),
                       pl.BlockSpec((tm, d), lambda i, *_: (jnp.maximum(i - n_ptiles, 0), 0))],
            scratch_shapes=[pltpu.VMEM((2, n_local, d // 2), U32), pltpu.SemaphoreType.DMA((2,))],
        ),
        out_shape=[jax.ShapeDtypeStruct((n_ptiles * tm, d), F32),
                   jax.ShapeDtypeStruct(((nt - n_ptiles) * tm, d), F32)],
        compiler_params=_cparams("arbitrary"),
        name="combine",
    )(*tables, x1, gt_seq, gt_tok, gate, pos, lnw, lnb, yb)


def _rotary_tables(pos, hd, rot_dim):
    half = rot_dim // 2
    inv_freq = jnp.power(jnp.float32(ROPE_THETA), -jnp.arange(half, dtype=F32) * (2.0 / rot_dim))
    ang = pos.astype(F32)[:, None] * inv_freq[None, :]
    cos, sin = jnp.cos(ang), jnp.sin(ang)
    n = pos.shape[0]
    ones = jnp.ones((n, hd - rot_dim), F32)
    zeros = jnp.zeros((n, hd - rot_dim), F32)
    zh = jnp.zeros((n, half), F32)
    ct = jnp.concatenate([cos, cos, ones], axis=1)
    s1 = jnp.concatenate([-sin, zh, zeros], axis=1)
    s2 = jnp.concatenate([zh, sin, zeros], axis=1)
    reps = LANES // hd
    return tuple(jnp.tile(a, (1, reps)) for a in (ct, s1, s2))


def kernel(x_prompt, x_sample, state_win_k, state_win_v, state_conv, state_ssm, c_prompt, c_sample, w_ada, b_ada, w_in, attn_sinks, w_conv, dn_a_log, dn_dt_bias, dn_norm_w, w_proj_attn, w_proj_dn, w_out, ln1_w, ln1_b, w_router, b_router, w_gu, b_gu, w_down, b_down, ln2_w, ln2_b):
    n_p, seq, d = x_prompt.shape
    n_s, l_s, _ = x_sample.shape
    depth = w_ada.shape[0]
    window, n_kv, hd = state_win_k.shape[2:]
    n_q = attn_sinks.shape[1]
    heads, dk, dv = state_ssm.shape[2:]
    conv_w, conv_dim = w_conv.shape[1:]
    n_e = w_router.shape[2]
    qd, kd = n_q * hd, n_kv * hd
    vdim = heads * dv
    rot_dim = hd // 4
    alpha = float((2 * depth) ** 0.25)
    tm = TOKEN_TILE
    t_p, t_s = n_p * seq, n_s * l_s
    t_all = t_p + t_s
    tps = seq // tm
    n_ptiles = t_p // tm
    l_pad = SUBLANES
    assert seq % tm == 0 and t_s % tm == 0 and tm % l_s == 0 and l_s <= l_pad and l_s >= conv_w - 1
    assert 2 * heads <= SUBLANES and n_e <= LANES and hd * 2 == LANES and rot_dim % 2 == 0

    sizes = [qd, kd, kd, conv_dim, vdim, heads, heads, d, d]
    offs = [int(o) for o in np.concatenate([[0], np.cumsum(sizes)])]
    cuts = (qd + 2 * kd, qd + 2 * kd + conv_dim, qd + 2 * kd + conv_dim + vdim, qd + 2 * kd + conv_dim + vdim + 2 * d)
    cuts = cuts + (cuts[-1] + LANES,)

    tabs_p = _rotary_tables(jnp.arange(seq, dtype=I32), hd, rot_dim)
    tabs_s = tuple(jnp.tile(a, (INPROJ_TILE // l_s, 1))
                   for a in _rotary_tables(PAST_LEN + jnp.arange(l_s, dtype=I32), hd, rot_dim))

    x_p = x_prompt.reshape(t_p, d)
    x_s = x_sample.reshape(t_s, d)
    c_all = jnp.concatenate([c_prompt, c_sample], axis=0)
    outs = {k: [] for k in ("pwk", "pwv", "pcv", "pss", "swk", "swv", "scv", "sss")}

    for l in range(depth):
        w_in_l = jnp.concatenate([w_in[l][:, :offs[5]], w_in[l][:, offs[7]:], w_in[l][:, offs[5]:offs[7]],
                                  jnp.zeros((d, LANES - 2 * heads), F32)], axis=1).astype(BF16)
        mod = _ada(c_all, w_ada[l], b_ada[l])
        mod_p = mod[:n_p].reshape(n_p, 6, 1, d)
        mod_s = jnp.repeat(mod[n_p:].reshape(n_s, 6, d).transpose(1, 0, 2), l_s, axis=1)
        sh1p, sc1p, gt1p, sh2p, sc2p, gt2p = [mod_p[:, k] for k in range(6)]
        sh1s, sc1s, gt1s, sh2s, sc2s, gt2s = [mod_s[k] for k in range(6)]

        qkva_p, dn_p, z_p, g_p, ba_p, tail_p, kvw_p = _inproj(
            x_p, sh1p, sc1p, tabs_p, w_in_l, cuts, per_token_mod=False, tiles_per_seq=seq // INPROJ_TILE, act_dtype=BF16,
            window=window, kv_cols=2 * kd, half_rot=rot_dim // 2)
        attn_p = _attn_prompt(qkva_p, attn_sinks[l], n_p, seq, n_q, n_kv, hd, window)
        chunk = min(DN_CHUNK, seq)
        nc = seq // chunk
        bat_p = ba_p[:, :SUBLANES].reshape(n_p, nc, chunk, SUBLANES).transpose(0, 1, 3, 2)
        hp = jnp.zeros((SUBLANES, LANES), F32).at[0, :heads].set(dn_a_log[l]).at[1, :heads].set(dn_dt_bias[l])
        nw = dn_norm_w[l].reshape(1, dv)
        o_p, ssm_p = _deltanet(dn_p.reshape(n_p, seq, conv_dim), z_p.reshape(n_p, seq, vdim),
                               ba_p.reshape(n_p, seq, LANES), bat_p,
                               jnp.zeros((n_p, SUBLANES, conv_dim), F32), jnp.zeros((n_p, heads, dk, dv), F32),
                               w_conv[l], hp, nw, chunk=chunk, l_real=chunk, nb=4)
        outs["pwk"].append(kvw_p[:, :, :kd].reshape(n_p, window, n_kv, hd))
        outs["pwv"].append(kvw_p[:, :, kd:].reshape(n_p, window, n_kv, hd))
        outs["pcv"].append(tail_p.reshape(n_p, seq // INPROJ_TILE, SUBLANES, conv_dim)[:, -1, SUBLANES - (conv_w - 1):])
        outs["pss"].append(ssm_p)

        qkva_s, dn_s, z_s, g_s, ba_s = _inproj(
            x_s, sh1s, sc1s, tabs_s, w_in_l, cuts, per_token_mod=True, tiles_per_seq=1, act_dtype=F32,
            window=window, kv_cols=2 * kd, half_rot=rot_dim // 2)
        pad_l = lambda a: jnp.pad(a.reshape(n_s, l_s, a.shape[-1]), ((0, 0), (0, l_pad - l_s), (0, 0)))
        attn_s, wk_s, wv_s = _attn_sample(pad_l(qkva_s), state_win_k[l].reshape(n_s, window, kd),
                                          state_win_v[l].reshape(n_s, window, kd), attn_sinks[l], n_q, n_kv, hd, l_s)
        attn_s = attn_s[:, :l_s].reshape(t_s, qd)
        ba_s3 = pad_l(ba_s)
        bat_s = ba_s3[:, :, :SUBLANES].transpose(0, 2, 1).reshape(n_s, 1, SUBLANES, l_pad)
        cs0 = jnp.pad(state_conv[l], ((0, 0), (SUBLANES - (conv_w - 1), 0), (0, 0)))
        o_s, ssm_s = _deltanet(pad_l(dn_s), pad_l(z_s), ba_s3, bat_s, cs0, state_ssm[l], w_conv[l], hp, nw,
                               chunk=l_pad, l_real=l_s, nb=8)
        o_s = o_s[:, :l_s].reshape(t_s, vdim)
        outs["swk"].append(wk_s.reshape(n_s, window, n_kv, hd))
        outs["swv"].append(wv_s.reshape(n_s, window, n_kv, hd))
        outs["scv"].append(jnp.concatenate([state_conv[l], dn_s.reshape(n_s, l_s, conv_dim)], axis=1)[:, -(conv_w - 1):])
        outs["sss"].append(ssm_s)

        wr = jnp.pad(w_router[l], ((0, 0), (0, LANES - n_e)))
        br = jnp.pad(b_router[l], (0, LANES - n_e), constant_values=NEG_BIG).reshape(1, LANES)
        wr_hi = wr.astype(BF16)
        wr_lo = (wr - wr_hi.astype(F32)).astype(BF16)
        wts = (w_proj_attn[l].astype(BF16), w_proj_dn[l].astype(BF16), w_out[l].astype(BF16),
               ln1_w[l].reshape(1, d), ln1_b[l].reshape(1, d), wr_hi, wr_lo, br)
        res_p = _outproj(attn_p, o_p.reshape(t_p, vdim), g_p, x_p, gt1p, sh2p, sc2p, wts, None,
                         per_token_mod=False, tiles_per_seq=tps, t_total=t_all, tile_off=0, alpha=alpha)
        x1, pos, gate, cnt = _outproj(attn_s, o_s, g_s, x_s, gt1s, sh2s, sc2s, wts, res_p, per_token_mod=True,
                                      tiles_per_seq=1, t_total=t_all, tile_off=n_ptiles, alpha=alpha)

        bm = EXPERT_ROWS
        nt_all = t_all // tm
        cnt8 = (cnt[:, 0, :n_e].astype(I32) + RUN_ALIGN - 1) // RUN_ALIGN * RUN_ALIGN
        tot = jnp.sum(cnt8, axis=0)
        padded = (tot + bm - 1) // bm * bm
        pad_end = jnp.cumsum(padded)
        pad_start = pad_end - padded
        gbase = pad_start[None, :] + jnp.cumsum(cnt8, axis=0) - cnt8
        toff = jnp.cumsum(cnt8, axis=1) - cnt8
        n_local = -(-(tm * TOP_K + n_e * (RUN_ALIGN - 1)) // LANES) * LANES
        n_rows = -(-(t_all * TOP_K + nt_all * n_e * (RUN_ALIGN - 1) + n_e * (bm - 1)) // bm) * bm
        nblk = n_rows // bm
        n_used = jnp.maximum(pad_end[-1:] // bm, 1).astype(I32)
        block_e = jnp.minimum(jnp.sum(pad_end[None, :] <= (jnp.arange(nblk, dtype=I32) * bm)[:, None], axis=1),
                              n_e - 1).astype(I32)
        flat = lambda a: a.astype(I32).reshape(nt_all * n_e)
        run_tables = (flat(toff), flat(cnt8), flat(gbase))
        tail_tables = ((pad_start + tot).astype(I32), (padded - tot).astype(I32))

        xs = _dispatch(run_tables + tail_tables, x1, sh2p, sc2p, sh2s, sc2s, pos, n_rows=n_rows, n_local=n_local,
                       n_ptiles=n_ptiles, tiles_per_seq=tps, n_experts=n_e)
        group_end_blk = pad_end // bm
        next_e = jnp.where(group_end_blk < n_used[0], block_e[jnp.minimum(group_end_blk, nblk - 1)], -1).astype(I32)
        yb = _experts(block_e, n_used, next_e, xs, w_gu[l], b_gu[l], w_down[l], b_down[l])
        x_p, x_s = _combine(run_tables, x1, gt2p, gt2s, gate, pos, ln2_w[l].reshape(1, d), ln2_b[l].reshape(1, d),
                            yb, n_local=n_local, n_ptiles=n_ptiles, tiles_per_seq=tps, n_experts=n_e, alpha=alpha)

    st = lambda k: outs[k][0][None] if depth == 1 else jnp.stack(outs[k])
    return (x_p.reshape(n_p, seq, d), x_s.reshape(n_s, l_s, d), st("pwk"), st("pwv"), st("pcv"), st("pss"),
            st("swk"), st("swv"), st("scv"), st("sss"))
```

```python
import functools

import numpy as np
import jax
import jax.numpy as jnp
from jax import lax
from jax.experimental import pallas as pl
from jax.experimental.pallas import tpu as pltpu

F32 = jnp.float32
BF16 = jnp.bfloat16
I32 = jnp.int32
U32 = jnp.uint32

PAST_LEN = 16384
ROPE_THETA = 500000.0
TOP_K = 4
SWIGLU_LIMIT = 7.0
SWIGLU_ALPHA = 1.702
DN_CHUNK = 64
LN_EPS = 1e-5
RMS_EPS = 1e-6
L2_EPS = 1e-6

LANES = 128
SUBLANES = 8
VMEM_LIMIT_BYTES = 56 * 1024 * 1024

TOKEN_TILE = 256
INPROJ_TILE = 512
EXPERT_ROWS = 1024
RUN_ALIGN = SUBLANES
OUTPROJ_SUBTILES = 2
OUTPROJ_MAX_SUBTILES = 4
N_ROUTE_BUFS = 4
NEG_BIG = -1e30


def _cparams(*sem):
    return pltpu.CompilerParams(dimension_semantics=sem, vmem_limit_bytes=VMEM_LIMIT_BYTES)


def _silu(x):
    return x * jax.nn.sigmoid(x)


def _bdot(a, b):
    return jnp.dot(a.astype(BF16), b.astype(BF16), preferred_element_type=F32)


def _pack_halves(x):
    n = x.shape[1] // 2
    lo = lax.bitcast_convert_type(x[:, :n], U32)
    hi = lax.bitcast_convert_type(x[:, n:], U32)
    return (hi & jnp.uint32(0xFFFF0000)) | (lo >> 16)


def _unpack_halves(w):
    lo = lax.bitcast_convert_type(w << 16, F32).astype(BF16)
    hi = lax.bitcast_convert_type(w & jnp.uint32(0xFFFF0000), F32).astype(BF16)
    return lo, hi


def _ada_body(c_ref, w_ref, b_ref, o_ref):
    o_ref[...] = _bdot(_silu(c_ref[...]), w_ref[...]) + b_ref[...]


def _ada(c_all, w_ada, b_ada):
    n, d = c_all.shape
    dout = w_ada.shape[1]
    tn = d
    return pl.pallas_call(
        _ada_body,
        grid=(dout // tn,),
        in_specs=[pl.BlockSpec((n, d), lambda j: (0, 0)),
                  pl.BlockSpec((d, tn), lambda j: (0, j)),
                  pl.BlockSpec((1, tn), lambda j: (0, j))],
        out_specs=pl.BlockSpec((n, tn), lambda j: (0, j)),
        out_shape=jax.ShapeDtypeStruct((n, dout), F32),
        compiler_params=_cparams("arbitrary"),
        name="ada",
    )(c_all, w_ada, b_ada.reshape(1, dout))


def _inproj_body(x_ref, sh_ref, sc_ref, ct_ref, s1_ref, s2_ref, w_ref,
                 a_ref, dn_ref, z_ref, g_ref, ba_ref, *win_refs, cuts, n_rot_chunks, half_rot, window):
    h = (x_ref[...] * (1.0 + sc_ref[...]) + sh_ref[...]).astype(BF16)

    def mm(lo, hi):
        return jnp.dot(h, w_ref[:, lo:hi], preferred_element_type=F32)

    c_a, c_dn, c_z, c_g, c_ba = cuts
    qkv = mm(0, c_a)
    ct, s1, s2 = ct_ref[...], s1_ref[...], s2_ref[...]
    cols = []
    for c in range(n_rot_chunks):
        xc = qkv[:, c * LANES:(c + 1) * LANES]
        cols.append(xc * ct + pltpu.roll(xc, LANES - half_rot, 1) * s1 + pltpu.roll(xc, half_rot, 1) * s2)
    cols.append(qkv[:, n_rot_chunks * LANES:])
    rot = jnp.concatenate(cols, axis=1)
    a_ref[...] = rot.astype(a_ref.dtype)
    dn = mm(c_a, c_dn)
    dn_ref[...] = dn.astype(dn_ref.dtype)
    z_ref[...] = mm(c_dn, c_z).astype(z_ref.dtype)
    g_ref[...] = mm(c_z, c_g).astype(g_ref.dtype)
    ba_ref[...] = mm(c_g, c_ba)
    if win_refs:
        tail_ref, kvw_ref = win_refs
        tm = dn.shape[0]
        tail_ref[...] = dn[tm - SUBLANES:, :]
        kvw_ref[...] = rot[tm - window:, n_rot_chunks * LANES - LANES:]


def _inproj(x, sh, sc, tabs, w_perm, cuts, *, per_token_mod, tiles_per_seq, act_dtype, window, kv_cols, half_rot):
    t, d = x.shape
    tm = INPROJ_TILE
    nt = t // tm
    assert t % tm == 0 and tm >= window and (per_token_mod or nt % tiles_per_seq == 0)
    c_a, c_dn, c_z, c_g, c_ba = cuts
    n_rot_chunks = (c_a - kv_cols // 2) // LANES
    if per_token_mod:
        mod_spec = pl.BlockSpec((tm, d), lambda i: (i, 0))
        tab_spec = pl.BlockSpec((tm, LANES), lambda i: (0, 0))
    else:
        mod_spec = pl.BlockSpec((None, 1, d), lambda i: (i // tiles_per_seq, 0, 0))
        tab_spec = pl.BlockSpec((tm, LANES), lambda i: (i % tiles_per_seq, 0))
    out_shape = [jax.ShapeDtypeStruct((t, c_a), act_dtype),
                 jax.ShapeDtypeStruct((t, c_dn - c_a), act_dtype),
                 jax.ShapeDtypeStruct((t, c_z - c_dn), act_dtype),
                 jax.ShapeDtypeStruct((t, c_g - c_z), act_dtype),
                 jax.ShapeDtypeStruct((t, c_ba - c_g), F32)]
    out_specs = [pl.BlockSpec((tm, s.shape[1]), lambda i: (i, 0)) for s in out_shape]
    with_win = not per_token_mod
    if with_win:
        n_seq = nt // tiles_per_seq
        out_shape += [jax.ShapeDtypeStruct((nt, SUBLANES, c_dn - c_a), F32),
                      jax.ShapeDtypeStruct((n_seq, window, kv_cols), F32)]
        out_specs += [pl.BlockSpec((None, SUBLANES, c_dn - c_a), lambda i: (i, 0, 0)),
                      pl.BlockSpec((None, window, kv_cols), lambda i: (i // tiles_per_seq, 0, 0))]
    body = functools.partial(_inproj_body, cuts=cuts, n_rot_chunks=n_rot_chunks, half_rot=half_rot, window=window)
    return pl.pallas_call(
        body,
        grid=(nt,),
        in_specs=[pl.BlockSpec((tm, d), lambda i: (i, 0)), mod_spec, mod_spec,
                  tab_spec, tab_spec, tab_spec,
                  pl.BlockSpec((d, c_ba), lambda i: (0, 0))],
        out_specs=out_specs,
        out_shape=out_shape,
        compiler_params=_cparams("arbitrary"),
        name="inproj",
    )(x, sh, sc, *tabs, w_perm)


def _softmax_sink_pv(scores, valid, sinks, values):
    ms = [jnp.where(valid, s, -jnp.inf) for s in scores]
    m = [jnp.maximum(jnp.max(x, axis=-1, keepdims=True), sk) for x, sk in zip(ms, sinks)]
    p = [jnp.exp(x - mi) for x, mi in zip(ms, m)]
    den = [jnp.sum(pi, axis=-1, keepdims=True) + jnp.exp(sk - mi) for pi, sk, mi in zip(p, sinks, m)]
    return [jnp.dot((pi / di).astype(BF16), v, preferred_element_type=F32) for pi, di, v in zip(p, den, values)]


def _attn_prompt_body(sink_ref, q_ref, kvp_ref, kvc_ref, o_ref, *, n_q, n_kv, hd, window):
    j = pl.program_id(1)
    group = n_q // n_kv
    scale = hd ** -0.5
    assert np.log2(scale) == int(np.log2(scale))
    r = lax.broadcasted_iota(I32, (window, 2 * window), 0)
    c = lax.broadcasted_iota(I32, (window, 2 * window), 1)
    rel = window + r - c
    band = (rel >= 0) & (rel < window)
    kv_cur = kvc_ref[...]
    kv_first = jnp.concatenate([kvp_ref[...], kv_cur[:window]], axis=0)
    for qb, (kv, valid) in enumerate(((kv_first, band & ((c >= window) | (j > 0))), (kv_cur, band))):
        q = q_ref[qb * window:(qb + 1) * window, :] * scale
        scores = [lax.dot_general(q[:, h * hd:(h + 1) * hd], kv[:, (h // group) * hd:(h // group + 1) * hd],
                                  (((1,), (1,)), ((), ())), preferred_element_type=F32) for h in range(n_q)]
        values = [kv[:, (n_kv + h // group) * hd:(n_kv + h // group + 1) * hd] for h in range(n_q)]
        outs = _softmax_sink_pv(scores, valid, [sink_ref[h] for h in range(n_q)], values)
        o_ref[qb * window:(qb + 1) * window, :] = jnp.concatenate(outs, axis=1).astype(o_ref.dtype)


def _attn_prompt(qkva, sinks, n_seq, seq, n_q, n_kv, hd, window):
    qd, kvd = n_q * hd, 2 * n_kv * hd
    x3 = qkva.reshape(n_seq, seq, qd + kvd)
    nb = seq // window
    assert nb % 2 == 0 and qd % kvd == 0
    kv_blk = qd // kvd
    body = functools.partial(_attn_prompt_body, n_q=n_q, n_kv=n_kv, hd=hd, window=window)
    out = pl.pallas_call(
        body,
        grid=(n_seq, nb // 2),
        in_specs=[pl.BlockSpec(memory_space=pltpu.SMEM),
                  pl.BlockSpec((None, 2 * window, qd), lambda n, j: (n, j, 0)),
                  pl.BlockSpec((None, window, kvd), lambda n, j: (n, jnp.maximum(2 * j - 1, 0), kv_blk)),
                  pl.BlockSpec((None, 2 * window, kvd), lambda n, j: (n, j, kv_blk))],
        out_specs=pl.BlockSpec((None, 2 * window, qd), lambda n, j: (n, j, 0)),
        out_shape=jax.ShapeDtypeStruct((n_seq, seq, qd), BF16),
        compiler_params=_cparams("arbitrary", "arbitrary"),
        name="attn_prompt",
    )(sinks, x3, x3, x3)
    return out.reshape(n_seq * seq, qd)


def _attn_sample_body(sink_ref, q_ref, wk_ref, wv_ref, o_ref, wko_ref, wvo_ref,
                      *, bs, n_q, n_kv, hd, window, l_new, l_pad):
    group = n_q // n_kv
    qd = n_q * hd
    kd = n_kv * hd
    rows = group * l_pad
    r = lax.broadcasted_iota(I32, (rows, window + l_pad), 0) % l_pad
    c = lax.broadcasted_iota(I32, (rows, window + l_pad), 1)
    rel = window + r - c
    valid = (rel >= 0) & (rel < window) & (c < window + l_new)
    sinks = [jnp.concatenate([jnp.full((l_pad, 1), sink_ref[kvh * group + g], F32) for g in range(group)], axis=0)
             for kvh in range(n_kv)]
    qs, ks, vs = [], [], []
    for b in range(bs):
        x = q_ref[b]
        k_new = x[:, qd:qd + kd]
        v_new = x[:, qd + kd:]
        wko_ref[b, 0:window - l_new, :] = wk_ref[b, l_new:window, :]
        wko_ref[b, window - l_new:window, :] = k_new[0:l_new, :]
        wvo_ref[b, 0:window - l_new, :] = wv_ref[b, l_new:window, :]
        wvo_ref[b, window - l_new:window, :] = v_new[0:l_new, :]
        k_all = jnp.concatenate([wk_ref[b], k_new], axis=0).astype(BF16)
        v_all = jnp.concatenate([wv_ref[b], v_new], axis=0).astype(BF16)
        for kvh in range(n_kv):
            qs.append(jnp.concatenate([x[:, (kvh * group + g) * hd:(kvh * group + g + 1) * hd]
                                       for g in range(group)], axis=0).astype(BF16))
            ks.append(k_all[:, kvh * hd:(kvh + 1) * hd])
            vs.append(v_all[:, kvh * hd:(kvh + 1) * hd])
    n = len(qs)
    s = [lax.dot_general(qs[i], ks[i], (((1,), (1,)), ((), ())), preferred_element_type=F32) * (hd ** -0.5)
         for i in range(n)]
    o = _softmax_sink_pv(s, valid, [sinks[i % n_kv] for i in range(n)], vs)
    for b in range(bs):
        outs = [o[b * n_kv + kvh][g * l_pad:(g + 1) * l_pad, :] for kvh in range(n_kv) for g in range(group)]
        o_ref[b] = jnp.concatenate(outs, axis=1).astype(o_ref.dtype)


def _attn_sample(qkva_pad, win_k, win_v, sinks, n_q, n_kv, hd, l_new):
    n, l_pad, width = qkva_pad.shape
    window, kd = win_k.shape[1], win_k.shape[2]
    qd = n_q * hd
    bs = 8
    body = functools.partial(_attn_sample_body, bs=bs, n_q=n_q, n_kv=n_kv, hd=hd, window=window,
                             l_new=l_new, l_pad=l_pad)
    return pl.pallas_call(
        body,
        grid=(n // bs,),
        in_specs=[pl.BlockSpec(memory_space=pltpu.SMEM),
                  pl.BlockSpec((bs, l_pad, width), lambda i: (i, 0, 0)),
                  pl.BlockSpec((bs, window, kd), lambda i: (i, 0, 0)),
                  pl.BlockSpec((bs, window, kd), lambda i: (i, 0, 0))],
        out_specs=[pl.BlockSpec((bs, l_pad, qd), lambda i: (i, 0, 0)),
                   pl.BlockSpec((bs, window, kd), lambda i: (i, 0, 0)),
                   pl.BlockSpec((bs, window, kd), lambda i: (i, 0, 0))],
        out_shape=[jax.ShapeDtypeStruct((n, l_pad, qd), BF16),
                   jax.ShapeDtypeStruct((n, window, kd), F32),
                   jax.ShapeDtypeStruct((n, window, kd), F32)],
        compiler_params=_cparams("arbitrary"),
        name="attn_sample",
    )(sinks, qkva_pad, win_k, win_v)


def _split_bf16(x):
    hi = x.astype(BF16)
    return hi, (x - hi.astype(F32)).astype(BF16)


def _tdot(a, b):
    ah, al = _split_bf16(a)
    bh, bl = _split_bf16(b)
    m = a.shape[0]
    t = jnp.dot(jnp.concatenate([ah, al], axis=0), bh, preferred_element_type=F32)
    return t[:m] + t[m:] + jnp.dot(ah, bl, preferred_element_type=F32)


def _dn_body(qkv_ref, z_ref, ba_ref, bat_ref, cs0_ref, s0_ref, wc_ref, hp_ref, nw_ref,
             o_ref, s_ref, xbuf, *, nb, chunk, heads, dk, dv, l_real, conv_w):
    c_idx = pl.program_id(1)
    hc = SUBLANES

    @pl.when(c_idx == 0)
    def _():
        xbuf[:, 0:hc, :] = cs0_ref[...]
        s_ref[...] = s0_ref[...]

    qk_dim = heads * dk
    row = lax.broadcasted_iota(I32, (chunk, chunk), 0)
    col = lax.broadcasted_iota(I32, (chunk, chunk), 1)
    incl = row >= col
    strict = row > col
    eye = (row == col).astype(F32)
    valid_c = row[:, 0:1] < l_real
    valid_r = col[0:1, :] < l_real
    n_levels = max(1, int(np.ceil(np.log2(chunk))))
    wc = wc_ref[...]
    hp = hp_ref[...]
    neg_exp_alog = -jnp.exp(hp[0:1, :])
    dt_bias = hp[1:2, :]
    nw = nw_ref[...]
    chains = [(b, h) for b in range(nb) for h in range(heads)]
    n = len(chains)

    ys = []
    for b in range(nb):
        xbuf[b, hc:hc + chunk, :] = qkv_ref[b].astype(F32)
        y = xbuf[b, hc:hc + chunk, :] * wc[conv_w - 1:conv_w, :]
        for j in range(conv_w - 1):
            off = hc - (conv_w - 1) + j
            y = y + xbuf[b, off:off + chunk, :] * wc[j:j + 1, :]
        ys.append(_silu(y))
        xbuf[b, 0:hc, :] = xbuf[b, chunk:chunk + hc, :]

    qn, kn, kb, vb, decay, e_gc, e_rest, e_last = [], [], [], [], [], [], [], []
    for b, h in chains:
        y = ys[b]
        qh = y[:, h * dk:(h + 1) * dk]
        kh = y[:, qk_dim + h * dk:qk_dim + (h + 1) * dk]
        vh = y[:, 2 * qk_dim + h * dv:2 * qk_dim + (h + 1) * dv]
        ba = ba_ref[b]
        bat = bat_ref[b]
        ne = neg_exp_alog[:, h:h + 1]
        db = dt_bias[:, h:h + 1]
        beta = jnp.where(valid_c, jax.nn.sigmoid(ba[:, h:h + 1]), 0.0)
        g_col = jnp.where(valid_c, ne * jax.nn.softplus(ba[:, heads + h:heads + h + 1] + db), 0.0)
        g_row = jnp.where(valid_r, ne * jax.nn.softplus(bat[heads + h:heads + h + 1, :] + db), 0.0)
        gc_col = jnp.sum(jnp.where(incl, g_row, 0.0), axis=1, keepdims=True)
        gc_row = jnp.sum(jnp.where(row <= col, g_col, 0.0), axis=0, keepdims=True)
        g_last = gc_col[chunk - 1:chunk, :]
        q_ = qh * lax.rsqrt(jnp.sum(qh * qh, -1, keepdims=True) + L2_EPS) * (dk ** -0.5)
        k_ = kh * lax.rsqrt(jnp.sum(kh * kh, -1, keepdims=True) + L2_EPS)
        qn.append(q_)
        kn.append(k_)
        kb.append(k_ * beta)
        vb.append(vh * beta)
        decay.append(jnp.where(incl, jnp.exp(gc_col - gc_row), 0.0))
        e_gc.append(jnp.exp(gc_col))
        e_rest.append(jnp.exp(g_last - gc_col))
        e_last.append(jnp.exp(g_last))

    sc = [lax.dot_general(jnp.concatenate([qn[i], kb[i]], axis=0).astype(BF16), kn[i].astype(BF16),
                          (((1,), (1,)), ((), ())), preferred_element_type=F32) for i in range(n)]
    qk = [sc[i][:chunk] * decay[i] for i in range(n)]
    p = [jnp.where(strict, -(sc[i][chunk:] * decay[i]), 0.0) for i in range(n)]
    t_inv = [eye + p[i] for i in range(n)]
    if n_levels > 1:
        p = [_tdot(p[i], p[i]) for i in range(n)]
    for lvl in range(1, n_levels):
        if lvl < n_levels - 1:
            yp = [_tdot(jnp.concatenate([t_inv[i], p[i]], axis=0), p[i]) for i in range(n)]
            t_inv = [t_inv[i] + yp[i][:chunk] for i in range(n)]
            p = [yp[i][chunk:] for i in range(n)]
        else:
            t_inv = [t_inv[i] + _tdot(t_inv[i], p[i]) for i in range(n)]
    sol = [_tdot(t_inv[i], jnp.concatenate([vb[i], kb[i] * e_gc[i]], axis=1)) for i in range(n)]
    s_old = [s_ref[b, h] for b, h in chains]
    wq = [_bdot(jnp.concatenate([sol[i][:, dv:], qn[i] * e_gc[i]], axis=0), s_old[i]) for i in range(n)]
    v_new = [sol[i][:, :dv] - wq[i][:chunk] for i in range(n)]
    o = [wq[i][chunk:] + _bdot(qk[i], v_new[i]) for i in range(n)]
    for i, (b, h) in enumerate(chains):
        s_ref[b, h] = s_old[i] * e_last[i] + lax.dot_general(
            (kn[i] * e_rest[i]).astype(BF16), v_new[i].astype(BF16), (((0,), (0,)), ((), ())),
            preferred_element_type=F32)
    for b in range(nb):
        zt = z_ref[b].astype(F32)
        outs = []
        for h in range(heads):
            oi = o[b * heads + h]
            on = oi * lax.rsqrt(jnp.mean(oi * oi, -1, keepdims=True) + RMS_EPS) * nw
            outs.append(on * _silu(zt[:, h * dv:(h + 1) * dv]))
        o_ref[b] = jnp.concatenate(outs, axis=1).astype(o_ref.dtype)


def _deltanet(qkv, z, ba, bat, cs0, s0, w_conv, hp, norm_w, *, chunk, l_real, nb):
    n, l, conv_dim = qkv.shape
    heads, dk, dv = s0.shape[1:]
    nc = l // chunk
    assert n % nb == 0 and l % chunk == 0
    conv_w = w_conv.shape[0]
    hist = pltpu.VMEM((nb, SUBLANES + chunk, conv_dim), F32)
    body = functools.partial(_dn_body, nb=nb, chunk=chunk, heads=heads, dk=dk, dv=dv, l_real=l_real, conv_w=conv_w)
    return pl.pallas_call(
        body,
        grid=(n // nb, nc),
        in_specs=[pl.BlockSpec((nb, chunk, conv_dim), lambda i, c: (i, c, 0)),
                  pl.BlockSpec((nb, chunk, heads * dv), lambda i, c: (i, c, 0)),
                  pl.BlockSpec((nb, chunk, LANES), lambda i, c: (i, c, 0)),
                  pl.BlockSpec((nb, None, SUBLANES, chunk), lambda i, c: (i, c, 0, 0)),
                  pl.BlockSpec((nb, SUBLANES, conv_dim), lambda i, c: (i, 0, 0)),
                  pl.BlockSpec((nb, heads, dk, dv), lambda i, c: (i, 0, 0, 0)),
                  pl.BlockSpec((conv_w, conv_dim), lambda i, c: (0, 0)),
                  pl.BlockSpec((SUBLANES, LANES), lambda i, c: (0, 0)),
                  pl.BlockSpec((1, dv), lambda i, c: (0, 0))],
        out_specs=[pl.BlockSpec((nb, chunk, heads * dv), lambda i, c: (i, c, 0)),
                   pl.BlockSpec((nb, heads, dk, dv), lambda i, c: (i, 0, 0, 0))],
        out_shape=[jax.ShapeDtypeStruct((n, l, heads * dv), BF16),
                   jax.ShapeDtypeStruct((n, heads, dk, dv), F32)],
        scratch_shapes=[hist],
        compiler_params=_cparams("arbitrary", "arbitrary"),
        name="deltanet",
    )(qkv, z, ba, bat, cs0, s0, w_conv, hp, norm_w)


def _layer_norm(r, w, b):
    mu = jnp.mean(r, -1, keepdims=True)
    var = jnp.mean(jnp.square(r - mu), -1, keepdims=True)
    return (r - mu) * lax.rsqrt(var + LN_EPS) * w + b


def _outproj_body(attn_ref, dn_ref, g_ref, x_ref, gt_ref, sh2_ref, sc2_ref, wpa_ref, wpd_ref, wo_ref,
                  lnw_ref, lnb_ref, wrh_ref, wrl_ref, br_ref, *refs, alpha, top_k, aliased):
    if aliased:
        refs = refs[N_ROUTE_BUFS:]
    x1_ref, pos_ref, gate_ref, cnt_ref = refs
    for s in range(cnt_ref.shape[0]):
        _outproj_tile(s, attn_ref, dn_ref, g_ref, x_ref, gt_ref, sh2_ref, sc2_ref, wpa_ref, wpd_ref, wo_ref, lnw_ref,
                      lnb_ref, wrh_ref, wrl_ref, br_ref, x1_ref, pos_ref, gate_ref, cnt_ref,
                      alpha=alpha, top_k=top_k)


def _outproj_tile(s, attn_ref, dn_ref, g_ref, x_ref, gt_ref, sh2_ref, sc2_ref, wpa_ref, wpd_ref, wo_ref, lnw_ref,
                  lnb_ref, wrh_ref, wrl_ref, br_ref, x1_ref, pos_ref, gate_ref, cnt_ref, *, alpha, top_k):
    d = x_ref.shape[1]
    tm = TOKEN_TILE
    rows = pl.ds(s * tm, tm)

    def mod(ref):
        return ref[...] if ref.shape[0] == 1 else ref[rows, :]

    g = g_ref[rows, :].astype(F32)
    pa = jnp.dot(attn_ref[rows, :], wpa_ref[...], preferred_element_type=F32)
    pd = jnp.dot(dn_ref[rows, :], wpd_ref[...], preferred_element_type=F32)
    merged = jax.nn.sigmoid(g[:, :d]) * pa + jax.nn.sigmoid(g[:, d:]) * pd
    mix = jnp.dot(merged.astype(BF16), wo_ref[...], preferred_element_type=F32)
    x1 = _layer_norm(alpha * x_ref[rows, :] + mod(gt_ref) * mix, lnw_ref[...], lnb_ref[...])
    x1_ref[rows, :] = x1
    h2 = x1 * (1.0 + mod(sc2_ref)) + mod(sh2_ref)
    h_hi, h_lo = _split_bf16(h2)
    lg = jnp.dot(jnp.concatenate([h_hi, h_lo], axis=0), wrh_ref[...], preferred_element_type=F32)
    logits = lg[:tm] + lg[tm:] + jnp.dot(h_hi, wrl_ref[...], preferred_element_type=F32) + br_ref[...]
    lane = lax.broadcasted_iota(I32, (tm, LANES), 1)
    lane_f = lane.astype(F32)
    vals, sels = [], []
    l = logits
    for _ in range(top_k):
        m = jnp.max(l, axis=1, keepdims=True)
        idx = jnp.min(jnp.where(l == m, lane_f, float(LANES)), axis=1, keepdims=True)
        sel = lane_f == idx
        vals.append(m)
        sels.append(sel)
        l = jnp.where(sel, -jnp.inf, l)
    ex = [jnp.exp(v - vals[0]) for v in vals]
    den = ex[0]
    for e in ex[1:]:
        den = den + e
    multi_hot = jnp.zeros((tm, LANES), F32)
    for sel in sels:
        multi_hot = multi_hot + jnp.where(sel, 1.0, 0.0)
    r_i = lax.broadcasted_iota(I32, (tm, tm), 0)
    c_i = lax.broadcasted_iota(I32, (tm, tm), 1)
    lower = jnp.where(r_i > c_i, 1.0, 0.0).astype(BF16)
    prefix = jnp.dot(lower, multi_hot.astype(BF16), preferred_element_type=F32)
    counts = jnp.sum(multi_hot, axis=0, keepdims=True)
    cnt_pad = jnp.floor((counts + (RUN_ALIGN - 1)) * (1.0 / RUN_ALIGN)) * RUN_ALIGN
    e_r = lax.broadcasted_iota(I32, (LANES, LANES), 0)
    e_c = lax.broadcasted_iota(I32, (LANES, LANES), 1)
    before = jnp.where(e_r < e_c, 1.0, 0.0).astype(BF16)
    run_off = jnp.dot(jnp.broadcast_to(cnt_pad, (SUBLANES, LANES)).astype(BF16), before,
                      preferred_element_type=F32)[0:1, :]
    g_out = jnp.zeros((tm, LANES), F32)
    p_out = jnp.zeros((tm, LANES), F32)
    for k in range(top_k):
        pos_k = jnp.sum(jnp.where(sels[k], prefix + run_off, 0.0), axis=1, keepdims=True)
        g_out = jnp.where(lane == k, ex[k] / den, g_out)
        p_out = jnp.where(lane == k, pos_k, p_out)
    pos_ref[rows, :] = p_out.astype(I32)
    gate_ref[rows, :] = g_out
    cnt_ref[s] = jnp.broadcast_to(counts, cnt_ref.shape[1:])


def _outproj(attn, dn, gates, x, gt, sh2, sc2, wts, bufs, *, per_token_mod, tiles_per_seq, t_total, tile_off, alpha):
    t, d = x.shape
    sub = OUTPROJ_SUBTILES
    while not per_token_mod and tiles_per_seq % (2 * sub) == 0 and 2 * sub <= OUTPROJ_MAX_SUBTILES:
        sub *= 2
    tm = TOKEN_TILE * sub
    nt = t // tm
    assert t % tm == 0 and tile_off % sub == 0 and (per_token_mod or tiles_per_seq % sub == 0)
    wpa, wpd, wo, lnw, lnb, wrh, wrl, br = wts
    if per_token_mod:
        mod_spec = pl.BlockSpec((tm, d), lambda i: (i, 0))
    else:
        mod_spec = pl.BlockSpec((None, 1, d), lambda i: (i // (tiles_per_seq // sub), 0, 0))

    def row(width):
        return pl.BlockSpec((tm, width), lambda i: (i, 0))

    def full(a):
        return pl.BlockSpec(a.shape, lambda i: (0,) * a.ndim)

    aliased = bufs is not None
    in_specs = [row(attn.shape[1]), row(dn.shape[1]), row(gates.shape[1]), row(d), mod_spec, mod_spec, mod_spec,
                full(wpa), full(wpd), full(wo), full(lnw), full(lnb), full(wrh), full(wrl), full(br)]
    args = [attn, dn, gates, x, gt, sh2, sc2, wpa, wpd, wo, lnw, lnb, wrh, wrl, br]
    io_alias = {}
    if aliased:
        for k, bfr in enumerate(bufs):
            in_specs.append(pl.BlockSpec(memory_space=pl.ANY))
            io_alias[len(args)] = k
            args.append(bfr)
    step_off = tile_off // sub
    out_row = lambda width: pl.BlockSpec((tm, width), lambda i: (i + step_off, 0))
    out_shape = [jax.ShapeDtypeStruct((t_total, d), F32), jax.ShapeDtypeStruct((t_total, LANES), I32),
                 jax.ShapeDtypeStruct((t_total, LANES), F32),
                 jax.ShapeDtypeStruct((t_total // TOKEN_TILE, SUBLANES, LANES), F32)]
    assert len(out_shape) == N_ROUTE_BUFS and TOKEN_TILE <= 256
    out_specs = [out_row(d), out_row(LANES), out_row(LANES),
                 pl.BlockSpec((sub, SUBLANES, LANES), lambda i: (i + step_off, 0, 0))]
    body = functools.partial(_outproj_body, alpha=alpha, top_k=TOP_K, aliased=aliased)
    return pl.pallas_call(
        body,
        grid=(nt,),
        in_specs=in_specs,
        out_specs=out_specs,
        out_shape=out_shape,
        input_output_aliases=io_alias,
        compiler_params=_cparams("arbitrary"),
        name="outproj",
    )(*args)


def _select_mod(i, n_ptiles, seq_ref, tok_ref):
    return jnp.where(i < n_ptiles, seq_ref[...], tok_ref[...])


def _for_run(rows, local_off, global_off, fn):
    @pl.when(rows > 0)
    def _():
        fn(pl.multiple_of(local_off, RUN_ALIGN), pl.multiple_of(global_off, RUN_ALIGN),
           pl.multiple_of(rows, RUN_ALIGN))


def _for_each_run(i, n_experts, toff_ref, cnt8_ref, gbase_ref, fn):
    for e in range(n_experts):
        idx = i * n_experts + e
        _for_run(cnt8_ref[idx], toff_ref[idx], gbase_ref[idx], fn)


def _tile_rows(tile, n_experts, toff_ref, cnt8_ref):
    last = tile * n_experts + n_experts - 1
    return pl.multiple_of(toff_ref[last] + cnt8_ref[last], RUN_ALIGN)


def _scatter_matrix(pos_ref, n_local, values):
    pos = pos_ref[...]
    col = lax.broadcasted_iota(I32, (pos.shape[0], n_local), 1)
    out = jnp.zeros((pos.shape[0], n_local), F32)
    for k in range(TOP_K):
        out = jnp.where(col == pos[:, k:k + 1], 1.0 if values is None else values[:, k:k + 1], out)
    return out


def _dispatch_body(toff_ref, cnt8_ref, gbase_ref, tstart_ref, trows_ref,
                   x1_ref, shs_ref, scs_ref, sht_ref, sct_ref, pos_ref,
                   xs_ref, lbuf, zbuf, sem, *, n_tiles, n_ptiles, n_experts):
    i = pl.program_id(0)
    sc = _select_mod(i, n_ptiles, scs_ref, sct_ref)
    sh = _select_mod(i, n_ptiles, shs_ref, sht_ref)
    h2 = (x1_ref[...] * (1.0 + sc) + sh).astype(BF16)
    n_local = lbuf.shape[1]
    slot = i % 2

    def runs(tile, sl, act):
        def fn(lo, go, rows):
            act(pltpu.make_async_copy(lbuf.at[sl, pl.ds(lo, rows), :], xs_ref.at[pl.ds(go, rows), :], sem.at[sl]))
        _for_each_run(tile, n_experts, toff_ref, cnt8_ref, gbase_ref, fn)

    start = lambda cp: cp.start()
    wait = lambda cp: cp.wait()

    def wait_runs(tile, sl):
        rows = _tile_rows(tile, n_experts, toff_ref, cnt8_ref)
        pltpu.make_async_copy(lbuf.at[sl, pl.ds(0, rows), :], xs_ref.at[pl.ds(0, rows), :], sem.at[sl]).wait()

    @pl.when(i >= 2)
    def _():
        wait_runs(i - 2, slot)

    onehot = _scatter_matrix(pos_ref, n_local, None)
    srt = lax.dot_general(onehot.astype(BF16), h2, (((0,), (0,)), ((), ())), preferred_element_type=F32)
    lbuf[slot] = _pack_halves(srt)
    runs(i, slot, start)

    @pl.when(i == n_tiles - 1)
    def _():
        wait_runs(i - 1, 1 - slot)
        wait_runs(i, slot)
        zbuf[...] = jnp.zeros(zbuf.shape, zbuf.dtype)

        def zero_fill(e, act):
            _for_run(trows_ref[e], 0, tstart_ref[e], lambda lo, go, rows: act(pltpu.make_async_copy(
                zbuf.at[pl.ds(0, rows), :], xs_ref.at[pl.ds(go, rows), :], sem.at[0])))

        for e in range(n_experts):
            zero_fill(e, start)
        for e in range(n_experts):
            zero_fill(e, wait)


def _dispatch(tables, x1, sh_seq, sc_seq, sh_tok, sc_tok, pos, *, n_rows, n_local, n_ptiles, tiles_per_seq,
              n_experts):
    t, d = x1.shape
    tm = TOKEN_TILE
    nt = t // tm
    n_seq = sh_seq.shape[0]
    assert nt >= 2 and d % 2 == 0
    body = functools.partial(_dispatch_body, n_tiles=nt, n_ptiles=n_ptiles, n_experts=n_experts)
    seq_spec = pl.BlockSpec((None, 1, d), lambda i, *_: (jnp.minimum(i // tiles_per_seq, n_seq - 1), 0, 0))
    tok_spec = pl.BlockSpec((tm, d), lambda i, *_: (jnp.maximum(i - n_ptiles, 0), 0))
    return pl.pallas_call(
        body,
        grid_spec=pltpu.PrefetchScalarGridSpec(
            num_scalar_prefetch=len(tables),
            grid=(nt,),
            in_specs=[pl.BlockSpec((tm, d), lambda i, *_: (i, 0)),
                      seq_spec, seq_spec, tok_spec, tok_spec,
                      pl.BlockSpec((tm, LANES), lambda i, *_: (i, 0))],
            out_specs=pl.BlockSpec(memory_space=pl.ANY),
            scratch_shapes=[pltpu.VMEM((2, n_local, d // 2), U32), pltpu.VMEM((EXPERT_ROWS, d // 2), U32),
                            pltpu.SemaphoreType.DMA((2,))],
        ),
        out_shape=jax.ShapeDtypeStruct((n_rows, d // 2), U32),
        compiler_params=_cparams("arbitrary"),
        name="dispatch",
    )(*tables, x1, sh_seq, sc_seq, sh_tok, sc_tok, pos)


def _expert_body(be_ref, nu_ref, first_ref, next_ref, xs_ref, wgu_hbm, bgu_ref, wd_hbm, bd_ref, y_ref,
                 stage_gu, stage_dn, wgu_bf, wd_bf, sem, *, de):
    i = pl.program_id(0)
    used = i < nu_ref[0]
    expert = be_ref[i]

    def weights(e, act):
        act(pltpu.make_async_copy(wgu_hbm.at[e], stage_gu, sem.at[0]))
        act(pltpu.make_async_copy(wd_hbm.at[e], stage_dn, sem.at[1]))

    @pl.when(i == 0)
    def _():
        weights(expert, lambda cp: cp.start())

    @pl.when(used & (first_ref[i] == 1))
    def _():
        weights(expert, lambda cp: cp.wait())
        wgu_bf[...] = stage_gu[...].astype(BF16)
        wd_bf[...] = stage_dn[...].astype(BF16)
        nxt = next_ref[expert]

        @pl.when(nxt >= 0)
        def _():
            weights(nxt, lambda cp: cp.start())

    @pl.when(used)
    def _():
        x = jnp.concatenate(_unpack_halves(xs_ref[...]), axis=1)
        gu = jnp.dot(x, wgu_bf[...], preferred_element_type=F32) + bgu_ref[...]
        glu = jnp.minimum(gu[:, :de], SWIGLU_LIMIT)
        lin = jnp.clip(gu[:, de:], -SWIGLU_LIMIT, SWIGLU_LIMIT)
        act = glu * jax.nn.sigmoid(SWIGLU_ALPHA * glu) * (lin + 1.0)
        y = jnp.dot(act.astype(BF16), wd_bf[...], preferred_element_type=F32) + bd_ref[...]
        y_ref[...] = _pack_halves(y.astype(BF16).astype(F32))


def _experts(block_e, n_used, next_e, xs, w_gu, b_gu, w_down, b_down):
    p, dh = xs.shape
    bm = EXPERT_ROWS
    n_e, d, de2 = w_gu.shape
    de = de2 // 2
    nblk = p // bm

    def blk(i, be, nu, *_):
        return jnp.minimum(i, nu[0] - 1)

    first = jnp.concatenate([jnp.ones((1,), I32), (block_e[1:] != block_e[:-1]).astype(I32)])
    body = functools.partial(_expert_body, de=de)
    return pl.pallas_call(
        body,
        grid_spec=pltpu.PrefetchScalarGridSpec(
            num_scalar_prefetch=4,
            grid=(nblk,),
            in_specs=[pl.BlockSpec((bm, dh), lambda i, be, nu, *_: (blk(i, be, nu), 0)),
                      pl.BlockSpec(memory_space=pl.ANY),
                      pl.BlockSpec((None, 1, de2), lambda i, be, nu, *_: (be[blk(i, be, nu)], 0, 0)),
                      pl.BlockSpec(memory_space=pl.ANY),
                      pl.BlockSpec((None, 1, d), lambda i, be, nu, *_: (be[blk(i, be, nu)], 0, 0))],
            out_specs=pl.BlockSpec((bm, dh), lambda i, be, nu, *_: (blk(i, be, nu), 0)),
            scratch_shapes=[pltpu.VMEM((d, de2), F32), pltpu.VMEM((de, d), F32),
                            pltpu.VMEM((d, de2), BF16), pltpu.VMEM((de, d), BF16), pltpu.SemaphoreType.DMA((2,))],
        ),
        out_shape=jax.ShapeDtypeStruct((p, dh), U32),
        compiler_params=_cparams("arbitrary"),
        name="experts",
    )(block_e, n_used, first, next_e, xs, w_gu, b_gu.reshape(n_e, 1, de2), w_down, b_down.reshape(n_e, 1, d))


def _combine_body(toff_ref, cnt8_ref, gbase_ref, x1_ref, gts_ref, gtt_ref, gate_ref, pos_ref,
                  lnw_ref, lnb_ref, yb_ref, yp_ref, ys_ref, ybuf, sem, *, top_k, n_ptiles, n_experts, alpha):
    i = pl.program_id(0)
    n_tiles = pl.num_programs(0)
    tm = x1_ref.shape[0]
    n_local = ybuf.shape[1]
    slot = i % 2

    def fetch(tile, sl, act):
        def fn(lo, go, rows):
            act(pltpu.make_async_copy(yb_ref.at[pl.ds(go, rows), :], ybuf.at[sl, pl.ds(lo, rows), :], sem.at[sl]))
        _for_each_run(tile, n_experts, toff_ref, cnt8_ref, gbase_ref, fn)

    def start_fetch(tile, sl):
        ybuf[sl, tm * top_k:, :] = jnp.zeros((n_local - tm * top_k, ybuf.shape[2]), ybuf.dtype)
        fetch(tile, sl, lambda cp: cp.start())

    @pl.when(i == 0)
    def _():
        start_fetch(i, slot)

    @pl.when(i + 1 < n_tiles)
    def _():
        start_fetch(i + 1, 1 - slot)

    weights = _scatter_matrix(pos_ref, n_local, gate_ref[...]).astype(BF16)
    rows = _tile_rows(i, n_experts, toff_ref, cnt8_ref)
    pltpu.make_async_copy(yb_ref.at[pl.ds(0, rows), :], ybuf.at[slot, pl.ds(0, rows), :], sem.at[slot]).wait()
    y_lo, y_hi = _unpack_halves(ybuf[slot])
    ff = jnp.concatenate([jnp.dot(weights, y_lo, preferred_element_type=F32),
                          jnp.dot(weights, y_hi, preferred_element_type=F32)], axis=1)
    gt = _select_mod(i, n_ptiles, gts_ref, gtt_ref)
    y = _layer_norm(alpha * x1_ref[...] + gt * ff, lnw_ref[...], lnb_ref[...])

    @pl.when(i < n_ptiles)
    def _():
        yp_ref[...] = y

    @pl.when(i >= n_ptiles)
    def _():
        ys_ref[...] = y


def _combine(tables, x1, gt_seq, gt_tok, gate, pos, lnw, lnb, yb, *, n_local, n_ptiles, tiles_per_seq, n_experts,
             alpha):
    t, d = x1.shape
    tm = TOKEN_TILE
    nt = t // tm
    n_seq = gt_seq.shape[0]
    body = functools.partial(_combine_body, top_k=TOP_K, n_ptiles=n_ptiles, n_experts=n_experts, alpha=alpha)
    lane_spec = pl.BlockSpec((tm, LANES), lambda i, *_: (i, 0))
    return pl.pallas_call(
        body,
        grid_spec=pltpu.PrefetchScalarGridSpec(
            num_scalar_prefetch=len(tables),
            grid=(nt,),
            in_specs=[pl.BlockSpec((tm, d), lambda i, *_: (i, 0)),
                      pl.BlockSpec((None, 1, d), lambda i, *_: (jnp.minimum(i // tiles_per_seq, n_seq - 1), 0, 0)),
                      pl.BlockSpec((tm, d), lambda i, *_: (jnp.maximum(i - n_ptiles, 0), 0)),
                      lane_spec,
                      pl.BlockSpec((tm, LANES), lambda i, *_: (i, 0)),
                      pl.BlockSpec((1, d), lambda i, *_: (0, 0)),
                      pl.BlockSpec((1, d), lambda i, *_: (0, 0)),
                      pl.BlockSpec(memory_space=pl.ANY)],
            out_specs=[pl.BlockSpec((tm, d), lambda i, *_: (jnp.minimum(i, n_ptiles - 1), 0)),
                       pl.BlockSpec((tm, d), lambda i, *_: (jnp.maximum(i - n_ptiles, 0), 0))],
            scratch_shapes=[pltpu.VMEM((2, n_local, d // 2), U32), pltpu.SemaphoreType.DMA((2,))],
        ),
        out_shape=[jax.ShapeDtypeStruct((n_ptiles * tm, d), F32),
                   jax.ShapeDtypeStruct(((nt - n_ptiles) * tm, d), F32)],
        compiler_params=_cparams("arbitrary"),
        name="combine",
    )(*tables, x1, gt_seq, gt_tok, gate, pos, lnw, lnb, yb)


def _rotary_tables(pos, hd, rot_dim):
    half = rot_dim // 2
    inv_freq = jnp.power(jnp.float32(ROPE_THETA), -jnp.arange(half, dtype=F32) * (2.0 / rot_dim))
    ang = pos.astype(F32)[:, None] * inv_freq[None, :]
    cos, sin = jnp.cos(ang), jnp.sin(ang)
    n = pos.shape[0]
    ones = jnp.ones((n, hd - rot_dim), F32)
    zeros = jnp.zeros((n, hd - rot_dim), F32)
    zh = jnp.zeros((n, half), F32)
    ct = jnp.concatenate([cos, cos, ones], axis=1)
    s1 = jnp.concatenate([-sin, zh, zeros], axis=1)
    s2 = jnp.concatenate([zh, sin, zeros], axis=1)
    reps = LANES // hd
    return tuple(jnp.tile(a, (1, reps)) for a in (ct, s1, s2))


def kernel(x_prompt, x_sample, state_win_k, state_win_v, state_conv, state_ssm, c_prompt, c_sample, w_ada, b_ada, w_in, attn_sinks, w_conv, dn_a_log, dn_dt_bias, dn_norm_w, w_proj_attn, w_proj_dn, w_out, ln1_w, ln1_b, w_router, b_router, w_gu, b_gu, w_down, b_down, ln2_w, ln2_b):
    n_p, seq, d = x_prompt.shape
    n_s, l_s, _ = x_sample.shape
    depth = w_ada.shape[0]
    window, n_kv, hd = state_win_k.shape[2:]
    n_q = attn_sinks.shape[1]
    heads, dk, dv = state_ssm.shape[2:]
    conv_w, conv_dim = w_conv.shape[1:]
    n_e = w_router.shape[2]
    qd, kd = n_q * hd, n_kv * hd
    vdim = heads * dv
    rot_dim = hd // 4
    alpha = float((2 * depth) ** 0.25)
    tm = TOKEN_TILE
    t_p, t_s = n_p * seq, n_s * l_s
    t_all = t_p + t_s
    tps = seq // tm
    n_ptiles = t_p // tm
    l_pad = SUBLANES
    assert seq % tm == 0 and t_s % tm == 0 and tm % l_s == 0 and l_s <= l_pad and l_s >= conv_w - 1
    assert 2 * heads <= SUBLANES and n_e <= LANES and hd * 2 == LANES and rot_dim % 2 == 0

    sizes = [qd, kd, kd, conv_dim, vdim, heads, heads, d, d]
    offs = [int(o) for o in np.concatenate([[0], np.cumsum(sizes)])]
    cuts = (qd + 2 * kd, qd + 2 * kd + conv_dim, qd + 2 * kd + conv_dim + vdim, qd + 2 * kd + conv_dim + vdim + 2 * d)
    cuts = cuts + (cuts[-1] + LANES,)

    tabs_p = _rotary_tables(jnp.arange(seq, dtype=I32), hd, rot_dim)
    tabs_s = tuple(jnp.tile(a, (INPROJ_TILE // l_s, 1))
                   for a in _rotary_tables(PAST_LEN + jnp.arange(l_s, dtype=I32), hd, rot_dim))

    x_p = x_prompt.reshape(t_p, d)
    x_s = x_sample.reshape(t_s, d)
    c_all = jnp.concatenate([c_prompt, c_sample], axis=0)
    outs = {k: [] for k in ("pwk", "pwv", "pcv", "pss", "swk", "swv", "scv", "sss")}

    for l in range(depth):
        w_in_l = jnp.concatenate([w_in[l][:, :offs[5]], w_in[l][:, offs[7]:], w_in[l][:, offs[5]:offs[7]],
                                  jnp.zeros((d, LANES - 2 * heads), F32)], axis=1).astype(BF16)
        mod = _ada(c_all, w_ada[l], b_ada[l])
        mod_p = mod[:n_p].reshape(n_p, 6, 1, d)
        mod_s = jnp.repeat(mod[n_p:].reshape(n_s, 6, d).transpose(1, 0, 2), l_s, axis=1)
        sh1p, sc1p, gt1p, sh2p, sc2p, gt2p = [mod_p[:, k] for k in range(6)]
        sh1s, sc1s, gt1s, sh2s, sc2s, gt2s = [mod_s[k] for k in range(6)]

        qkva_p, dn_p, z_p, g_p, ba_p, tail_p, kvw_p = _inproj(
            x_p, sh1p, sc1p, tabs_p, w_in_l, cuts, per_token_mod=False, tiles_per_seq=seq // INPROJ_TILE, act_dtype=BF16,
            window=window, kv_cols=2 * kd, half_rot=rot_dim // 2)
        attn_p = _attn_prompt(qkva_p, attn_sinks[l], n_p, seq, n_q, n_kv, hd, window)
        chunk = min(DN_CHUNK, seq)
        nc = seq // chunk
        bat_p = ba_p[:, :SUBLANES].reshape(n_p, nc, chunk, SUBLANES).transpose(0, 1, 3, 2)
        hp = jnp.zeros((SUBLANES, LANES), F32).at[0, :heads].set(dn_a_log[l]).at[1, :heads].set(dn_dt_bias[l])
        nw = dn_norm_w[l].reshape(1, dv)
        o_p, ssm_p = _deltanet(dn_p.reshape(n_p, seq, conv_dim), z_p.reshape(n_p, seq, vdim),
                               ba_p.reshape(n_p, seq, LANES), bat_p,
                               jnp.zeros((n_p, SUBLANES, conv_dim), F32), jnp.zeros((n_p, heads, dk, dv), F32),
                               w_conv[l], hp, nw, chunk=chunk, l_real=chunk, nb=4)
        outs["pwk"].append(kvw_p[:, :, :kd].reshape(n_p, window, n_kv, hd))
        outs["pwv"].append(kvw_p[:, :, kd:].reshape(n_p, window, n_kv, hd))
        outs["pcv"].append(tail_p.reshape(n_p, seq // INPROJ_TILE, SUBLANES, conv_dim)[:, -1, SUBLANES - (conv_w - 1):])
        outs["pss"].append(ssm_p)

        qkva_s, dn_s, z_s, g_s, ba_s = _inproj(
            x_s, sh1s, sc1s, tabs_s, w_in_l, cuts, per_token_mod=True, tiles_per_seq=1, act_dtype=F32,
            window=window, kv_cols=2 * kd, half_rot=rot_dim // 2)
        pad_l = lambda a: jnp.pad(a.reshape(n_s, l_s, a.shape[-1]), ((0, 0), (0, l_pad - l_s), (0, 0)))
        attn_s, wk_s, wv_s = _attn_sample(pad_l(qkva_s), state_win_k[l].reshape(n_s, window, kd),
                                          state_win_v[l].reshape(n_s, window, kd), attn_sinks[l], n_q, n_kv, hd, l_s)
        attn_s = attn_s[:, :l_s].reshape(t_s, qd)
        ba_s3 = pad_l(ba_s)
        bat_s = ba_s3[:, :, :SUBLANES].transpose(0, 2, 1).reshape(n_s, 1, SUBLANES, l_pad)
        cs0 = jnp.pad(state_conv[l], ((0, 0), (SUBLANES - (conv_w - 1), 0), (0, 0)))
        o_s, ssm_s = _deltanet(pad_l(dn_s), pad_l(z_s), ba_s3, bat_s, cs0, state_ssm[l], w_conv[l], hp, nw,
                               chunk=l_pad, l_real=l_s, nb=8)
        o_s = o_s[:, :l_s].reshape(t_s, vdim)
        outs["swk"].append(wk_s.reshape(n_s, window, n_kv, hd))
        outs["swv"].append(wv_s.reshape(n_s, window, n_kv, hd))
        outs["scv"].append(jnp.concatenate([state_conv[l], dn_s.reshape(n_s, l_s, conv_dim)], axis=1)[:, -(conv_w - 1):])
        outs["sss"].append(ssm_s)

        wr = jnp.pad(w_router[l], ((0, 0), (0, LANES - n_e)))
        br = jnp.pad(b_router[l], (0, LANES - n_e), constant_values=NEG_BIG).reshape(1, LANES)
        wr_hi = wr.astype(BF16)
        wr_lo = (wr - wr_hi.astype(F32)).astype(BF16)
        wts = (w_proj_attn[l].astype(BF16), w_proj_dn[l].astype(BF16), w_out[l].astype(BF16),
               ln1_w[l].reshape(1, d), ln1_b[l].reshape(1, d), wr_hi, wr_lo, br)
        res_p = _outproj(attn_p, o_p.reshape(t_p, vdim), g_p, x_p, gt1p, sh2p, sc2p, wts, None,
                         per_token_mod=False, tiles_per_seq=tps, t_total=t_all, tile_off=0, alpha=alpha)
        x1, pos, gate, cnt = _outproj(attn_s, o_s, g_s, x_s, gt1s, sh2s, sc2s, wts, res_p, per_token_mod=True,
                                      tiles_per_seq=1, t_total=t_all, tile_off=n_ptiles, alpha=alpha)

        bm = EXPERT_ROWS
        nt_all = t_all // tm
        cnt8 = (cnt[:, 0, :n_e].astype(I32) + RUN_ALIGN - 1) // RUN_ALIGN * RUN_ALIGN
        tot = jnp.sum(cnt8, axis=0)
        padded = (tot + bm - 1) // bm * bm
        pad_end = jnp.cumsum(padded)
        pad_start = pad_end - padded
        gbase = pad_start[None, :] + jnp.cumsum(cnt8, axis=0) - cnt8
        toff = jnp.cumsum(cnt8, axis=1) - cnt8
        n_local = -(-(tm * TOP_K + n_e * (RUN_ALIGN - 1)) // LANES) * LANES
        n_rows = -(-(t_all * TOP_K + nt_all * n_e * (RUN_ALIGN - 1) + n_e * (bm - 1)) // bm) * bm
        nblk = n_rows // bm
        n_used = jnp.maximum(pad_end[-1:] // bm, 1).astype(I32)
        block_e = jnp.minimum(jnp.sum(pad_end[None, :] <= (jnp.arange(nblk, dtype=I32) * bm)[:, None], axis=1),
                              n_e - 1).astype(I32)
        flat = lambda a: a.astype(I32).reshape(nt_all * n_e)
        run_tables = (flat(toff), flat(cnt8), flat(gbase))
        tail_tables = ((pad_start + tot).astype(I32), (padded - tot).astype(I32))

        xs = _dispatch(run_tables + tail_tables, x1, sh2p, sc2p, sh2s, sc2s, pos, n_rows=n_rows, n_local=n_local,
                       n_ptiles=n_ptiles, tiles_per_seq=tps, n_experts=n_e)
        group_end_blk = pad_end // bm
        next_e = jnp.where(group_end_blk < n_used[0], block_e[jnp.minimum(group_end_blk, nblk - 1)], -1).astype(I32)
        yb = _experts(block_e, n_used, next_e, xs, w_gu[l], b_gu[l], w_down[l], b_down[l])
        x_p, x_s = _combine(run_tables, x1, gt2p, gt2s, gate, pos, ln2_w[l].reshape(1, d), ln2_b[l].reshape(1, d),
                            yb, n_local=n_local, n_ptiles=n_ptiles, tiles_per_seq=tps, n_experts=n_e, alpha=alpha)

    st = lambda k: outs[k][0][None] if depth == 1 else jnp.stack(outs[k])
    return (x_p.reshape(n_p, seq, d), x_s.reshape(n_s, l_s, d), st("pwk"), st("pwv"), st("pcv"), st("pss"),
            st("swk"), st("swv"), st("scv"), st("sss"))
```

```python
import functools

import numpy as np
import jax
import jax.numpy as jnp
from jax import lax
from jax.experimental import pallas as pl
from jax.experimental.pallas import tpu as pltpu

F32 = jnp.float32
BF16 = jnp.bfloat16
I32 = jnp.int32
U32 = jnp.uint32

PAST_LEN = 16384
ROPE_THETA = 500000.0
TOP_K = 4
SWIGLU_LIMIT = 7.0
SWIGLU_ALPHA = 1.702
DN_CHUNK = 64
LN_EPS = 1e-5
RMS_EPS = 1e-6
L2_EPS = 1e-6

LANES = 128
SUBLANES = 8
VMEM_LIMIT_BYTES = 56 * 1024 * 1024

TOKEN_TILE = 256
INPROJ_TILE = 512
EXPERT_ROWS = 512
RUN_ALIGN = SUBLANES
OUTPROJ_SUBTILES = 2
N_ROUTE_BUFS = 4
NEG_BIG = -1e30


def _cparams(*sem):
    return pltpu.CompilerParams(dimension_semantics=sem, vmem_limit_bytes=VMEM_LIMIT_BYTES)


def _silu(x):
    return x * jax.nn.sigmoid(x)


def _bdot(a, b):
    return jnp.dot(a.astype(BF16), b.astype(BF16), preferred_element_type=F32)


def _pack_halves(x):
    n = x.shape[1] // 2
    lo = lax.bitcast_convert_type(x[:, :n], U32)
    hi = lax.bitcast_convert_type(x[:, n:], U32)
    return (hi & jnp.uint32(0xFFFF0000)) | (lo >> 16)


def _unpack_halves(w):
    lo = lax.bitcast_convert_type(w << 16, F32).astype(BF16)
    hi = lax.bitcast_convert_type(w & jnp.uint32(0xFFFF0000), F32).astype(BF16)
    return lo, hi


def _ada_body(c_ref, w_ref, b_ref, o_ref):
    o_ref[...] = _bdot(_silu(c_ref[...]), w_ref[...]) + b_ref[...]


def _ada(c_all, w_ada, b_ada):
    n, d = c_all.shape
    dout = w_ada.shape[1]
    tn = d
    return pl.pallas_call(
        _ada_body,
        grid=(dout // tn,),
        in_specs=[pl.BlockSpec((n, d), lambda j: (0, 0)),
                  pl.BlockSpec((d, tn), lambda j: (0, j)),
                  pl.BlockSpec((1, tn), lambda j: (0, j))],
        out_specs=pl.BlockSpec((n, tn), lambda j: (0, j)),
        out_shape=jax.ShapeDtypeStruct((n, dout), F32),
        compiler_params=_cparams("arbitrary"),
        name="ada",
    )(c_all, w_ada, b_ada.reshape(1, dout))


def _inproj_body(x_ref, sh_ref, sc_ref, ct_ref, s1_ref, s2_ref, w_ref,
                 a_ref, dn_ref, z_ref, g_ref, ba_ref, *win_refs, cuts, n_rot_chunks, half_rot, window):
    h = (x_ref[...] * (1.0 + sc_ref[...]) + sh_ref[...]).astype(BF16)

    def mm(lo, hi):
        return jnp.dot(h, w_ref[:, lo:hi], preferred_element_type=F32)

    c_a, c_dn, c_z, c_g, c_ba = cuts
    qkv = mm(0, c_a)
    ct, s1, s2 = ct_ref[...], s1_ref[...], s2_ref[...]
    cols = []
    for c in range(n_rot_chunks):
        xc = qkv[:, c * LANES:(c + 1) * LANES]
        cols.append(xc * ct + pltpu.roll(xc, LANES - half_rot, 1) * s1 + pltpu.roll(xc, half_rot, 1) * s2)
    cols.append(qkv[:, n_rot_chunks * LANES:])
    rot = jnp.concatenate(cols, axis=1)
    a_ref[...] = rot.astype(a_ref.dtype)
    dn = mm(c_a, c_dn)
    dn_ref[...] = dn.astype(dn_ref.dtype)
    z_ref[...] = mm(c_dn, c_z).astype(z_ref.dtype)
    g_ref[...] = mm(c_z, c_g).astype(g_ref.dtype)
    ba_ref[...] = mm(c_g, c_ba)
    if win_refs:
        tail_ref, kvw_ref = win_refs
        tm = dn.shape[0]
        tail_ref[...] = dn[tm - SUBLANES:, :]
        kvw_ref[...] = rot[tm - window:, n_rot_chunks * LANES - LANES:]


def _inproj(x, sh, sc, tabs, w_perm, cuts, *, per_token_mod, tiles_per_seq, act_dtype, window, kv_cols, half_rot):
    t, d = x.shape
    tm = INPROJ_TILE
    nt = t // tm
    assert t % tm == 0 and tm >= window and (per_token_mod or nt % tiles_per_seq == 0)
    c_a, c_dn, c_z, c_g, c_ba = cuts
    n_rot_chunks = (c_a - kv_cols // 2) // LANES
    if per_token_mod:
        mod_spec = pl.BlockSpec((tm, d), lambda i: (i, 0))
        tab_spec = pl.BlockSpec((tm, LANES), lambda i: (0, 0))
    else:
        mod_spec = pl.BlockSpec((None, 1, d), lambda i: (i // tiles_per_seq, 0, 0))
        tab_spec = pl.BlockSpec((tm, LANES), lambda i: (i % tiles_per_seq, 0))
    out_shape = [jax.ShapeDtypeStruct((t, c_a), act_dtype),
                 jax.ShapeDtypeStruct((t, c_dn - c_a), act_dtype),
                 jax.ShapeDtypeStruct((t, c_z - c_dn), act_dtype),
                 jax.ShapeDtypeStruct((t, c_g - c_z), act_dtype),
                 jax.ShapeDtypeStruct((t, c_ba - c_g), F32)]
    out_specs = [pl.BlockSpec((tm, s.shape[1]), lambda i: (i, 0)) for s in out_shape]
    with_win = not per_token_mod
    if with_win:
        n_seq = nt // tiles_per_seq
        out_shape += [jax.ShapeDtypeStruct((nt, SUBLANES, c_dn - c_a), F32),
                      jax.ShapeDtypeStruct((n_seq, window, kv_cols), F32)]
        out_specs += [pl.BlockSpec((None, SUBLANES, c_dn - c_a), lambda i: (i, 0, 0)),
                      pl.BlockSpec((None, window, kv_cols), lambda i: (i // tiles_per_seq, 0, 0))]
    body = functools.partial(_inproj_body, cuts=cuts, n_rot_chunks=n_rot_chunks, half_rot=half_rot, window=window)
    return pl.pallas_call(
        body,
        grid=(nt,),
        in_specs=[pl.BlockSpec((tm, d), lambda i: (i, 0)), mod_spec, mod_spec,
                  tab_spec, tab_spec, tab_spec,
                  pl.BlockSpec((d, c_ba), lambda i: (0, 0))],
        out_specs=out_specs,
        out_shape=out_shape,
        compiler_params=_cparams("arbitrary"),
        name="inproj",
    )(x, sh, sc, *tabs, w_perm)


def _softmax_sink_pv(scores, valid, sinks, values):
    ms = [jnp.where(valid, s, -jnp.inf) for s in scores]
    m = [jnp.maximum(jnp.max(x, axis=-1, keepdims=True), sk) for x, sk in zip(ms, sinks)]
    p = [jnp.exp(x - mi) for x, mi in zip(ms, m)]
    den = [jnp.sum(pi, axis=-1, keepdims=True) + jnp.exp(sk - mi) for pi, sk, mi in zip(p, sinks, m)]
    return [jnp.dot((pi / di).astype(BF16), v, preferred_element_type=F32) for pi, di, v in zip(p, den, values)]


def _attn_prompt_body(sink_ref, q_ref, kvp_ref, kvc_ref, o_ref, *, n_q, n_kv, hd, window):
    j = pl.program_id(1)
    group = n_q // n_kv
    scale = hd ** -0.5
    assert np.log2(scale) == int(np.log2(scale))
    r = lax.broadcasted_iota(I32, (window, 2 * window), 0)
    c = lax.broadcasted_iota(I32, (window, 2 * window), 1)
    rel = window + r - c
    band = (rel >= 0) & (rel < window)
    kv_cur = kvc_ref[...]
    kv_first = jnp.concatenate([kvp_ref[...], kv_cur[:window]], axis=0)
    for qb, (kv, valid) in enumerate(((kv_first, band & ((c >= window) | (j > 0))), (kv_cur, band))):
        q = q_ref[qb * window:(qb + 1) * window, :] * scale
        scores = [lax.dot_general(q[:, h * hd:(h + 1) * hd], kv[:, (h // group) * hd:(h // group + 1) * hd],
                                  (((1,), (1,)), ((), ())), preferred_element_type=F32) for h in range(n_q)]
        values = [kv[:, (n_kv + h // group) * hd:(n_kv + h // group + 1) * hd] for h in range(n_q)]
        outs = _softmax_sink_pv(scores, valid, [sink_ref[h] for h in range(n_q)], values)
        o_ref[qb * window:(qb + 1) * window, :] = jnp.concatenate(outs, axis=1).astype(o_ref.dtype)


def _attn_prompt(qkva, sinks, n_seq, seq, n_q, n_kv, hd, window):
    qd, kvd = n_q * hd, 2 * n_kv * hd
    x3 = qkva.reshape(n_seq, seq, qd + kvd)
    nb = seq // window
    assert nb % 2 == 0 and qd % kvd == 0
    kv_blk = qd // kvd
    body = functools.partial(_attn_prompt_body, n_q=n_q, n_kv=n_kv, hd=hd, window=window)
    out = pl.pallas_call(
        body,
        grid=(n_seq, nb // 2),
        in_specs=[pl.BlockSpec(memory_space=pltpu.SMEM),
                  pl.BlockSpec((None, 2 * window, qd), lambda n, j: (n, j, 0)),
                  pl.BlockSpec((None, window, kvd), lambda n, j: (n, jnp.maximum(2 * j - 1, 0), kv_blk)),
                  pl.BlockSpec((None, 2 * window, kvd), lambda n, j: (n, j, kv_blk))],
        out_specs=pl.BlockSpec((None, 2 * window, qd), lambda n, j: (n, j, 0)),
        out_shape=jax.ShapeDtypeStruct((n_seq, seq, qd), BF16),
        compiler_params=_cparams("arbitrary", "arbitrary"),
        name="attn_prompt",
    )(sinks, x3, x3, x3)
    return out.reshape(n_seq * seq, qd)


def _attn_sample_body(sink_ref, q_ref, wk_ref, wv_ref, o_ref, wko_ref, wvo_ref,
                      *, bs, n_q, n_kv, hd, window, l_new, l_pad):
    group = n_q // n_kv
    qd = n_q * hd
    kd = n_kv * hd
    rows = group * l_pad
    r = lax.broadcasted_iota(I32, (rows, window + l_pad), 0) % l_pad
    c = lax.broadcasted_iota(I32, (rows, window + l_pad), 1)
    rel = window + r - c
    valid = (rel >= 0) & (rel < window) & (c < window + l_new)
    sinks = [jnp.concatenate([jnp.full((l_pad, 1), sink_ref[kvh * group + g], F32) for g in range(group)], axis=0)
             for kvh in range(n_kv)]
    qs, ks, vs = [], [], []
    for b in range(bs):
        x = q_ref[b]
        k_new = x[:, qd:qd + kd]
        v_new = x[:, qd + kd:]
        wko_ref[b, 0:window - l_new, :] = wk_ref[b, l_new:window, :]
        wko_ref[b, window - l_new:window, :] = k_new[0:l_new, :]
        wvo_ref[b, 0:window - l_new, :] = wv_ref[b, l_new:window, :]
        wvo_ref[b, window - l_new:window, :] = v_new[0:l_new, :]
        k_all = jnp.concatenate([wk_ref[b], k_new], axis=0).astype(BF16)
        v_all = jnp.concatenate([wv_ref[b], v_new], axis=0).astype(BF16)
        for kvh in range(n_kv):
            qs.append(jnp.concatenate([x[:, (kvh * group + g) * hd:(kvh * group + g + 1) * hd]
                                       for g in range(group)], axis=0).astype(BF16))
            ks.append(k_all[:, kvh * hd:(kvh + 1) * hd])
            vs.append(v_all[:, kvh * hd:(kvh + 1) * hd])
    n = len(qs)
    s = [lax.dot_general(qs[i], ks[i], (((1,), (1,)), ((), ())), preferred_element_type=F32) * (hd ** -0.5)
         for i in range(n)]
    o = _softmax_sink_pv(s, valid, [sinks[i % n_kv] for i in range(n)], vs)
    for b in range(bs):
        outs = [o[b * n_kv + kvh][g * l_pad:(g + 1) * l_pad, :] for kvh in range(n_kv) for g in range(group)]
        o_ref[b] = jnp.concatenate(outs, axis=1).astype(o_ref.dtype)


def _attn_sample(qkva_pad, win_k, win_v, sinks, n_q, n_kv, hd, l_new):
    n, l_pad, width = qkva_pad.shape
    window, kd = win_k.shape[1], win_k.shape[2]
    qd = n_q * hd
    bs = 8
    body = functools.partial(_attn_sample_body, bs=bs, n_q=n_q, n_kv=n_kv, hd=hd, window=window,
                             l_new=l_new, l_pad=l_pad)
    return pl.pallas_call(
        body,
        grid=(n // bs,),
        in_specs=[pl.BlockSpec(memory_space=pltpu.SMEM),
                  pl.BlockSpec((bs, l_pad, width), lambda i: (i, 0, 0)),
                  pl.BlockSpec((bs, window, kd), lambda i: (i, 0, 0)),
                  pl.BlockSpec((bs, window, kd), lambda i: (i, 0, 0))],
        out_specs=[pl.BlockSpec((bs, l_pad, qd), lambda i: (i, 0, 0)),
                   pl.BlockSpec((bs, window, kd), lambda i: (i, 0, 0)),
                   pl.BlockSpec((bs, window, kd), lambda i: (i, 0, 0))],
        out_shape=[jax.ShapeDtypeStruct((n, l_pad, qd), BF16),
                   jax.ShapeDtypeStruct((n, window, kd), F32),
                   jax.ShapeDtypeStruct((n, window, kd), F32)],
        compiler_params=_cparams("arbitrary"),
        name="attn_sample",
    )(sinks, qkva_pad, win_k, win_v)


def _split_bf16(x):
    hi = x.astype(BF16)
    return hi, (x - hi.astype(F32)).astype(BF16)


def _tdot(a, b):
    ah, al = _split_bf16(a)
    bh, bl = _split_bf16(b)
    m = a.shape[0]
    t = jnp.dot(jnp.concatenate([ah, al], axis=0), bh, preferred_element_type=F32)
    return t[:m] + t[m:] + jnp.dot(ah, bl, preferred_element_type=F32)


def _dn_body(qkv_ref, z_ref, ba_ref, bat_ref, cs0_ref, s0_ref, wc_ref, hp_ref, nw_ref,
             o_ref, s_ref, xbuf, *, nb, chunk, heads, dk, dv, l_real, conv_w):
    c_idx = pl.program_id(1)
    hc = SUBLANES

    @pl.when(c_idx == 0)
    def _():
        xbuf[:, 0:hc, :] = cs0_ref[...]
        s_ref[...] = s0_ref[...]

    qk_dim = heads * dk
    row = lax.broadcasted_iota(I32, (chunk, chunk), 0)
    col = lax.broadcasted_iota(I32, (chunk, chunk), 1)
    incl = row >= col
    strict = row > col
    eye = (row == col).astype(F32)
    valid_c = row[:, 0:1] < l_real
    valid_r = col[0:1, :] < l_real
    n_levels = max(1, int(np.ceil(np.log2(chunk))))
    wc = wc_ref[...]
    hp = hp_ref[...]
    neg_exp_alog = -jnp.exp(hp[0:1, :])
    dt_bias = hp[1:2, :]
    nw = nw_ref[...]
    chains = [(b, h) for b in range(nb) for h in range(heads)]
    n = len(chains)

    ys = []
    for b in range(nb):
        xbuf[b, hc:hc + chunk, :] = qkv_ref[b].astype(F32)
        y = xbuf[b, hc:hc + chunk, :] * wc[conv_w - 1:conv_w, :]
        for j in range(conv_w - 1):
            off = hc - (conv_w - 1) + j
            y = y + xbuf[b, off:off + chunk, :] * wc[j:j + 1, :]
        ys.append(_silu(y))
        xbuf[b, 0:hc, :] = xbuf[b, chunk:chunk + hc, :]

    qn, kn, kb, vb, decay, e_gc, e_rest, e_last = [], [], [], [], [], [], [], []
    for b, h in chains:
        y = ys[b]
        qh = y[:, h * dk:(h + 1) * dk]
        kh = y[:, qk_dim + h * dk:qk_dim + (h + 1) * dk]
        vh = y[:, 2 * qk_dim + h * dv:2 * qk_dim + (h + 1) * dv]
        ba = ba_ref[b]
        bat = bat_ref[b]
        ne = neg_exp_alog[:, h:h + 1]
        db = dt_bias[:, h:h + 1]
        beta = jnp.where(valid_c, jax.nn.sigmoid(ba[:, h:h + 1]), 0.0)
        g_col = jnp.where(valid_c, ne * jax.nn.softplus(ba[:, heads + h:heads + h + 1] + db), 0.0)
        g_row = jnp.where(valid_r, ne * jax.nn.softplus(bat[heads + h:heads + h + 1, :] + db), 0.0)
        gc_col = jnp.sum(jnp.where(incl, g_row, 0.0), axis=1, keepdims=True)
        gc_row = jnp.sum(jnp.where(row <= col, g_col, 0.0), axis=0, keepdims=True)
        g_last = gc_col[chunk - 1:chunk, :]
        q_ = qh * lax.rsqrt(jnp.sum(qh * qh, -1, keepdims=True) + L2_EPS) * (dk ** -0.5)
        k_ = kh * lax.rsqrt(jnp.sum(kh * kh, -1, keepdims=True) + L2_EPS)
        qn.append(q_)
        kn.append(k_)
        kb.append(k_ * beta)
        vb.append(vh * beta)
        decay.append(jnp.where(incl, jnp.exp(gc_col - gc_row), 0.0))
        e_gc.append(jnp.exp(gc_col))
        e_rest.append(jnp.exp(g_last - gc_col))
        e_last.append(jnp.exp(g_last))

    sc = [lax.dot_general(jnp.concatenate([qn[i], kb[i]], axis=0).astype(BF16), kn[i].astype(BF16),
                          (((1,), (1,)), ((), ())), preferred_element_type=F32) for i in range(n)]
    qk = [sc[i][:chunk] * decay[i] for i in range(n)]
    p = [jnp.where(strict, -(sc[i][chunk:] * decay[i]), 0.0) for i in range(n)]
    t_inv = [eye + p[i] for i in range(n)]
    if n_levels > 1:
        p = [_tdot(p[i], p[i]) for i in range(n)]
    for lvl in range(1, n_levels):
        if lvl < n_levels - 1:
            yp = [_tdot(jnp.concatenate([t_inv[i], p[i]], axis=0), p[i]) for i in range(n)]
            t_inv = [t_inv[i] + yp[i][:chunk] for i in range(n)]
            p = [yp[i][chunk:] for i in range(n)]
        else:
            t_inv = [t_inv[i] + _tdot(t_inv[i], p[i]) for i in range(n)]
    sol = [_tdot(t_inv[i], jnp.concatenate([vb[i], kb[i] * e_gc[i]], axis=1)) for i in range(n)]
    s_old = [s_ref[b, h] for b, h in chains]
    wq = [_bdot(jnp.concatenate([sol[i][:, dv:], qn[i] * e_gc[i]], axis=0), s_old[i]) for i in range(n)]
    v_new = [sol[i][:, :dv] - wq[i][:chunk] for i in range(n)]
    o = [wq[i][chunk:] + _bdot(qk[i], v_new[i]) for i in range(n)]
    for i, (b, h) in enumerate(chains):
        s_ref[b, h] = s_old[i] * e_last[i] + lax.dot_general(
            (kn[i] * e_rest[i]).astype(BF16), v_new[i].astype(BF16), (((0,), (0,)), ((), ())),
            preferred_element_type=F32)
    for b in range(nb):
        zt = z_ref[b].astype(F32)
        outs = []
        for h in range(heads):
            oi = o[b * heads + h]
            on = oi * lax.rsqrt(jnp.mean(oi * oi, -1, keepdims=True) + RMS_EPS) * nw
            outs.append(on * _silu(zt[:, h * dv:(h + 1) * dv]))
        o_ref[b] = jnp.concatenate(outs, axis=1).astype(o_ref.dtype)


def _deltanet(qkv, z, ba, bat, cs0, s0, w_conv, hp, norm_w, *, chunk, l_real, nb):
    n, l, conv_dim = qkv.shape
    heads, dk, dv = s0.shape[1:]
    nc = l // chunk
    assert n % nb == 0 and l % chunk == 0
    conv_w = w_conv.shape[0]
    hist = pltpu.VMEM((nb, SUBLANES + chunk, conv_dim), F32)
    body = functools.partial(_dn_body, nb=nb, chunk=chunk, heads=heads, dk=dk, dv=dv, l_real=l_real, conv_w=conv_w)
    return pl.pallas_call(
        body,
        grid=(n // nb, nc),
        in_specs=[pl.BlockSpec((nb, chunk, conv_dim), lambda i, c: (i, c, 0)),
                  pl.BlockSpec((nb, chunk, heads * dv), lambda i, c: (i, c, 0)),
                  pl.BlockSpec((nb, chunk, LANES), lambda i, c: (i, c, 0)),
                  pl.BlockSpec((nb, None, SUBLANES, chunk), lambda i, c: (i, c, 0, 0)),
                  pl.BlockSpec((nb, SUBLANES, conv_dim), lambda i, c: (i, 0, 0)),
                  pl.BlockSpec((nb, heads, dk, dv), lambda i, c: (i, 0, 0, 0)),
                  pl.BlockSpec((conv_w, conv_dim), lambda i, c: (0, 0)),
                  pl.BlockSpec((SUBLANES, LANES), lambda i, c: (0, 0)),
                  pl.BlockSpec((1, dv), lambda i, c: (0, 0))],
        out_specs=[pl.BlockSpec((nb, chunk, heads * dv), lambda i, c: (i, c, 0)),
                   pl.BlockSpec((nb, heads, dk, dv), lambda i, c: (i, 0, 0, 0))],
        out_shape=[jax.ShapeDtypeStruct((n, l, heads * dv), BF16),
                   jax.ShapeDtypeStruct((n, heads, dk, dv), F32)],
        scratch_shapes=[hist],
        compiler_params=_cparams("arbitrary", "arbitrary"),
        name="deltanet",
    )(qkv, z, ba, bat, cs0, s0, w_conv, hp, norm_w)


def _layer_norm(r, w, b):
    mu = jnp.mean(r, -1, keepdims=True)
    var = jnp.mean(jnp.square(r - mu), -1, keepdims=True)
    return (r - mu) * lax.rsqrt(var + LN_EPS) * w + b


def _outproj_body(attn_ref, dn_ref, g_ref, x_ref, gt_ref, sh2_ref, sc2_ref, wpa_ref, wpd_ref, wo_ref,
                  lnw_ref, lnb_ref, wrh_ref, wrl_ref, br_ref, *refs, alpha, top_k, aliased):
    if aliased:
        refs = refs[N_ROUTE_BUFS:]
    x1_ref, pos_ref, gate_ref, cnt_ref = refs
    for s in range(cnt_ref.shape[0]):
        _outproj_tile(s, attn_ref, dn_ref, g_ref, x_ref, gt_ref, sh2_ref, sc2_ref, wpa_ref, wpd_ref, wo_ref, lnw_ref,
                      lnb_ref, wrh_ref, wrl_ref, br_ref, x1_ref, pos_ref, gate_ref, cnt_ref,
                      alpha=alpha, top_k=top_k)


def _outproj_tile(s, attn_ref, dn_ref, g_ref, x_ref, gt_ref, sh2_ref, sc2_ref, wpa_ref, wpd_ref, wo_ref, lnw_ref,
                  lnb_ref, wrh_ref, wrl_ref, br_ref, x1_ref, pos_ref, gate_ref, cnt_ref, *, alpha, top_k):
    d = x_ref.shape[1]
    tm = TOKEN_TILE
    rows = pl.ds(s * tm, tm)

    def mod(ref):
        return ref[...] if ref.shape[0] == 1 else ref[rows, :]

    g = g_ref[rows, :].astype(F32)
    pa = jnp.dot(attn_ref[rows, :], wpa_ref[...], preferred_element_type=F32)
    pd = jnp.dot(dn_ref[rows, :], wpd_ref[...], preferred_element_type=F32)
    merged = jax.nn.sigmoid(g[:, :d]) * pa + jax.nn.sigmoid(g[:, d:]) * pd
    mix = jnp.dot(merged.astype(BF16), wo_ref[...], preferred_element_type=F32)
    x1 = _layer_norm(alpha * x_ref[rows, :] + mod(gt_ref) * mix, lnw_ref[...], lnb_ref[...])
    x1_ref[rows, :] = x1
    h2 = x1 * (1.0 + mod(sc2_ref)) + mod(sh2_ref)
    h_hi, h_lo = _split_bf16(h2)
    lg = jnp.dot(jnp.concatenate([h_hi, h_lo], axis=0), wrh_ref[...], preferred_element_type=F32)
    logits = lg[:tm] + lg[tm:] + jnp.dot(h_hi, wrl_ref[...], preferred_element_type=F32) + br_ref[...]
    lane = lax.broadcasted_iota(I32, (tm, LANES), 1)
    lane_f = lane.astype(F32)
    vals, sels = [], []
    l = logits
    for _ in range(top_k):
        m = jnp.max(l, axis=1, keepdims=True)
        idx = jnp.min(jnp.where(l == m, lane_f, float(LANES)), axis=1, keepdims=True)
        sel = lane_f == idx
        vals.append(m)
        sels.append(sel)
        l = jnp.where(sel, -jnp.inf, l)
    ex = [jnp.exp(v - vals[0]) for v in vals]
    den = ex[0]
    for e in ex[1:]:
        den = den + e
    multi_hot = jnp.zeros((tm, LANES), F32)
    for sel in sels:
        multi_hot = multi_hot + jnp.where(sel, 1.0, 0.0)
    r_i = lax.broadcasted_iota(I32, (tm, tm), 0)
    c_i = lax.broadcasted_iota(I32, (tm, tm), 1)
    lower = jnp.where(r_i > c_i, 1.0, 0.0).astype(BF16)
    prefix = jnp.dot(lower, multi_hot.astype(BF16), preferred_element_type=F32)
    counts = jnp.sum(multi_hot, axis=0, keepdims=True)
    cnt_pad = jnp.floor((counts + (RUN_ALIGN - 1)) * (1.0 / RUN_ALIGN)) * RUN_ALIGN
    e_r = lax.broadcasted_iota(I32, (LANES, LANES), 0)
    e_c = lax.broadcasted_iota(I32, (LANES, LANES), 1)
    before = jnp.where(e_r < e_c, 1.0, 0.0).astype(BF16)
    run_off = jnp.dot(jnp.broadcast_to(cnt_pad, (SUBLANES, LANES)).astype(BF16), before,
                      preferred_element_type=F32)[0:1, :]
    g_out = jnp.zeros((tm, LANES), F32)
    p_out = jnp.zeros((tm, LANES), F32)
    for k in range(top_k):
        pos_k = jnp.sum(jnp.where(sels[k], prefix + run_off, 0.0), axis=1, keepdims=True)
        g_out = jnp.where(lane == k, ex[k] / den, g_out)
        p_out = jnp.where(lane == k, pos_k, p_out)
    pos_ref[rows, :] = p_out.astype(I32)
    gate_ref[rows, :] = g_out
    cnt_ref[s] = jnp.broadcast_to(counts, cnt_ref.shape[1:])


def _outproj(attn, dn, gates, x, gt, sh2, sc2, wts, bufs, *, per_token_mod, tiles_per_seq, t_total, tile_off, alpha):
    t, d = x.shape
    sub = OUTPROJ_SUBTILES
    tm = TOKEN_TILE * sub
    nt = t // tm
    assert t % tm == 0 and tile_off % sub == 0 and (per_token_mod or tiles_per_seq % sub == 0)
    wpa, wpd, wo, lnw, lnb, wrh, wrl, br = wts
    if per_token_mod:
        mod_spec = pl.BlockSpec((tm, d), lambda i: (i, 0))
    else:
        mod_spec = pl.BlockSpec((None, 1, d), lambda i: (i // (tiles_per_seq // sub), 0, 0))

    def row(width):
        return pl.BlockSpec((tm, width), lambda i: (i, 0))

    def full(a):
        return pl.BlockSpec(a.shape, lambda i: (0,) * a.ndim)

    aliased = bufs is not None
    in_specs = [row(attn.shape[1]), row(dn.shape[1]), row(gates.shape[1]), row(d), mod_spec, mod_spec, mod_spec,
                full(wpa), full(wpd), full(wo), full(lnw), full(lnb), full(wrh), full(wrl), full(br)]
    args = [attn, dn, gates, x, gt, sh2, sc2, wpa, wpd, wo, lnw, lnb, wrh, wrl, br]
    io_alias = {}
    if aliased:
        for k, bfr in enumerate(bufs):
            in_specs.append(pl.BlockSpec(memory_space=pl.ANY))
            io_alias[len(args)] = k
            args.append(bfr)
    step_off = tile_off // sub
    out_row = lambda width: pl.BlockSpec((tm, width), lambda i: (i + step_off, 0))
    out_shape = [jax.ShapeDtypeStruct((t_total, d), F32), jax.ShapeDtypeStruct((t_total, LANES), I32),
                 jax.ShapeDtypeStruct((t_total, LANES), F32),
                 jax.ShapeDtypeStruct((t_total // TOKEN_TILE, SUBLANES, LANES), F32)]
    assert len(out_shape) == N_ROUTE_BUFS and TOKEN_TILE <= 256
    out_specs = [out_row(d), out_row(LANES), out_row(LANES),
                 pl.BlockSpec((sub, SUBLANES, LANES), lambda i: (i + step_off, 0, 0))]
    body = functools.partial(_outproj_body, alpha=alpha, top_k=TOP_K, aliased=aliased)
    return pl.pallas_call(
        body,
        grid=(nt,),
        in_specs=in_specs,
        out_specs=out_specs,
        out_shape=out_shape,
        input_output_aliases=io_alias,
        compiler_params=_cparams("arbitrary"),
        name="outproj",
    )(*args)


def _select_mod(i, n_ptiles, seq_ref, tok_ref):
    return jnp.where(i < n_ptiles, seq_ref[...], tok_ref[...])


def _for_run(rows, local_off, global_off, fn, lane=0):
    @pl.when(rows > 0)
    def _():
        fn(pl.multiple_of(local_off, RUN_ALIGN), pl.multiple_of(global_off, RUN_ALIGN),
           pl.multiple_of(rows, RUN_ALIGN), lane)


def _for_each_run(i, n_experts, toff_ref, cnt8_ref, gbase_ref, fn):
    for e in range(n_experts):
        idx = i * n_experts + e
        _for_run(cnt8_ref[idx], toff_ref[idx], gbase_ref[idx], fn, lane=e % 2)


def _tile_rows(tile, n_experts, toff_ref, cnt8_ref):
    last = tile * n_experts + n_experts - 1
    return pl.multiple_of(toff_ref[last] + cnt8_ref[last], RUN_ALIGN)


def _scatter_matrix(pos_ref, n_local, values):
    pos = pos_ref[...]
    col = lax.broadcasted_iota(I32, (pos.shape[0], n_local), 1)
    out = jnp.zeros((pos.shape[0], n_local), F32)
    for k in range(TOP_K):
        out = jnp.where(col == pos[:, k:k + 1], 1.0 if values is None else values[:, k:k + 1], out)
    return out


def _dispatch_body(toff_ref, cnt8_ref, gbase_ref, tstart_ref, trows_ref,
                   x1_ref, shs_ref, scs_ref, sht_ref, sct_ref, pos_ref,
                   xs_ref, lbuf, zbuf, sem, *, n_tiles, n_ptiles, n_experts):
    i = pl.program_id(0)
    sc = _select_mod(i, n_ptiles, scs_ref, sct_ref)
    sh = _select_mod(i, n_ptiles, shs_ref, sht_ref)
    h2 = (x1_ref[...] * (1.0 + sc) + sh).astype(BF16)
    n_local = lbuf.shape[1]
    slot = i % 2

    def runs(tile, sl, act):
        def fn(lo, go, rows, lane):
            act(pltpu.make_async_copy(lbuf.at[sl, pl.ds(lo, rows), :], xs_ref.at[pl.ds(go, rows), :], sem.at[sl]),
                lane)
        _for_each_run(tile, n_experts, toff_ref, cnt8_ref, gbase_ref, fn)

    start = lambda cp, lane: cp.start(priority=lane)
    wait = lambda cp, lane: cp.wait()

    def wait_runs(tile, sl):
        rows = _tile_rows(tile, n_experts, toff_ref, cnt8_ref)
        pltpu.make_async_copy(lbuf.at[sl, pl.ds(0, rows), :], xs_ref.at[pl.ds(0, rows), :], sem.at[sl]).wait()

    @pl.when(i >= 2)
    def _():
        wait_runs(i - 2, slot)

    onehot = _scatter_matrix(pos_ref, n_local, None)
    srt = lax.dot_general(onehot.astype(BF16), h2, (((0,), (0,)), ((), ())), preferred_element_type=F32)
    lbuf[slot] = _pack_halves(srt)
    runs(i, slot, start)

    @pl.when(i == n_tiles - 1)
    def _():
        wait_runs(i - 1, 1 - slot)
        wait_runs(i, slot)
        zbuf[...] = jnp.zeros(zbuf.shape, zbuf.dtype)

        def zero_fill(e, act):
            _for_run(trows_ref[e], 0, tstart_ref[e], lambda lo, go, rows, lane: act(pltpu.make_async_copy(
                zbuf.at[pl.ds(0, rows), :], xs_ref.at[pl.ds(go, rows), :], sem.at[0]), lane), lane=e % 2)

        for e in range(n_experts):
            zero_fill(e, start)
        for e in range(n_experts):
            zero_fill(e, wait)


def _dispatch(tables, x1, sh_seq, sc_seq, sh_tok, sc_tok, pos, *, n_rows, n_local, n_ptiles, tiles_per_seq,
              n_experts):
    t, d = x1.shape
    tm = TOKEN_TILE
    nt = t // tm
    n_seq = sh_seq.shape[0]
    assert nt >= 2 and d % 2 == 0
    body = functools.partial(_dispatch_body, n_tiles=nt, n_ptiles=n_ptiles, n_experts=n_experts)
    seq_spec = pl.BlockSpec((None, 1, d), lambda i, *_: (jnp.minimum(i // tiles_per_seq, n_seq - 1), 0, 0))
    tok_spec = pl.BlockSpec((tm, d), lambda i, *_: (jnp.maximum(i - n_ptiles, 0), 0))
    return pl.pallas_call(
        body,
        grid_spec=pltpu.PrefetchScalarGridSpec(
            num_scalar_prefetch=len(tables),
            grid=(nt,),
            in_specs=[pl.BlockSpec((tm, d), lambda i, *_: (i, 0)),
                      seq_spec, seq_spec, tok_spec, tok_spec,
                      pl.BlockSpec((tm, LANES), lambda i, *_: (i, 0))],
            out_specs=pl.BlockSpec(memory_space=pl.ANY),
            scratch_shapes=[pltpu.VMEM((2, n_local, d // 2), U32), pltpu.VMEM((EXPERT_ROWS, d // 2), U32),
                            pltpu.SemaphoreType.DMA((2,))],
        ),
        out_shape=jax.ShapeDtypeStruct((n_rows, d // 2), U32),
        compiler_params=_cparams("arbitrary"),
        name="dispatch",
    )(*tables, x1, sh_seq, sc_seq, sh_tok, sc_tok, pos)


def _expert_body(be_ref, nu_ref, first_ref, next_ref, xs_ref, wgu_hbm, bgu_ref, wd_hbm, bd_ref, y_ref,
                 stage_gu, stage_dn, wgu_bf, wd_bf, sem, *, de):
    i = pl.program_id(0)
    used = i < nu_ref[0]
    expert = be_ref[i]

    def weights(e, act):
        act(pltpu.make_async_copy(wgu_hbm.at[e], stage_gu, sem.at[0]))
        act(pltpu.make_async_copy(wd_hbm.at[e], stage_dn, sem.at[1]))

    @pl.when(i == 0)
    def _():
        weights(expert, lambda cp: cp.start())

    @pl.when(used & (first_ref[i] == 1))
    def _():
        weights(expert, lambda cp: cp.wait())
        wgu_bf[...] = stage_gu[...].astype(BF16)
        wd_bf[...] = stage_dn[...].astype(BF16)
        nxt = next_ref[expert]

        @pl.when(nxt >= 0)
        def _():
            weights(nxt, lambda cp: cp.start())

    @pl.when(used)
    def _():
        x = jnp.concatenate(_unpack_halves(xs_ref[...]), axis=1)
        gu = jnp.dot(x, wgu_bf[...], preferred_element_type=F32) + bgu_ref[...]
        glu = jnp.minimum(gu[:, :de], SWIGLU_LIMIT)
        lin = jnp.clip(gu[:, de:], -SWIGLU_LIMIT, SWIGLU_LIMIT)
        act = glu * jax.nn.sigmoid(SWIGLU_ALPHA * glu) * (lin + 1.0)
        y = jnp.dot(act.astype(BF16), wd_bf[...], preferred_element_type=F32) + bd_ref[...]
        y_ref[...] = _pack_halves(y.astype(BF16).astype(F32))


def _experts(block_e, n_used, next_e, xs, w_gu, b_gu, w_down, b_down):
    p, dh = xs.shape
    bm = EXPERT_ROWS
    n_e, d, de2 = w_gu.shape
    de = de2 // 2
    nblk = p // bm

    def blk(i, be, nu, *_):
        return jnp.minimum(i, nu[0] - 1)

    first = jnp.concatenate([jnp.ones((1,), I32), (block_e[1:] != block_e[:-1]).astype(I32)])
    body = functools.partial(_expert_body, de=de)
    return pl.pallas_call(
        body,
        grid_spec=pltpu.PrefetchScalarGridSpec(
            num_scalar_prefetch=4,
            grid=(nblk,),
            in_specs=[pl.BlockSpec((bm, dh), lambda i, be, nu, *_: (blk(i, be, nu), 0)),
                      pl.BlockSpec(memory_space=pl.ANY),
                      pl.BlockSpec((None, 1, de2), lambda i, be, nu, *_: (be[blk(i, be, nu)], 0, 0)),
                      pl.BlockSpec(memory_space=pl.ANY),
                      pl.BlockSpec((None, 1, d), lambda i, be, nu, *_: (be[blk(i, be, nu)], 0, 0))],
            out_specs=pl.BlockSpec((bm, dh), lambda i, be, nu, *_: (blk(i, be, nu), 0)),
            scratch_shapes=[pltpu.VMEM((d, de2), F32), pltpu.VMEM((de, d), F32),
                            pltpu.VMEM((d, de2), BF16), pltpu.VMEM((de, d), BF16), pltpu.SemaphoreType.DMA((2,))],
        ),
        out_shape=jax.ShapeDtypeStruct((p, dh), U32),
        compiler_params=_cparams("arbitrary"),
        name="experts",
    )(block_e, n_used, first, next_e, xs, w_gu, b_gu.reshape(n_e, 1, de2), w_down, b_down.reshape(n_e, 1, d))


def _combine_body(toff_ref, cnt8_ref, gbase_ref, x1_ref, gts_ref, gtt_ref, gate_ref, pos_ref,
                  lnw_ref, lnb_ref, yb_ref, yp_ref, ys_ref, ybuf, sem, *, top_k, n_ptiles, n_experts, alpha):
    i = pl.program_id(0)
    n_tiles = pl.num_programs(0)
    tm = x1_ref.shape[0]
    n_local = ybuf.shape[1]
    slot = i % 2

    def fetch(tile, sl, act):
        def fn(lo, go, rows, lane):
            act(pltpu.make_async_copy(yb_ref.at[pl.ds(go, rows), :], ybuf.at[sl, pl.ds(lo, rows), :], sem.at[sl]),
                lane)
        _for_each_run(tile, n_experts, toff_ref, cnt8_ref, gbase_ref, fn)

    def start_fetch(tile, sl):
        ybuf[sl, tm * top_k:, :] = jnp.zeros((n_local - tm * top_k, ybuf.shape[2]), ybuf.dtype)
        fetch(tile, sl, lambda cp, lane: cp.start(priority=lane))

    @pl.when(i == 0)
    def _():
        start_fetch(i, slot)

    @pl.when(i + 1 < n_tiles)
    def _():
        start_fetch(i + 1, 1 - slot)

    weights = _scatter_matrix(pos_ref, n_local, gate_ref[...]).astype(BF16)
    rows = _tile_rows(i, n_experts, toff_ref, cnt8_ref)
    pltpu.make_async_copy(yb_ref.at[pl.ds(0, rows), :], ybuf.at[slot, pl.ds(0, rows), :], sem.at[slot]).wait()
    y_lo, y_hi = _unpack_halves(ybuf[slot])
    ff = jnp.concatenate([jnp.dot(weights, y_lo, preferred_element_type=F32),
                          jnp.dot(weights, y_hi, preferred_element_type=F32)], axis=1)
    gt = _select_mod(i, n_ptiles, gts_ref, gtt_ref)
    y = _layer_norm(alpha * x1_ref[...] + gt * ff, lnw_ref[...], lnb_ref[...])

    @pl.when(i < n_ptiles)
    def _():
        yp_ref[...] = y

    @pl.when(i >= n_ptiles)
    def _():
        ys_ref[...] = y


def _combine(tables, x1, gt_seq, gt_tok, gate, pos, lnw, lnb, yb, *, n_local, n_ptiles, tiles_per_seq, n_experts,
             alpha):
    t, d = x1.shape
    tm = TOKEN_TILE
    nt = t // tm
    n_seq = gt_seq.shape[0]
    body = functools.partial(_combine_body, top_k=TOP_K, n_ptiles=n_ptiles, n_experts=n_experts, alpha=alpha)
    lane_spec = pl.BlockSpec((tm, LANES), lambda i, *_: (i, 0))
    return pl.pallas_call(
        body,
        grid_spec=pltpu.PrefetchScalarGridSpec(
            num_scalar_prefetch=len(tables),
            grid=(nt,),
            in_specs=[pl.BlockSpec((tm, d), lambda i, *_: (i, 0)),
                      pl.BlockSpec((None, 1, d), lambda i, *_: (jnp.minimum(i // tiles_per_seq, n_seq - 1), 0, 0)),
                      pl.BlockSpec((tm, d), lambda i, *_: (jnp.maximum(i - n_ptiles, 0), 0)),
                      lane_spec,
                      pl.BlockSpec((tm, LANES), lambda i, *_: (i, 0)),
                      pl.BlockSpec((1, d), lambda i, *_: (0, 0)),
                      pl.BlockSpec((1, d), lambda i, *_: (0, 0)),
                      pl.BlockSpec(memory_space=pl.ANY)],
            out_specs=[pl.BlockSpec((tm, d), lambda i, *_: (jnp.minimum(i, n_ptiles - 1), 0)),
                       pl.BlockSpec((tm, d), lambda i, *_: (jnp.maximum(i - n_ptiles, 0), 0))],
            scratch_shapes=[pltpu.VMEM((2, n_local, d // 2), U32), pltpu.SemaphoreType.DMA((2,))],
        ),
        out_shape=[jax.ShapeDtypeStruct((n_ptiles * tm, d), F32),
                   jax.ShapeDtypeStruct(((nt - n_ptiles) * tm, d), F32)],
        compiler_params=_cparams("arbitrary"),
        name="combine",
    )(*tables, x1, gt_seq, gt_tok, gate, pos, lnw, lnb, yb)


def _rotary_tables(pos, hd, rot_dim):
    half = rot_dim // 2
    inv_freq = jnp.power(jnp.float32(ROPE_THETA), -jnp.arange(half, dtype=F32) * (2.0 / rot_dim))
    ang = pos.astype(F32)[:, None] * inv_freq[None, :]
    cos, sin = jnp.cos(ang), jnp.sin(ang)
    n = pos.shape[0]
    ones = jnp.ones((n, hd - rot_dim), F32)
    zeros = jnp.zeros((n, hd - rot_dim), F32)
    zh = jnp.zeros((n, half), F32)
    ct = jnp.concatenate([cos, cos, ones], axis=1)
    s1 = jnp.concatenate([-sin, zh, zeros], axis=1)
    s2 = jnp.concatenate([zh, sin, zeros], axis=1)
    reps = LANES // hd
    return tuple(jnp.tile(a, (1, reps)) for a in (ct, s1, s2))


def kernel(x_prompt, x_sample, state_win_k, state_win_v, state_conv, state_ssm, c_prompt, c_sample, w_ada, b_ada, w_in, attn_sinks, w_conv, dn_a_log, dn_dt_bias, dn_norm_w, w_proj_attn, w_proj_dn, w_out, ln1_w, ln1_b, w_router, b_router, w_gu, b_gu, w_down, b_down, ln2_w, ln2_b):
    n_p, seq, d = x_prompt.shape
    n_s, l_s, _ = x_sample.shape
    depth = w_ada.shape[0]
    window, n_kv, hd = state_win_k.shape[2:]
    n_q = attn_sinks.shape[1]
    heads, dk, dv = state_ssm.shape[2:]
    conv_w, conv_dim = w_conv.shape[1:]
    n_e = w_router.shape[2]
    qd, kd = n_q * hd, n_kv * hd
    vdim = heads * dv
    rot_dim = hd // 4
    alpha = float((2 * depth) ** 0.25)
    tm = TOKEN_TILE
    t_p, t_s = n_p * seq, n_s * l_s
    t_all = t_p + t_s
    tps = seq // tm
    n_ptiles = t_p // tm
    l_pad = SUBLANES
    assert seq % tm == 0 and t_s % tm == 0 and tm % l_s == 0 and l_s <= l_pad and l_s >= conv_w - 1
    assert 2 * heads <= SUBLANES and n_e <= LANES and hd * 2 == LANES and rot_dim % 2 == 0

    sizes = [qd, kd, kd, conv_dim, vdim, heads, heads, d, d]
    offs = [int(o) for o in np.concatenate([[0], np.cumsum(sizes)])]
    cuts = (qd + 2 * kd, qd + 2 * kd + conv_dim, qd + 2 * kd + conv_dim + vdim, qd + 2 * kd + conv_dim + vdim + 2 * d)
    cuts = cuts + (cuts[-1] + LANES,)

    tabs_p = _rotary_tables(jnp.arange(seq, dtype=I32), hd, rot_dim)
    tabs_s = tuple(jnp.tile(a, (INPROJ_TILE // l_s, 1))
                   for a in _rotary_tables(PAST_LEN + jnp.arange(l_s, dtype=I32), hd, rot_dim))

    x_p = x_prompt.reshape(t_p, d)
    x_s = x_sample.reshape(t_s, d)
    c_all = jnp.concatenate([c_prompt, c_sample], axis=0)
    outs = {k: [] for k in ("pwk", "pwv", "pcv", "pss", "swk", "swv", "scv", "sss")}

    for l in range(depth):
        w_in_l = jnp.concatenate([w_in[l][:, :offs[5]], w_in[l][:, offs[7]:], w_in[l][:, offs[5]:offs[7]],
                                  jnp.zeros((d, LANES - 2 * heads), F32)], axis=1).astype(BF16)
        mod = _ada(c_all, w_ada[l], b_ada[l])
        mod_p = mod[:n_p].reshape(n_p, 6, 1, d)
        mod_s = jnp.repeat(mod[n_p:].reshape(n_s, 6, d).transpose(1, 0, 2), l_s, axis=1)
        sh1p, sc1p, gt1p, sh2p, sc2p, gt2p = [mod_p[:, k] for k in range(6)]
        sh1s, sc1s, gt1s, sh2s, sc2s, gt2s = [mod_s[k] for k in range(6)]

        qkva_p, dn_p, z_p, g_p, ba_p, tail_p, kvw_p = _inproj(
            x_p, sh1p, sc1p, tabs_p, w_in_l, cuts, per_token_mod=False, tiles_per_seq=seq // INPROJ_TILE, act_dtype=BF16,
            window=window, kv_cols=2 * kd, half_rot=rot_dim // 2)
        attn_p = _attn_prompt(qkva_p, attn_sinks[l], n_p, seq, n_q, n_kv, hd, window)
        chunk = min(DN_CHUNK, seq)
        nc = seq // chunk
        bat_p = ba_p[:, :SUBLANES].reshape(n_p, nc, chunk, SUBLANES).transpose(0, 1, 3, 2)
        hp = jnp.zeros((SUBLANES, LANES), F32).at[0, :heads].set(dn_a_log[l]).at[1, :heads].set(dn_dt_bias[l])
        nw = dn_norm_w[l].reshape(1, dv)
        o_p, ssm_p = _deltanet(dn_p.reshape(n_p, seq, conv_dim), z_p.reshape(n_p, seq, vdim),
                               ba_p.reshape(n_p, seq, LANES), bat_p,
                               jnp.zeros((n_p, SUBLANES, conv_dim), F32), jnp.zeros((n_p, heads, dk, dv), F32),
                               w_conv[l], hp, nw, chunk=chunk, l_real=chunk, nb=4)
        outs["pwk"].append(kvw_p[:, :, :kd].reshape(n_p, window, n_kv, hd))
        outs["pwv"].append(kvw_p[:, :, kd:].reshape(n_p, window, n_kv, hd))
        outs["pcv"].append(tail_p.reshape(n_p, seq // INPROJ_TILE, SUBLANES, conv_dim)[:, -1, SUBLANES - (conv_w - 1):])
        outs["pss"].append(ssm_p)

        qkva_s, dn_s, z_s, g_s, ba_s = _inproj(
            x_s, sh1s, sc1s, tabs_s, w_in_l, cuts, per_token_mod=True, tiles_per_seq=1, act_dtype=F32,
            window=window, kv_cols=2 * kd, half_rot=rot_dim // 2)
        pad_l = lambda a: jnp.pad(a.reshape(n_s, l_s, a.shape[-1]), ((0, 0), (0, l_pad - l_s), (0, 0)))
        attn_s, wk_s, wv_s = _attn_sample(pad_l(qkva_s), state_win_k[l].reshape(n_s, window, kd),
                                          state_win_v[l].reshape(n_s, window, kd), attn_sinks[l], n_q, n_kv, hd, l_s)
        attn_s = attn_s[:, :l_s].reshape(t_s, qd)
        ba_s3 = pad_l(ba_s)
        bat_s = ba_s3[:, :, :SUBLANES].transpose(0, 2, 1).reshape(n_s, 1, SUBLANES, l_pad)
        cs0 = jnp.pad(state_conv[l], ((0, 0), (SUBLANES - (conv_w - 1), 0), (0, 0)))
        o_s, ssm_s = _deltanet(pad_l(dn_s), pad_l(z_s), ba_s3, bat_s, cs0, state_ssm[l], w_conv[l], hp, nw,
                               chunk=l_pad, l_real=l_s, nb=8)
        o_s = o_s[:, :l_s].reshape(t_s, vdim)
        outs["swk"].append(wk_s.reshape(n_s, window, n_kv, hd))
        outs["swv"].append(wv_s.reshape(n_s, window, n_kv, hd))
        outs["scv"].append(jnp.concatenate([state_conv[l], dn_s.reshape(n_s, l_s, conv_dim)], axis=1)[:, -(conv_w - 1):])
        outs["sss"].append(ssm_s)

        wr = jnp.pad(w_router[l], ((0, 0), (0, LANES - n_e)))
        br = jnp.pad(b_router[l], (0, LANES - n_e), constant_values=NEG_BIG).reshape(1, LANES)
        wr_hi = wr.astype(BF16)
        wr_lo = (wr - wr_hi.astype(F32)).astype(BF16)
        wts = (w_proj_attn[l].astype(BF16), w_proj_dn[l].astype(BF16), w_out[l].astype(BF16),
               ln1_w[l].reshape(1, d), ln1_b[l].reshape(1, d), wr_hi, wr_lo, br)
        res_p = _outproj(attn_p, o_p.reshape(t_p, vdim), g_p, x_p, gt1p, sh2p, sc2p, wts, None,
                         per_token_mod=False, tiles_per_seq=tps, t_total=t_all, tile_off=0, alpha=alpha)
        x1, pos, gate, cnt = _outproj(attn_s, o_s, g_s, x_s, gt1s, sh2s, sc2s, wts, res_p, per_token_mod=True,
                                      tiles_per_seq=1, t_total=t_all, tile_off=n_ptiles, alpha=alpha)

        bm = EXPERT_ROWS
        nt_all = t_all // tm
        cnt8 = (cnt[:, 0, :n_e].astype(I32) + RUN_ALIGN - 1) // RUN_ALIGN * RUN_ALIGN
        tot = jnp.sum(cnt8, axis=0)
        padded = (tot + bm - 1) // bm * bm
        pad_end = jnp.cumsum(padded)
        pad_start = pad_end - padded
        gbase = pad_start[None, :] + jnp.cumsum(cnt8, axis=0) - cnt8
        toff = jnp.cumsum(cnt8, axis=1) - cnt8
        n_local = -(-(tm * TOP_K + n_e * (RUN_ALIGN - 1)) // LANES) * LANES
        n_rows = -(-(t_all * TOP_K + nt_all * n_e * (RUN_ALIGN - 1) + n_e * (bm - 1)) // bm) * bm
        nblk = n_rows // bm
        n_used = jnp.maximum(pad_end[-1:] // bm, 1).astype(I32)
        block_e = jnp.minimum(jnp.sum(pad_end[None, :] <= (jnp.arange(nblk, dtype=I32) * bm)[:, None], axis=1),
                              n_e - 1).astype(I32)
        flat = lambda a: a.astype(I32).reshape(nt_all * n_e)
        run_tables = (flat(toff), flat(cnt8), flat(gbase))
        tail_tables = ((pad_start + tot).astype(I32), (padded - tot).astype(I32))

        xs = _dispatch(run_tables + tail_tables, x1, sh2p, sc2p, sh2s, sc2s, pos, n_rows=n_rows, n_local=n_local,
                       n_ptiles=n_ptiles, tiles_per_seq=tps, n_experts=n_e)
        group_end_blk = pad_end // bm
        next_e = jnp.where(group_end_blk < n_used[0], block_e[jnp.minimum(group_end_blk, nblk - 1)], -1).astype(I32)
        yb = _experts(block_e, n_used, next_e, xs, w_gu[l], b_gu[l], w_down[l], b_down[l])
        x_p, x_s = _combine(run_tables, x1, gt2p, gt2s, gate, pos, ln2_w[l].reshape(1, d), ln2_b[l].reshape(1, d),
                            yb, n_local=n_local, n_ptiles=n_ptiles, tiles_per_seq=tps, n_experts=n_e, alpha=alpha)

    st = lambda k: outs[k][0][None] if depth == 1 else jnp.stack(outs[k])
    return (x_p.reshape(n_p, seq, d), x_s.reshape(n_s, l_s, d), st("pwk"), st("pwv"), st("pcv"), st("pss"),
            st("swk"), st("swv"), st("scv"), st("sss"))
```

```python
import functools

import numpy as np
import jax
import jax.numpy as jnp
from jax import lax
from jax.experimental import pallas as pl
from jax.experimental.pallas import tpu as pltpu

F32 = jnp.float32
BF16 = jnp.bfloat16
I32 = jnp.int32
U32 = jnp.uint32

PAST_LEN = 16384
ROPE_THETA = 500000.0
TOP_K = 4
SWIGLU_LIMIT = 7.0
SWIGLU_ALPHA = 1.702
DN_CHUNK = 64
LN_EPS = 1e-5
RMS_EPS = 1e-6
L2_EPS = 1e-6

LANES = 128
SUBLANES = 8
VMEM_LIMIT_BYTES = 56 * 1024 * 1024

TOKEN_TILE = 256
INPROJ_TILE = 512
EXPERT_ROWS = 512
RUN_ALIGN = SUBLANES
OUTPROJ_SUBTILES = 2
N_ROUTE_BUFS = 4
NEG_BIG = -1e30


def _cparams(*sem):
    return pltpu.CompilerParams(dimension_semantics=sem, vmem_limit_bytes=VMEM_LIMIT_BYTES)


def _silu(x):
    return x * jax.nn.sigmoid(x)


def _bdot(a, b):
    return jnp.dot(a.astype(BF16), b.astype(BF16), preferred_element_type=F32)


def _pack_halves(x):
    n = x.shape[1] // 2
    lo = lax.bitcast_convert_type(x[:, :n], U32)
    hi = lax.bitcast_convert_type(x[:, n:], U32)
    return (hi & jnp.uint32(0xFFFF0000)) | (lo >> 16)


def _unpack_halves(w):
    lo = lax.bitcast_convert_type(w << 16, F32).astype(BF16)
    hi = lax.bitcast_convert_type(w & jnp.uint32(0xFFFF0000), F32).astype(BF16)
    return lo, hi


def _ada_body(c_ref, w_ref, b_ref, o_ref):
    o_ref[...] = _bdot(_silu(c_ref[...]), w_ref[...]) + b_ref[...]


def _ada(c_all, w_ada, b_ada):
    n, d = c_all.shape
    dout = w_ada.shape[1]
    tn = d
    return pl.pallas_call(
        _ada_body,
        grid=(dout // tn,),
        in_specs=[pl.BlockSpec((n, d), lambda j: (0, 0)),
                  pl.BlockSpec((d, tn), lambda j: (0, j)),
                  pl.BlockSpec((1, tn), lambda j: (0, j))],
        out_specs=pl.BlockSpec((n, tn), lambda j: (0, j)),
        out_shape=jax.ShapeDtypeStruct((n, dout), F32),
        compiler_params=_cparams("arbitrary"),
        name="ada",
    )(c_all, w_ada, b_ada.reshape(1, dout))


def _inproj_body(x_ref, sh_ref, sc_ref, ct_ref, s1_ref, s2_ref, w_ref,
                 a_ref, dn_ref, z_ref, g_ref, ba_ref, *win_refs, cuts, n_rot_chunks, half_rot, window):
    h = (x_ref[...] * (1.0 + sc_ref[...]) + sh_ref[...]).astype(BF16)

    def mm(lo, hi):
        return jnp.dot(h, w_ref[:, lo:hi], preferred_element_type=F32)

    c_a, c_dn, c_z, c_g, c_ba = cuts
    qkv = mm(0, c_a)
    ct, s1, s2 = ct_ref[...], s1_ref[...], s2_ref[...]
    cols = []
    for c in range(n_rot_chunks):
        xc = qkv[:, c * LANES:(c + 1) * LANES]
        cols.append(xc * ct + pltpu.roll(xc, LANES - half_rot, 1) * s1 + pltpu.roll(xc, half_rot, 1) * s2)
    cols.append(qkv[:, n_rot_chunks * LANES:])
    rot = jnp.concatenate(cols, axis=1)
    a_ref[...] = rot.astype(a_ref.dtype)
    dn = mm(c_a, c_dn)
    dn_ref[...] = dn.astype(dn_ref.dtype)
    z_ref[...] = mm(c_dn, c_z).astype(z_ref.dtype)
    g_ref[...] = mm(c_z, c_g).astype(g_ref.dtype)
    ba_ref[...] = mm(c_g, c_ba)
    if win_refs:
        tail_ref, kvw_ref = win_refs
        tm = dn.shape[0]
        tail_ref[...] = dn[tm - SUBLANES:, :]
        kvw_ref[...] = rot[tm - window:, n_rot_chunks * LANES - LANES:]


def _inproj(x, sh, sc, tabs, w_perm, cuts, *, per_token_mod, tiles_per_seq, act_dtype, window, kv_cols, half_rot):
    t, d = x.shape
    tm = INPROJ_TILE
    nt = t // tm
    assert t % tm == 0 and tm >= window and (per_token_mod or nt % tiles_per_seq == 0)
    c_a, c_dn, c_z, c_g, c_ba = cuts
    n_rot_chunks = (c_a - kv_cols // 2) // LANES
    if per_token_mod:
        mod_spec = pl.BlockSpec((tm, d), lambda i: (i, 0))
        tab_spec = pl.BlockSpec((tm, LANES), lambda i: (0, 0))
    else:
        mod_spec = pl.BlockSpec((None, 1, d), lambda i: (i // tiles_per_seq, 0, 0))
        tab_spec = pl.BlockSpec((tm, LANES), lambda i: (i % tiles_per_seq, 0))
    out_shape = [jax.ShapeDtypeStruct((t, c_a), act_dtype),
                 jax.ShapeDtypeStruct((t, c_dn - c_a), act_dtype),
                 jax.ShapeDtypeStruct((t, c_z - c_dn), act_dtype),
                 jax.ShapeDtypeStruct((t, c_g - c_z), act_dtype),
                 jax.ShapeDtypeStruct((t, c_ba - c_g), F32)]
    out_specs = [pl.BlockSpec((tm, s.shape[1]), lambda i: (i, 0)) for s in out_shape]
    with_win = not per_token_mod
    if with_win:
        n_seq = nt // tiles_per_seq
        out_shape += [jax.ShapeDtypeStruct((nt, SUBLANES, c_dn - c_a), F32),
                      jax.ShapeDtypeStruct((n_seq, window, kv_cols), F32)]
        out_specs += [pl.BlockSpec((None, SUBLANES, c_dn - c_a), lambda i: (i, 0, 0)),
                      pl.BlockSpec((None, window, kv_cols), lambda i: (i // tiles_per_seq, 0, 0))]
    body = functools.partial(_inproj_body, cuts=cuts, n_rot_chunks=n_rot_chunks, half_rot=half_rot, window=window)
    return pl.pallas_call(
        body,
        grid=(nt,),
        in_specs=[pl.BlockSpec((tm, d), lambda i: (i, 0)), mod_spec, mod_spec,
                  tab_spec, tab_spec, tab_spec,
                  pl.BlockSpec((d, c_ba), lambda i: (0, 0))],
        out_specs=out_specs,
        out_shape=out_shape,
        compiler_params=_cparams("arbitrary"),
        name="inproj",
    )(x, sh, sc, *tabs, w_perm)


def _softmax_sink_pv(scores, valid, sinks, values):
    ms = [jnp.where(valid, s, -jnp.inf) for s in scores]
    m = [jnp.maximum(jnp.max(x, axis=-1, keepdims=True), sk) for x, sk in zip(ms, sinks)]
    p = [jnp.exp(x - mi) for x, mi in zip(ms, m)]
    den = [jnp.sum(pi, axis=-1, keepdims=True) + jnp.exp(sk - mi) for pi, sk, mi in zip(p, sinks, m)]
    return [jnp.dot((pi / di).astype(BF16), v, preferred_element_type=F32) for pi, di, v in zip(p, den, values)]


def _attn_prompt_body(sink_ref, q_ref, kvp_ref, kvc_ref, o_ref, *, n_q, n_kv, hd, window):
    j = pl.program_id(1)
    group = n_q // n_kv
    scale = hd ** -0.5
    assert np.log2(scale) == int(np.log2(scale))
    r = lax.broadcasted_iota(I32, (window, 2 * window), 0)
    c = lax.broadcasted_iota(I32, (window, 2 * window), 1)
    rel = window + r - c
    band = (rel >= 0) & (rel < window)
    kv_cur = kvc_ref[...]
    kv_first = jnp.concatenate([kvp_ref[...], kv_cur[:window]], axis=0)
    for qb, (kv, valid) in enumerate(((kv_first, band & ((c >= window) | (j > 0))), (kv_cur, band))):
        q = q_ref[qb * window:(qb + 1) * window, :] * scale
        scores = [lax.dot_general(q[:, h * hd:(h + 1) * hd], kv[:, (h // group) * hd:(h // group + 1) * hd],
                                  (((1,), (1,)), ((), ())), preferred_element_type=F32) for h in range(n_q)]
        values = [kv[:, (n_kv + h // group) * hd:(n_kv + h // group + 1) * hd] for h in range(n_q)]
        outs = _softmax_sink_pv(scores, valid, [sink_ref[h] for h in range(n_q)], values)
        o_ref[qb * window:(qb + 1) * window, :] = jnp.concatenate(outs, axis=1).astype(o_ref.dtype)


def _attn_prompt(qkva, sinks, n_seq, seq, n_q, n_kv, hd, window):
    qd, kvd = n_q * hd, 2 * n_kv * hd
    x3 = qkva.reshape(n_seq, seq, qd + kvd)
    nb = seq // window
    assert nb % 2 == 0 and qd % kvd == 0
    kv_blk = qd // kvd
    body = functools.partial(_attn_prompt_body, n_q=n_q, n_kv=n_kv, hd=hd, window=window)
    out = pl.pallas_call(
        body,
        grid=(n_seq, nb // 2),
        in_specs=[pl.BlockSpec(memory_space=pltpu.SMEM),
                  pl.BlockSpec((None, 2 * window, qd), lambda n, j: (n, j, 0)),
                  pl.BlockSpec((None, window, kvd), lambda n, j: (n, jnp.maximum(2 * j - 1, 0), kv_blk)),
                  pl.BlockSpec((None, 2 * window, kvd), lambda n, j: (n, j, kv_blk))],
        out_specs=pl.BlockSpec((None, 2 * window, qd), lambda n, j: (n, j, 0)),
        out_shape=jax.ShapeDtypeStruct((n_seq, seq, qd), BF16),
        compiler_params=_cparams("arbitrary", "arbitrary"),
        name="attn_prompt",
    )(sinks, x3, x3, x3)
    return out.reshape(n_seq * seq, qd)


def _attn_sample_body(sink_ref, q_ref, wk_ref, wv_ref, o_ref, wko_ref, wvo_ref,
                      *, bs, n_q, n_kv, hd, window, l_new, l_pad):
    group = n_q // n_kv
    qd = n_q * hd
    kd = n_kv * hd
    rows = group * l_pad
    r = lax.broadcasted_iota(I32, (rows, window + l_pad), 0) % l_pad
    c = lax.broadcasted_iota(I32, (rows, window + l_pad), 1)
    rel = window + r - c
    valid = (rel >= 0) & (rel < window) & (c < window + l_new)
    sinks = [jnp.concatenate([jnp.full((l_pad, 1), sink_ref[kvh * group + g], F32) for g in range(group)], axis=0)
             for kvh in range(n_kv)]
    qs, ks, vs = [], [], []
    for b in range(bs):
        x = q_ref[b]
        k_new = x[:, qd:qd + kd]
        v_new = x[:, qd + kd:]
        wko_ref[b, 0:window - l_new, :] = wk_ref[b, l_new:window, :]
        wko_ref[b, window - l_new:window, :] = k_new[0:l_new, :]
        wvo_ref[b, 0:window - l_new, :] = wv_ref[b, l_new:window, :]
        wvo_ref[b, window - l_new:window, :] = v_new[0:l_new, :]
        k_all = jnp.concatenate([wk_ref[b], k_new], axis=0).astype(BF16)
        v_all = jnp.concatenate([wv_ref[b], v_new], axis=0).astype(BF16)
        for kvh in range(n_kv):
            qs.append(jnp.concatenate([x[:, (kvh * group + g) * hd:(kvh * group + g + 1) * hd]
                                       for g in range(group)], axis=0).astype(BF16))
            ks.append(k_all[:, kvh * hd:(kvh + 1) * hd])
            vs.append(v_all[:, kvh * hd:(kvh + 1) * hd])
    n = len(qs)
    s = [lax.dot_general(qs[i], ks[i], (((1,), (1,)), ((), ())), preferred_element_type=F32) * (hd ** -0.5)
         for i in range(n)]
    o = _softmax_sink_pv(s, valid, [sinks[i % n_kv] for i in range(n)], vs)
    for b in range(bs):
        outs = [o[b * n_kv + kvh][g * l_pad:(g + 1) * l_pad, :] for kvh in range(n_kv) for g in range(group)]
        o_ref[b] = jnp.concatenate(outs, axis=1).astype(o_ref.dtype)


def _attn_sample(qkva_pad, win_k, win_v, sinks, n_q, n_kv, hd, l_new):
    n, l_pad, width = qkva_pad.shape
    window, kd = win_k.shape[1], win_k.shape[2]
    qd = n_q * hd
    bs = 8
    body = functools.partial(_attn_sample_body, bs=bs, n_q=n_q, n_kv=n_kv, hd=hd, window=window,
                             l_new=l_new, l_pad=l_pad)
    return pl.pallas_call(
        body,
        grid=(n // bs,),
        in_specs=[pl.BlockSpec(memory_space=pltpu.SMEM),
                  pl.BlockSpec((bs, l_pad, width), lambda i: (i, 0, 0)),
                  pl.BlockSpec((bs, window, kd), lambda i: (i, 0, 0)),
                  pl.BlockSpec((bs, window, kd), lambda i: (i, 0, 0))],
        out_specs=[pl.BlockSpec((bs, l_pad, qd), lambda i: (i, 0, 0)),
                   pl.BlockSpec((bs, window, kd), lambda i: (i, 0, 0)),
                   pl.BlockSpec((bs, window, kd), lambda i: (i, 0, 0))],
        out_shape=[jax.ShapeDtypeStruct((n, l_pad, qd), BF16),
                   jax.ShapeDtypeStruct((n, window, kd), F32),
                   jax.ShapeDtypeStruct((n, window, kd), F32)],
        compiler_params=_cparams("arbitrary"),
        name="attn_sample",
    )(sinks, qkva_pad, win_k, win_v)


def _split_bf16(x):
    hi = x.astype(BF16)
    return hi, (x - hi.astype(F32)).astype(BF16)


def _tdot(a, b):
    ah, al = _split_bf16(a)
    bh, bl = _split_bf16(b)
    m = a.shape[0]
    t = jnp.dot(jnp.concatenate([ah, al], axis=0), bh, preferred_element_type=F32)
    return t[:m] + t[m:] + jnp.dot(ah, bl, preferred_element_type=F32)


def _dn_body(qkv_ref, z_ref, ba_ref, bat_ref, cs0_ref, s0_ref, wc_ref, hp_ref, nw_ref,
             o_ref, s_ref, xbuf, *, nb, chunk, heads, dk, dv, l_real, conv_w):
    c_idx = pl.program_id(1)
    hc = SUBLANES

    @pl.when(c_idx == 0)
    def _():
        xbuf[:, 0:hc, :] = cs0_ref[...]
        s_ref[...] = s0_ref[...]

    qk_dim = heads * dk
    row = lax.broadcasted_iota(I32, (chunk, chunk), 0)
    col = lax.broadcasted_iota(I32, (chunk, chunk), 1)
    incl = row >= col
    strict = row > col
    eye = (row == col).astype(F32)
    valid_c = row[:, 0:1] < l_real
    valid_r = col[0:1, :] < l_real
    n_levels = max(1, int(np.ceil(np.log2(chunk))))
    wc = wc_ref[...]
    hp = hp_ref[...]
    neg_exp_alog = -jnp.exp(hp[0:1, :])
    dt_bias = hp[1:2, :]
    nw = nw_ref[...]
    chains = [(b, h) for b in range(nb) for h in range(heads)]
    n = len(chains)

    ys = []
    for b in range(nb):
        xbuf[b, hc:hc + chunk, :] = qkv_ref[b].astype(F32)
        y = xbuf[b, hc:hc + chunk, :] * wc[conv_w - 1:conv_w, :]
        for j in range(conv_w - 1):
            off = hc - (conv_w - 1) + j
            y = y + xbuf[b, off:off + chunk, :] * wc[j:j + 1, :]
        ys.append(_silu(y))
        xbuf[b, 0:hc, :] = xbuf[b, chunk:chunk + hc, :]

    qn, kn, kb, vb, decay, e_gc, e_rest, e_last = [], [], [], [], [], [], [], []
    for b, h in chains:
        y = ys[b]
        qh = y[:, h * dk:(h + 1) * dk]
        kh = y[:, qk_dim + h * dk:qk_dim + (h + 1) * dk]
        vh = y[:, 2 * qk_dim + h * dv:2 * qk_dim + (h + 1) * dv]
        ba = ba_ref[b]
        bat = bat_ref[b]
        ne = neg_exp_alog[:, h:h + 1]
        db = dt_bias[:, h:h + 1]
        beta = jnp.where(valid_c, jax.nn.sigmoid(ba[:, h:h + 1]), 0.0)
        g_col = jnp.where(valid_c, ne * jax.nn.softplus(ba[:, heads + h:heads + h + 1] + db), 0.0)
        g_row = jnp.where(valid_r, ne * jax.nn.softplus(bat[heads + h:heads + h + 1, :] + db), 0.0)
        gc_col = jnp.sum(jnp.where(incl, g_row, 0.0), axis=1, keepdims=True)
        gc_row = jnp.sum(jnp.where(row <= col, g_col, 0.0), axis=0, keepdims=True)
        g_last = gc_col[chunk - 1:chunk, :]
        q_ = qh * lax.rsqrt(jnp.sum(qh * qh, -1, keepdims=True) + L2_EPS) * (dk ** -0.5)
        k_ = kh * lax.rsqrt(jnp.sum(kh * kh, -1, keepdims=True) + L2_EPS)
        qn.append(q_)
        kn.append(k_)
        kb.append(k_ * beta)
        vb.append(vh * beta)
        decay.append(jnp.where(incl, jnp.exp(gc_col - gc_row), 0.0))
        e_gc.append(jnp.exp(gc_col))
        e_rest.append(jnp.exp(g_last - gc_col))
        e_last.append(jnp.exp(g_last))

    sc = [lax.dot_general(jnp.concatenate([qn[i], kb[i]], axis=0).astype(BF16), kn[i].astype(BF16),
                          (((1,), (1,)), ((), ())), preferred_element_type=F32) for i in range(n)]
    qk = [sc[i][:chunk] * decay[i] for i in range(n)]
    p = [jnp.where(strict, -(sc[i][chunk:] * decay[i]), 0.0) for i in range(n)]
    t_inv = [eye + p[i] for i in range(n)]
    if n_levels > 1:
        p = [_tdot(p[i], p[i]) for i in range(n)]
    for lvl in range(1, n_levels):
        if lvl < n_levels - 1:
            yp = [_tdot(jnp.concatenate([t_inv[i], p[i]], axis=0), p[i]) for i in range(n)]
            t_inv = [t_inv[i] + yp[i][:chunk] for i in range(n)]
            p = [yp[i][chunk:] for i in range(n)]
        else:
            t_inv = [t_inv[i] + _tdot(t_inv[i], p[i]) for i in range(n)]
    sol = [_tdot(t_inv[i], jnp.concatenate([vb[i], kb[i] * e_gc[i]], axis=1)) for i in range(n)]
    s_old = [s_ref[b, h] for b, h in chains]
    wq = [_bdot(jnp.concatenate([sol[i][:, dv:], qn[i] * e_gc[i]], axis=0), s_old[i]) for i in range(n)]
    v_new = [sol[i][:, :dv] - wq[i][:chunk] for i in range(n)]
    o = [wq[i][chunk:] + _bdot(qk[i], v_new[i]) for i in range(n)]
    for i, (b, h) in enumerate(chains):
        s_ref[b, h] = s_old[i] * e_last[i] + lax.dot_general(
            (kn[i] * e_rest[i]).astype(BF16), v_new[i].astype(BF16), (((0,), (0,)), ((), ())),
            preferred_element_type=F32)
    for b in range(nb):
        zt = z_ref[b].astype(F32)
        outs = []
        for h in range(heads):
            oi = o[b * heads + h]
            on = oi * lax.rsqrt(jnp.mean(oi * oi, -1, keepdims=True) + RMS_EPS) * nw
            outs.append(on * _silu(zt[:, h * dv:(h + 1) * dv]))
        o_ref[b] = jnp.concatenate(outs, axis=1).astype(o_ref.dtype)


def _deltanet(qkv, z, ba, bat, cs0, s0, w_conv, hp, norm_w, *, chunk, l_real, nb):
    n, l, conv_dim = qkv.shape
    heads, dk, dv = s0.shape[1:]
    nc = l // chunk
    assert n % nb == 0 and l % chunk == 0
    conv_w = w_conv.shape[0]
    hist = pltpu.VMEM((nb, SUBLANES + chunk, conv_dim), F32)
    body = functools.partial(_dn_body, nb=nb, chunk=chunk, heads=heads, dk=dk, dv=dv, l_real=l_real, conv_w=conv_w)
    return pl.pallas_call(
        body,
        grid=(n // nb, nc),
        in_specs=[pl.BlockSpec((nb, chunk, conv_dim), lambda i, c: (i, c, 0)),
                  pl.BlockSpec((nb, chunk, heads * dv), lambda i, c: (i, c, 0)),
                  pl.BlockSpec((nb, chunk, LANES), lambda i, c: (i, c, 0)),
                  pl.BlockSpec((nb, None, SUBLANES, chunk), lambda i, c: (i, c, 0, 0)),
                  pl.BlockSpec((nb, SUBLANES, conv_dim), lambda i, c: (i, 0, 0)),
                  pl.BlockSpec((nb, heads, dk, dv), lambda i, c: (i, 0, 0, 0)),
                  pl.BlockSpec((conv_w, conv_dim), lambda i, c: (0, 0)),
                  pl.BlockSpec((SUBLANES, LANES), lambda i, c: (0, 0)),
                  pl.BlockSpec((1, dv), lambda i, c: (0, 0))],
        out_specs=[pl.BlockSpec((nb, chunk, heads * dv), lambda i, c: (i, c, 0)),
                   pl.BlockSpec((nb, heads, dk, dv), lambda i, c: (i, 0, 0, 0))],
        out_shape=[jax.ShapeDtypeStruct((n, l, heads * dv), BF16),
                   jax.ShapeDtypeStruct((n, heads, dk, dv), F32)],
        scratch_shapes=[hist],
        compiler_params=_cparams("arbitrary", "arbitrary"),
        name="deltanet",
    )(qkv, z, ba, bat, cs0, s0, w_conv, hp, norm_w)


def _layer_norm(r, w, b):
    mu = jnp.mean(r, -1, keepdims=True)
    var = jnp.mean(jnp.square(r - mu), -1, keepdims=True)
    return (r - mu) * lax.rsqrt(var + LN_EPS) * w + b


def _outproj_body(attn_ref, dn_ref, g_ref, x_ref, gt_ref, sh2_ref, sc2_ref, wpa_ref, wpd_ref, wo_ref,
                  lnw_ref, lnb_ref, wrh_ref, wrl_ref, br_ref, *refs, alpha, top_k, aliased):
    if aliased:
        refs = refs[N_ROUTE_BUFS:]
    x1_ref, pos_ref, gate_ref, cnt_ref = refs
    for s in range(cnt_ref.shape[0]):
        _outproj_tile(s, attn_ref, dn_ref, g_ref, x_ref, gt_ref, sh2_ref, sc2_ref, wpa_ref, wpd_ref, wo_ref, lnw_ref,
                      lnb_ref, wrh_ref, wrl_ref, br_ref, x1_ref, pos_ref, gate_ref, cnt_ref,
                      alpha=alpha, top_k=top_k)


def _outproj_tile(s, attn_ref, dn_ref, g_ref, x_ref, gt_ref, sh2_ref, sc2_ref, wpa_ref, wpd_ref, wo_ref, lnw_ref,
                  lnb_ref, wrh_ref, wrl_ref, br_ref, x1_ref, pos_ref, gate_ref, cnt_ref, *, alpha, top_k):
    d = x_ref.shape[1]
    tm = TOKEN_TILE
    rows = pl.ds(s * tm, tm)

    def mod(ref):
        return ref[...] if ref.shape[0] == 1 else ref[rows, :]

    g = g_ref[rows, :].astype(F32)
    pa = jnp.dot(attn_ref[rows, :], wpa_ref[...], preferred_element_type=F32)
    pd = jnp.dot(dn_ref[rows, :], wpd_ref[...], preferred_element_type=F32)
    merged = jax.nn.sigmoid(g[:, :d]) * pa + jax.nn.sigmoid(g[:, d:]) * pd
    mix = jnp.dot(merged.astype(BF16), wo_ref[...], preferred_element_type=F32)
    x1 = _layer_norm(alpha * x_ref[rows, :] + mod(gt_ref) * mix, lnw_ref[...], lnb_ref[...])
    x1_ref[rows, :] = x1
    h2 = x1 * (1.0 + mod(sc2_ref)) + mod(sh2_ref)
    h_hi, h_lo = _split_bf16(h2)
    lg = jnp.dot(jnp.concatenate([h_hi, h_lo], axis=0), wrh_ref[...], preferred_element_type=F32)
    logits = lg[:tm] + lg[tm:] + jnp.dot(h_hi, wrl_ref[...], preferred_element_type=F32) + br_ref[...]
    lane = lax.broadcasted_iota(I32, (tm, LANES), 1)
    lane_f = lane.astype(F32)
    vals, sels = [], []
    l = logits
    for _ in range(top_k):
        m = jnp.max(l, axis=1, keepdims=True)
        idx = jnp.min(jnp.where(l == m, lane_f, float(LANES)), axis=1, keepdims=True)
        sel = lane_f == idx
        vals.append(m)
        sels.append(sel)
        l = jnp.where(sel, -jnp.inf, l)
    ex = [jnp.exp(v - vals[0]) for v in vals]
    den = ex[0]
    for e in ex[1:]:
        den = den + e
    multi_hot = jnp.zeros((tm, LANES), F32)
    for sel in sels:
        multi_hot = multi_hot + jnp.where(sel, 1.0, 0.0)
    r_i = lax.broadcasted_iota(I32, (tm, tm), 0)
    c_i = lax.broadcasted_iota(I32, (tm, tm), 1)
    lower = jnp.where(r_i > c_i, 1.0, 0.0).astype(BF16)
    prefix = jnp.dot(lower, multi_hot.astype(BF16), preferred_element_type=F32)
    counts = jnp.sum(multi_hot, axis=0, keepdims=True)
    cnt_pad = jnp.floor((counts + (RUN_ALIGN - 1)) * (1.0 / RUN_ALIGN)) * RUN_ALIGN
    e_r = lax.broadcasted_iota(I32, (LANES, LANES), 0)
    e_c = lax.broadcasted_iota(I32, (LANES, LANES), 1)
    before = jnp.where(e_r < e_c, 1.0, 0.0).astype(BF16)
    run_off = jnp.dot(jnp.broadcast_to(cnt_pad, (SUBLANES, LANES)).astype(BF16), before,
                      preferred_element_type=F32)[0:1, :]
    g_out = jnp.zeros((tm, LANES), F32)
    p_out = jnp.zeros((tm, LANES), F32)
    for k in range(top_k):
        pos_k = jnp.sum(jnp.where(sels[k], prefix + run_off, 0.0), axis=1, keepdims=True)
        g_out = jnp.where(lane == k, ex[k] / den, g_out)
        p_out = jnp.where(lane == k, pos_k, p_out)
    pos_ref[rows, :] = p_out.astype(I32)
    gate_ref[rows, :] = g_out
    cnt_ref[s] = jnp.broadcast_to(counts, cnt_ref.shape[1:])


def _outproj(attn, dn, gates, x, gt, sh2, sc2, wts, bufs, *, per_token_mod, tiles_per_seq, t_total, tile_off, alpha):
    t, d = x.shape
    sub = OUTPROJ_SUBTILES
    tm = TOKEN_TILE * sub
    nt = t // tm
    assert t % tm == 0 and tile_off % sub == 0 and (per_token_mod or tiles_per_seq % sub == 0)
    wpa, wpd, wo, lnw, lnb, wrh, wrl, br = wts
    if per_token_mod:
        mod_spec = pl.BlockSpec((tm, d), lambda i: (i, 0))
    else:
        mod_spec = pl.BlockSpec((None, 1, d), lambda i: (i // (tiles_per_seq // sub), 0, 0))

    def row(width):
        return pl.BlockSpec((tm, width), lambda i: (i, 0))

    def full(a):
        return pl.BlockSpec(a.shape, lambda i: (0,) * a.ndim)

    aliased = bufs is not None
    in_specs = [row(attn.shape[1]), row(dn.shape[1]), row(gates.shape[1]), row(d), mod_spec, mod_spec, mod_spec,
                full(wpa), full(wpd), full(wo), full(lnw), full(lnb), full(wrh), full(wrl), full(br)]
    args = [attn, dn, gates, x, gt, sh2, sc2, wpa, wpd, wo, lnw, lnb, wrh, wrl, br]
    io_alias = {}
    if aliased:
        for k, bfr in enumerate(bufs):
            in_specs.append(pl.BlockSpec(memory_space=pl.ANY))
            io_alias[len(args)] = k
            args.append(bfr)
    step_off = tile_off // sub
    out_row = lambda width: pl.BlockSpec((tm, width), lambda i: (i + step_off, 0))
    out_shape = [jax.ShapeDtypeStruct((t_total, d), F32), jax.ShapeDtypeStruct((t_total, LANES), I32),
                 jax.ShapeDtypeStruct((t_total, LANES), F32),
                 jax.ShapeDtypeStruct((t_total // TOKEN_TILE, SUBLANES, LANES), F32)]
    assert len(out_shape) == N_ROUTE_BUFS and TOKEN_TILE <= 256
    out_specs = [out_row(d), out_row(LANES), out_row(LANES),
                 pl.BlockSpec((sub, SUBLANES, LANES), lambda i: (i + step_off, 0, 0))]
    body = functools.partial(_outproj_body, alpha=alpha, top_k=TOP_K, aliased=aliased)
    return pl.pallas_call(
        body,
        grid=(nt,),
        in_specs=in_specs,
        out_specs=out_specs,
        out_shape=out_shape,
        input_output_aliases=io_alias,
        compiler_params=_cparams("arbitrary"),
        name="outproj",
    )(*args)


def _select_mod(i, n_ptiles, seq_ref, tok_ref):
    return jnp.where(i < n_ptiles, seq_ref[...], tok_ref[...])


def _for_run(rows, local_off, global_off, fn):
    @pl.when(rows > 0)
    def _():
        fn(pl.multiple_of(local_off, RUN_ALIGN), pl.multiple_of(global_off, RUN_ALIGN),
           pl.multiple_of(rows, RUN_ALIGN))


def _for_each_run(i, n_experts, toff_ref, cnt8_ref, gbase_ref, fn):
    for e in range(n_experts):
        idx = i * n_experts + e
        _for_run(cnt8_ref[idx], toff_ref[idx], gbase_ref[idx], fn)


def _tile_rows(tile, n_experts, toff_ref, cnt8_ref):
    last = tile * n_experts + n_experts - 1
    return pl.multiple_of(toff_ref[last] + cnt8_ref[last], RUN_ALIGN)


def _scatter_matrix(pos_ref, n_local, values):
    pos = pos_ref[...]
    col = lax.broadcasted_iota(I32, (pos.shape[0], n_local), 1)
    out = jnp.zeros((pos.shape[0], n_local), F32)
    for k in range(TOP_K):
        out = jnp.where(col == pos[:, k:k + 1], 1.0 if values is None else values[:, k:k + 1], out)
    return out


def _dispatch_body(toff_ref, cnt8_ref, gbase_ref, tstart_ref, trows_ref,
                   x1_ref, shs_ref, scs_ref, sht_ref, sct_ref, pos_ref,
                   xs_ref, lbuf, zbuf, sem, *, n_tiles, n_ptiles, n_experts):
    i = pl.program_id(0)
    sc = _select_mod(i, n_ptiles, scs_ref, sct_ref)
    sh = _select_mod(i, n_ptiles, shs_ref, sht_ref)
    h2 = (x1_ref[...] * (1.0 + sc) + sh).astype(BF16)
    n_local = lbuf.shape[1]
    slot = i % 2

    def runs(tile, sl, act):
        def fn(lo, go, rows):
            act(pltpu.make_async_copy(lbuf.at[sl, pl.ds(lo, rows), :], xs_ref.at[pl.ds(go, rows), :], sem.at[sl]))
        _for_each_run(tile, n_experts, toff_ref, cnt8_ref, gbase_ref, fn)

    start = lambda cp: cp.start()
    wait = lambda cp: cp.wait()

    def wait_runs(tile, sl):
        rows = _tile_rows(tile, n_experts, toff_ref, cnt8_ref)
        pltpu.make_async_copy(lbuf.at[sl, pl.ds(0, rows), :], xs_ref.at[pl.ds(0, rows), :], sem.at[sl]).wait()

    onehot = _scatter_matrix(pos_ref, n_local, None)
    srt = lax.dot_general(onehot.astype(BF16), h2, (((0,), (0,)), ((), ())), preferred_element_type=F32)
    packed = _pack_halves(srt)

    @pl.when(i >= 2)
    def _():
        wait_runs(i - 2, slot)

    lbuf[slot] = packed
    runs(i, slot, start)

    @pl.when(i == n_tiles - 1)
    def _():
        wait_runs(i - 1, 1 - slot)
        wait_runs(i, slot)
        zbuf[...] = jnp.zeros(zbuf.shape, zbuf.dtype)

        def zero_fill(e, act):
            _for_run(trows_ref[e], 0, tstart_ref[e], lambda lo, go, rows: act(pltpu.make_async_copy(
                zbuf.at[pl.ds(0, rows), :], xs_ref.at[pl.ds(go, rows), :], sem.at[0])))

        for e in range(n_experts):
            zero_fill(e, start)
        for e in range(n_experts):
            zero_fill(e, wait)


def _dispatch(tables, x1, sh_seq, sc_seq, sh_tok, sc_tok, pos, *, n_rows, n_local, n_ptiles, tiles_per_seq,
              n_experts):
    t, d = x1.shape
    tm = TOKEN_TILE
    nt = t // tm
    n_seq = sh_seq.shape[0]
    assert nt >= 2 and d % 2 == 0
    body = functools.partial(_dispatch_body, n_tiles=nt, n_ptiles=n_ptiles, n_experts=n_experts)
    seq_spec = pl.BlockSpec((None, 1, d), lambda i, *_: (jnp.minimum(i // tiles_per_seq, n_seq - 1), 0, 0))
    tok_spec = pl.BlockSpec((tm, d), lambda i, *_: (jnp.maximum(i - n_ptiles, 0), 0))
    return pl.pallas_call(
        body,
        grid_spec=pltpu.PrefetchScalarGridSpec(
            num_scalar_prefetch=len(tables),
            grid=(nt,),
            in_specs=[pl.BlockSpec((tm, d), lambda i, *_: (i, 0)),
                      seq_spec, seq_spec, tok_spec, tok_spec,
                      pl.BlockSpec((tm, LANES), lambda i, *_: (i, 0))],
            out_specs=pl.BlockSpec(memory_space=pl.ANY),
            scratch_shapes=[pltpu.VMEM((2, n_local, d // 2), U32), pltpu.VMEM((EXPERT_ROWS, d // 2), U32),
                            pltpu.SemaphoreType.DMA((2,))],
        ),
        out_shape=jax.ShapeDtypeStruct((n_rows, d // 2), U32),
        compiler_params=_cparams("arbitrary"),
        name="dispatch",
    )(*tables, x1, sh_seq, sc_seq, sh_tok, sc_tok, pos)


def _expert_body(be_ref, nu_ref, first_ref, next_ref, xs_ref, wgu_hbm, bgu_ref, wd_hbm, bd_ref, y_ref,
                 stage_gu, stage_dn, wgu_bf, wd_bf, sem, *, de):
    i = pl.program_id(0)
    used = i < nu_ref[0]
    expert = be_ref[i]

    def weights(e, act):
        act(pltpu.make_async_copy(wgu_hbm.at[e], stage_gu, sem.at[0]))
        act(pltpu.make_async_copy(wd_hbm.at[e], stage_dn, sem.at[1]))

    @pl.when(i == 0)
    def _():
        weights(expert, lambda cp: cp.start())

    @pl.when(used & (first_ref[i] == 1))
    def _():
        weights(expert, lambda cp: cp.wait())
        wgu_bf[...] = stage_gu[...].astype(BF16)
        wd_bf[...] = stage_dn[...].astype(BF16)
        nxt = next_ref[expert]

        @pl.when(nxt >= 0)
        def _():
            weights(nxt, lambda cp: cp.start())

    @pl.when(used)
    def _():
        x = jnp.concatenate(_unpack_halves(xs_ref[...]), axis=1)
        gu = jnp.dot(x, wgu_bf[...], preferred_element_type=F32) + bgu_ref[...]
        glu = jnp.minimum(gu[:, :de], SWIGLU_LIMIT)
        lin = jnp.clip(gu[:, de:], -SWIGLU_LIMIT, SWIGLU_LIMIT)
        act = glu * jax.nn.sigmoid(SWIGLU_ALPHA * glu) * (lin + 1.0)
        y = jnp.dot(act.astype(BF16), wd_bf[...], preferred_element_type=F32) + bd_ref[...]
        y_ref[...] = _pack_halves(y.astype(BF16).astype(F32))


def _experts(block_e, n_used, next_e, xs, w_gu, b_gu, w_down, b_down):
    p, dh = xs.shape
    bm = EXPERT_ROWS
    n_e, d, de2 = w_gu.shape
    de = de2 // 2
    nblk = p // bm

    def blk(i, be, nu, *_):
        return jnp.minimum(i, nu[0] - 1)

    first = jnp.concatenate([jnp.ones((1,), I32), (block_e[1:] != block_e[:-1]).astype(I32)])
    body = functools.partial(_expert_body, de=de)
    return pl.pallas_call(
        body,
        grid_spec=pltpu.PrefetchScalarGridSpec(
            num_scalar_prefetch=4,
            grid=(nblk,),
            in_specs=[pl.BlockSpec((bm, dh), lambda i, be, nu, *_: (blk(i, be, nu), 0)),
                      pl.BlockSpec(memory_space=pl.ANY),
                      pl.BlockSpec((None, 1, de2), lambda i, be, nu, *_: (be[blk(i, be, nu)], 0, 0)),
                      pl.BlockSpec(memory_space=pl.ANY),
                      pl.BlockSpec((None, 1, d), lambda i, be, nu, *_: (be[blk(i, be, nu)], 0, 0))],
            out_specs=pl.BlockSpec((bm, dh), lambda i, be, nu, *_: (blk(i, be, nu), 0)),
            scratch_shapes=[pltpu.VMEM((d, de2), F32), pltpu.VMEM((de, d), F32),
                            pltpu.VMEM((d, de2), BF16), pltpu.VMEM((de, d), BF16), pltpu.SemaphoreType.DMA((2,))],
        ),
        out_shape=jax.ShapeDtypeStruct((p, dh), U32),
        compiler_params=_cparams("arbitrary"),
        name="experts",
    )(block_e, n_used, first, next_e, xs, w_gu, b_gu.reshape(n_e, 1, de2), w_down, b_down.reshape(n_e, 1, d))


def _combine_body(toff_ref, cnt8_ref, gbase_ref, x1_ref, gts_ref, gtt_ref, gate_ref, pos_ref,
                  lnw_ref, lnb_ref, yb_ref, yp_ref, ys_ref, ybuf, sem, *, top_k, n_ptiles, n_experts, alpha):
    i = pl.program_id(0)
    n_tiles = pl.num_programs(0)
    tm = x1_ref.shape[0]
    n_local = ybuf.shape[1]
    slot = i % 2

    def fetch(tile, sl, act):
        def fn(lo, go, rows):
            act(pltpu.make_async_copy(yb_ref.at[pl.ds(go, rows), :], ybuf.at[sl, pl.ds(lo, rows), :], sem.at[sl]))
        _for_each_run(tile, n_experts, toff_ref, cnt8_ref, gbase_ref, fn)

    def start_fetch(tile, sl):
        ybuf[sl, tm * top_k:, :] = jnp.zeros((n_local - tm * top_k, ybuf.shape[2]), ybuf.dtype)
        fetch(tile, sl, lambda cp: cp.start())

    weights = _scatter_matrix(pos_ref, n_local, gate_ref[...]).astype(BF16)

    @pl.when(i == 0)
    def _():
        start_fetch(i, slot)

    @pl.when(i + 1 < n_tiles)
    def _():
        start_fetch(i + 1, 1 - slot)

    rows = _tile_rows(i, n_experts, toff_ref, cnt8_ref)
    pltpu.make_async_copy(yb_ref.at[pl.ds(0, rows), :], ybuf.at[slot, pl.ds(0, rows), :], sem.at[slot]).wait()
    y_lo, y_hi = _unpack_halves(ybuf[slot])
    ff = jnp.concatenate([jnp.dot(weights, y_lo, preferred_element_type=F32),
                          jnp.dot(weights, y_hi, preferred_element_type=F32)], axis=1)
    gt = _select_mod(i, n_ptiles, gts_ref, gtt_ref)
    y = _layer_norm(alpha * x1_ref[...] + gt * ff, lnw_ref[...], lnb_ref[...])

    @pl.when(i < n_ptiles)
    def _():
        yp_ref[...] = y

    @pl.when(i >= n_ptiles)
    def _():
        ys_ref[...] = y


def _combine(tables, x1, gt_seq, gt_tok, gate, pos, lnw, lnb, yb, *, n_local, n_ptiles, tiles_per_seq, n_experts,
             alpha):
    t, d = x1.shape
    tm = TOKEN_TILE
    nt = t // tm
    n_seq = gt_seq.shape[0]
    body = functools.partial(_combine_body, top_k=TOP_K, n_ptiles=n_ptiles, n_experts=n_experts, alpha=alpha)
    lane_spec = pl.BlockSpec((tm, LANES), lambda i, *_: (i, 0))
    return pl.pallas_call(
        body,
        grid_spec=pltpu.PrefetchScalarGridSpec(
            num_scalar_prefetch=len(tables),
            grid=(nt,),
            in_specs=[pl.BlockSpec((tm, d), lambda i, *_: (i, 0)),
                      pl.BlockSpec((None, 1, d), lambda i, *_: (jnp.minimum(i // tiles_per_seq, n_seq - 1), 0, 0)),
                      pl.BlockSpec((tm, d), lambda i, *_: (jnp.maximum(i - n_ptiles, 0), 0)),
                      lane_spec,
                      pl.BlockSpec((tm, LANES), lambda i, *_: (i, 0)),
                      pl.BlockSpec((1, d), lambda i, *_: (0, 0)),
                      pl.BlockSpec((1, d), lambda i, *_: (0, 0)),
                      pl.BlockSpec(memory_space=pl.ANY)],
            out_specs=[pl.BlockSpec((tm, d), lambda i, *_: (jnp.minimum(i, n_ptiles - 1), 0)),
                       pl.BlockSpec((tm, d), lambda i, *_: (jnp.maximum(i - n_ptiles, 0), 0))],
            scratch_shapes=[pltpu.VMEM((2, n_local, d // 2), U32), pltpu.SemaphoreType.DMA((2,))],
        ),
        out_shape=[jax.ShapeDtypeStruct((n_ptiles * tm, d), F32),
                   jax.ShapeDtypeStruct(((nt - n_ptiles) * tm, d), F32)],
        compiler_params=_cparams("arbitrary"),
        name="combine",
    )(*tables, x1, gt_seq, gt_tok, gate, pos, lnw, lnb, yb)


def _rotary_tables(pos, hd, rot_dim):
    half = rot_dim // 2
    inv_freq = jnp.power(jnp.float32(ROPE_THETA), -jnp.arange(half, dtype=F32) * (2.0 / rot_dim))
    ang = pos.astype(F32)[:, None] * inv_freq[None, :]
    cos, sin = jnp.cos(ang), jnp.sin(ang)
    n = pos.shape[0]
    ones = jnp.ones((n, hd - rot_dim), F32)
    zeros = jnp.zeros((n, hd - rot_dim), F32)
    zh = jnp.zeros((n, half), F32)
    ct = jnp.concatenate([cos, cos, ones], axis=1)
    s1 = jnp.concatenate([-sin, zh, zeros], axis=1)
    s2 = jnp.concatenate([zh, sin, zeros], axis=1)
    reps = LANES // hd
    return tuple(jnp.tile(a, (1, reps)) for a in (ct, s1, s2))


def kernel(x_prompt, x_sample, state_win_k, state_win_v, state_conv, state_ssm, c_prompt, c_sample, w_ada, b_ada, w_in, attn_sinks, w_conv, dn_a_log, dn_dt_bias, dn_norm_w, w_proj_attn, w_proj_dn, w_out, ln1_w, ln1_b, w_router, b_router, w_gu, b_gu, w_down, b_down, ln2_w, ln2_b):
    n_p, seq, d = x_prompt.shape
    n_s, l_s, _ = x_sample.shape
    depth = w_ada.shape[0]
    window, n_kv, hd = state_win_k.shape[2:]
    n_q = attn_sinks.shape[1]
    heads, dk, dv = state_ssm.shape[2:]
    conv_w, conv_dim = w_conv.shape[1:]
    n_e = w_router.shape[2]
    qd, kd = n_q * hd, n_kv * hd
    vdim = heads * dv
    rot_dim = hd // 4
    alpha = float((2 * depth) ** 0.25)
    tm = TOKEN_TILE
    t_p, t_s = n_p * seq, n_s * l_s
    t_all = t_p + t_s
    tps = seq // tm
    n_ptiles = t_p // tm
    l_pad = SUBLANES
    assert seq % tm == 0 and t_s % tm == 0 and tm % l_s == 0 and l_s <= l_pad and l_s >= conv_w - 1
    assert 2 * heads <= SUBLANES and n_e <= LANES and hd * 2 == LANES and rot_dim % 2 == 0

    sizes = [qd, kd, kd, conv_dim, vdim, heads, heads, d, d]
    offs = [int(o) for o in np.concatenate([[0], np.cumsum(sizes)])]
    cuts = (qd + 2 * kd, qd + 2 * kd + conv_dim, qd + 2 * kd + conv_dim + vdim, qd + 2 * kd + conv_dim + vdim + 2 * d)
    cuts = cuts + (cuts[-1] + LANES,)

    tabs_p = _rotary_tables(jnp.arange(seq, dtype=I32), hd, rot_dim)
    tabs_s = tuple(jnp.tile(a, (INPROJ_TILE // l_s, 1))
                   for a in _rotary_tables(PAST_LEN + jnp.arange(l_s, dtype=I32), hd, rot_dim))

    x_p = x_prompt.reshape(t_p, d)
    x_s = x_sample.reshape(t_s, d)
    c_all = jnp.concatenate([c_prompt, c_sample], axis=0)
    outs = {k: [] for k in ("pwk", "pwv", "pcv", "pss", "swk", "swv", "scv", "sss")}

    for l in range(depth):
        w_in_l = jnp.concatenate([w_in[l][:, :offs[5]], w_in[l][:, offs[7]:], w_in[l][:, offs[5]:offs[7]],
                                  jnp.zeros((d, LANES - 2 * heads), F32)], axis=1).astype(BF16)
        mod = _ada(c_all, w_ada[l], b_ada[l])
        mod_p = mod[:n_p].reshape(n_p, 6, 1, d)
        mod_s = jnp.repeat(mod[n_p:].reshape(n_s, 6, d).transpose(1, 0, 2), l_s, axis=1)
        sh1p, sc1p, gt1p, sh2p, sc2p, gt2p = [mod_p[:, k] for k in range(6)]
        sh1s, sc1s, gt1s, sh2s, sc2s, gt2s = [mod_s[k] for k in range(6)]

        qkva_p, dn_p, z_p, g_p, ba_p, tail_p, kvw_p = _inproj(
            x_p, sh1p, sc1p, tabs_p, w_in_l, cuts, per_token_mod=False, tiles_per_seq=seq // INPROJ_TILE, act_dtype=BF16,
            window=window, kv_cols=2 * kd, half_rot=rot_dim // 2)
        attn_p = _attn_prompt(qkva_p, attn_sinks[l], n_p, seq, n_q, n_kv, hd, window)
        chunk = min(DN_CHUNK, seq)
        nc = seq // chunk
        bat_p = ba_p[:, :SUBLANES].reshape(n_p, nc, chunk, SUBLANES).transpose(0, 1, 3, 2)
        hp = jnp.zeros((SUBLANES, LANES), F32).at[0, :heads].set(dn_a_log[l]).at[1, :heads].set(dn_dt_bias[l])
        nw = dn_norm_w[l].reshape(1, dv)
        o_p, ssm_p = _deltanet(dn_p.reshape(n_p, seq, conv_dim), z_p.reshape(n_p, seq, vdim),
                               ba_p.reshape(n_p, seq, LANES), bat_p,
                               jnp.zeros((n_p, SUBLANES, conv_dim), F32), jnp.zeros((n_p, heads, dk, dv), F32),
                               w_conv[l], hp, nw, chunk=chunk, l_real=chunk, nb=4)
        outs["pwk"].append(kvw_p[:, :, :kd].reshape(n_p, window, n_kv, hd))
        outs["pwv"].append(kvw_p[:, :, kd:].reshape(n_p, window, n_kv, hd))
        outs["pcv"].append(tail_p.reshape(n_p, seq // INPROJ_TILE, SUBLANES, conv_dim)[:, -1, SUBLANES - (conv_w - 1):])
        outs["pss"].append(ssm_p)

        qkva_s, dn_s, z_s, g_s, ba_s = _inproj(
            x_s, sh1s, sc1s, tabs_s, w_in_l, cuts, per_token_mod=True, tiles_per_seq=1, act_dtype=F32,
            window=window, kv_cols=2 * kd, half_rot=rot_dim // 2)
        pad_l = lambda a: jnp.pad(a.reshape(n_s, l_s, a.shape[-1]), ((0, 0), (0, l_pad - l_s), (0, 0)))
        attn_s, wk_s, wv_s = _attn_sample(pad_l(qkva_s), state_win_k[l].reshape(n_s, window, kd),
                                          state_win_v[l].reshape(n_s, window, kd), attn_sinks[l], n_q, n_kv, hd, l_s)
        attn_s = attn_s[:, :l_s].reshape(t_s, qd)
        ba_s3 = pad_l(ba_s)
        bat_s = ba_s3[:, :, :SUBLANES].transpose(0, 2, 1).reshape(n_s, 1, SUBLANES, l_pad)
        cs0 = jnp.pad(state_conv[l], ((0, 0), (SUBLANES - (conv_w - 1), 0), (0, 0)))
        o_s, ssm_s = _deltanet(pad_l(dn_s), pad_l(z_s), ba_s3, bat_s, cs0, state_ssm[l], w_conv[l], hp, nw,
                               chunk=l_pad, l_real=l_s, nb=8)
        o_s = o_s[:, :l_s].reshape(t_s, vdim)
        outs["swk"].append(wk_s.reshape(n_s, window, n_kv, hd))
        outs["swv"].append(wv_s.reshape(n_s, window, n_kv, hd))
        outs["scv"].append(jnp.concatenate([state_conv[l], dn_s.reshape(n_s, l_s, conv_dim)], axis=1)[:, -(conv_w - 1):])
        outs["sss"].append(ssm_s)

        wr = jnp.pad(w_router[l], ((0, 0), (0, LANES - n_e)))
        br = jnp.pad(b_router[l], (0, LANES - n_e), constant_values=NEG_BIG).reshape(1, LANES)
        wr_hi = wr.astype(BF16)
        wr_lo = (wr - wr_hi.astype(F32)).astype(BF16)
        wts = (w_proj_attn[l].astype(BF16), w_proj_dn[l].astype(BF16), w_out[l].astype(BF16),
               ln1_w[l].reshape(1, d), ln1_b[l].reshape(1, d), wr_hi, wr_lo, br)
        res_p = _outproj(attn_p, o_p.reshape(t_p, vdim), g_p, x_p, gt1p, sh2p, sc2p, wts, None,
                         per_token_mod=False, tiles_per_seq=tps, t_total=t_all, tile_off=0, alpha=alpha)
        x1, pos, gate, cnt = _outproj(attn_s, o_s, g_s, x_s, gt1s, sh2s, sc2s, wts, res_p, per_token_mod=True,
                                      tiles_per_seq=1, t_total=t_all, tile_off=n_ptiles, alpha=alpha)

        bm = EXPERT_ROWS
        nt_all = t_all // tm
        cnt8 = (cnt[:, 0, :n_e].astype(I32) + RUN_ALIGN - 1) // RUN_ALIGN * RUN_ALIGN
        tot = jnp.sum(cnt8, axis=0)
        padded = (tot + bm - 1) // bm * bm
        pad_end = jnp.cumsum(padded)
        pad_start = pad_end - padded
        gbase = pad_start[None, :] + jnp.cumsum(cnt8, axis=0) - cnt8
        toff = jnp.cumsum(cnt8, axis=1) - cnt8
        n_local = -(-(tm * TOP_K + n_e * (RUN_ALIGN - 1)) // LANES) * LANES
        n_rows = -(-(t_all * TOP_K + nt_all * n_e * (RUN_ALIGN - 1) + n_e * (bm - 1)) // bm) * bm
        nblk = n_rows // bm
        n_used = jnp.maximum(pad_end[-1:] // bm, 1).astype(I32)
        block_e = jnp.minimum(jnp.sum(pad_end[None, :] <= (jnp.arange(nblk, dtype=I32) * bm)[:, None], axis=1),
                              n_e - 1).astype(I32)
        flat = lambda a: a.astype(I32).reshape(nt_all * n_e)
        run_tables = (flat(toff), flat(cnt8), flat(gbase))
        tail_tables = ((pad_start + tot).astype(I32), (padded - tot).astype(I32))

        xs = _dispatch(run_tables + tail_tables, x1, sh2p, sc2p, sh2s, sc2s, pos, n_rows=n_rows, n_local=n_local,
                       n_ptiles=n_ptiles, tiles_per_seq=tps, n_experts=n_e)
        group_end_blk = pad_end // bm
        next_e = jnp.where(group_end_blk < n_used[0], block_e[jnp.minimum(group_end_blk, nblk - 1)], -1).astype(I32)
        yb = _experts(block_e, n_used, next_e, xs, w_gu[l], b_gu[l], w_down[l], b_down[l])
        x_p, x_s = _combine(run_tables, x1, gt2p, gt2s, gate, pos, ln2_w[l].reshape(1, d), ln2_b[l].reshape(1, d),
                            yb, n_local=n_local, n_ptiles=n_ptiles, tiles_per_seq=tps, n_experts=n_e, alpha=alpha)

    st = lambda k: outs[k][0][None] if depth == 1 else jnp.stack(outs[k])
    return (x_p.reshape(n_p, seq, d), x_s.reshape(n_s, l_s, d), st("pwk"), st("pwv"), st("pcv"), st("pss"),
            st("swk"), st("swv"), st("scv"), st("sss"))
```
